```python
import math
import jax, jax.numpy as jnp
from jax import lax
import numpy as np

D_MODEL = 1024
BATCH = 16
SEQ = 2048
DEPTH = 2
DEC_BATCH = 16
DEC_SEQ = 64
PAST_LEN = 1024

CHUNK = 64
QBLK = 128
HEAD_DIM = 64
N_HEADS = D_MODEL // 128
BRANCH_W = N_HEADS * HEAD_DIM
N_BRANCH = 3
N_IDX = 4
IDX_DIM = 64
TOPK_MAX = 256
ROT_DIM = HEAD_DIM // 4
ROPE_THETA = 500000.0
D_FF = ((8 * D_MODEL // 3 + 255) // 256) * 256
CONV_W = 3
FORGET_BIAS = 2.0
EPS = 1e-6
NEG = -1e30
IN_SIZES = (BRANCH_W, HEAD_DIM, HEAD_DIM, N_IDX * IDX_DIM, IDX_DIM, N_IDX,
            BRANCH_W, BRANCH_W, BRANCH_W,
            BRANCH_W, BRANCH_W, BRANCH_W, N_HEADS)
IN_W = sum(IN_SIZES)

kernel_name = 'hybrid_stream_encoder_dsa_stickbreak_fox_step'


def rmsnorm(x, g):
    xf = x.astype(jnp.float32)
    y = xf * lax.rsqrt(jnp.mean(xf * xf, axis=-1, keepdims=True) + EPS)
    return (y * g.astype(jnp.float32)).astype(x.dtype)


def rope(x, pos):
    half = ROT_DIM // 2
    freq = ROPE_THETA ** (-jnp.arange(half, dtype=jnp.float32) / half)
    ang = pos.astype(jnp.float32)[:, None] * freq[None, :]
    shape = (1, pos.shape[0]) + (1,) * (x.ndim - 3) + (half,)
    cos = jnp.cos(ang).reshape(shape)
    sin = jnp.sin(ang).reshape(shape)
    xr = x[..., :ROT_DIM].astype(jnp.float32)
    x1, x2 = xr[..., :half], xr[..., half:]
    rot = jnp.concatenate([x1 * cos - x2 * sin, x2 * cos + x1 * sin], axis=-1).astype(x.dtype)
    return jnp.concatenate([rot, x[..., ROT_DIM:]], axis=-1)


def sweep_queries(fn, q_pos, *q_arrays):
    T = q_pos.shape[0]
    if T <= QBLK:
        return fn(q_pos, *q_arrays)
    nb = T // QBLK
    def split(a):
        return jnp.moveaxis(a.reshape((a.shape[0], nb, QBLK) + a.shape[2:]), 1, 0)
    out = lax.map(lambda args: fn(*args), (q_pos.reshape(nb, QBLK),) + tuple(split(a) for a in q_arrays))
    out = jnp.moveaxis(out, 0, 1)
    return out.reshape((out.shape[0], T) + out.shape[3:])


def dsa_block(q_pos, q, qi, wi, k, v, ki, k_pos, topk):
    f32 = jnp.float32
    dots = jnp.einsum('bqhd,bkd->bqhk', qi.astype(f32), ki.astype(f32)) * IDX_DIM ** -0.5
    score = jnp.einsum('bqhk,bqh->bqk', jax.nn.relu(dots), wi.astype(f32))
    adm = (k_pos[None, :] // CHUNK) <= (q_pos[:, None] // CHUNK)
    score = jnp.where(adm[None], score, NEG)
    top_val, top_idx = lax.top_k(score, topk)
    valid = top_val > 0.5 * NEG
    gather = jax.vmap(lambda rows, idx: rows[idx])
    kg = gather(k, top_idx).astype(f32)
    vg = gather(v, top_idx).astype(f32)
    s = jnp.einsum('bqhd,bqkd->bhqk', q.astype(f32), kg) * HEAD_DIM ** -0.5
    s = jnp.where(valid[:, None], s, NEG)
    p = jax.nn.softmax(s, axis=-1)
    return jnp.einsum('bhqk,bqkd->bqhd', p, vg).astype(q.dtype)


def sb_block(q_pos, q, k, v, k_pos):
    f32 = jnp.float32
    z = jnp.einsum('bqhd,bkhd->bhqk', q.astype(f32), k.astype(f32)) * HEAD_DIM ** -0.5
    strict = k_pos[None, :] < q_pos[:, None]
    la = jnp.where(strict, jax.nn.log_sigmoid(-z), 0.0)
    after = lax.cumsum(la, axis=3, reverse=True) - la
    a = jnp.where(strict, jnp.exp(jax.nn.log_sigmoid(z) + after), 0.0)
    return jnp.einsum('bhqk,bkhd->bqhd', a, v.astype(f32)).astype(q.dtype)


def fox_block(q_pos, q, fq, k, v, fk, k_pos):
    f32 = jnp.float32
    s = jnp.einsum('bqhd,bkhd->bhqk', q.astype(f32), k.astype(f32)) * HEAD_DIM ** -0.5
    s = s + jnp.moveaxis(fq, 1, 2)[:, :, :, None] - jnp.moveaxis(fk, 1, 2)[:, :, None, :]
    causal = k_pos[None, :] <= q_pos[:, None]
    p = jax.nn.softmax(jnp.where(causal, s, NEG), axis=-1)
    return jnp.einsum('bhqk,bkhd->bqhd', p, v.astype(f32)).astype(q.dtype)


def token_mix(xn, past, w_in, b_f, w_branch, w_gate, w_out):
    B, T, _ = xn.shape
    L = T if past is None else past[0].shape[1] + T
    pos = jnp.arange(L - T, L)
    points = np.cumsum(IN_SIZES)[:-1].tolist()
    (a_q, a_k, a_v, i_q, i_k, i_w, b_q, b_k, b_v, c_q, c_k, c_v, c_f) = jnp.split(xn @ w_in, points, axis=-1)
    heads = lambda t: t.reshape(B, T, N_HEADS, HEAD_DIM)
    a_q = rope(heads(a_q), pos)
    a_k = rope(a_k, pos)
    i_q = rope(i_q.reshape(B, T, N_IDX, IDX_DIM), pos)
    i_k = rope(i_k, pos)
    i_w = i_w * N_IDX ** -0.5
    b_q, b_k, b_v = heads(b_q), heads(b_k), heads(b_v)
    c_q, c_k, c_v = heads(c_q), heads(c_k), heads(c_v)
    c_logf = jax.nn.log_sigmoid(c_f.astype(jnp.float32) + b_f.astype(jnp.float32))
    new_rows = (a_k, a_v, i_k, b_k, b_v, c_k, c_v, c_logf)
    if past is None:
        ka, va, kia, kb, vb, kc, vc, logf = new_rows
    else:
        ka, va, kia, kb, vb, kc, vc, logf = (jnp.concatenate([p_, n_], axis=1) for p_, n_ in zip(past, new_rows))
    k_pos = jnp.arange(L)
    topk = max(1, min(TOPK_MAX, L // 4))
    fk = jnp.cumsum(logf.astype(jnp.float32), axis=1)
    fq = fk[:, L - T:]
    o_a = sweep_queries(lambda qp, q, qi, wi: dsa_block(qp, q, qi, wi, ka, va, kia, k_pos, topk), pos, a_q, i_q, i_w)
    o_b = sweep_queries(lambda qp, q: sb_block(qp, q, kb, vb, k_pos), pos, b_q)
    o_c = sweep_queries(lambda qp, q, f: fox_block(qp, q, f, kc, vc, fk, k_pos), pos, c_q, fq)
    o = jnp.stack([o_a, o_b, o_c], axis=2).reshape(B, T, N_BRANCH, BRANCH_W)
    branch = jnp.einsum('btnw,nwd->btnd', o, w_branch)
    gate = jax.nn.sigmoid((xn @ w_gate).reshape(B, T, N_BRANCH, D_MODEL))
    return jnp.einsum('btnd,btnd->btd', gate, branch) @ w_out, new_rows


def conv_ffn(xn, conv_past, w_up, conv_w, conv_b, w_down):
    T = xn.shape[1]
    h = xn @ w_up
    hp = jnp.concatenate([conv_past.astype(h.dtype), h], axis=1)
    hc = conv_b + conv_w[0] * hp[:, 0:T]
    for i in range(1, CONV_W):
        hc = hc + conv_w[i] * hp[:, i:i + T]
    g, u = jnp.split(hc, 2, axis=-1)
    return (jax.nn.silu(g) * u) @ w_down, hp[:, -(CONV_W - 1):]


def trunk(x, past, norm_mix, w_in, b_f, w_branch, w_gate, w_out, norm_ffn, w_up, conv_w, conv_b, w_down, norm_final):
    per_layer = []
    for l in range(DEPTH):
        lp = None if past is None else tuple(c[l] for c in past[:-1])
        m, rows = token_mix(rmsnorm(x, norm_mix[l]), lp, w_in[l], b_f[l], w_branch[l], w_gate[l], w_out[l])
        x = x + m
        if past is None:
            conv_past = jnp.zeros((x.shape[0], CONV_W - 1, 2 * D_FF), x.dtype)
        else:
            conv_past = past[-1][l]
        f, conv_state = conv_ffn(rmsnorm(x, norm_ffn[l]), conv_past, w_up[l], conv_w[l], conv_b[l], w_down[l])
        x = x + f
        per_layer.append(rows + (conv_state,))
    new_state = tuple(jnp.stack(s, axis=0) for s in zip(*per_layer))
    return rmsnorm(x, norm_final), new_state


def setup_inputs(seed: int = 0) -> dict:
    key = jax.random.key(seed)
    ks = jax.random.split(key, 24)
    f32 = jnp.float32
    nrm = lambda k, shape, scale=1.0: scale * jax.random.normal(k, shape, f32)
    F2 = 2 * D_FF
    return {
        'x_prompt': nrm(ks[0], (BATCH, SEQ, D_MODEL)),
        'x_sample': nrm(ks[1], (DEC_BATCH, DEC_SEQ, D_MODEL)),
        'cache_a_k': nrm(ks[2], (DEPTH, DEC_BATCH, PAST_LEN, HEAD_DIM)),
        'cache_a_v': nrm(ks[3], (DEPTH, DEC_BATCH, PAST_LEN, HEAD_DIM)),
        'cache_a_idx_k': nrm(ks[4], (DEPTH, DEC_BATCH, PAST_LEN, IDX_DIM)),
        'cache_b_k': nrm(ks[5], (DEPTH, DEC_BATCH, PAST_LEN, N_HEADS, HEAD_DIM)),
        'cache_b_v': nrm(ks[6], (DEPTH, DEC_BATCH, PAST_LEN, N_HEADS, HEAD_DIM)),
        'cache_c_k': nrm(ks[7], (DEPTH, DEC_BATCH, PAST_LEN, N_HEADS, HEAD_DIM)),
        'cache_c_v': nrm(ks[8], (DEPTH, DEC_BATCH, PAST_LEN, N_HEADS, HEAD_DIM)),
        'cache_c_logf': jax.nn.log_sigmoid(FORGET_BIAS + nrm(ks[9], (DEPTH, DEC_BATCH, PAST_LEN, N_HEADS))),
        'state_ffn_conv': nrm(ks[10], (DEPTH, DEC_BATCH, CONV_W - 1, F2)),
        'norm_mix': 1.0 + nrm(ks[11], (DEPTH, D_MODEL), 0.02),
        'w_in': nrm(ks[12], (DEPTH, D_MODEL, IN_W), D_MODEL ** -0.5),
        'b_f': FORGET_BIAS + nrm(ks[13], (DEPTH, N_HEADS), 0.5),
        'w_branch': nrm(ks[14], (DEPTH, N_BRANCH, BRANCH_W, D_MODEL), BRANCH_W ** -0.5),
        'w_gate': nrm(ks[15], (DEPTH, D_MODEL, N_BRANCH * D_MODEL), D_MODEL ** -0.5),
        'w_out': nrm(ks[16], (DEPTH, D_MODEL, D_MODEL), D_MODEL ** -0.5),
        'norm_ffn': 1.0 + nrm(ks[17], (DEPTH, D_MODEL), 0.02),
        'w_up': nrm(ks[18], (DEPTH, D_MODEL, F2), D_MODEL ** -0.5),
        'conv_w': nrm(ks[19], (DEPTH, CONV_W, F2), CONV_W ** -0.5),
        'conv_b': nrm(ks[20], (DEPTH, F2), 0.02),
        'w_down': nrm(ks[21], (DEPTH, D_FF, D_MODEL), D_FF ** -0.5),
        'norm_final': 1.0 + nrm(ks[22], (D_MODEL,), 0.02),
    }


def reference(x_prompt, x_sample, cache_a_k, cache_a_v, cache_a_idx_k, cache_b_k, cache_b_v, cache_c_k, cache_c_v,
              cache_c_logf, state_ffn_conv, norm_mix, w_in, b_f, w_branch, w_gate, w_out, norm_ffn, w_up, conv_w,
              conv_b, w_down, norm_final):
    weights = (norm_mix, w_in, b_f, w_branch, w_gate, w_out, norm_ffn, w_up, conv_w, conv_b, w_down, norm_final)
    y_prompt, (p_a_k, p_a_v, p_a_idx_k, p_b_k, p_b_v, p_c_k, p_c_v, p_c_logf, p_ffn_conv) = trunk(x_prompt, None, *weights)
    past = (cache_a_k, cache_a_v, cache_a_idx_k, cache_b_k, cache_b_v, cache_c_k, cache_c_v, cache_c_logf, state_ffn_conv)
    y_sample, (s_a_k, s_a_v, s_a_idx_k, s_b_k, s_b_v, s_c_k, s_c_v, s_c_logf, s_ffn_conv) = trunk(x_sample, past, *weights)
    return (y_prompt, y_sample, p_a_k, p_a_v, p_a_idx_k, p_b_k, p_b_v, p_c_k, p_c_v, p_c_logf, p_ffn_conv,
            s_a_k, s_a_v, s_a_idx_k, s_b_k, s_b_v, s_c_k, s_c_v, s_c_logf, s_ffn_conv)
```

```python
import functools

import jax
import jax.numpy as jnp
import numpy as np
from jax import lax
from jax.experimental import pallas as pl
from jax.experimental.pallas import tpu as pltpu

F32 = jnp.float32
BF16 = jnp.bfloat16

LANES = 128
SUBLANES = 8
HEAD_DIM = 64
N_HEADS = 8
N_PAIRS = N_HEADS // 2
BRANCH_W = N_HEADS * HEAD_DIM
N_BRANCH = 3
N_IDX = 4
IDX_DIM = 64
CHUNK = 64
CHUNK_SHIFT = 6
TOPK_MAX = 256
ROT_DIM = HEAD_DIM // 4
ROPE_THETA = 500000.0
CONV_W = 3
EPS = 1e-6
NEG = -1e30
KEY_TILE = 128
FF_CHUNK = 256
VMEM_LIMIT = 56 * 1024 * 1024

ROPE_W = 1024
MISC_W = 256
CF_LANE = 8


def _cparams(sem):
    return pltpu.CompilerParams(dimension_semantics=sem, vmem_limit_bytes=VMEM_LIMIT)


def _rms(x, g):
    return x * lax.rsqrt(jnp.mean(x * x, axis=-1, keepdims=True) + EPS) * g


def _sigmoid(z):
    return 1.0 / (1.0 + jnp.exp(-z))


def _log_sigmoid(z):
    return jnp.minimum(z, 0.0) - jnp.log1p(jnp.exp(-jnp.abs(z)))


def _dot_t(a, b):
    return lax.dot_general(a, b, (((1,), (1,)), ((), ())), preferred_element_type=F32)


def _dot(a, b):
    return jnp.dot(a, b, preferred_element_type=F32)


def _keep_head(blk, which):
    lane = lax.broadcasted_iota(jnp.int32, blk.shape, blk.ndim - 1)
    keep = (lane >= HEAD_DIM) if which else (lane < HEAD_DIM)
    return jnp.where(keep, blk, jnp.zeros_like(blk))


def _merge_pair(o0, o1):
    lane = lax.broadcasted_iota(jnp.int32, o0.shape, o0.ndim - 1)
    return jnp.where(lane < HEAD_DIM, o0, o1)


def _proj_kernel(x_ref, g_ref, w_ref, bf_ref, rc_ref, rs1_ref, rs2_ref,
                 ak_ref, av_ref, ik_ref, bk_ref, bv_ref, ck_ref, cv_ref, lf_ref, iw_ref,
                 aq16, iq16, ak16, av16, ik16, bq16, bk16, bv16, cq16, ck16, cv16):
    xn = _rms(x_ref[0], g_ref[...]).astype(BF16)
    rc, rs1, rs2 = rc_ref[...], rs1_ref[...], rs2_ref[...]

    def cols(c0):
        return _dot(xn, w_ref[:, c0:c0 + LANES])

    def rope(blk):
        half = ROT_DIM // 2
        return blk * rc + pltpu.roll(blk, LANES - half, 1) * rs1 + pltpu.roll(blk, half, 1) * rs2

    for g in range(4):
        aq16[0, :, g * LANES:(g + 1) * LANES] = (rope(cols(g * LANES)) * HEAD_DIM ** -0.5).astype(BF16)
    for g in range(2):
        iq16[0, :, g * LANES:(g + 1) * LANES] = (rope(cols(512 + g * LANES)) * IDX_DIM ** -0.5).astype(BF16)
    blk = rope(cols(768))
    ak_ref[0] = blk[:, :HEAD_DIM]
    ak16[0] = blk.astype(BF16)
    blk = rope(cols(896))
    ik_ref[0] = blk[:, :IDX_DIM]
    ik16[0] = blk.astype(BF16)

    blk = cols(ROPE_W)
    av_ref[0] = blk[:, :HEAD_DIM]
    av16[0] = blk.astype(BF16)
    blk = cols(ROPE_W + LANES)
    iw_ref[0] = blk[:, :SUBLANES] * N_IDX ** -0.5
    lf_ref[0] = _log_sigmoid(blk[:, CF_LANE:CF_LANE + N_HEADS] + bf_ref[:, CF_LANE:CF_LANE + N_HEADS])

    base = ROPE_W + MISC_W
    outs = ((None, bq16), (bk_ref, bk16), (bv_ref, bv16), (None, cq16), (ck_ref, ck16), (cv_ref, cv16))
    for n, (o32, o16) in enumerate(outs):
        for g in range(BRANCH_W // LANES):
            blk = cols(base + n * BRANCH_W + g * LANES)
            sl = slice(g * LANES, (g + 1) * LANES)
            if o32 is None:
                o16[0, :, sl] = (blk * HEAD_DIM ** -0.5).astype(BF16)
            else:
                o32[0, :, sl] = blk
                o16[0, :, sl] = blk.astype(BF16)


def _project(x, g, w, bf_row, tables, tm):
    B, T, D = x.shape
    tok = lambda w_, dt: jax.ShapeDtypeStruct((B, T, w_), dt)
    tspec = lambda w_: pl.BlockSpec((1, tm, w_), lambda b, t: (b, t, 0))
    const = lambda a: pl.BlockSpec(a.shape, lambda b, t: (0,) * a.ndim)
    out_shapes = ([tok(64, F32)] * 3 + [tok(512, F32)] * 4 + [tok(8, F32)] * 2
                  + [tok(512, BF16), tok(256, BF16)] + [tok(128, BF16)] * 3 + [tok(512, BF16)] * 6)
    rspec = pl.BlockSpec((tm, LANES), lambda b, t: (t, 0))
    return pl.pallas_call(
        _proj_kernel,
        grid=(B, T // tm),
        in_specs=[tspec(D), const(g), const(w), const(bf_row), rspec, rspec, rspec],
        out_specs=[tspec(s.shape[-1]) for s in out_shapes],
        out_shape=out_shapes,
        compiler_params=_cparams(("parallel", "parallel")),
        name="in_proj",
    )(x, g, w, bf_row, *tables)


def _split3(x):
    hi = x.astype(BF16)
    r = x - hi.astype(F32)
    mid = r.astype(BF16)
    lo = (r - mid.astype(F32)).astype(BF16)
    return hi, mid, lo


def _cumsum_kernel(lf_ref, o_ref, *, blk):
    L = lf_ref.shape[-1]
    r = lax.broadcasted_iota(jnp.int32, (blk, blk), 0)
    c = lax.broadcasted_iota(jnp.int32, (blk, blk), 1)
    upper = jnp.where(r <= c, 1.0, 0.0).astype(BF16)
    carry = jnp.zeros((N_HEADS, 1), F32)
    for j in range(L // blk):
        hi, mid, lo = _split3(lf_ref[0, :, j * blk:(j + 1) * blk])
        f = _dot(hi, upper) + _dot(mid, upper) + _dot(lo, upper) + carry
        o_ref[0, :, j * blk:(j + 1) * blk] = f
        carry = f[:, blk - 1:blk]


def _cumsum_time(lf_t):
    B, H, L = lf_t.shape
    blk = LANES
    assert L % blk == 0
    spec = pl.BlockSpec((1, H, L), lambda b: (b, 0, 0))
    return pl.pallas_call(
        functools.partial(_cumsum_kernel, blk=blk),
        grid=(B,),
        in_specs=[spec],
        out_specs=spec,
        out_shape=jax.ShapeDtypeStruct((B, H, L), F32),
        compiler_params=_cparams(("parallel",)),
        name="logf_cumsum",
    )(lf_t)


def _head_column(blk, h):
    lane = lax.broadcasted_iota(jnp.int32, blk.shape, 1)
    return jnp.sum(jnp.where(lane == h, blk, 0.0), axis=1, keepdims=True)


def _positions(tq, tk, q0, k0):
    qpos = q0 + lax.broadcasted_iota(jnp.int32, (tq, tk), 0)
    kpos = k0 + lax.broadcasted_iota(jnp.int32, (tq, tk), 1)
    return qpos, kpos


def _pair_specs(tq, Lp):
    qspec = pl.BlockSpec((1, tq, LANES), lambda b, hp, qi: (b, qi, hp))
    kspec = pl.BlockSpec((1, Lp, LANES), lambda b, hp, qi: (b, 0, hp))
    return qspec, kspec


def _fox_kernel(q_ref, k_ref, v_ref, fq_ref, fk_ref, o_ref, *, tq, tk, q_off):
    hp = pl.program_id(1)
    q0 = q_off + pl.program_id(2) * tq
    diag = q0 // tk
    qp = q_ref[0]
    fq_blk = fq_ref[0]
    heads = []
    for hh in range(2):
        h = hp * 2 + hh
        qm = _keep_head(qp, hh)
        fq = _head_column(fq_blk, h)

        def tile(j, carry, masked):
            m, l, acc = carry
            k0 = pl.multiple_of(j * tk, tk)
            kt = k_ref[0, pl.ds(k0, tk), :]
            vt = v_ref[0, pl.ds(k0, tk), :]
            fk = fk_ref[0, h, :, pl.ds(k0, tk)]
            s = _dot_t(qm, kt) + fq - fk
            if masked:
                qpos, kpos = _positions(tq, tk, q0, k0)
                s = jnp.where(kpos <= qpos, s, NEG)
            m_new = jnp.maximum(m, jnp.max(s, axis=1, keepdims=True))
            alpha = jnp.exp(m - m_new)
            p = jnp.exp(s - m_new)
            l = alpha * l + jnp.sum(p, axis=1, keepdims=True)
            acc = alpha * acc + _dot(p.astype(BF16), vt)
            return m_new, l, acc

        init = (jnp.full((tq, 1), NEG, F32), jnp.zeros((tq, 1), F32), jnp.zeros((tq, LANES), F32))
        carry = lax.fori_loop(0, diag, lambda j, c: tile(j, c, False), init)
        _, l, acc = tile(diag, carry, True)
        heads.append(acc / l)
    o_ref[0] = _merge_pair(heads[0], heads[1]).astype(o_ref.dtype)


def _fox(q, k, v, fq, fk_t, *, tq, tk, q_off):
    B, T, _ = q.shape
    Lp = k.shape[1]
    qspec, kspec = _pair_specs(tq, Lp)
    return pl.pallas_call(
        functools.partial(_fox_kernel, tq=tq, tk=tk, q_off=q_off),
        grid=(B, N_PAIRS, T // tq),
        in_specs=[qspec, kspec, kspec,
                  pl.BlockSpec((1, tq, N_HEADS), lambda b, hp, qi: (b, qi, 0)),
                  pl.BlockSpec((1, N_HEADS, 1, Lp), lambda b, hp, qi: (b, 0, 0, 0))],
        out_specs=qspec,
        out_shape=jax.ShapeDtypeStruct(q.shape, BF16),
        compiler_params=_cparams(("parallel", "parallel", "parallel")),
        name="fox_attn",
    )(q, k, v, fq, fk_t[:, :, None, :])


def _sb_kernel(q_ref, k_ref, v_ref, o_ref, *, tq, tk, q_off):
    q0 = q_off + pl.program_id(2) * tq
    diag = q0 // tk
    qp = q_ref[0]
    r = lax.broadcasted_iota(jnp.int32, (tk, tk), 0)
    c = lax.broadcasted_iota(jnp.int32, (tk, tk), 1)
    later = jnp.where(r > c, 1.0, 0.0).astype(BF16)
    heads = []
    for hh in range(2):
        qm = _keep_head(qp, hh)

        def tile(j, carry, masked):
            tail, acc = carry
            k0 = pl.multiple_of(j * tk, tk)
            kt = k_ref[0, pl.ds(k0, tk), :]
            vt = v_ref[0, pl.ds(k0, tk), :]
            z = _dot_t(qm, kt)
            ls = _log_sigmoid(z)
            la = ls - z
            if masked:
                qpos, kpos = _positions(tq, tk, q0, k0)
                strict = kpos < qpos
                la = jnp.where(strict, la, 0.0)
            hi = la.astype(BF16)
            lo = (la - hi.astype(F32)).astype(BF16)
            after = _dot(hi, later) + _dot(lo, later) + tail
            a = jnp.exp(ls + after)
            if masked:
                a = jnp.where(strict, a, 0.0)
            acc = acc + _dot(a.astype(BF16), vt)
            tail = tail + jnp.sum(la, axis=1, keepdims=True)
            return tail, acc

        carry = tile(diag, (jnp.zeros((tq, 1), F32), jnp.zeros((tq, LANES), F32)), True)
        _, acc = lax.fori_loop(0, diag, lambda i, cr: tile(diag - 1 - i, cr, False), carry)
        heads.append(acc)
    o_ref[0] = _merge_pair(heads[0], heads[1]).astype(o_ref.dtype)


def _sb(q, k, v, *, tq, tk, q_off):
    B, T, _ = q.shape
    qspec, kspec = _pair_specs(tq, k.shape[1])
    return pl.pallas_call(
        functools.partial(_sb_kernel, tq=tq, tk=tk, q_off=q_off),
        grid=(B, N_PAIRS, T // tq),
        in_specs=[qspec, kspec, kspec],
        out_specs=qspec,
        out_shape=jax.ShapeDtypeStruct(q.shape, BF16),
        compiler_params=_cparams(("parallel", "parallel", "parallel")),
        name="sb_attn",
    )(q, k, v)


def _dsa_kernel(aq_ref, iq_ref, iw_ref, k_ref, v_ref, ik_ref, o_ref, s_ref, q8_ref, m_ref, l_ref, acc_ref,
                *, tq, q_off, topk):
    tk = KEY_TILE
    q0 = q_off + pl.program_id(1) * tq
    nkt = (q0 + tq + tk - 1) // tk

    def tile_start(j):
        return pl.multiple_of(j * tk, tk)

    def over_tiles(fn, init):
        return lax.fori_loop(0, nkt, lambda j, cr: fn(s_ref[:, pl.ds(tile_start(j), tk)], cr), init)

    iq = iq_ref[0]
    iw = iw_ref[0]
    qh = [_keep_head(iq[:, (h // 2) * LANES:(h // 2 + 1) * LANES], h % 2) for h in range(N_IDX)]
    row_chunk = jnp.right_shift(q0 + lax.broadcasted_iota(jnp.int32, (tq, tk), 0), CHUNK_SHIFT)

    def score_tile(j, _):
        k0 = tile_start(j)
        ik = ik_ref[0, pl.ds(k0, tk), :]
        sc = jnp.zeros((tq, tk), F32)
        for h in range(N_IDX):
            sc = sc + jnp.maximum(_dot_t(qh[h], ik), 0.0) * iw[:, h:h + 1]
        key_chunk = jnp.right_shift(k0 + lax.broadcasted_iota(jnp.int32, (tq, tk), 1), CHUNK_SHIFT)
        s_ref[:, pl.ds(k0, tk)] = jnp.where(key_chunk <= row_chunk, sc, NEG)
        return 0

    lax.fori_loop(0, nkt, score_tile, 0)

    def search():
        def minmax(s, cr):
            return jnp.minimum(cr[0], s), jnp.maximum(cr[1], s)

        mn, mx = over_tiles(minmax, (jnp.full((tq, tk), jnp.inf, F32), jnp.full((tq, tk), -jnp.inf, F32)))
        lo0 = jnp.min(mn, axis=1, keepdims=True)
        hi0 = jnp.max(mx, axis=1, keepdims=True)

        def unresolved(cr):
            return jnp.max(jnp.where(cr[0] < cr[1], 1.0, 0.0)) > 0.0

        def step(cr):
            lo, hi = cr
            mid = lo + (hi - lo) * 0.5
            mid = jnp.where(mid >= hi, lo, mid)
            midb = jnp.broadcast_to(mid, (tq, tk))

            def stats(s, st):
                gt = s > midb
                return (st[0] + jnp.where(gt, 1.0, 0.0),
                        jnp.maximum(st[1], jnp.where(gt, -jnp.inf, s)),
                        jnp.minimum(st[2], jnp.where(gt, s, jnp.inf)))

            cnt, below, above = over_tiles(stats, (jnp.zeros((tq, tk), F32), jnp.full((tq, tk), -jnp.inf, F32),
                                                   jnp.full((tq, tk), jnp.inf, F32)))
            enough = jnp.sum(cnt, axis=1, keepdims=True) >= topk
            below = jnp.max(below, axis=1, keepdims=True)
            above = jnp.min(above, axis=1, keepdims=True)
            return jnp.where(enough, above, lo), jnp.where(enough, hi, below)

        return lax.while_loop(unresolved, step, (lo0, hi0))[0]

    thr = lax.cond(q0 + tq > topk, search, lambda: jnp.full((tq, 1), NEG, F32))
    thrb = jnp.broadcast_to(thr, (tq, tk))

    n_above = jnp.sum(over_tiles(lambda s, cnt: cnt + jnp.where(s > thrb, 1.0, 0.0), jnp.zeros((tq, tk), F32)),
                      axis=1, keepdims=True)
    room = topk - n_above
    r = lax.broadcasted_iota(jnp.int32, (tk, tk), 0)
    c = lax.broadcasted_iota(jnp.int32, (tk, tk), 1)
    earlier = jnp.where(r < c, 1.0, 0.0).astype(BF16)

    def select_tile(j, seen):
        k0 = tile_start(j)
        s = s_ref[:, pl.ds(k0, tk)]
        eq = jnp.where(s == thrb, 1.0, 0.0)
        rank = _dot(eq.astype(BF16), earlier) + seen
        tie_bias = jnp.where(rank < room, 0.0, NEG)
        bias = jnp.where(s > thrb, 0.0, jnp.where(s == thrb, tie_bias, NEG))
        s_ref[:, pl.ds(k0, tk)] = jnp.where(s > 0.5 * NEG, bias, NEG)
        return seen + jnp.sum(eq, axis=1, keepdims=True)

    lax.fori_loop(0, nkt, select_tile, jnp.zeros((tq, 1), F32))

    aq = aq_ref[0]
    for h in range(N_HEADS):
        q8_ref[h * tq:(h + 1) * tq, :] = _keep_head(aq[:, (h // 2) * LANES:(h // 2 + 1) * LANES], h % 2)
    m_ref[...] = jnp.full(m_ref.shape, NEG, F32)
    l_ref[...] = jnp.zeros(l_ref.shape, F32)
    acc_ref[...] = jnp.zeros(acc_ref.shape, F32)

    def attend_tile(j, _):
        k0 = tile_start(j)
        kt = k_ref[0, pl.ds(k0, tk), :]
        vt = v_ref[0, pl.ds(k0, tk), :]
        bias = s_ref[:, pl.ds(k0, tk)]
        s8 = _dot_t(q8_ref[...], kt).reshape(N_HEADS, tq, tk) + bias[None]
        m_old = m_ref[...]
        m_new = jnp.maximum(m_old, jnp.max(s8, axis=2, keepdims=True))
        alpha = jnp.exp(m_old - m_new)
        p = jnp.exp(s8 - m_new)
        l_ref[...] = alpha * l_ref[...] + jnp.sum(p, axis=2, keepdims=True)
        pv = _dot(p.astype(BF16).reshape(N_HEADS * tq, tk), vt).reshape(N_HEADS, tq, LANES)
        acc_ref[...] = alpha * acc_ref[...] + pv
        m_ref[...] = m_new
        return 0

    lax.fori_loop(0, nkt, attend_tile, 0)
    for hp in range(N_PAIRS):
        o0 = acc_ref[2 * hp] / l_ref[2 * hp]
        o1 = acc_ref[2 * hp + 1] / l_ref[2 * hp + 1]
        o_ref[0, :, hp * LANES:(hp + 1) * LANES] = _merge_pair(o0, o1).astype(o_ref.dtype)


def _dsa(aq, iq, iw, k2, v2, ik2, *, tq, q_off, topk):
    B, T, _ = aq.shape
    Lp = k2.shape[1]
    qspec = lambda w: pl.BlockSpec((1, tq, w), lambda b, qi: (b, qi, 0))
    kspec = pl.BlockSpec((1, Lp, LANES), lambda b, qi: (b, 0, 0))
    return pl.pallas_call(
        functools.partial(_dsa_kernel, tq=tq, q_off=q_off, topk=topk),
        grid=(B, T // tq),
        in_specs=[qspec(BRANCH_W), qspec(N_IDX * IDX_DIM), qspec(SUBLANES), kspec, kspec, kspec],
        out_specs=qspec(BRANCH_W),
        out_shape=jax.ShapeDtypeStruct(aq.shape, BF16),
        scratch_shapes=[pltpu.VMEM((tq, Lp), F32), pltpu.VMEM((N_HEADS * tq, LANES), BF16),
                        pltpu.VMEM((N_HEADS, tq, 1), F32), pltpu.VMEM((N_HEADS, tq, 1), F32),
                        pltpu.VMEM((N_HEADS, tq, LANES), F32)],
        compiler_params=_cparams(("parallel", "parallel")),
        name="dsa_attn",
    )(aq, iq, iw, k2, v2, ik2)


def _merge_kernel(x_ref, oa_ref, ob_ref, oc_ref, g_ref, wg_ref, wb_ref, wo_ref, y_ref):
    x = x_ref[0]
    D = x.shape[-1]
    xn = _rms(x, g_ref[...]).astype(BF16)
    merged = jnp.zeros(x.shape, F32)
    for n, o_ref in enumerate((oa_ref, ob_ref, oc_ref)):
        gate = _sigmoid(_dot(xn, wg_ref[:, n * D:(n + 1) * D]))
        merged = merged + gate * _dot(o_ref[0], wb_ref[n])
    y_ref[0] = x + _dot(merged.astype(BF16), wo_ref[...])


def _merge(x, oa, ob, oc, g, wg, wb, wo, tm):
    B, T, D = x.shape
    tspec = lambda w_: pl.BlockSpec((1, tm, w_), lambda b, t: (b, t, 0))
    const = lambda a: pl.BlockSpec(a.shape, lambda b, t: (0,) * a.ndim)
    return pl.pallas_call(
        _merge_kernel,
        grid=(B, T // tm),
        in_specs=[tspec(D), tspec(BRANCH_W), tspec(BRANCH_W), tspec(BRANCH_W), const(g), const(wg), const(wb),
                  const(wo)],
        out_specs=tspec(D),
        out_shape=jax.ShapeDtypeStruct(x.shape, F32),
        compiler_params=_cparams(("parallel", "parallel")),
        name="merge_out",
    )(x, oa, ob, oc, g, wg, wb, wo)


def _ffn_kernel(x_ref, g_ref, past_ref, wu_ref, cw_ref, cb_ref, wd_ref, gf_ref, y_ref, st_ref,
                hbuf_ref, carry_ref, acc_ref, *, tm, final_norm):
    fc = FF_CHUNK
    n_chunks = wu_ref.shape[0]

    @pl.when(pl.program_id(1) == 0)
    def _():
        carry_ref[...] = past_ref[0]

    x = x_ref[0]
    xn = _rms(x, g_ref[...]).astype(BF16)
    acc_ref[...] = jnp.zeros(acc_ref.shape, F32)

    def chunk(c, _):
        h = _dot(xn, wu_ref[c])
        hbuf_ref[0:SUBLANES, :] = carry_ref[c]
        hbuf_ref[SUBLANES:SUBLANES + tm, :] = h
        carry_ref[c] = h[tm - SUBLANES:tm, :]
        cw = cw_ref[c]
        hc = (cb_ref[c] + cw[0:1] * hbuf_ref[SUBLANES - 2:SUBLANES - 2 + tm, :]
              + cw[1:2] * hbuf_ref[SUBLANES - 1:SUBLANES - 1 + tm, :] + cw[2:3] * h)
        gate, up = hc[:, :fc], hc[:, fc:]
        act = gate * _sigmoid(gate) * up
        acc_ref[...] += _dot(act.astype(BF16), wd_ref[c])
        return 0

    lax.fori_loop(0, n_chunks, chunk, 0)
    st_ref[0] = carry_ref[...]
    y = x + acc_ref[...]
    if final_norm:
        y = _rms(y, gf_ref[...])
    y_ref[0] = y


def _ffn(x, g, past, wu, cw, cb, wd, gf, tm, final_norm):
    B, T, D = x.shape
    nc, _, fc2 = wu.shape
    tspec = pl.BlockSpec((1, tm, D), lambda b, t: (b, t, 0))
    const = lambda a: pl.BlockSpec(a.shape, lambda b, t: (0,) * a.ndim)
    sspec = pl.BlockSpec((1, nc, SUBLANES, fc2), lambda b, t: (b, 0, 0, 0))
    return pl.pallas_call(
        functools.partial(_ffn_kernel, tm=tm, final_norm=final_norm),
        grid=(B, T // tm),
        in_specs=[tspec, const(g), sspec, const(wu), const(cw), const(cb), const(wd), const(gf)],
        out_specs=[tspec, sspec],
        out_shape=[jax.ShapeDtypeStruct(x.shape, F32), jax.ShapeDtypeStruct((B, nc, SUBLANES, fc2), F32)],
        scratch_shapes=[pltpu.VMEM((tm + SUBLANES, fc2), F32), pltpu.VMEM((nc, SUBLANES, fc2), F32),
                        pltpu.VMEM((tm, D), F32)],
        compiler_params=_cparams(("arbitrary", "arbitrary")),
        name="conv_ffn",
    )(x, g, past, wu, cw, cb, wd, gf)


def _rope_tables(pos):
    half = ROT_DIM // 2
    freq = ROPE_THETA ** (-jnp.arange(half, dtype=F32) / half)
    ang = pos.astype(F32)[:, None] * freq[None, :]
    lane = np.arange(LANES) % HEAD_DIM
    cos = jnp.cos(ang)[:, lane % half]
    sin = jnp.sin(ang)[:, lane % half]
    rc = jnp.where(lane < ROT_DIM, cos, 1.0)
    rs1 = jnp.where(lane < half, -sin, 0.0)
    rs2 = jnp.where((lane >= half) & (lane < ROT_DIM), sin, 0.0)
    return rc, rs1, rs2


def _pack_layer(l, norm_mix, w_in, b_f, w_branch, w_gate, w_out, norm_ffn, w_up, conv_w, conv_b, w_down):
    D = w_in.shape[1]
    sizes = (BRANCH_W, HEAD_DIM, HEAD_DIM, N_IDX * IDX_DIM, IDX_DIM, N_IDX) + (BRANCH_W,) * 6 + (N_HEADS,)
    offs = np.concatenate([[0], np.cumsum(sizes)])
    (a_q, a_k, a_v, i_q, i_k, i_w, b_q, b_k, b_v, c_q, c_k, c_v, c_f) = (
        w_in[l][:, offs[i]:offs[i + 1]] for i in range(len(sizes)))
    zeros = lambda n: jnp.zeros((D, n), w_in.dtype)
    w = jnp.concatenate([a_q, i_q, a_k, a_k, i_k, i_k, a_v, a_v,
                         i_w, zeros(CF_LANE - N_IDX), c_f, zeros(LANES - CF_LANE - N_HEADS),
                         b_q, b_k, b_v, c_q, c_k, c_v], axis=1).astype(BF16)
    bf_row = jnp.zeros((1, LANES), F32).at[0, CF_LANE:CF_LANE + N_HEADS].set(b_f[l])

    d_ff = w_down.shape[1]
    fc = FF_CHUNK
    nc = d_ff // fc
    assert nc * fc == d_ff
    halves = lambda a: jnp.concatenate([a[..., :d_ff].reshape(a.shape[:-1] + (nc, fc)),
                                        a[..., d_ff:].reshape(a.shape[:-1] + (nc, fc))], axis=-1)
    wu = jnp.moveaxis(halves(w_up[l]), 1, 0).astype(BF16)
    cw = jnp.moveaxis(halves(conv_w[l]), 1, 0)
    cb = halves(conv_b[l])[:, None, :]
    wd = w_down[l].reshape(nc, fc, D).astype(BF16)
    return dict(g_mix=norm_mix[l][None], w=w, bf_row=bf_row, wg=w_gate[l].astype(BF16),
                wb=w_branch[l].astype(BF16), wo=w_out[l].astype(BF16), g_ffn=norm_ffn[l][None],
                wu=wu, cw=cw, cb=cb, wd=wd, halves=halves, nc=nc, d_ff=d_ff)


def _conv_state_in(state, halves):
    st = jnp.moveaxis(halves(state), 2, 1)
    return jnp.pad(st, ((0, 0), (0, 0), (SUBLANES - (CONV_W - 1), 0), (0, 0)))


def _conv_state_out(st, d_ff):
    st = st[:, :, SUBLANES - (CONV_W - 1):, :]
    B, nc, r, fc2 = st.shape
    fc = fc2 // 2
    gate = jnp.moveaxis(st[..., :fc], 1, 2).reshape(B, r, d_ff)
    up = jnp.moveaxis(st[..., fc:], 1, 2).reshape(B, r, d_ff)
    return jnp.concatenate([gate, up], axis=-1)


def _pad_keys(a, Lp):
    return jnp.pad(a, ((0, 0), (0, Lp - a.shape[1]), (0, 0)))


def _group_layer(x, past, lw, gf, final_norm, tiles):
    B, T, D = x.shape
    P = 0 if past is None else past[0].shape[1]
    L = P + T
    Lp = -(-L // KEY_TILE) * KEY_TILE
    topk = max(1, min(TOPK_MAX, L // 4))
    tables = _rope_tables(jnp.arange(P, L))

    (ak, av, ik, bk, bv, ck, cv, lf, iw,
     aq16, iq16, ak16, av16, ik16, bq16, bk16, bv16, cq16, ck16, cv16) = _project(
         x, lw["g_mix"], lw["w"], lw["bf_row"], tables, tiles["proj"])

    if past is None:
        keys = (ak16, av16, ik16, bk16, bv16, ck16, cv16)
        lf_all = lf
        conv_in = jnp.zeros((B, lw["nc"], SUBLANES, 2 * FF_CHUNK), F32)
    else:
        p_ak, p_av, p_ik, p_bk, p_bv, p_ck, p_cv, p_lf, p_conv = past
        dup = lambda a: jnp.concatenate([a, a], axis=-1).astype(BF16)
        flat = lambda a: a.reshape(B, P, BRANCH_W).astype(BF16)
        olds = (dup(p_ak), dup(p_av), dup(p_ik), flat(p_bk), flat(p_bv), flat(p_ck), flat(p_cv))
        news = (ak16, av16, ik16, bk16, bv16, ck16, cv16)
        keys = tuple(jnp.concatenate([o, n], axis=1) for o, n in zip(olds, news))
        lf_all = jnp.concatenate([p_lf, lf], axis=1)
        conv_in = _conv_state_in(p_conv, lw["halves"])
    ak2, av2, ik2, bkk, bvv, ckk, cvv = (_pad_keys(a, Lp) for a in keys)

    fk_t = _cumsum_time(jnp.swapaxes(_pad_keys(lf_all, Lp), 1, 2))
    fq = jnp.swapaxes(fk_t[:, :, P:L], 1, 2)

    o_a = _dsa(aq16, iq16, iw, ak2, av2, ik2, tq=tiles["dsa"], q_off=P, topk=topk)
    o_b = _sb(bq16, bkk, bvv, tq=tiles["sb"], tk=max(tiles["sb"], KEY_TILE), q_off=P)
    o_c = _fox(cq16, ckk, cvv, fq, fk_t, tq=tiles["fox"], tk=max(tiles["fox"], KEY_TILE), q_off=P)

    x = _merge(x, o_a, o_b, o_c, lw["g_mix"], lw["wg"], lw["wb"], lw["wo"], tiles["merge"])
    x, st = _ffn(x, lw["g_ffn"], conv_in, lw["wu"], lw["cw"], lw["cb"], lw["wd"], gf, tiles["ffn"], final_norm)

    heads = lambda a: a.reshape(B, T, N_HEADS, HEAD_DIM)
    rows = (ak, av, ik, heads(bk), heads(bv), heads(ck), heads(cv), lf, _conv_state_out(st, lw["d_ff"]))
    return x, rows


def _tiles(T):
    pick = lambda want: min(want, T)
    return dict(proj=pick(256), dsa=pick(128), sb=pick(128), fox=pick(256), merge=pick(512), ffn=pick(256))


def kernel(x_prompt, x_sample, cache_a_k, cache_a_v, cache_a_idx_k, cache_b_k, cache_b_v, cache_c_k, cache_c_v,
           cache_c_logf, state_ffn_conv, norm_mix, w_in, b_f, w_branch, w_gate, w_out, norm_ffn, w_up, conv_w,
           conv_b, w_down, norm_final):
    depth = w_in.shape[0]
    caches = (cache_a_k, cache_a_v, cache_a_idx_k, cache_b_k, cache_b_v, cache_c_k, cache_c_v, cache_c_logf,
              state_ffn_conv)
    layers = [_pack_layer(l, norm_mix, w_in, b_f, w_branch, w_gate, w_out, norm_ffn, w_up, conv_w, conv_b, w_down)
              for l in range(depth)]
    gf = norm_final[None]

    def trunk(x, past):
        tiles = _tiles(x.shape[1])
        per_layer = []
        for l in range(depth):
            lp = None if past is None else tuple(c[l] for c in past)
            x, rows = _group_layer(x, lp, layers[l], gf, l == depth - 1, tiles)
            per_layer.append(rows)
        return x, tuple(jnp.stack(s, axis=0) for s in zip(*per_layer))

    y_prompt, p_state = trunk(x_prompt, None)
    y_sample, s_state = trunk(x_sample, caches)
    return (y_prompt, y_sample) + p_state + s_state
```

```python
import functools

import jax
import jax.numpy as jnp
import numpy as np
from jax import lax
from jax.experimental import pallas as pl
from jax.experimental.pallas import tpu as pltpu

F32 = jnp.float32
BF16 = jnp.bfloat16

LANES = 128
SUBLANES = 8
HEAD_DIM = 64
N_HEADS = 8
N_PAIRS = N_HEADS // 2
BRANCH_W = N_HEADS * HEAD_DIM
N_BRANCH = 3
N_IDX = 4
IDX_DIM = 64
CHUNK = 64
CHUNK_SHIFT = 6
TOPK_MAX = 256
ROT_DIM = HEAD_DIM // 4
ROPE_THETA = 500000.0
CONV_W = 3
EPS = 1e-6
NEG = -1e30
KEY_TILE = 128
FF_CHUNK = 256
VMEM_LIMIT = 56 * 1024 * 1024

ROPE_W = 1024
MISC_W = 256
CF_LANE = 8


def _cparams(sem):
    return pltpu.CompilerParams(dimension_semantics=sem, vmem_limit_bytes=VMEM_LIMIT)


def _rms(x, g):
    return x * lax.rsqrt(jnp.mean(x * x, axis=-1, keepdims=True) + EPS) * g


def _sigmoid(z):
    return 1.0 / (1.0 + jnp.exp(-z))


def _log_sigmoid(z):
    return jnp.minimum(z, 0.0) - jnp.log1p(jnp.exp(-jnp.abs(z)))


def _dot_t(a, b):
    return lax.dot_general(a, b, (((1,), (1,)), ((), ())), preferred_element_type=F32)


def _dot(a, b):
    return jnp.dot(a, b, preferred_element_type=F32)


def _keep_head(blk, which):
    lane = lax.broadcasted_iota(jnp.int32, blk.shape, blk.ndim - 1)
    keep = (lane >= HEAD_DIM) if which else (lane < HEAD_DIM)
    return jnp.where(keep, blk, jnp.zeros_like(blk))


def _merge_pair(o0, o1):
    lane = lax.broadcasted_iota(jnp.int32, o0.shape, o0.ndim - 1)
    return jnp.where(lane < HEAD_DIM, o0, o1)


def _proj_kernel(x_ref, g_ref, w_ref, bf_ref, rc_ref, rs1_ref, rs2_ref,
                 ak_ref, av_ref, ik_ref, bk_ref, bv_ref, ck_ref, cv_ref, lf_ref, iw_ref,
                 aq16, iq16, ak16, av16, ik16, bq16, bk16, bv16, cq16, ck16, cv16):
    xn = _rms(x_ref[0], g_ref[...]).astype(BF16)
    rc, rs1, rs2 = rc_ref[...], rs1_ref[...], rs2_ref[...]

    def cols(c0):
        return _dot(xn, w_ref[:, c0:c0 + LANES])

    def rope(blk):
        half = ROT_DIM // 2
        return blk * rc + pltpu.roll(blk, LANES - half, 1) * rs1 + pltpu.roll(blk, half, 1) * rs2

    for g in range(4):
        aq16[0, :, g * LANES:(g + 1) * LANES] = (rope(cols(g * LANES)) * HEAD_DIM ** -0.5).astype(BF16)
    for g in range(2):
        iq16[0, :, g * LANES:(g + 1) * LANES] = (rope(cols(512 + g * LANES)) * IDX_DIM ** -0.5).astype(BF16)
    blk = rope(cols(768))
    ak_ref[0] = blk[:, :HEAD_DIM]
    ak16[0] = blk.astype(BF16)
    blk = rope(cols(896))
    ik_ref[0] = blk[:, :IDX_DIM]
    ik16[0] = blk.astype(BF16)

    blk = cols(ROPE_W)
    av_ref[0] = blk[:, :HEAD_DIM]
    av16[0] = blk.astype(BF16)
    blk = cols(ROPE_W + LANES)
    iw_ref[0] = blk[:, :SUBLANES] * N_IDX ** -0.5
    lf_ref[0] = _log_sigmoid(blk[:, CF_LANE:CF_LANE + N_HEADS] + bf_ref[:, CF_LANE:CF_LANE + N_HEADS])

    base = ROPE_W + MISC_W
    outs = ((None, bq16), (bk_ref, bk16), (bv_ref, bv16), (None, cq16), (ck_ref, ck16), (cv_ref, cv16))
    for n, (o32, o16) in enumerate(outs):
        for g in range(BRANCH_W // LANES):
            blk = cols(base + n * BRANCH_W + g * LANES)
            sl = slice(g * LANES, (g + 1) * LANES)
            if o32 is None:
                o16[0, :, sl] = (blk * HEAD_DIM ** -0.5).astype(BF16)
            else:
                o32[0, :, sl] = blk
                o16[0, :, sl] = blk.astype(BF16)


def _project(x, g, w, bf_row, tables, tm):
    B, T, D = x.shape
    tok = lambda w_, dt: jax.ShapeDtypeStruct((B, T, w_), dt)
    tspec = lambda w_: pl.BlockSpec((1, tm, w_), lambda b, t: (b, t, 0))
    const = lambda a: pl.BlockSpec(a.shape, lambda b, t: (0,) * a.ndim)
    out_shapes = ([tok(64, F32)] * 3 + [tok(512, F32)] * 4 + [tok(8, F32)] * 2
                  + [tok(512, BF16), tok(256, BF16)] + [tok(128, BF16)] * 3 + [tok(512, BF16)] * 6)
    rspec = pl.BlockSpec((tm, LANES), lambda b, t: (t, 0))
    return pl.pallas_call(
        _proj_kernel,
        grid=(B, T // tm),
        in_specs=[tspec(D), const(g), const(w), const(bf_row), rspec, rspec, rspec],
        out_specs=[tspec(s.shape[-1]) for s in out_shapes],
        out_shape=out_shapes,
        compiler_params=_cparams(("parallel", "parallel")),
        name="in_proj",
    )(x, g, w, bf_row, *tables)


def _split3(x):
    hi = x.astype(BF16)
    r = x - hi.astype(F32)
    mid = r.astype(BF16)
    lo = (r - mid.astype(F32)).astype(BF16)
    return hi, mid, lo


def _cumsum_kernel(lf_ref, o_ref, *, blk):
    L = lf_ref.shape[-1]
    r = lax.broadcasted_iota(jnp.int32, (blk, blk), 0)
    c = lax.broadcasted_iota(jnp.int32, (blk, blk), 1)
    upper = jnp.where(r <= c, 1.0, 0.0).astype(BF16)
    carry = jnp.zeros((N_HEADS, 1), F32)
    for j in range(L // blk):
        hi, mid, lo = _split3(lf_ref[0, :, j * blk:(j + 1) * blk])
        f = _dot(hi, upper) + _dot(mid, upper) + _dot(lo, upper) + carry
        o_ref[0, :, j * blk:(j + 1) * blk] = f
        carry = f[:, blk - 1:blk]


def _cumsum_time(lf_t):
    B, H, L = lf_t.shape
    blk = LANES
    assert L % blk == 0
    spec = pl.BlockSpec((1, H, L), lambda b: (b, 0, 0))
    return pl.pallas_call(
        functools.partial(_cumsum_kernel, blk=blk),
        grid=(B,),
        in_specs=[spec],
        out_specs=spec,
        out_shape=jax.ShapeDtypeStruct((B, H, L), F32),
        compiler_params=_cparams(("parallel",)),
        name="logf_cumsum",
    )(lf_t)


def _head_column(blk, h):
    lane = lax.broadcasted_iota(jnp.int32, blk.shape, 1)
    return jnp.sum(jnp.where(lane == h, blk, 0.0), axis=1, keepdims=True)


def _positions(tq, tk, q0, k0):
    qpos = q0 + lax.broadcasted_iota(jnp.int32, (tq, tk), 0)
    kpos = k0 + lax.broadcasted_iota(jnp.int32, (tq, tk), 1)
    return qpos, kpos


def _pair_specs(tq, Lp):
    qspec = pl.BlockSpec((1, tq, LANES), lambda b, hp, qi: (b, qi, hp))
    kspec = pl.BlockSpec((1, Lp, LANES), lambda b, hp, qi: (b, 0, hp))
    return qspec, kspec


def _fox_kernel(q_ref, k_ref, v_ref, fq_ref, fk_ref, o_ref, m_ref, acc_ref, *, tq, tk, q_off):
    hp = pl.program_id(1)
    q0 = q_off + pl.program_id(2) * tq
    diag = q0 // tk
    qp = q_ref[0]
    fq_blk = fq_ref[0]
    qm = (_keep_head(qp, 0), _keep_head(qp, 1))
    fq = (_head_column(fq_blk, hp * 2), _head_column(fq_blk, hp * 2 + 1))
    ones = jnp.ones((tk, LANES), BF16)
    m_ref[...] = jnp.full(m_ref.shape, NEG, F32)
    acc_ref[...] = jnp.zeros(acc_ref.shape, F32)

    def tile(j, masked):
        k0 = pl.multiple_of(j * tk, tk)
        kt = k_ref[0, pl.ds(k0, tk), :]
        v1 = jnp.concatenate([v_ref[0, pl.ds(k0, tk), :], ones], axis=1)
        if masked:
            qpos, kpos = _positions(tq, tk, q0, k0)
            causal = kpos <= qpos
        for hh in range(2):
            fk = fk_ref[0, hp * 2 + hh, :, pl.ds(k0, tk)]
            s = _dot_t(qm[hh], kt) + fq[hh] - fk
            if masked:
                s = jnp.where(causal, s, NEG)
            m_old = m_ref[hh]
            m_new = jnp.maximum(m_old, jnp.max(s, axis=1, keepdims=True))
            p = jnp.exp(s - m_new).astype(BF16)
            acc_ref[hh] = jnp.exp(m_old - m_new) * acc_ref[hh] + _dot(p, v1)
            m_ref[hh] = m_new

    def body(j, _):
        tile(j, False)
        return 0

    lax.fori_loop(0, diag, body, 0)
    tile(diag, True)
    o0 = acc_ref[0, :, :LANES] / acc_ref[0, :, LANES:]
    o1 = acc_ref[1, :, :LANES] / acc_ref[1, :, LANES:]
    o_ref[0] = _merge_pair(o0, o1).astype(o_ref.dtype)


def _fox(q, k, v, fq, fk_t, *, tq, tk, q_off):
    B, T, _ = q.shape
    Lp = k.shape[1]
    qspec, kspec = _pair_specs(tq, Lp)
    return pl.pallas_call(
        functools.partial(_fox_kernel, tq=tq, tk=tk, q_off=q_off),
        grid=(B, N_PAIRS, T // tq),
        in_specs=[qspec, kspec, kspec,
                  pl.BlockSpec((1, tq, N_HEADS), lambda b, hp, qi: (b, qi, 0)),
                  pl.BlockSpec((1, N_HEADS, 1, Lp), lambda b, hp, qi: (b, 0, 0, 0))],
        out_specs=qspec,
        out_shape=jax.ShapeDtypeStruct(q.shape, BF16),
        scratch_shapes=[pltpu.VMEM((2, tq, 1), F32), pltpu.VMEM((2, tq, 2 * LANES), F32)],
        compiler_params=_cparams(("parallel", "parallel", "parallel")),
        name="fox_attn",
    )(q, k, v, fq, fk_t[:, :, None, :])


def _sb_kernel(q_ref, k_ref, v_ref, o_ref, acc_ref, tail_ref, *, tq, tk, q_off):
    q0 = q_off + pl.program_id(2) * tq
    diag = q0 // tk
    qp = q_ref[0]
    qm = (_keep_head(qp, 0), _keep_head(qp, 1))
    r = jnp.bitwise_and(lax.broadcasted_iota(jnp.int32, (2 * LANES, 2 * LANES), 0), LANES - 1)
    c = lax.broadcasted_iota(jnp.int32, (2 * LANES, 2 * LANES), 1)
    suffix_w = jnp.where((c >= LANES) | (r > c), 1.0, 0.0).astype(BF16)
    acc_ref[...] = jnp.zeros(acc_ref.shape, F32)
    tail_ref[...] = jnp.zeros(tail_ref.shape, F32)

    def tile(j, masked):
        k0 = pl.multiple_of(j * tk, tk)
        kt = k_ref[0, pl.ds(k0, tk), :]
        vt = v_ref[0, pl.ds(k0, tk), :]
        if masked:
            qpos, kpos = _positions(tq, tk, q0, k0)
            strict = kpos < qpos
        for hh in range(2):
            z = _dot_t(qm[hh], kt)
            ls = jnp.minimum(z, 0.0) - jnp.log(1.0 + jnp.exp(-jnp.abs(z)))
            la = ls - z
            if masked:
                la = jnp.where(strict, la, 0.0)
            hi = la.astype(BF16)
            lo = (la - hi.astype(F32)).astype(BF16)
            run = tail_ref[hh]
            after = [None] * (tk // LANES)
            for b in reversed(range(tk // LANES)):
                sl = slice(b * LANES, (b + 1) * LANES)
                res = _dot(jnp.concatenate([hi[:, sl], lo[:, sl]], axis=1), suffix_w)
                after[b] = res[:, :LANES] + run
                run = run + res[:, LANES:]
            tail_ref[hh] = run
            a = jnp.exp(ls + jnp.concatenate(after, axis=1))
            if masked:
                a = jnp.where(strict, a, 0.0)
            acc_ref[hh] += _dot(a.astype(BF16), vt)

    tile(diag, True)

    def body(i, _):
        tile(diag - 1 - i, False)
        return 0

    lax.fori_loop(0, diag, body, 0)
    o_ref[0] = _merge_pair(acc_ref[0], acc_ref[1]).astype(o_ref.dtype)


def _sb(q, k, v, *, tq, tk, q_off):
    B, T, _ = q.shape
    qspec, kspec = _pair_specs(tq, k.shape[1])
    return pl.pallas_call(
        functools.partial(_sb_kernel, tq=tq, tk=tk, q_off=q_off),
        grid=(B, N_PAIRS, T // tq),
        in_specs=[qspec, kspec, kspec],
        out_specs=qspec,
        out_shape=jax.ShapeDtypeStruct(q.shape, BF16),
        scratch_shapes=[pltpu.VMEM((2, tq, LANES), F32), pltpu.VMEM((2, tq, LANES), F32)],
        compiler_params=_cparams(("parallel", "parallel", "parallel")),
        name="sb_attn",
    )(q, k, v)


def _const_key(v):
    bits = int(np.float32(v).view(np.int32))
    return (bits ^ 0x7FFFFFFF) + 1 if bits < 0 else bits


def _float_key(x):
    bits = lax.bitcast_convert_type(x, jnp.int32)
    return jnp.where(bits < 0, jnp.bitwise_xor(bits, 0x7FFFFFFF) + 1, bits)


KEY_MIN = -2 ** 31
KEY_HALF_NEG = _const_key(0.5 * NEG)


def _dsa_kernel(aq_ref, iq_ref, iw_ref, k_ref, v_ref, ik_ref, o_ref, key_ref, q8_ref, m_ref, acc_ref,
                *, tq, tk, q_off, topk):
    q0 = q_off + pl.program_id(1) * tq
    n_tiles = (q0 + tq + tk - 1) // tk
    blocks = [slice(b * LANES, (b + 1) * LANES) for b in range(tk // LANES)]

    def tile_start(j):
        return pl.multiple_of(j * tk, LANES)

    iq = iq_ref[0]
    iw = iw_ref[0]
    qh = [_keep_head(iq[:, (h // 2) * LANES:(h // 2 + 1) * LANES], h % 2) for h in range(N_IDX)]
    row_chunk = jnp.right_shift(q0 + lax.broadcasted_iota(jnp.int32, (tq, tk), 0), CHUNK_SHIFT)

    def score_tile(j, cr):
        mn, mx = cr
        k0 = tile_start(j)
        ik = ik_ref[0, pl.ds(k0, tk), :]
        sc = jnp.zeros((tq, tk), F32)
        for h in range(N_IDX):
            sc = sc + jnp.maximum(_dot_t(qh[h], ik), 0.0) * iw[:, h:h + 1]
        key_chunk = jnp.right_shift(k0 + lax.broadcasted_iota(jnp.int32, (tq, tk), 1), CHUNK_SHIFT)
        sc = jnp.where(key_chunk <= row_chunk, sc, NEG)
        key_ref[:, pl.ds(k0, tk)] = _float_key(sc)
        for sl in blocks:
            mn = jnp.minimum(mn, sc[:, sl])
            mx = jnp.maximum(mx, sc[:, sl])
        return mn, mx

    mn, mx = lax.fori_loop(0, n_tiles, score_tile,
                           (jnp.full((tq, LANES), jnp.inf, F32), jnp.full((tq, LANES), -jnp.inf, F32)))

    def search():
        lo0 = _float_key(jnp.min(mn, axis=1, keepdims=True)) - 1
        hi0 = _float_key(jnp.max(mx, axis=1, keepdims=True))

        def active(lo, hi):
            return lo + 1 < hi

        def unresolved(cr):
            return jnp.max(jnp.where(active(cr[0], cr[1]), 1.0, 0.0)) > 0.0

        def step(cr):
            lo, hi, above_hi = cr
            mid = jnp.right_shift(lo, 1) + jnp.right_shift(hi, 1) + jnp.bitwise_and(jnp.bitwise_and(lo, hi), 1)
            midb = jnp.broadcast_to(mid, (tq, LANES))

            def count_tile(j, cnt):
                k0 = tile_start(j)
                for b in range(len(blocks)):
                    keys = key_ref[:, pl.ds(k0 + b * LANES, LANES)]
                    cnt = cnt + jnp.where(keys > midb, 1.0, 0.0)
                return cnt

            cnt = jnp.sum(lax.fori_loop(0, n_tiles, count_tile, jnp.zeros((tq, LANES), F32)), axis=1, keepdims=True)
            exact = cnt == topk
            less = cnt < topk
            down = jnp.logical_or(less, exact)
            act = active(lo, hi)
            lo_n = jnp.where(exact, mid - 1, jnp.where(less, lo, mid))
            return (jnp.where(act, lo_n, lo), jnp.where(jnp.logical_and(act, down), mid, hi),
                    jnp.where(jnp.logical_and(act, down), cnt, above_hi))

        _, hi, above_hi = lax.while_loop(unresolved, step, (lo0, hi0, jnp.zeros((tq, 1), F32)))
        return hi, topk - above_hi

    thr, room = lax.cond(q0 + tq > topk, search,
                         lambda: (jnp.full((tq, 1), KEY_MIN, jnp.int32), jnp.zeros((tq, 1), F32)))

    @pl.when(jnp.max(room) > 0.0)
    def _():
        thrb = jnp.broadcast_to(thr, (tq, LANES))
        roomb = jnp.broadcast_to(room, (tq, LANES))
        r = lax.broadcasted_iota(jnp.int32, (LANES, 2 * LANES), 0)
        c = lax.broadcasted_iota(jnp.int32, (LANES, 2 * LANES), 1)
        rank_w = jnp.where((c >= LANES) | (r < c), 1.0, 0.0).astype(BF16)

        def tie_tile(j, seen):
            k0 = tile_start(j)
            for b in range(len(blocks)):
                keys = key_ref[:, pl.ds(k0 + b * LANES, LANES)]
                eq = keys == thrb
                res = _dot(jnp.where(eq, 1.0, 0.0).astype(BF16), rank_w)
                lifted = jnp.where(res[:, :LANES] + seen < roomb, keys + 1, keys)
                key_ref[:, pl.ds(k0 + b * LANES, LANES)] = jnp.where(eq, lifted, keys)
                seen = seen + res[:, LANES:]
            return seen

        lax.fori_loop(0, n_tiles, tie_tile, jnp.zeros((tq, LANES), F32))

    aq = aq_ref[0]
    for h in range(N_HEADS):
        q8_ref[h * tq:(h + 1) * tq, :] = _keep_head(aq[:, (h // 2) * LANES:(h // 2 + 1) * LANES], h % 2)
    m_ref[...] = jnp.full(m_ref.shape, NEG, F32)
    acc_ref[...] = jnp.zeros(acc_ref.shape, F32)
    thr_sel = jnp.broadcast_to(jnp.maximum(thr, KEY_HALF_NEG), (tq, tk))

    def attend_tile(j, _):
        k0 = tile_start(j)
        kt = k_ref[0, pl.ds(k0, tk), :]
        vt = v_ref[0, pl.ds(k0, tk), :]
        lane = lax.broadcasted_iota(jnp.int32, vt.shape, 1)
        v1 = jnp.where(lane < HEAD_DIM, vt, jnp.ones_like(vt))
        bias = jnp.where(key_ref[:, pl.ds(k0, tk)] > thr_sel, 0.0, NEG)
        s8 = _dot_t(q8_ref[...], kt).reshape(N_HEADS, tq, tk) + bias[None]
        m_old = m_ref[...]
        m_new = jnp.maximum(m_old, jnp.max(s8, axis=2, keepdims=True))
        p = jnp.exp(s8 - m_new).astype(BF16)
        pv = _dot(p.reshape(N_HEADS * tq, tk), v1).reshape(N_HEADS, tq, LANES)
        acc_ref[...] = jnp.exp(m_old - m_new) * acc_ref[...] + pv
        m_ref[...] = m_new
        return 0

    lax.fori_loop(0, n_tiles, attend_tile, 0)
    for hp in range(N_PAIRS):
        a0 = acc_ref[2 * hp]
        a1 = acc_ref[2 * hp + 1]
        o0 = a0 / pltpu.roll(a0, HEAD_DIM, 1)
        o1 = pltpu.roll(a1, HEAD_DIM, 1) / a1
        o_ref[0, :, hp * LANES:(hp + 1) * LANES] = _merge_pair(o0, o1).astype(o_ref.dtype)


def _dsa(aq, iq, iw, k2, v2, ik2, *, tq, q_off, topk):
    B, T, _ = aq.shape
    Lp = k2.shape[1]
    tk = next(t for t in (512, 384, 256, 128) if Lp % t == 0)
    qspec = lambda w: pl.BlockSpec((1, tq, w), lambda b, qi: (b, qi, 0))
    kspec = pl.BlockSpec((1, Lp, LANES), lambda b, qi: (b, 0, 0))
    return pl.pallas_call(
        functools.partial(_dsa_kernel, tq=tq, tk=tk, q_off=q_off, topk=topk),
        grid=(B, T // tq),
        in_specs=[qspec(BRANCH_W), qspec(N_IDX * IDX_DIM), qspec(SUBLANES), kspec, kspec, kspec],
        out_specs=qspec(BRANCH_W),
        out_shape=jax.ShapeDtypeStruct(aq.shape, BF16),
        scratch_shapes=[pltpu.VMEM((tq, Lp), jnp.int32), pltpu.VMEM((N_HEADS * tq, LANES), BF16),
                        pltpu.VMEM((N_HEADS, tq, 1), F32), pltpu.VMEM((N_HEADS, tq, LANES), F32)],
        compiler_params=_cparams(("parallel", "parallel")),
        name="dsa_attn",
    )(aq, iq, iw, k2, v2, ik2)


def _merge_kernel(x_ref, oa_ref, ob_ref, oc_ref, g_ref, wg_ref, wb_ref, wo_ref, y_ref):
    x = x_ref[0]
    D = x.shape[-1]
    xn = _rms(x, g_ref[...]).astype(BF16)
    merged = jnp.zeros(x.shape, F32)
    for n, o_ref in enumerate((oa_ref, ob_ref, oc_ref)):
        gate = _sigmoid(_dot(xn, wg_ref[:, n * D:(n + 1) * D]))
        merged = merged + gate * _dot(o_ref[0], wb_ref[n])
    y_ref[0] = x + _dot(merged.astype(BF16), wo_ref[...])


def _merge(x, oa, ob, oc, g, wg, wb, wo, tm):
    B, T, D = x.shape
    tspec = lambda w_: pl.BlockSpec((1, tm, w_), lambda b, t: (b, t, 0))
    const = lambda a: pl.BlockSpec(a.shape, lambda b, t: (0,) * a.ndim)
    return pl.pallas_call(
        _merge_kernel,
        grid=(B, T // tm),
        in_specs=[tspec(D), tspec(BRANCH_W), tspec(BRANCH_W), tspec(BRANCH_W), const(g), const(wg), const(wb),
                  const(wo)],
        out_specs=tspec(D),
        out_shape=jax.ShapeDtypeStruct(x.shape, F32),
        compiler_params=_cparams(("parallel", "parallel")),
        name="merge_out",
    )(x, oa, ob, oc, g, wg, wb, wo)


def _ffn_kernel(x_ref, g_ref, past_ref, wu_ref, cw_ref, cb_ref, wd_ref, gf_ref, y_ref, st_ref,
                hbuf_ref, carry_ref, acc_ref, *, tm, final_norm):
    fc = FF_CHUNK
    n_chunks = wu_ref.shape[0]

    @pl.when(pl.program_id(1) == 0)
    def _():
        carry_ref[...] = past_ref[0]

    x = x_ref[0]
    xn = _rms(x, g_ref[...]).astype(BF16)
    acc_ref[...] = jnp.zeros(acc_ref.shape, F32)

    def chunk(c, _):
        h = _dot(xn, wu_ref[c])
        hbuf_ref[0:SUBLANES, :] = carry_ref[c]
        hbuf_ref[SUBLANES:SUBLANES + tm, :] = h
        carry_ref[c] = h[tm - SUBLANES:tm, :]
        cw = cw_ref[c]
        hc = (cb_ref[c] + cw[0:1] * hbuf_ref[SUBLANES - 2:SUBLANES - 2 + tm, :]
              + cw[1:2] * hbuf_ref[SUBLANES - 1:SUBLANES - 1 + tm, :] + cw[2:3] * h)
        gate, up = hc[:, :fc], hc[:, fc:]
        act = gate * _sigmoid(gate) * up
        acc_ref[...] += _dot(act.astype(BF16), wd_ref[c])
        return 0

    lax.fori_loop(0, n_chunks, chunk, 0)
    st_ref[0] = carry_ref[...]
    y = x + acc_ref[...]
    if final_norm:
        y = _rms(y, gf_ref[...])
    y_ref[0] = y


def _ffn(x, g, past, wu, cw, cb, wd, gf, tm, final_norm):
    B, T, D = x.shape
    nc, _, fc2 = wu.shape
    tspec = pl.BlockSpec((1, tm, D), lambda b, t: (b, t, 0))
    const = lambda a: pl.BlockSpec(a.shape, lambda b, t: (0,) * a.ndim)
    sspec = pl.BlockSpec((1, nc, SUBLANES, fc2), lambda b, t: (b, 0, 0, 0))
    return pl.pallas_call(
        functools.partial(_ffn_kernel, tm=tm, final_norm=final_norm),
        grid=(B, T // tm),
        in_specs=[tspec, const(g), sspec, const(wu), const(cw), const(cb), const(wd), const(gf)],
        out_specs=[tspec, sspec],
        out_shape=[jax.ShapeDtypeStruct(x.shape, F32), jax.ShapeDtypeStruct((B, nc, SUBLANES, fc2), F32)],
        scratch_shapes=[pltpu.VMEM((tm + SUBLANES, fc2), F32), pltpu.VMEM((nc, SUBLANES, fc2), F32),
                        pltpu.VMEM((tm, D), F32)],
        compiler_params=_cparams(("arbitrary", "arbitrary")),
        name="conv_ffn",
    )(x, g, past, wu, cw, cb, wd, gf)


def _rope_tables(pos):
    half = ROT_DIM // 2
    freq = ROPE_THETA ** (-jnp.arange(half, dtype=F32) / half)
    ang = pos.astype(F32)[:, None] * freq[None, :]
    lane = np.arange(LANES) % HEAD_DIM
    cos = jnp.cos(ang)[:, lane % half]
    sin = jnp.sin(ang)[:, lane % half]
    rc = jnp.where(lane < ROT_DIM, cos, 1.0)
    rs1 = jnp.where(lane < half, -sin, 0.0)
    rs2 = jnp.where((lane >= half) & (lane < ROT_DIM), sin, 0.0)
    return rc, rs1, rs2


def _pack_layer(l, norm_mix, w_in, b_f, w_branch, w_gate, w_out, norm_ffn, w_up, conv_w, conv_b, w_down):
    D = w_in.shape[1]
    sizes = (BRANCH_W, HEAD_DIM, HEAD_DIM, N_IDX * IDX_DIM, IDX_DIM, N_IDX) + (BRANCH_W,) * 6 + (N_HEADS,)
    offs = np.concatenate([[0], np.cumsum(sizes)])
    (a_q, a_k, a_v, i_q, i_k, i_w, b_q, b_k, b_v, c_q, c_k, c_v, c_f) = (
        w_in[l][:, offs[i]:offs[i + 1]] for i in range(len(sizes)))
    zeros = lambda n: jnp.zeros((D, n), w_in.dtype)
    w = jnp.concatenate([a_q, i_q, a_k, a_k, i_k, i_k, a_v, a_v,
                         i_w, zeros(CF_LANE - N_IDX), c_f, zeros(LANES - CF_LANE - N_HEADS),
                         b_q, b_k, b_v, c_q, c_k, c_v], axis=1).astype(BF16)
    bf_row = jnp.zeros((1, LANES), F32).at[0, CF_LANE:CF_LANE + N_HEADS].set(b_f[l])

    d_ff = w_down.shape[1]
    fc = FF_CHUNK
    nc = d_ff // fc
    assert nc * fc == d_ff
    halves = lambda a: jnp.concatenate([a[..., :d_ff].reshape(a.shape[:-1] + (nc, fc)),
                                        a[..., d_ff:].reshape(a.shape[:-1] + (nc, fc))], axis=-1)
    wu = jnp.moveaxis(halves(w_up[l]), 1, 0).astype(BF16)
    cw = jnp.moveaxis(halves(conv_w[l]), 1, 0)
    cb = halves(conv_b[l])[:, None, :]
    wd = w_down[l].reshape(nc, fc, D).astype(BF16)
    return dict(g_mix=norm_mix[l][None], w=w, bf_row=bf_row, wg=w_gate[l].astype(BF16),
                wb=w_branch[l].astype(BF16), wo=w_out[l].astype(BF16), g_ffn=norm_ffn[l][None],
                wu=wu, cw=cw, cb=cb, wd=wd, halves=halves, nc=nc, d_ff=d_ff)


def _conv_state_in(state, halves):
    st = jnp.moveaxis(halves(state), 2, 1)
    return jnp.pad(st, ((0, 0), (0, 0), (SUBLANES - (CONV_W - 1), 0), (0, 0)))


def _conv_state_out(st, d_ff):
    st = st[:, :, SUBLANES - (CONV_W - 1):, :]
    B, nc, r, fc2 = st.shape
    fc = fc2 // 2
    gate = jnp.moveaxis(st[..., :fc], 1, 2).reshape(B, r, d_ff)
    up = jnp.moveaxis(st[..., fc:], 1, 2).reshape(B, r, d_ff)
    return jnp.concatenate([gate, up], axis=-1)


def _pad_keys(a, Lp):
    return jnp.pad(a, ((0, 0), (0, Lp - a.shape[1]), (0, 0)))


def _group_layer(x, past, lw, gf, final_norm, tiles):
    B, T, D = x.shape
    P = 0 if past is None else past[0].shape[1]
    L = P + T
    Lp = -(-L // KEY_TILE) * KEY_TILE
    topk = max(1, min(TOPK_MAX, L // 4))
    tables = _rope_tables(jnp.arange(P, L))

    (ak, av, ik, bk, bv, ck, cv, lf, iw,
     aq16, iq16, ak16, av16, ik16, bq16, bk16, bv16, cq16, ck16, cv16) = _project(
         x, lw["g_mix"], lw["w"], lw["bf_row"], tables, tiles["proj"])

    if past is None:
        keys = (ak16, av16, ik16, bk16, bv16, ck16, cv16)
        lf_all = lf
        conv_in = jnp.zeros((B, lw["nc"], SUBLANES, 2 * FF_CHUNK), F32)
    else:
        p_ak, p_av, p_ik, p_bk, p_bv, p_ck, p_cv, p_lf, p_conv = past
        dup = lambda a: jnp.concatenate([a, a], axis=-1).astype(BF16)
        flat = lambda a: a.reshape(B, P, BRANCH_W).astype(BF16)
        olds = (dup(p_ak), dup(p_av), dup(p_ik), flat(p_bk), flat(p_bv), flat(p_ck), flat(p_cv))
        news = (ak16, av16, ik16, bk16, bv16, ck16, cv16)
        keys = tuple(jnp.concatenate([o, n], axis=1) for o, n in zip(olds, news))
        lf_all = jnp.concatenate([p_lf, lf], axis=1)
        conv_in = _conv_state_in(p_conv, lw["halves"])
    ak2, av2, ik2, bkk, bvv, ckk, cvv = (_pad_keys(a, Lp) for a in keys)

    fk_t = _cumsum_time(jnp.swapaxes(_pad_keys(lf_all, Lp), 1, 2))
    fq = jnp.swapaxes(fk_t[:, :, P:L], 1, 2)

    o_a = _dsa(aq16, iq16, iw, ak2, av2, ik2, tq=tiles["dsa"], q_off=P, topk=topk)
    o_b = _sb(bq16, bkk, bvv, tq=tiles["sb"][0], tk=tiles["sb"][1], q_off=P)
    o_c = _fox(cq16, ckk, cvv, fq, fk_t, tq=tiles["fox"][0], tk=tiles["fox"][1], q_off=P)

    x = _merge(x, o_a, o_b, o_c, lw["g_mix"], lw["wg"], lw["wb"], lw["wo"], tiles["merge"])
    x, st = _ffn(x, lw["g_ffn"], conv_in, lw["wu"], lw["cw"], lw["cb"], lw["wd"], gf, tiles["ffn"], final_norm)

    heads = lambda a: a.reshape(B, T, N_HEADS, HEAD_DIM)
    rows = (ak, av, ik, heads(bk), heads(bv), heads(ck), heads(cv), lf, _conv_state_out(st, lw["d_ff"]))
    return x, rows


def _tiles(T, P):
    pick = lambda want: min(want, T)
    if T % 512 == 0 and P % 512 == 0:
        sb, fox = (256, 256), (256, 512)
    else:
        assert T <= KEY_TILE and P % KEY_TILE + T <= KEY_TILE
        sb = fox = (T, KEY_TILE)
    return dict(proj=pick(256), dsa=pick(128), sb=sb, fox=fox, merge=pick(512), ffn=pick(256))


def kernel(x_prompt, x_sample, cache_a_k, cache_a_v, cache_a_idx_k, cache_b_k, cache_b_v, cache_c_k, cache_c_v,
           cache_c_logf, state_ffn_conv, norm_mix, w_in, b_f, w_branch, w_gate, w_out, norm_ffn, w_up, conv_w,
           conv_b, w_down, norm_final):
    depth = w_in.shape[0]
    caches = (cache_a_k, cache_a_v, cache_a_idx_k, cache_b_k, cache_b_v, cache_c_k, cache_c_v, cache_c_logf,
              state_ffn_conv)
    layers = [_pack_layer(l, norm_mix, w_in, b_f, w_branch, w_gate, w_out, norm_ffn, w_up, conv_w, conv_b, w_down)
              for l in range(depth)]
    gf = norm_final[None]

    def trunk(x, past):
        tiles = _tiles(x.shape[1], 0 if past is None else past[0].shape[2])
        per_layer = []
        for l in range(depth):
            lp = None if past is None else tuple(c[l] for c in past)
            x, rows = _group_layer(x, lp, layers[l], gf, l == depth - 1, tiles)
            per_layer.append(rows)
        return x, tuple(jnp.stack(s, axis=0) for s in zip(*per_layer))

    y_prompt, p_state = trunk(x_prompt, None)
    y_sample, s_state = trunk(x_sample, caches)
    return (y_prompt, y_sample) + p_state + s_state
```

```python
import functools

import jax
import jax.numpy as jnp
import numpy as np
from jax import lax
from jax.experimental import pallas as pl
from jax.experimental.pallas import tpu as pltpu

F32 = jnp.float32
BF16 = jnp.bfloat16

LANES = 128
SUBLANES = 8
HEAD_DIM = 64
N_HEADS = 8
N_PAIRS = N_HEADS // 2
BRANCH_W = N_HEADS * HEAD_DIM
N_BRANCH = 3
N_IDX = 4
IDX_DIM = 64
CHUNK = 64
CHUNK_SHIFT = 6
TOPK_MAX = 256
ROT_DIM = HEAD_DIM // 4
ROPE_THETA = 500000.0
CONV_W = 3
EPS = 1e-6
NEG = -1e30
KEY_TILE = 128
FF_CHUNK = 256
VMEM_LIMIT = 56 * 1024 * 1024

ROPE_W = 1024
MISC_W = 256
CF_LANE = 8


def _cparams(sem):
    return pltpu.CompilerParams(dimension_semantics=sem, vmem_limit_bytes=VMEM_LIMIT)


def _rms(x, g):
    return x * lax.rsqrt(jnp.mean(x * x, axis=-1, keepdims=True) + EPS) * g


def _sigmoid(z):
    return 1.0 / (1.0 + jnp.exp(-z))


def _log_sigmoid(z):
    return jnp.minimum(z, 0.0) - jnp.log1p(jnp.exp(-jnp.abs(z)))


def _dot_t(a, b):
    return lax.dot_general(a, b, (((1,), (1,)), ((), ())), preferred_element_type=F32)


def _dot(a, b):
    return jnp.dot(a, b, preferred_element_type=F32)


def _keep_head(blk, which):
    lane = lax.broadcasted_iota(jnp.int32, blk.shape, blk.ndim - 1)
    keep = (lane >= HEAD_DIM) if which else (lane < HEAD_DIM)
    return jnp.where(keep, blk, jnp.zeros_like(blk))


def _merge_pair(o0, o1):
    lane = lax.broadcasted_iota(jnp.int32, o0.shape, o0.ndim - 1)
    return jnp.where(lane < HEAD_DIM, o0, o1)


def _proj_kernel(x_ref, g_ref, w_ref, bf_ref, rc_ref, rs1_ref, rs2_ref,
                 ak_ref, av_ref, ik_ref, bk_ref, bv_ref, ck_ref, cv_ref, lf_ref, iw_ref,
                 aq16, iq16, ak16, av16, ik16, bq16, bk16, bv16, cq16, ck16, cv16):
    xn = _rms(x_ref[0], g_ref[...]).astype(BF16)
    rc, rs1, rs2 = rc_ref[...], rs1_ref[...], rs2_ref[...]

    def cols(c0, width):
        return _dot(xn, w_ref[:, c0:c0 + width])

    def group(blk, g):
        return blk[:, g * LANES:(g + 1) * LANES]

    def rope(blk):
        half = ROT_DIM // 2
        return blk * rc + pltpu.roll(blk, LANES - half, 1) * rs1 + pltpu.roll(blk, half, 1) * rs2

    wide = cols(0, BRANCH_W)
    for g in range(4):
        aq16[0, :, g * LANES:(g + 1) * LANES] = (rope(group(wide, g)) * HEAD_DIM ** -0.5).astype(BF16)
    wide = cols(BRANCH_W, ROPE_W - BRANCH_W)
    for g in range(2):
        iq16[0, :, g * LANES:(g + 1) * LANES] = (rope(group(wide, g)) * IDX_DIM ** -0.5).astype(BF16)
    blk = rope(group(wide, 2))
    ak_ref[0] = blk[:, :HEAD_DIM]
    ak16[0] = blk.astype(BF16)
    blk = rope(group(wide, 3))
    ik_ref[0] = blk[:, :IDX_DIM]
    ik16[0] = blk.astype(BF16)

    wide = cols(ROPE_W, MISC_W)
    blk = group(wide, 0)
    av_ref[0] = blk[:, :HEAD_DIM]
    av16[0] = blk.astype(BF16)
    blk = group(wide, 1)
    iw_ref[0] = blk[:, :SUBLANES] * N_IDX ** -0.5
    lf_ref[0] = _log_sigmoid(blk[:, CF_LANE:CF_LANE + N_HEADS] + bf_ref[:, CF_LANE:CF_LANE + N_HEADS])

    base = ROPE_W + MISC_W
    outs = ((None, bq16), (bk_ref, bk16), (bv_ref, bv16), (None, cq16), (ck_ref, ck16), (cv_ref, cv16))
    for n, (o32, o16) in enumerate(outs):
        wide = cols(base + n * BRANCH_W, BRANCH_W)
        if o32 is None:
            o16[0] = (wide * HEAD_DIM ** -0.5).astype(BF16)
        else:
            o32[0] = wide
            o16[0] = wide.astype(BF16)


def _project(x, g, w, bf_row, tables, tm):
    B, T, D = x.shape
    tok = lambda w_, dt: jax.ShapeDtypeStruct((B, T, w_), dt)
    tspec = lambda w_: pl.BlockSpec((1, tm, w_), lambda b, t: (b, t, 0))
    const = lambda a: pl.BlockSpec(a.shape, lambda b, t: (0,) * a.ndim, pipeline_mode=pl.Buffered(1))
    out_shapes = ([tok(64, F32)] * 3 + [tok(512, F32)] * 4 + [tok(8, F32)] * 2
                  + [tok(512, BF16), tok(256, BF16)] + [tok(128, BF16)] * 3 + [tok(512, BF16)] * 6)
    rspec = pl.BlockSpec((tm, LANES), lambda b, t: (t, 0))
    return pl.pallas_call(
        _proj_kernel,
        grid=(B, T // tm),
        in_specs=[tspec(D), const(g), const(w), const(bf_row), rspec, rspec, rspec],
        out_specs=[tspec(s.shape[-1]) for s in out_shapes],
        out_shape=out_shapes,
        compiler_params=_cparams(("parallel", "parallel")),
        name="in_proj",
    )(x, g, w, bf_row, *tables)


def _split3(x):
    hi = x.astype(BF16)
    r = x - hi.astype(F32)
    mid = r.astype(BF16)
    lo = (r - mid.astype(F32)).astype(BF16)
    return hi, mid, lo


def _cumsum_kernel(lf_ref, o_ref, *, blk):
    L = lf_ref.shape[-1]
    r = lax.broadcasted_iota(jnp.int32, (blk, blk), 0)
    c = lax.broadcasted_iota(jnp.int32, (blk, blk), 1)
    upper = jnp.where(r <= c, 1.0, 0.0).astype(BF16)
    carry = jnp.zeros((N_HEADS, 1), F32)
    for j in range(L // blk):
        hi, mid, lo = _split3(lf_ref[0, :, j * blk:(j + 1) * blk])
        f = _dot(hi, upper) + _dot(mid, upper) + _dot(lo, upper) + carry
        o_ref[0, :, j * blk:(j + 1) * blk] = f
        carry = f[:, blk - 1:blk]


def _cumsum_time(lf_t):
    B, H, L = lf_t.shape
    blk = LANES
    assert L % blk == 0
    spec = pl.BlockSpec((1, H, L), lambda b: (b, 0, 0))
    return pl.pallas_call(
        functools.partial(_cumsum_kernel, blk=blk),
        grid=(B,),
        in_specs=[spec],
        out_specs=spec,
        out_shape=jax.ShapeDtypeStruct((B, H, L), F32),
        compiler_params=_cparams(("parallel",)),
        name="logf_cumsum",
    )(lf_t)


def _head_column(blk, h):
    lane = lax.broadcasted_iota(jnp.int32, blk.shape, 1)
    return jnp.sum(jnp.where(lane == h, blk, 0.0), axis=1, keepdims=True)


def _positions(tq, tk, q0, k0):
    qpos = q0 + lax.broadcasted_iota(jnp.int32, (tq, tk), 0)
    kpos = k0 + lax.broadcasted_iota(jnp.int32, (tq, tk), 1)
    return qpos, kpos


def _pair_specs(tq, Lp):
    qspec = pl.BlockSpec((1, tq, LANES), lambda b, hp, qi: (b, qi, hp))
    kspec = pl.BlockSpec((1, Lp, LANES), lambda b, hp, qi: (b, 0, hp))
    return qspec, kspec


def _fox_kernel(q_ref, k_ref, v_ref, fq_ref, fk_ref, o_ref, m_ref, acc_ref, *, tq, tk, q_off):
    hp = pl.program_id(1)
    q0 = q_off + pl.program_id(2) * tq
    diag = q0 // tk
    qp = q_ref[0]
    fq_blk = fq_ref[0]
    qm = (_keep_head(qp, 0), _keep_head(qp, 1))
    fq = (_head_column(fq_blk, hp * 2), _head_column(fq_blk, hp * 2 + 1))
    ones = jnp.ones((tk, LANES), BF16)
    m_ref[...] = jnp.full(m_ref.shape, NEG, F32)
    acc_ref[...] = jnp.zeros(acc_ref.shape, F32)

    def tile(j, masked):
        k0 = pl.multiple_of(j * tk, tk)
        kt = k_ref[0, pl.ds(k0, tk), :]
        v1 = jnp.concatenate([v_ref[0, pl.ds(k0, tk), :], ones], axis=1)
        if masked:
            qpos, kpos = _positions(tq, tk, q0, k0)
            causal = kpos <= qpos
        for hh in range(2):
            fk = fk_ref[0, hp * 2 + hh, :, pl.ds(k0, tk)]
            s = _dot_t(qm[hh], kt) + fq[hh] - fk
            if masked:
                s = jnp.where(causal, s, NEG)
            m_old = m_ref[hh]
            m_new = jnp.maximum(m_old, jnp.max(s, axis=1, keepdims=True))
            p = jnp.exp(s - m_new).astype(BF16)
            acc_ref[hh] = jnp.exp(m_old - m_new) * acc_ref[hh] + _dot(p, v1)
            m_ref[hh] = m_new

    def body(j, _):
        tile(j, False)
        return 0

    lax.fori_loop(0, diag, body, 0)
    tile(diag, True)
    o0 = acc_ref[0, :, :LANES] / acc_ref[0, :, LANES:]
    o1 = acc_ref[1, :, :LANES] / acc_ref[1, :, LANES:]
    o_ref[0] = _merge_pair(o0, o1).astype(o_ref.dtype)


def _fox(q, k, v, fq, fk_t, *, tq, tk, q_off):
    B, T, _ = q.shape
    Lp = k.shape[1]
    qspec, kspec = _pair_specs(tq, Lp)
    return pl.pallas_call(
        functools.partial(_fox_kernel, tq=tq, tk=tk, q_off=q_off),
        grid=(B, N_PAIRS, T // tq),
        in_specs=[qspec, kspec, kspec,
                  pl.BlockSpec((1, tq, N_HEADS), lambda b, hp, qi: (b, qi, 0)),
                  pl.BlockSpec((1, N_HEADS, 1, Lp), lambda b, hp, qi: (b, 0, 0, 0))],
        out_specs=qspec,
        out_shape=jax.ShapeDtypeStruct(q.shape, BF16),
        scratch_shapes=[pltpu.VMEM((2, tq, 1), F32), pltpu.VMEM((2, tq, 2 * LANES), F32)],
        compiler_params=_cparams(("parallel", "parallel", "parallel")),
        name="fox_attn",
    )(q, k, v, fq, fk_t[:, :, None, :])


def _sb_kernel(q_ref, k_ref, v_ref, o_ref, acc_ref, tail_ref, *, tq, tk, q_off):
    q0 = q_off + pl.program_id(2) * tq
    diag = q0 // tk
    qp = q_ref[0]
    qm = (_keep_head(qp, 0), _keep_head(qp, 1))
    r = jnp.bitwise_and(lax.broadcasted_iota(jnp.int32, (2 * LANES, 2 * LANES), 0), LANES - 1)
    c = lax.broadcasted_iota(jnp.int32, (2 * LANES, 2 * LANES), 1)
    suffix_w = jnp.where((c >= LANES) | (r > c), 1.0, 0.0).astype(BF16)
    acc_ref[...] = jnp.zeros(acc_ref.shape, F32)
    tail_ref[...] = jnp.zeros(tail_ref.shape, F32)

    def tile(j, masked):
        k0 = pl.multiple_of(j * tk, tk)
        kt = k_ref[0, pl.ds(k0, tk), :]
        vt = v_ref[0, pl.ds(k0, tk), :]
        if masked:
            qpos, kpos = _positions(tq, tk, q0, k0)
            strict = kpos < qpos
        for hh in range(2):
            z = _dot_t(qm[hh], kt)
            ls = jnp.minimum(z, 0.0) - jnp.log(1.0 + jnp.exp(-jnp.abs(z)))
            la = ls - z
            if masked:
                la = jnp.where(strict, la, 0.0)
            hi = la.astype(BF16)
            lo = (la - hi.astype(F32)).astype(BF16)
            run = tail_ref[hh]
            after = [None] * (tk // LANES)
            for b in reversed(range(tk // LANES)):
                sl = slice(b * LANES, (b + 1) * LANES)
                res = _dot(jnp.concatenate([hi[:, sl], lo[:, sl]], axis=1), suffix_w)
                after[b] = res[:, :LANES] + run
                run = run + res[:, LANES:]
            tail_ref[hh] = run
            a = jnp.exp(ls + jnp.concatenate(after, axis=1))
            if masked:
                a = jnp.where(strict, a, 0.0)
            acc_ref[hh] += _dot(a.astype(BF16), vt)

    tile(diag, True)

    def body(i, _):
        tile(diag - 1 - i, False)
        return 0

    lax.fori_loop(0, diag, body, 0)
    o_ref[0] = _merge_pair(acc_ref[0], acc_ref[1]).astype(o_ref.dtype)


def _sb(q, k, v, *, tq, tk, q_off):
    B, T, _ = q.shape
    qspec, kspec = _pair_specs(tq, k.shape[1])
    return pl.pallas_call(
        functools.partial(_sb_kernel, tq=tq, tk=tk, q_off=q_off),
        grid=(B, N_PAIRS, T // tq),
        in_specs=[qspec, kspec, kspec],
        out_specs=qspec,
        out_shape=jax.ShapeDtypeStruct(q.shape, BF16),
        scratch_shapes=[pltpu.VMEM((2, tq, LANES), F32), pltpu.VMEM((2, tq, LANES), F32)],
        compiler_params=_cparams(("parallel", "parallel", "parallel")),
        name="sb_attn",
    )(q, k, v)


def _const_key(v):
    bits = int(np.float32(v).view(np.int32))
    return (bits ^ 0x7FFFFFFF) + 1 if bits < 0 else bits


def _float_key(x):
    bits = lax.bitcast_convert_type(x, jnp.int32)
    return jnp.where(bits < 0, jnp.bitwise_xor(bits, 0x7FFFFFFF) + 1, bits)


KEY_MIN = -2 ** 31
KEY_HALF_NEG = _const_key(0.5 * NEG)


KEY_MAX = 2 ** 31 - 1
KEY_BITS = 32
COUNT_CHAINS = 8


def _dsa_kernel(aq_ref, iq_ref, iwt_ref, k_ref, vt_ref, ik_ref, o_ref, key_ref, q8_ref, m_ref, acc_ref,
                *, tq, tk, q_off, n_rows, topk):
    q0 = q_off + pl.program_id(1) * tq
    n_tiles = (q0 + n_rows + tk - 1) // tk

    def tile_start(j):
        return pl.multiple_of(j * tk, LANES)

    iq = iq_ref[0]
    iwt = iwt_ref[0]
    qh = [_keep_head(iq[:, (h // 2) * LANES:(h // 2 + 1) * LANES], h % 2) for h in range(N_IDX)]
    q_chunk = jnp.right_shift(q0 + lax.broadcasted_iota(jnp.int32, (tk, tq), 1), CHUNK_SHIFT)

    def score_tile(j, _):
        k0 = tile_start(j)
        ik = ik_ref[0, pl.ds(k0, tk), :]
        sc = jnp.zeros((tk, tq), F32)
        for h in range(N_IDX):
            sc = sc + jnp.maximum(_dot_t(ik, qh[h]), 0.0) * iwt[h:h + 1, :]
        key_chunk = jnp.right_shift(k0 + lax.broadcasted_iota(jnp.int32, (tk, tq), 0), CHUNK_SHIFT)
        key_ref[pl.ds(k0, tk), :] = _float_key(jnp.where(key_chunk <= q_chunk, sc, NEG))
        return 0

    lax.fori_loop(0, n_tiles, score_tile, 0)

    def search():
        def step(_, cr):
            lo, hi, above_hi = cr
            mid = jnp.right_shift(lo, 1) + jnp.right_shift(hi, 1) + jnp.bitwise_and(jnp.bitwise_and(lo, hi), 1)

            def count_tile(j, cnt):
                above = jnp.where(key_ref[pl.ds(tile_start(j), tk), :] > mid, 1.0, 0.0)
                return cnt + jnp.sum(above.reshape((-1,) + cnt.shape), axis=0)

            cnt = lax.fori_loop(0, n_tiles, count_tile, jnp.zeros((COUNT_CHAINS, SUBLANES, tq), F32))
            cnt = jnp.sum(jnp.sum(cnt, axis=0), axis=0, keepdims=True)
            less = cnt < topk
            return jnp.where(less, lo, mid), jnp.where(less, mid, hi), jnp.where(less, cnt, above_hi)

        init = (jnp.full((1, tq), KEY_MIN, jnp.int32), jnp.full((1, tq), KEY_MAX, jnp.int32), jnp.zeros((1, tq), F32))
        _, hi, above_hi = lax.fori_loop(0, KEY_BITS, step, init)
        return hi, topk - above_hi

    thr, room = lax.cond(q0 + n_rows > topk, search,
                         lambda: (jnp.full((1, tq), KEY_MIN, jnp.int32), jnp.zeros((1, tq), F32)))
    row_ok = lax.broadcasted_iota(jnp.int32, (1, tq), 1) < n_rows
    room = jnp.where(row_ok, room, 0.0)

    @pl.when(jnp.max(room) > 0.0)
    def _():
        r = lax.broadcasted_iota(jnp.int32, (LANES, LANES), 0)
        c = lax.broadcasted_iota(jnp.int32, (LANES, LANES), 1)
        before = jnp.where(c < r, 1.0, 0.0).astype(BF16)

        def tie_tile(j, seen):
            k0 = tile_start(j)
            for b in range(tk // LANES):
                rows = pl.ds(k0 + b * LANES, LANES)
                keys = key_ref[rows, :]
                eq = keys == thr
                eqf = jnp.where(eq, 1.0, 0.0)
                rank = _dot(before, eqf.astype(BF16)) + seen
                key_ref[rows, :] = jnp.where(eq, jnp.where(rank < room, keys + 1, keys), keys)
                seen = seen + jnp.sum(eqf, axis=0, keepdims=True)
            return seen

        lax.fori_loop(0, n_tiles, tie_tile, jnp.zeros((1, tq), F32))

    aq = aq_ref[0]
    for h in range(N_HEADS):
        q8_ref[h * tq:(h + 1) * tq, :] = _keep_head(aq[:, (h // 2) * LANES:(h // 2 + 1) * LANES], h % 2)
    m_ref[...] = jnp.full(m_ref.shape, NEG, F32)
    acc_ref[...] = jnp.zeros(acc_ref.shape, F32)
    thr_sel = jnp.maximum(thr, KEY_HALF_NEG)

    def attend_tile(j, _):
        k0 = tile_start(j)
        kt = k_ref[0, pl.ds(k0, tk), :]
        v1t = vt_ref[0, :, pl.ds(k0, tk)]
        bias = jnp.where(key_ref[pl.ds(k0, tk), :] > thr_sel, 0.0, NEG)
        st = _dot_t(kt, q8_ref[...])
        st = jnp.concatenate([st[:, h * tq:(h + 1) * tq] + bias for h in range(N_HEADS)], axis=1)
        m_old = m_ref[...]
        m_new = jnp.maximum(m_old, jnp.max(st, axis=0, keepdims=True))
        p = jnp.exp(st - m_new).astype(BF16)
        acc_ref[...] = jnp.exp(m_old - m_new) * acc_ref[...] + _dot(v1t, p)
        m_ref[...] = m_new
        return 0

    lax.fori_loop(0, n_tiles, attend_tile, 0)
    for hp in range(N_PAIRS):
        pair = []
        for h in (2 * hp, 2 * hp + 1):
            a = acc_ref[:, h * tq:(h + 1) * tq]
            pair.append(a[:HEAD_DIM] / a[HEAD_DIM:])
        o_ref[0, :, hp * LANES:(hp + 1) * LANES] = jnp.concatenate(pair, axis=0).T.astype(o_ref.dtype)


def _dsa(aq, iq, iw, k2, v2, ik2, *, tq, q_off, topk):
    B, T, _ = aq.shape
    Lp = k2.shape[1]
    tk = next(t for t in (512, 384, 256, 128) if Lp % t == 0)
    Tp = -(-T // LANES) * LANES
    tq = max(tq, LANES)
    n_rows = min(T, tq)
    pad_t = lambda a: jnp.pad(a, ((0, 0), (0, Tp - T), (0, 0)))
    v1t = jnp.concatenate([jnp.swapaxes(v2[:, :, :HEAD_DIM], 1, 2), jnp.ones((B, HEAD_DIM, Lp), v2.dtype)], axis=1)
    qspec = lambda w: pl.BlockSpec((1, tq, w), lambda b, qi: (b, qi, 0))
    kspec = pl.BlockSpec((1, Lp, LANES), lambda b, qi: (b, 0, 0))
    out = pl.pallas_call(
        functools.partial(_dsa_kernel, tq=tq, tk=tk, q_off=q_off, n_rows=n_rows, topk=topk),
        grid=(B, Tp // tq),
        in_specs=[qspec(BRANCH_W), qspec(N_IDX * IDX_DIM), pl.BlockSpec((1, SUBLANES, tq), lambda b, qi: (b, 0, qi)),
                  kspec, pl.BlockSpec((1, LANES, Lp), lambda b, qi: (b, 0, 0)), kspec],
        out_specs=qspec(BRANCH_W),
        out_shape=jax.ShapeDtypeStruct((B, Tp, BRANCH_W), BF16),
        scratch_shapes=[pltpu.VMEM((Lp, tq), jnp.int32), pltpu.VMEM((N_HEADS * tq, LANES), BF16),
                        pltpu.VMEM((1, N_HEADS * tq), F32), pltpu.VMEM((LANES, N_HEADS * tq), F32)],
        compiler_params=_cparams(("parallel", "parallel")),
        name="dsa_attn",
    )(pad_t(aq), pad_t(iq), jnp.swapaxes(pad_t(iw), 1, 2), k2, v1t, ik2)
    return out[:, :T]


def _merge_kernel(x_ref, oa_ref, ob_ref, oc_ref, g_ref, wg_ref, wb_ref, wo_ref, y_ref):
    x = x_ref[0]
    D = x.shape[-1]
    xn = _rms(x, g_ref[...]).astype(BF16)
    merged = jnp.zeros(x.shape, F32)
    for n, o_ref in enumerate((oa_ref, ob_ref, oc_ref)):
        gate = _sigmoid(_dot(xn, wg_ref[:, n * D:(n + 1) * D]))
        merged = merged + gate * _dot(o_ref[0], wb_ref[n])
    y_ref[0] = x + _dot(merged.astype(BF16), wo_ref[...])


def _merge(x, oa, ob, oc, g, wg, wb, wo, tm):
    B, T, D = x.shape
    tspec = lambda w_: pl.BlockSpec((1, tm, w_), lambda b, t: (b, t, 0))
    const = lambda a: pl.BlockSpec(a.shape, lambda b, t: (0,) * a.ndim)
    return pl.pallas_call(
        _merge_kernel,
        grid=(B, T // tm),
        in_specs=[tspec(D), tspec(BRANCH_W), tspec(BRANCH_W), tspec(BRANCH_W), const(g), const(wg), const(wb),
                  const(wo)],
        out_specs=tspec(D),
        out_shape=jax.ShapeDtypeStruct(x.shape, F32),
        compiler_params=_cparams(("parallel", "parallel")),
        name="merge_out",
    )(x, oa, ob, oc, g, wg, wb, wo)


def _ffn_kernel(x_ref, g_ref, past_ref, wu_ref, cw_ref, cb_ref, wd_ref, gf_ref, y_ref, st_ref,
                hbuf_ref, carry_ref, acc_ref, *, tm, final_norm):
    fc = FF_CHUNK
    n_chunks = wu_ref.shape[0]

    @pl.when(pl.program_id(1) == 0)
    def _():
        carry_ref[...] = past_ref[0]

    x = x_ref[0]
    xn = _rms(x, g_ref[...]).astype(BF16)
    acc_ref[...] = jnp.zeros(acc_ref.shape, F32)

    for c in range(n_chunks):
        hbuf = hbuf_ref.at[c % 2]
        h = _dot(xn, wu_ref[c])
        hbuf[0:SUBLANES, :] = carry_ref[c]
        hbuf[SUBLANES:SUBLANES + tm, :] = h
        carry_ref[c] = h[tm - SUBLANES:tm, :]
        cw = cw_ref[c]
        hc = (cb_ref[c] + cw[0:1] * hbuf[SUBLANES - 2:SUBLANES - 2 + tm, :]
              + cw[1:2] * hbuf[SUBLANES - 1:SUBLANES - 1 + tm, :] + cw[2:3] * h)
        gate, up = hc[:, :fc], hc[:, fc:]
        act = gate * _sigmoid(gate) * up
        acc_ref[...] += _dot(act.astype(BF16), wd_ref[c])
    st_ref[0] = carry_ref[...]
    y = x + acc_ref[...]
    if final_norm:
        y = _rms(y, gf_ref[...])
    y_ref[0] = y


def _ffn(x, g, past, wu, cw, cb, wd, gf, tm, final_norm):
    B, T, D = x.shape
    nc, _, fc2 = wu.shape
    tspec = pl.BlockSpec((1, tm, D), lambda b, t: (b, t, 0))
    const = lambda a: pl.BlockSpec(a.shape, lambda b, t: (0,) * a.ndim, pipeline_mode=pl.Buffered(1))
    sspec = pl.BlockSpec((1, nc, SUBLANES, fc2), lambda b, t: (b, 0, 0, 0))
    return pl.pallas_call(
        functools.partial(_ffn_kernel, tm=tm, final_norm=final_norm),
        grid=(B, T // tm),
        in_specs=[tspec, const(g), sspec, const(wu), const(cw), const(cb), const(wd), const(gf)],
        out_specs=[tspec, sspec],
        out_shape=[jax.ShapeDtypeStruct(x.shape, F32), jax.ShapeDtypeStruct((B, nc, SUBLANES, fc2), F32)],
        scratch_shapes=[pltpu.VMEM((2, tm + SUBLANES, fc2), F32), pltpu.VMEM((nc, SUBLANES, fc2), F32),
                        pltpu.VMEM((tm, D), F32)],
        compiler_params=_cparams(("arbitrary", "arbitrary")),
        name="conv_ffn",
    )(x, g, past, wu, cw, cb, wd, gf)


def _rope_tables(pos):
    half = ROT_DIM // 2
    freq = ROPE_THETA ** (-jnp.arange(half, dtype=F32) / half)
    ang = pos.astype(F32)[:, None] * freq[None, :]
    lane = np.arange(LANES) % HEAD_DIM
    cos = jnp.cos(ang)[:, lane % half]
    sin = jnp.sin(ang)[:, lane % half]
    rc = jnp.where(lane < ROT_DIM, cos, 1.0)
    rs1 = jnp.where(lane < half, -sin, 0.0)
    rs2 = jnp.where((lane >= half) & (lane < ROT_DIM), sin, 0.0)
    return rc, rs1, rs2


def _pack_layer(l, norm_mix, w_in, b_f, w_branch, w_gate, w_out, norm_ffn, w_up, conv_w, conv_b, w_down):
    D = w_in.shape[1]
    sizes = (BRANCH_W, HEAD_DIM, HEAD_DIM, N_IDX * IDX_DIM, IDX_DIM, N_IDX) + (BRANCH_W,) * 6 + (N_HEADS,)
    offs = np.concatenate([[0], np.cumsum(sizes)])
    (a_q, a_k, a_v, i_q, i_k, i_w, b_q, b_k, b_v, c_q, c_k, c_v, c_f) = (
        w_in[l][:, offs[i]:offs[i + 1]] for i in range(len(sizes)))
    zeros = lambda n: jnp.zeros((D, n), w_in.dtype)
    w = jnp.concatenate([a_q, i_q, a_k, a_k, i_k, i_k, a_v, a_v,
                         i_w, zeros(CF_LANE - N_IDX), c_f, zeros(LANES - CF_LANE - N_HEADS),
                         b_q, b_k, b_v, c_q, c_k, c_v], axis=1).astype(BF16)
    bf_row = jnp.zeros((1, LANES), F32).at[0, CF_LANE:CF_LANE + N_HEADS].set(b_f[l])

    d_ff = w_down.shape[1]
    fc = FF_CHUNK
    nc = d_ff // fc
    assert nc * fc == d_ff
    halves = lambda a: jnp.concatenate([a[..., :d_ff].reshape(a.shape[:-1] + (nc, fc)),
                                        a[..., d_ff:].reshape(a.shape[:-1] + (nc, fc))], axis=-1)
    wu = jnp.moveaxis(halves(w_up[l]), 1, 0).astype(BF16)
    cw = jnp.moveaxis(halves(conv_w[l]), 1, 0)
    cb = halves(conv_b[l])[:, None, :]
    wd = w_down[l].reshape(nc, fc, D).astype(BF16)
    return dict(g_mix=norm_mix[l][None], w=w, bf_row=bf_row, wg=w_gate[l].astype(BF16),
                wb=w_branch[l].astype(BF16), wo=w_out[l].astype(BF16), g_ffn=norm_ffn[l][None],
                wu=wu, cw=cw, cb=cb, wd=wd, halves=halves, nc=nc, d_ff=d_ff)


def _conv_state_in(state, halves):
    st = jnp.moveaxis(halves(state), 2, 1)
    return jnp.pad(st, ((0, 0), (0, 0), (SUBLANES - (CONV_W - 1), 0), (0, 0)))


def _conv_state_out(st, d_ff):
    st = st[:, :, SUBLANES - (CONV_W - 1):, :]
    B, nc, r, fc2 = st.shape
    fc = fc2 // 2
    gate = jnp.moveaxis(st[..., :fc], 1, 2).reshape(B, r, d_ff)
    up = jnp.moveaxis(st[..., fc:], 1, 2).reshape(B, r, d_ff)
    return jnp.concatenate([gate, up], axis=-1)


def _pad_keys(a, Lp):
    return jnp.pad(a, ((0, 0), (0, Lp - a.shape[1]), (0, 0)))


def _group_layer(x, past, lw, gf, final_norm, tiles):
    B, T, D = x.shape
    P = 0 if past is None else past[0].shape[1]
    L = P + T
    Lp = -(-L // KEY_TILE) * KEY_TILE
    topk = max(1, min(TOPK_MAX, L // 4))
    tables = _rope_tables(jnp.arange(P, L))

    (ak, av, ik, bk, bv, ck, cv, lf, iw,
     aq16, iq16, ak16, av16, ik16, bq16, bk16, bv16, cq16, ck16, cv16) = _project(
         x, lw["g_mix"], lw["w"], lw["bf_row"], tables, tiles["proj"])

    if past is None:
        keys = (ak16, av16, ik16, bk16, bv16, ck16, cv16)
        lf_all = lf
        conv_in = jnp.zeros((B, lw["nc"], SUBLANES, 2 * FF_CHUNK), F32)
    else:
        p_ak, p_av, p_ik, p_bk, p_bv, p_ck, p_cv, p_lf, p_conv = past
        dup = lambda a: jnp.concatenate([a, a], axis=-1).astype(BF16)
        flat = lambda a: a.reshape(B, P, BRANCH_W).astype(BF16)
        olds = (dup(p_ak), dup(p_av), dup(p_ik), flat(p_bk), flat(p_bv), flat(p_ck), flat(p_cv))
        news = (ak16, av16, ik16, bk16, bv16, ck16, cv16)
        keys = tuple(jnp.concatenate([o, n], axis=1) for o, n in zip(olds, news))
        lf_all = jnp.concatenate([p_lf, lf], axis=1)
        conv_in = _conv_state_in(p_conv, lw["halves"])
    ak2, av2, ik2, bkk, bvv, ckk, cvv = (_pad_keys(a, Lp) for a in keys)

    fk_t = _cumsum_time(jnp.swapaxes(_pad_keys(lf_all, Lp), 1, 2))
    fq = jnp.swapaxes(fk_t[:, :, P:L], 1, 2)

    o_a = _dsa(aq16, iq16, iw, ak2, av2, ik2, tq=tiles["dsa"], q_off=P, topk=topk)
    o_b = _sb(bq16, bkk, bvv, tq=tiles["sb"][0], tk=tiles["sb"][1], q_off=P)
    o_c = _fox(cq16, ckk, cvv, fq, fk_t, tq=tiles["fox"][0], tk=tiles["fox"][1], q_off=P)

    x = _merge(x, o_a, o_b, o_c, lw["g_mix"], lw["wg"], lw["wb"], lw["wo"], tiles["merge"])
    x, st = _ffn(x, lw["g_ffn"], conv_in, lw["wu"], lw["cw"], lw["cb"], lw["wd"], gf, tiles["ffn"], final_norm)

    heads = lambda a: a.reshape(B, T, N_HEADS, HEAD_DIM)
    rows = (ak, av, ik, heads(bk), heads(bv), heads(ck), heads(cv), lf, _conv_state_out(st, lw["d_ff"]))
    return x, rows


def _tiles(T, P):
    pick = lambda want: min(want, T)
    if T % 512 == 0 and P % 512 == 0:
        sb, fox = (256, 256), (256, 512)
    else:
        assert T <= KEY_TILE and P % KEY_TILE + T <= KEY_TILE
        sb = fox = (T, KEY_TILE)
    return dict(proj=pick(512), dsa=pick(128), sb=sb, fox=fox, merge=pick(512), ffn=pick(512))


def kernel(x_prompt, x_sample, cache_a_k, cache_a_v, cache_a_idx_k, cache_b_k, cache_b_v, cache_c_k, cache_c_v,
           cache_c_logf, state_ffn_conv, norm_mix, w_in, b_f, w_branch, w_gate, w_out, norm_ffn, w_up, conv_w,
           conv_b, w_down, norm_final):
    depth = w_in.shape[0]
    caches = (cache_a_k, cache_a_v, cache_a_idx_k, cache_b_k, cache_b_v, cache_c_k, cache_c_v, cache_c_logf,
              state_ffn_conv)
    layers = [_pack_layer(l, norm_mix, w_in, b_f, w_branch, w_gate, w_out, norm_ffn, w_up, conv_w, conv_b, w_down)
              for l in range(depth)]
    gf = norm_final[None]

    def trunk(x, past):
        tiles = _tiles(x.shape[1], 0 if past is None else past[0].shape[2])
        per_layer = []
        for l in range(depth):
            lp = None if past is None else tuple(c[l] for c in past)
            x, rows = _group_layer(x, lp, layers[l], gf, l == depth - 1, tiles)
            per_layer.append(rows)
        return x, tuple(jnp.stack(s, axis=0) for s in zip(*per_layer))

    y_prompt, p_state = trunk(x_prompt, None)
    y_sample, s_state = trunk(x_sample, caches)
    return (y_prompt, y_sample) + p_state + s_state
```

```python
import functools

import jax
import jax.numpy as jnp
import numpy as np
from jax import lax
from jax.experimental import pallas as pl
from jax.experimental.pallas import tpu as pltpu

F32 = jnp.float32
BF16 = jnp.bfloat16

LANES = 128
SUBLANES = 8
HEAD_DIM = 64
N_HEADS = 8
N_PAIRS = N_HEADS // 2
BRANCH_W = N_HEADS * HEAD_DIM
N_BRANCH = 3
N_IDX = 4
IDX_DIM = 64
CHUNK = 64
CHUNK_SHIFT = 6
TOPK_MAX = 256
ROT_DIM = HEAD_DIM // 4
ROPE_THETA = 500000.0
CONV_W = 3
EPS = 1e-6
NEG = -1e30
KEY_TILE = 128
FF_CHUNK = 256
VMEM_LIMIT = 56 * 1024 * 1024

ROPE_W = 1024
MISC_W = 256
CF_LANE = 8


def _cparams(sem):
    return pltpu.CompilerParams(dimension_semantics=sem, vmem_limit_bytes=VMEM_LIMIT)


def _rms(x, g):
    return x * lax.rsqrt(jnp.mean(x * x, axis=-1, keepdims=True) + EPS) * g


def _sigmoid(z):
    return 1.0 / (1.0 + jnp.exp(-z))


def _log_sigmoid(z):
    return jnp.minimum(z, 0.0) - jnp.log1p(jnp.exp(-jnp.abs(z)))


def _dot_t(a, b):
    return lax.dot_general(a, b, (((1,), (1,)), ((), ())), preferred_element_type=F32)


def _dot(a, b):
    return jnp.dot(a, b, preferred_element_type=F32)


def _keep_head(blk, which):
    lane = lax.broadcasted_iota(jnp.int32, blk.shape, blk.ndim - 1)
    keep = (lane >= HEAD_DIM) if which else (lane < HEAD_DIM)
    return jnp.where(keep, blk, jnp.zeros_like(blk))


def _merge_pair(o0, o1):
    lane = lax.broadcasted_iota(jnp.int32, o0.shape, o0.ndim - 1)
    return jnp.where(lane < HEAD_DIM, o0, o1)


N_CACHE_ROWS = 8


def _proj_kernel(x_ref, g_ref, w_ref, bf_ref, rc_ref, rs1_ref, rs2_ref, *refs):
    (ak_ref, av_ref, ik_ref, bk_ref, bv_ref, ck_ref, cv_ref, lf_ref, iw_ref,
     aq16, iq16, ak16, avt16, ik16, bq16, bk16, bv16, cq16, ck16, cv16) = refs[-(N_CACHE_ROWS + 12):]
    xn = _rms(x_ref[0], g_ref[...]).astype(BF16)
    rc, rs1, rs2 = rc_ref[...], rs1_ref[...], rs2_ref[...]

    def cols(c0, width):
        return _dot(xn, w_ref[:, c0:c0 + width])

    def group(blk, g):
        return blk[:, g * LANES:(g + 1) * LANES]

    def rope(blk):
        half = ROT_DIM // 2
        return blk * rc + pltpu.roll(blk, LANES - half, 1) * rs1 + pltpu.roll(blk, half, 1) * rs2

    wide = cols(0, BRANCH_W)
    for g in range(4):
        aq16[0, :, g * LANES:(g + 1) * LANES] = (rope(group(wide, g)) * HEAD_DIM ** -0.5).astype(BF16)
    wide = cols(BRANCH_W, ROPE_W - BRANCH_W)
    for g in range(2):
        iq16[0, :, g * LANES:(g + 1) * LANES] = (rope(group(wide, g)) * IDX_DIM ** -0.5).astype(BF16)
    blk = rope(group(wide, 2))
    ak_ref[0, 0] = blk[:, :HEAD_DIM]
    ak16[0] = blk.astype(BF16)
    blk = rope(group(wide, 3))
    ik_ref[0, 0] = blk[:, :IDX_DIM]
    ik16[0] = blk.astype(BF16)

    wide = cols(ROPE_W, MISC_W)
    blk = group(wide, 0)
    av_ref[0, 0] = blk[:, :HEAD_DIM]
    row = lax.broadcasted_iota(jnp.int32, (LANES, blk.shape[0]), 0)
    avt16[0] = jnp.where(row < HEAD_DIM, blk.T, 1.0).astype(BF16)
    blk = group(wide, 1)
    iw_ref[0] = blk[:, :SUBLANES] * N_IDX ** -0.5
    lf_ref[0, 0] = _log_sigmoid(blk[:, CF_LANE:CF_LANE + N_HEADS] + bf_ref[:, CF_LANE:CF_LANE + N_HEADS])

    base = ROPE_W + MISC_W
    outs = ((None, bq16), (bk_ref, bk16), (bv_ref, bv16), (None, cq16), (ck_ref, ck16), (cv_ref, cv16))
    for n, (o32, o16) in enumerate(outs):
        wide = cols(base + n * BRANCH_W, BRANCH_W)
        if o32 is None:
            o16[0] = (wide * HEAD_DIM ** -0.5).astype(BF16)
        else:
            o32[0, 0] = wide
            o16[0] = wide.astype(BF16)


def _project(x, g, w, bf_row, tables, tm, layer, depth, stacked):
    B, T, D = x.shape
    tok = lambda w_, dt: jax.ShapeDtypeStruct((B, T, w_), dt)
    tspec = lambda w_: pl.BlockSpec((1, tm, w_), lambda b, t: (b, t, 0))
    const = lambda a: pl.BlockSpec(a.shape, lambda b, t: (0,) * a.ndim, pipeline_mode=pl.Buffered(1))
    cache_w = (64, 64, 64, 512, 512, 512, 512, 8)
    cache_shapes = [jax.ShapeDtypeStruct((depth, B, T, w_), F32) for w_ in cache_w]
    cache_specs = [pl.BlockSpec((1, 1, tm, w_), lambda b, t: (layer, b, t, 0)) for w_ in cache_w]
    work = [(tok(w_, dt), tspec(w_)) for w_, dt in ((8, F32), (512, BF16), (256, BF16), (128, BF16))]
    work += [(jax.ShapeDtypeStruct((B, LANES, T), BF16), pl.BlockSpec((1, LANES, tm), lambda b, t: (b, 0, t)))]
    work += [(tok(w_, BF16), tspec(w_)) for w_ in (128,) + (512,) * 6]
    work_shapes, work_specs = [s for s, _ in work], [p for _, p in work]
    rspec = pl.BlockSpec((tm, LANES), lambda b, t: (t, 0))
    prev = () if stacked is None else tuple(stacked)
    n_in = 7
    outs = pl.pallas_call(
        _proj_kernel,
        grid=(B, T // tm),
        in_specs=[tspec(D), const(g), const(w), const(bf_row), rspec, rspec, rspec]
        + [pl.BlockSpec(memory_space=pl.ANY)] * len(prev),
        out_specs=cache_specs + work_specs,
        out_shape=cache_shapes + work_shapes,
        input_output_aliases={n_in + i: i for i in range(len(prev))},
        compiler_params=_cparams(("parallel", "parallel")),
        name="in_proj",
    )(x, g, w, bf_row, *tables, *prev)
    return outs[:N_CACHE_ROWS], outs[N_CACHE_ROWS:]


def _split3(x):
    hi = x.astype(BF16)
    r = x - hi.astype(F32)
    mid = r.astype(BF16)
    lo = (r - mid.astype(F32)).astype(BF16)
    return hi, mid, lo


def _cumsum_kernel(lf_ref, o_ref, *, blk):
    L = lf_ref.shape[-1]
    r = lax.broadcasted_iota(jnp.int32, (blk, blk), 0)
    c = lax.broadcasted_iota(jnp.int32, (blk, blk), 1)
    upper = jnp.where(r <= c, 1.0, 0.0).astype(BF16)
    carry = jnp.zeros((N_HEADS, 1), F32)
    for j in range(L // blk):
        hi, mid, lo = _split3(lf_ref[0, :, j * blk:(j + 1) * blk])
        f = _dot(hi, upper) + _dot(mid, upper) + _dot(lo, upper) + carry
        o_ref[0, :, j * blk:(j + 1) * blk] = f
        carry = f[:, blk - 1:blk]


def _cumsum_time(lf_t):
    B, H, L = lf_t.shape
    blk = LANES
    assert L % blk == 0
    spec = pl.BlockSpec((1, H, L), lambda b: (b, 0, 0))
    return pl.pallas_call(
        functools.partial(_cumsum_kernel, blk=blk),
        grid=(B,),
        in_specs=[spec],
        out_specs=spec,
        out_shape=jax.ShapeDtypeStruct((B, H, L), F32),
        compiler_params=_cparams(("parallel",)),
        name="logf_cumsum",
    )(lf_t)


def _head_column(blk, h):
    lane = lax.broadcasted_iota(jnp.int32, blk.shape, 1)
    return jnp.sum(jnp.where(lane == h, blk, 0.0), axis=1, keepdims=True)


def _positions(tq, tk, q0, k0):
    qpos = q0 + lax.broadcasted_iota(jnp.int32, (tq, tk), 0)
    kpos = k0 + lax.broadcasted_iota(jnp.int32, (tq, tk), 1)
    return qpos, kpos


def _pair_specs(tq, Lp):
    qspec = pl.BlockSpec((1, tq, LANES), lambda b, hp, qi: (b, qi, hp))
    kspec = pl.BlockSpec((1, Lp, LANES), lambda b, hp, qi: (b, 0, hp))
    return qspec, kspec


def _fox_kernel(q_ref, k_ref, v_ref, fq_ref, fk_ref, o_ref, m_ref, acc_ref, *, tq, tk, q_off):
    hp = pl.program_id(1)
    q0 = q_off + pl.program_id(2) * tq
    diag = q0 // tk
    qp = q_ref[0]
    fq_blk = fq_ref[0]
    qm = (_keep_head(qp, 0), _keep_head(qp, 1))
    fq = (_head_column(fq_blk, hp * 2), _head_column(fq_blk, hp * 2 + 1))
    ones = jnp.ones((tk, LANES), BF16)
    m_ref[...] = jnp.full(m_ref.shape, NEG, F32)
    acc_ref[...] = jnp.zeros(acc_ref.shape, F32)

    def score(j, masked):
        k0 = pl.multiple_of(j * tk, tk)
        kt = k_ref[0, pl.ds(k0, tk), :]
        if masked:
            qpos, kpos = _positions(tq, tk, q0, k0)
            causal = kpos <= qpos
        out = []
        for hh in range(2):
            fk = fk_ref[0, hp * 2 + hh, :, pl.ds(k0, tk)]
            s = _dot_t(qm[hh], kt) + fq[hh] - fk
            out.append(jnp.where(causal, s, NEG) if masked else s)
        return tuple(out)

    def fold(j, scores):
        v1 = jnp.concatenate([v_ref[0, pl.ds(pl.multiple_of(j * tk, tk), tk), :], ones], axis=1)
        for hh in range(2):
            s = scores[hh]
            m_old = m_ref[hh]
            m_new = jnp.maximum(m_old, jnp.max(s, axis=1, keepdims=True))
            p = jnp.exp(s - m_new).astype(BF16)
            acc_ref[hh] = jnp.exp(m_old - m_new) * acc_ref[hh] + _dot(p, v1)
            m_ref[hh] = m_new

    def body(i, scores):
        nxt = score(diag - 1 - i, False)
        fold(diag - i, scores)
        return nxt

    fold(0, lax.fori_loop(0, diag, body, score(diag, True)))
    o0 = acc_ref[0, :, :LANES] / acc_ref[0, :, LANES:]
    o1 = acc_ref[1, :, :LANES] / acc_ref[1, :, LANES:]
    o_ref[0] = _merge_pair(o0, o1).astype(o_ref.dtype)


def _fox(q, k, v, fq, fk_t, *, tq, tk, q_off):
    B, T, _ = q.shape
    Lp = k.shape[1]
    qspec, kspec = _pair_specs(tq, Lp)
    return pl.pallas_call(
        functools.partial(_fox_kernel, tq=tq, tk=tk, q_off=q_off),
        grid=(B, N_PAIRS, T // tq),
        in_specs=[qspec, kspec, kspec,
                  pl.BlockSpec((1, tq, N_HEADS), lambda b, hp, qi: (b, qi, 0)),
                  pl.BlockSpec((1, N_HEADS, 1, Lp), lambda b, hp, qi: (b, 0, 0, 0))],
        out_specs=qspec,
        out_shape=jax.ShapeDtypeStruct(q.shape, BF16),
        scratch_shapes=[pltpu.VMEM((2, tq, 1), F32), pltpu.VMEM((2, tq, 2 * LANES), F32)],
        compiler_params=_cparams(("parallel", "parallel", "parallel")),
        name="fox_attn",
    )(q, k, v, fq, fk_t[:, :, None, :])


def _sb_kernel(q_ref, k_ref, v_ref, o_ref, acc_ref, tail_ref, *, tq, tk, q_off):
    q0 = q_off + pl.program_id(2) * tq
    diag = q0 // tk
    qp = q_ref[0]
    qm = (_keep_head(qp, 0), _keep_head(qp, 1))
    r = jnp.bitwise_and(lax.broadcasted_iota(jnp.int32, (2 * LANES, 2 * LANES), 0), LANES - 1)
    c = lax.broadcasted_iota(jnp.int32, (2 * LANES, 2 * LANES), 1)
    suffix_w = jnp.where((c >= LANES) | (r > c), 1.0, 0.0).astype(BF16)
    acc_ref[...] = jnp.zeros(acc_ref.shape, F32)
    tail_ref[...] = jnp.zeros(tail_ref.shape, F32)

    def tile(j, masked):
        k0 = pl.multiple_of(j * tk, tk)
        kt = k_ref[0, pl.ds(k0, tk), :]
        vt = v_ref[0, pl.ds(k0, tk), :]
        if masked:
            qpos, kpos = _positions(tq, tk, q0, k0)
            strict = kpos < qpos
        for hh in range(2):
            z = _dot_t(qm[hh], kt)
            ls = jnp.minimum(z, 0.0) - jnp.log(1.0 + jnp.exp(-jnp.abs(z)))
            la = ls - z
            if masked:
                la = jnp.where(strict, la, 0.0)
                ls = jnp.where(strict, ls, NEG)
            hi = la.astype(BF16)
            lo = (la - hi.astype(F32)).astype(BF16)
            run = tail_ref[hh]
            after = [None] * (tk // LANES)
            for b in reversed(range(tk // LANES)):
                sl = slice(b * LANES, (b + 1) * LANES)
                res = _dot(jnp.concatenate([hi[:, sl], lo[:, sl]], axis=1), suffix_w)
                after[b] = res[:, :LANES] + run
                run = run + res[:, LANES:]
            tail_ref[hh] = run
            a = jnp.exp(ls + jnp.concatenate(after, axis=1))
            acc_ref[hh] += _dot(a.astype(BF16), vt)

    tile(diag, True)

    def body(i, _):
        tile(diag - 1 - i, False)
        return 0

    lax.fori_loop(0, diag, body, 0)
    o_ref[0] = _merge_pair(acc_ref[0], acc_ref[1]).astype(o_ref.dtype)


def _sb(q, k, v, *, tq, tk, q_off):
    B, T, _ = q.shape
    qspec, kspec = _pair_specs(tq, k.shape[1])
    return pl.pallas_call(
        functools.partial(_sb_kernel, tq=tq, tk=tk, q_off=q_off),
        grid=(B, N_PAIRS, T // tq),
        in_specs=[qspec, kspec, kspec],
        out_specs=qspec,
        out_shape=jax.ShapeDtypeStruct(q.shape, BF16),
        scratch_shapes=[pltpu.VMEM((2, tq, LANES), F32), pltpu.VMEM((2, tq, LANES), F32)],
        compiler_params=_cparams(("parallel", "parallel", "parallel")),
        name="sb_attn",
    )(q, k, v)


def _const_key(v):
    bits = int(np.float32(v).view(np.int32))
    return (bits ^ 0x7FFFFFFF) + 1 if bits < 0 else bits


def _float_key(x):
    bits = lax.bitcast_convert_type(x, jnp.int32)
    return jnp.where(bits < 0, jnp.bitwise_xor(bits, 0x7FFFFFFF) + 1, bits)


KEY_MIN = -2 ** 31
KEY_HALF_NEG = _const_key(0.5 * NEG)


KEY_MAX = 2 ** 31 - 1
KEY_BITS = 32
COUNT_CHAINS = 8


def _dsa_kernel(aq_ref, iq_ref, iwt_ref, k_ref, vt_ref, ik_ref, o_ref, key_ref, q8_ref, m_ref, acc_ref,
                *, tq, tk, q_off, n_rows, topk):
    q0 = q_off + pl.program_id(1) * tq
    n_tiles = (q0 + n_rows + tk - 1) // tk

    def tile_start(j):
        return pl.multiple_of(j * tk, LANES)

    iq = iq_ref[0]
    iwt = iwt_ref[0]
    qh = [_keep_head(iq[:, (h // 2) * LANES:(h // 2 + 1) * LANES], h % 2) for h in range(N_IDX)]
    q_chunk = jnp.right_shift(q0 + lax.broadcasted_iota(jnp.int32, (tk, tq), 1), CHUNK_SHIFT)

    def score_tile(j, _):
        k0 = tile_start(j)
        ik = ik_ref[0, pl.ds(k0, tk), :]
        sc = jnp.zeros((tk, tq), F32)
        for h in range(N_IDX):
            sc = sc + jnp.maximum(_dot_t(ik, qh[h]), 0.0) * iwt[h:h + 1, :]
        key_chunk = jnp.right_shift(k0 + lax.broadcasted_iota(jnp.int32, (tk, tq), 0), CHUNK_SHIFT)
        key_ref[pl.ds(k0, tk), :] = _float_key(jnp.where(key_chunk <= q_chunk, sc, NEG))
        return 0

    lax.fori_loop(0, n_tiles, score_tile, 0)

    def search():
        def step(_, cr):
            lo, hi, above_hi = cr
            mid = jnp.right_shift(lo, 1) + jnp.right_shift(hi, 1) + jnp.bitwise_and(jnp.bitwise_and(lo, hi), 1)

            def count_tile(j, cnt):
                above = jnp.where(key_ref[pl.ds(tile_start(j), tk), :] > mid, 1.0, 0.0)
                return cnt + jnp.sum(above.reshape((-1,) + cnt.shape), axis=0)

            cnt = lax.fori_loop(0, n_tiles, count_tile, jnp.zeros((COUNT_CHAINS, SUBLANES, tq), F32))
            cnt = jnp.sum(jnp.sum(cnt, axis=0), axis=0, keepdims=True)
            less = cnt < topk
            return jnp.where(less, lo, mid), jnp.where(less, mid, hi), jnp.where(less, cnt, above_hi)

        init = (jnp.full((1, tq), KEY_MIN, jnp.int32), jnp.full((1, tq), KEY_MAX, jnp.int32), jnp.zeros((1, tq), F32))
        _, hi, above_hi = lax.fori_loop(0, KEY_BITS, step, init)
        return hi, topk - above_hi

    thr, room = lax.cond(q0 + n_rows > topk, search,
                         lambda: (jnp.full((1, tq), KEY_MIN, jnp.int32), jnp.zeros((1, tq), F32)))
    row_ok = lax.broadcasted_iota(jnp.int32, (1, tq), 1) < n_rows
    room = jnp.where(row_ok, room, 0.0)

    @pl.when(jnp.max(room) > 0.0)
    def _():
        r = lax.broadcasted_iota(jnp.int32, (LANES, LANES), 0)
        c = lax.broadcasted_iota(jnp.int32, (LANES, LANES), 1)
        before = jnp.where(c < r, 1.0, 0.0).astype(BF16)

        def tie_tile(j, seen):
            k0 = tile_start(j)
            for b in range(tk // LANES):
                rows = pl.ds(k0 + b * LANES, LANES)
                keys = key_ref[rows, :]
                eq = keys == thr
                eqf = jnp.where(eq, 1.0, 0.0)
                rank = _dot(before, eqf.astype(BF16)) + seen
                key_ref[rows, :] = jnp.where(eq, jnp.where(rank < room, keys + 1, keys), keys)
                seen = seen + jnp.sum(eqf, axis=0, keepdims=True)
            return seen

        lax.fori_loop(0, n_tiles, tie_tile, jnp.zeros((1, tq), F32))

    aq = aq_ref[0]
    for h in range(N_HEADS):
        q8_ref[h * tq:(h + 1) * tq, :] = _keep_head(aq[:, (h // 2) * LANES:(h // 2 + 1) * LANES], h % 2)
    m_ref[...] = jnp.full(m_ref.shape, NEG, F32)
    acc_ref[...] = jnp.zeros(acc_ref.shape, F32)
    thr_sel = jnp.maximum(thr, KEY_HALF_NEG)

    def attend_tile(j, _):
        k0 = tile_start(j)
        kt = k_ref[0, pl.ds(k0, tk), :]
        v1t = vt_ref[0, :, pl.ds(k0, tk)]
        bias = jnp.where(key_ref[pl.ds(k0, tk), :] > thr_sel, 0.0, NEG)
        st = _dot_t(kt, q8_ref[...])
        st = jnp.concatenate([st[:, h * tq:(h + 1) * tq] + bias for h in range(N_HEADS)], axis=1)
        m_old = m_ref[...]
        m_new = jnp.maximum(m_old, jnp.max(st, axis=0, keepdims=True))
        p = jnp.exp(st - m_new).astype(BF16)
        acc_ref[...] = jnp.exp(m_old - m_new) * acc_ref[...] + _dot(v1t, p)
        m_ref[...] = m_new
        return 0

    lax.fori_loop(0, n_tiles, attend_tile, 0)
    for hp in range(N_PAIRS):
        pair = []
        for h in (2 * hp, 2 * hp + 1):
            a = acc_ref[:, h * tq:(h + 1) * tq]
            pair.append(a[:HEAD_DIM] / a[HEAD_DIM:])
        o_ref[0, :, hp * LANES:(hp + 1) * LANES] = jnp.concatenate(pair, axis=0).T.astype(o_ref.dtype)


def _dsa(aq, iq, iw, k2, v1t, ik2, *, tq, q_off, topk):
    B, T, _ = aq.shape
    Lp = k2.shape[1]
    tk = Lp if T <= LANES else next(t for t in (512, 384, 256, 128) if Lp % t == 0)
    Tp = -(-T // LANES) * LANES
    tq = max(tq, LANES)
    n_rows = min(T, tq)
    pad_t = lambda a: jnp.pad(a, ((0, 0), (0, Tp - T), (0, 0)))
    qspec = lambda w: pl.BlockSpec((1, tq, w), lambda b, qi: (b, qi, 0))
    kspec = pl.BlockSpec((1, Lp, LANES), lambda b, qi: (b, 0, 0))
    out = pl.pallas_call(
        functools.partial(_dsa_kernel, tq=tq, tk=tk, q_off=q_off, n_rows=n_rows, topk=topk),
        grid=(B, Tp // tq),
        in_specs=[qspec(BRANCH_W), qspec(N_IDX * IDX_DIM), pl.BlockSpec((1, SUBLANES, tq), lambda b, qi: (b, 0, qi)),
                  kspec, pl.BlockSpec((1, LANES, Lp), lambda b, qi: (b, 0, 0)), kspec],
        out_specs=qspec(BRANCH_W),
        out_shape=jax.ShapeDtypeStruct((B, Tp, BRANCH_W), BF16),
        scratch_shapes=[pltpu.VMEM((Lp, tq), jnp.int32), pltpu.VMEM((N_HEADS * tq, LANES), BF16),
                        pltpu.VMEM((1, N_HEADS * tq), F32), pltpu.VMEM((LANES, N_HEADS * tq), F32)],
        compiler_params=_cparams(("parallel", "parallel")),
        name="dsa_attn",
    )(pad_t(aq), pad_t(iq), jnp.swapaxes(pad_t(iw), 1, 2), k2, v1t, ik2)
    return out[:, :T]


def _merge_kernel(x_ref, oa_ref, ob_ref, oc_ref, g_ref, wg_ref, wb_ref, wo_ref, y_ref):
    x = x_ref[0]
    D = x.shape[-1]
    xn = _rms(x, g_ref[...]).astype(BF16)
    merged = jnp.zeros(x.shape, F32)
    for n, o_ref in enumerate((oa_ref, ob_ref, oc_ref)):
        gate = _sigmoid(_dot(xn, wg_ref[:, n * D:(n + 1) * D]))
        merged = merged + gate * _dot(o_ref[0], wb_ref[n])
    y_ref[0] = x + _dot(merged.astype(BF16), wo_ref[...])


def _merge(x, oa, ob, oc, g, wg, wb, wo, tm):
    B, T, D = x.shape
    tspec = lambda w_: pl.BlockSpec((1, tm, w_), lambda b, t: (b, t, 0))
    const = lambda a: pl.BlockSpec(a.shape, lambda b, t: (0,) * a.ndim)
    return pl.pallas_call(
        _merge_kernel,
        grid=(B, T // tm),
        in_specs=[tspec(D), tspec(BRANCH_W), tspec(BRANCH_W), tspec(BRANCH_W), const(g), const(wg), const(wb),
                  const(wo)],
        out_specs=tspec(D),
        out_shape=jax.ShapeDtypeStruct(x.shape, F32),
        compiler_params=_cparams(("parallel", "parallel")),
        name="merge_out",
    )(x, oa, ob, oc, g, wg, wb, wo)


def _ffn_kernel(x_ref, g_ref, past_ref, wu_ref, cw_ref, cb_ref, wd_ref, gf_ref, y_ref, st_ref,
                hbuf_ref, carry_ref, acc_ref, *, tm, final_norm):
    fc = FF_CHUNK
    n_chunks = wu_ref.shape[0]

    @pl.when(pl.program_id(1) == 0)
    def _():
        carry_ref[...] = past_ref[0]

    x = x_ref[0]
    xn = _rms(x, g_ref[...]).astype(BF16)
    acc_ref[...] = jnp.zeros(acc_ref.shape, F32)

    for c in range(n_chunks):
        hbuf = hbuf_ref.at[c % 2]
        h = _dot(xn, wu_ref[c])
        hbuf[0:SUBLANES, :] = carry_ref[c]
        hbuf[SUBLANES:SUBLANES + tm, :] = h
        carry_ref[c] = h[tm - SUBLANES:tm, :]
        cw = cw_ref[c]
        hc = (cb_ref[c] + cw[0:1] * hbuf[SUBLANES - 2:SUBLANES - 2 + tm, :]
              + cw[1:2] * hbuf[SUBLANES - 1:SUBLANES - 1 + tm, :] + cw[2:3] * h)
        gate, up = hc[:, :fc], hc[:, fc:]
        act = gate * _sigmoid(gate) * up
        acc_ref[...] += _dot(act.astype(BF16), wd_ref[c])
    st_ref[0] = carry_ref[...]
    y = x + acc_ref[...]
    if final_norm:
        y = _rms(y, gf_ref[...])
    y_ref[0] = y


def _ffn(x, g, past, wu, cw, cb, wd, gf, tm, final_norm):
    B, T, D = x.shape
    nc, _, fc2 = wu.shape
    tspec = pl.BlockSpec((1, tm, D), lambda b, t: (b, t, 0))
    const = lambda a: pl.BlockSpec(a.shape, lambda b, t: (0,) * a.ndim, pipeline_mode=pl.Buffered(1))
    sspec = pl.BlockSpec((1, nc, SUBLANES, fc2), lambda b, t: (b, 0, 0, 0))
    return pl.pallas_call(
        functools.partial(_ffn_kernel, tm=tm, final_norm=final_norm),
        grid=(B, T // tm),
        in_specs=[tspec, const(g), sspec, const(wu), const(cw), const(cb), const(wd), const(gf)],
        out_specs=[tspec, sspec],
        out_shape=[jax.ShapeDtypeStruct(x.shape, F32), jax.ShapeDtypeStruct((B, nc, SUBLANES, fc2), F32)],
        scratch_shapes=[pltpu.VMEM((2, tm + SUBLANES, fc2), F32), pltpu.VMEM((nc, SUBLANES, fc2), F32),
                        pltpu.VMEM((tm, D), F32)],
        compiler_params=_cparams(("arbitrary", "arbitrary")),
        name="conv_ffn",
    )(x, g, past, wu, cw, cb, wd, gf)


def _rope_tables(pos):
    half = ROT_DIM // 2
    freq = ROPE_THETA ** (-jnp.arange(half, dtype=F32) / half)
    ang = pos.astype(F32)[:, None] * freq[None, :]
    lane = np.arange(LANES) % HEAD_DIM
    cos = jnp.cos(ang)[:, lane % half]
    sin = jnp.sin(ang)[:, lane % half]
    rc = jnp.where(lane < ROT_DIM, cos, 1.0)
    rs1 = jnp.where(lane < half, -sin, 0.0)
    rs2 = jnp.where((lane >= half) & (lane < ROT_DIM), sin, 0.0)
    return rc, rs1, rs2


def _pack_layer(l, norm_mix, w_in, b_f, w_branch, w_gate, w_out, norm_ffn, w_up, conv_w, conv_b, w_down):
    D = w_in.shape[1]
    sizes = (BRANCH_W, HEAD_DIM, HEAD_DIM, N_IDX * IDX_DIM, IDX_DIM, N_IDX) + (BRANCH_W,) * 6 + (N_HEADS,)
    offs = np.concatenate([[0], np.cumsum(sizes)])
    (a_q, a_k, a_v, i_q, i_k, i_w, b_q, b_k, b_v, c_q, c_k, c_v, c_f) = (
        w_in[l][:, offs[i]:offs[i + 1]] for i in range(len(sizes)))
    zeros = lambda n: jnp.zeros((D, n), w_in.dtype)
    w = jnp.concatenate([a_q, i_q, a_k, a_k, i_k, i_k, a_v, a_v,
                         i_w, zeros(CF_LANE - N_IDX), c_f, zeros(LANES - CF_LANE - N_HEADS),
                         b_q, b_k, b_v, c_q, c_k, c_v], axis=1).astype(BF16)
    bf_row = jnp.zeros((1, LANES), F32).at[0, CF_LANE:CF_LANE + N_HEADS].set(b_f[l])

    d_ff = w_down.shape[1]
    fc = FF_CHUNK
    nc = d_ff // fc
    assert nc * fc == d_ff
    halves = lambda a: jnp.concatenate([a[..., :d_ff].reshape(a.shape[:-1] + (nc, fc)),
                                        a[..., d_ff:].reshape(a.shape[:-1] + (nc, fc))], axis=-1)
    wu = jnp.moveaxis(halves(w_up[l]), 1, 0).astype(BF16)
    cw = jnp.moveaxis(halves(conv_w[l]), 1, 0)
    cb = halves(conv_b[l])[:, None, :]
    wd = w_down[l].reshape(nc, fc, D).astype(BF16)
    return dict(g_mix=norm_mix[l][None], w=w, bf_row=bf_row, wg=w_gate[l].astype(BF16),
                wb=w_branch[l].astype(BF16), wo=w_out[l].astype(BF16), g_ffn=norm_ffn[l][None],
                wu=wu, cw=cw, cb=cb, wd=wd, halves=halves, nc=nc, d_ff=d_ff)


def _conv_state_in(state, halves):
    st = jnp.moveaxis(halves(state), 2, 1)
    return jnp.pad(st, ((0, 0), (0, 0), (SUBLANES - (CONV_W - 1), 0), (0, 0)))


def _conv_state_out(st, d_ff):
    st = st[:, :, SUBLANES - (CONV_W - 1):, :]
    B, nc, r, fc2 = st.shape
    fc = fc2 // 2
    gate = jnp.moveaxis(st[..., :fc], 1, 2).reshape(B, r, d_ff)
    up = jnp.moveaxis(st[..., fc:], 1, 2).reshape(B, r, d_ff)
    return jnp.concatenate([gate, up], axis=-1)


def _pad_keys(a, Lp):
    return jnp.pad(a, ((0, 0), (0, Lp - a.shape[1]), (0, 0)))


def _group_layer(x, past, lw, gf, layer, depth, stacked, tiles):
    B, T, D = x.shape
    P = 0 if past is None else past[0].shape[1]
    L = P + T
    Lp = -(-L // KEY_TILE) * KEY_TILE
    topk = max(1, min(TOPK_MAX, L // 4))
    tables = _rope_tables(jnp.arange(P, L))

    stacked, (iw, aq16, iq16, ak16, avt16, ik16, bq16, bk16, bv16, cq16, ck16, cv16) = _project(
        x, lw["g_mix"], lw["w"], lw["bf_row"], tables, tiles["proj"], layer, depth, stacked)
    lf = stacked[-1][layer]

    if past is None:
        keys = (ak16, ik16, bk16, bv16, ck16, cv16)
        lf_all = lf
        conv_in = jnp.zeros((B, lw["nc"], SUBLANES, 2 * FF_CHUNK), F32)
    else:
        p_ak, p_av, p_ik, p_bk, p_bv, p_ck, p_cv, p_lf, p_conv = past
        dup = lambda a: jnp.concatenate([a, a], axis=-1).astype(BF16)
        flat = lambda a: a.reshape(B, P, BRANCH_W).astype(BF16)
        olds = (dup(p_ak), dup(p_ik), flat(p_bk), flat(p_bv), flat(p_ck), flat(p_cv))
        news = (ak16, ik16, bk16, bv16, ck16, cv16)
        keys = tuple(jnp.concatenate([o, n], axis=1) for o, n in zip(olds, news))
        old_vt = jnp.swapaxes(p_av, 1, 2).astype(BF16)
        avt16 = jnp.concatenate([jnp.concatenate([old_vt, jnp.ones_like(old_vt)], axis=1), avt16], axis=2)
        lf_all = jnp.concatenate([p_lf, lf], axis=1)
        conv_in = _conv_state_in(p_conv, lw["halves"])
    ak2, ik2, bkk, bvv, ckk, cvv = (_pad_keys(a, Lp) for a in keys)
    v1t = jnp.pad(avt16, ((0, 0), (0, 0), (0, Lp - L)))

    fk_t = _cumsum_time(jnp.swapaxes(_pad_keys(lf_all, Lp), 1, 2))
    fq = jnp.swapaxes(fk_t[:, :, P:L], 1, 2)

    o_a = _dsa(aq16, iq16, iw, ak2, v1t, ik2, tq=tiles["dsa"], q_off=P, topk=topk)
    o_b = _sb(bq16, bkk, bvv, tq=tiles["sb"][0], tk=tiles["sb"][1], q_off=P)
    o_c = _fox(cq16, ckk, cvv, fq, fk_t, tq=tiles["fox"][0], tk=tiles["fox"][1], q_off=P)

    x = _merge(x, o_a, o_b, o_c, lw["g_mix"], lw["wg"], lw["wb"], lw["wo"], tiles["merge"])
    x, st = _ffn(x, lw["g_ffn"], conv_in, lw["wu"], lw["cw"], lw["cb"], lw["wd"], gf, tiles["ffn"],
                 layer == depth - 1)
    return x, stacked, _conv_state_out(st, lw["d_ff"])


def _tiles(T, P):
    pick = lambda want: min(want, T)
    if T % 512 == 0 and P % 512 == 0:
        sb, fox = (256, 256), (256, 512)
    else:
        assert T <= KEY_TILE
        sb = fox = (T, -(-(P + T) // KEY_TILE) * KEY_TILE)
    return dict(proj=pick(512), dsa=pick(128), sb=sb, fox=fox, merge=pick(512), ffn=pick(512))


def kernel(x_prompt, x_sample, cache_a_k, cache_a_v, cache_a_idx_k, cache_b_k, cache_b_v, cache_c_k, cache_c_v,
           cache_c_logf, state_ffn_conv, norm_mix, w_in, b_f, w_branch, w_gate, w_out, norm_ffn, w_up, conv_w,
           conv_b, w_down, norm_final):
    depth = w_in.shape[0]
    caches = (cache_a_k, cache_a_v, cache_a_idx_k, cache_b_k, cache_b_v, cache_c_k, cache_c_v, cache_c_logf,
              state_ffn_conv)
    layers = [_pack_layer(l, norm_mix, w_in, b_f, w_branch, w_gate, w_out, norm_ffn, w_up, conv_w, conv_b, w_down)
              for l in range(depth)]
    gf = norm_final[None]

    def trunk(x, past):
        B, T, _ = x.shape
        tiles = _tiles(T, 0 if past is None else past[0].shape[2])
        stacked, conv_states = None, []
        for l in range(depth):
            lp = None if past is None else tuple(c[l] for c in past)
            x, stacked, conv_state = _group_layer(x, lp, layers[l], gf, l, depth, stacked, tiles)
            conv_states.append(conv_state)
        ak, av, ik, bk, bv, ck, cv, lf = stacked
        heads = lambda a: a.reshape(depth, B, T, N_HEADS, HEAD_DIM)
        return x, (ak, av, ik, heads(bk), heads(bv), heads(ck), heads(cv), lf, jnp.stack(conv_states, axis=0))

    y_prompt, p_state = trunk(x_prompt, None)
    y_sample, s_state = trunk(x_sample, caches)
    return (y_prompt, y_sample) + p_state + s_state
```

```python
import functools

import jax
import jax.numpy as jnp
import numpy as np
from jax import lax
from jax.experimental import pallas as pl
from jax.experimental.pallas import tpu as pltpu

F32 = jnp.float32
BF16 = jnp.bfloat16

LANES = 128
SUBLANES = 8
HEAD_DIM = 64
N_HEADS = 8
N_PAIRS = N_HEADS // 2
BRANCH_W = N_HEADS * HEAD_DIM
N_BRANCH = 3
N_IDX = 4
IDX_DIM = 64
CHUNK = 64
CHUNK_SHIFT = 6
TOPK_MAX = 256
ROT_DIM = HEAD_DIM // 4
ROPE_THETA = 500000.0
CONV_W = 3
EPS = 1e-6
NEG = -1e30
KEY_TILE = 128
FF_CHUNK = 256
VMEM_LIMIT = 56 * 1024 * 1024

ROPE_W = 1024
MISC_W = 256
CF_LANE = 8


def _cparams(sem):
    return pltpu.CompilerParams(dimension_semantics=sem, vmem_limit_bytes=VMEM_LIMIT)


def _rms(x, g):
    return x * lax.rsqrt(jnp.mean(x * x, axis=-1, keepdims=True) + EPS) * g


def _sigmoid(z):
    return 1.0 / (1.0 + jnp.exp(-z))


def _log_sigmoid(z):
    return jnp.minimum(z, 0.0) - jnp.log1p(jnp.exp(-jnp.abs(z)))


def _dot_t(a, b):
    return lax.dot_general(a, b, (((1,), (1,)), ((), ())), preferred_element_type=F32)


def _dot(a, b):
    return jnp.dot(a, b, preferred_element_type=F32)


def _keep_head(blk, which):
    lane = lax.broadcasted_iota(jnp.int32, blk.shape, blk.ndim - 1)
    keep = (lane >= HEAD_DIM) if which else (lane < HEAD_DIM)
    return jnp.where(keep, blk, jnp.zeros_like(blk))


def _merge_pair(o0, o1):
    lane = lax.broadcasted_iota(jnp.int32, o0.shape, o0.ndim - 1)
    return jnp.where(lane < HEAD_DIM, o0, o1)


N_CACHE_ROWS = 8


def _proj_kernel(x_ref, g_ref, w_ref, bf_ref, rc_ref, rs1_ref, rs2_ref, *refs):
    (ak_ref, av_ref, ik_ref, bk_ref, bv_ref, ck_ref, cv_ref, lf_ref, iw_ref,
     aq16, iq16, ak16, avt16, ik16, bq16, bk16, bv16, cq16, ck16, cv16) = refs[-(N_CACHE_ROWS + 12):]
    xn = _rms(x_ref[0], g_ref[...]).astype(BF16)
    rc, rs1, rs2 = rc_ref[...], rs1_ref[...], rs2_ref[...]

    def cols(c0, width):
        return _dot(xn, w_ref[:, c0:c0 + width])

    def group(blk, g):
        return blk[:, g * LANES:(g + 1) * LANES]

    def rope(blk):
        half = ROT_DIM // 2
        return blk * rc + pltpu.roll(blk, LANES - half, 1) * rs1 + pltpu.roll(blk, half, 1) * rs2

    wide = cols(0, BRANCH_W)
    for g in range(4):
        aq16[0, :, g * LANES:(g + 1) * LANES] = (rope(group(wide, g)) * HEAD_DIM ** -0.5).astype(BF16)
    wide = cols(BRANCH_W, ROPE_W - BRANCH_W)
    for g in range(2):
        iq16[0, :, g * LANES:(g + 1) * LANES] = (rope(group(wide, g)) * IDX_DIM ** -0.5).astype(BF16)
    blk = rope(group(wide, 2))
    ak_ref[0, 0] = blk[:, :HEAD_DIM]
    ak16[0] = blk.astype(BF16)
    blk = rope(group(wide, 3))
    ik_ref[0, 0] = blk[:, :IDX_DIM]
    ik16[0] = blk.astype(BF16)

    wide = cols(ROPE_W, MISC_W)
    blk = group(wide, 0)
    av_ref[0, 0] = blk[:, :HEAD_DIM]
    row = lax.broadcasted_iota(jnp.int32, (LANES, blk.shape[0]), 0)
    avt16[0] = jnp.where(row < HEAD_DIM, blk.T, 1.0).astype(BF16)
    blk = group(wide, 1)
    iw_ref[0] = blk[:, :SUBLANES] * N_IDX ** -0.5
    lf_ref[0, 0] = _log_sigmoid(blk[:, CF_LANE:CF_LANE + N_HEADS] + bf_ref[:, CF_LANE:CF_LANE + N_HEADS])

    base = ROPE_W + MISC_W
    outs = ((None, bq16), (bk_ref, bk16), (bv_ref, bv16), (None, cq16), (ck_ref, ck16), (cv_ref, cv16))
    for n, (o32, o16) in enumerate(outs):
        wide = cols(base + n * BRANCH_W, BRANCH_W)
        if o32 is None:
            o16[0] = (wide * HEAD_DIM ** -0.5).astype(BF16)
        else:
            o32[0, 0] = wide
            o16[0] = wide.astype(BF16)


def _project(x, g, w, bf_row, tables, tm, layer, depth, stacked):
    B, T, D = x.shape
    tok = lambda w_, dt: jax.ShapeDtypeStruct((B, T, w_), dt)
    tspec = lambda w_: pl.BlockSpec((1, tm, w_), lambda b, t: (b, t, 0))
    const = lambda a: pl.BlockSpec(a.shape, lambda b, t: (0,) * a.ndim, pipeline_mode=pl.Buffered(1))
    cache_w = (64, 64, 64, 512, 512, 512, 512, 8)
    cache_shapes = [jax.ShapeDtypeStruct((depth, B, T, w_), F32) for w_ in cache_w]
    cache_specs = [pl.BlockSpec((1, 1, tm, w_), lambda b, t: (layer, b, t, 0)) for w_ in cache_w]
    work = [(tok(w_, dt), tspec(w_)) for w_, dt in ((8, F32), (512, BF16), (256, BF16), (128, BF16))]
    work += [(jax.ShapeDtypeStruct((B, LANES, T), BF16), pl.BlockSpec((1, LANES, tm), lambda b, t: (b, 0, t)))]
    work += [(tok(w_, BF16), tspec(w_)) for w_ in (128,) + (512,) * 6]
    work_shapes, work_specs = [s for s, _ in work], [p for _, p in work]
    rspec = pl.BlockSpec((tm, LANES), lambda b, t: (t, 0))
    prev = () if stacked is None else tuple(stacked)
    n_in = 7
    outs = pl.pallas_call(
        _proj_kernel,
        grid=(B, T // tm),
        in_specs=[tspec(D), const(g), const(w), const(bf_row), rspec, rspec, rspec]
        + [pl.BlockSpec(memory_space=pl.ANY)] * len(prev),
        out_specs=cache_specs + work_specs,
        out_shape=cache_shapes + work_shapes,
        input_output_aliases={n_in + i: i for i in range(len(prev))},
        compiler_params=_cparams(("parallel", "parallel")),
        name="in_proj",
    )(x, g, w, bf_row, *tables, *prev)
    return outs[:N_CACHE_ROWS], outs[N_CACHE_ROWS:]


def _split3(x):
    hi = x.astype(BF16)
    r = x - hi.astype(F32)
    mid = r.astype(BF16)
    lo = (r - mid.astype(F32)).astype(BF16)
    return hi, mid, lo


def _cumsum_kernel(lf_ref, o_ref, *, blk):
    L = lf_ref.shape[-1]
    r = lax.broadcasted_iota(jnp.int32, (blk, blk), 0)
    c = lax.broadcasted_iota(jnp.int32, (blk, blk), 1)
    upper = jnp.where(r <= c, 1.0, 0.0).astype(BF16)
    carry = jnp.zeros((N_HEADS, 1), F32)
    for j in range(L // blk):
        hi, mid, lo = _split3(lf_ref[0, :, j * blk:(j + 1) * blk])
        f = _dot(hi, upper) + _dot(mid, upper) + _dot(lo, upper) + carry
        o_ref[0, :, j * blk:(j + 1) * blk] = f
        carry = f[:, blk - 1:blk]


def _cumsum_time(lf_t):
    B, H, L = lf_t.shape
    blk = LANES
    assert L % blk == 0
    spec = pl.BlockSpec((1, H, L), lambda b: (b, 0, 0))
    return pl.pallas_call(
        functools.partial(_cumsum_kernel, blk=blk),
        grid=(B,),
        in_specs=[spec],
        out_specs=spec,
        out_shape=jax.ShapeDtypeStruct((B, H, L), F32),
        compiler_params=_cparams(("parallel",)),
        name="logf_cumsum",
    )(lf_t)


def _head_column(blk, h):
    lane = lax.broadcasted_iota(jnp.int32, blk.shape, 1)
    return jnp.sum(jnp.where(lane == h, blk, 0.0), axis=1, keepdims=True)


def _positions(tq, tk, q0, k0):
    qpos = q0 + lax.broadcasted_iota(jnp.int32, (tq, tk), 0)
    kpos = k0 + lax.broadcasted_iota(jnp.int32, (tq, tk), 1)
    return qpos, kpos


def _pair_specs(tq, Lp):
    qspec = pl.BlockSpec((1, tq, LANES), lambda b, hp, qi: (b, qi, hp))
    kspec = pl.BlockSpec((1, Lp, LANES), lambda b, hp, qi: (b, 0, hp))
    return qspec, kspec


def _fox_kernel(q_ref, k_ref, v_ref, fq_ref, fk_ref, o_ref, m_ref, acc_ref, *, tq, tk, q_off):
    hp = pl.program_id(1)
    q0 = q_off + pl.program_id(2) * tq
    diag = q0 // tk
    qp = q_ref[0]
    fq_blk = fq_ref[0]
    qm = (_keep_head(qp, 0), _keep_head(qp, 1))
    fq = (_head_column(fq_blk, hp * 2), _head_column(fq_blk, hp * 2 + 1))
    ones = jnp.ones((tk, LANES), BF16)
    m_ref[...] = jnp.full(m_ref.shape, NEG, F32)
    acc_ref[...] = jnp.zeros(acc_ref.shape, F32)

    def score(j, masked):
        k0 = pl.multiple_of(j * tk, tk)
        kt = k_ref[0, pl.ds(k0, tk), :]
        if masked:
            qpos, kpos = _positions(tq, tk, q0, k0)
            causal = kpos <= qpos
        out = []
        for hh in range(2):
            fk = fk_ref[0, hp * 2 + hh, :, pl.ds(k0, tk)]
            s = _dot_t(qm[hh], kt) + fq[hh] - fk
            out.append(jnp.where(causal, s, NEG) if masked else s)
        return tuple(out)

    def fold(j, scores):
        v1 = jnp.concatenate([v_ref[0, pl.ds(pl.multiple_of(j * tk, tk), tk), :], ones], axis=1)
        m_old = [m_ref[hh] for hh in range(2)]
        m_new = [jnp.maximum(m_old[hh], jnp.max(scores[hh], axis=1, keepdims=True)) for hh in range(2)]
        p = [jnp.exp(scores[hh] - m_new[hh]).astype(BF16) for hh in range(2)]
        for hh in range(2):
            acc_ref[hh] = jnp.exp(m_old[hh] - m_new[hh]) * acc_ref[hh] + _dot(p[hh], v1)
            m_ref[hh] = m_new[hh]

    fold(diag, score(diag, True))

    def body(i, _):
        j = diag - 1 - i
        fold(j, score(j, False))
        return 0

    lax.fori_loop(0, diag, body, 0)
    o0 = acc_ref[0, :, :LANES] / acc_ref[0, :, LANES:]
    o1 = acc_ref[1, :, :LANES] / acc_ref[1, :, LANES:]
    o_ref[0] = _merge_pair(o0, o1).astype(o_ref.dtype)


def _fox(q, k, v, fq, fk_t, *, tq, tk, q_off):
    B, T, _ = q.shape
    Lp = k.shape[1]
    qspec, kspec = _pair_specs(tq, Lp)
    return pl.pallas_call(
        functools.partial(_fox_kernel, tq=tq, tk=tk, q_off=q_off),
        grid=(B, N_PAIRS, T // tq),
        in_specs=[qspec, kspec, kspec,
                  pl.BlockSpec((1, tq, N_HEADS), lambda b, hp, qi: (b, qi, 0)),
                  pl.BlockSpec((1, N_HEADS, 1, Lp), lambda b, hp, qi: (b, 0, 0, 0))],
        out_specs=qspec,
        out_shape=jax.ShapeDtypeStruct(q.shape, BF16),
        scratch_shapes=[pltpu.VMEM((2, tq, 1), F32), pltpu.VMEM((2, tq, 2 * LANES), F32)],
        compiler_params=_cparams(("parallel", "parallel", "parallel")),
        name="fox_attn",
    )(q, k, v, fq, fk_t[:, :, None, :])


def _sb_kernel(q_ref, k_ref, v_ref, o_ref, acc_ref, tail_ref, *, tq, tk, q_off):
    q0 = q_off + pl.program_id(2) * tq
    diag = q0 // tk
    qp = q_ref[0]
    qm = (_keep_head(qp, 0), _keep_head(qp, 1))
    r = jnp.bitwise_and(lax.broadcasted_iota(jnp.int32, (2 * LANES, 2 * LANES), 0), LANES - 1)
    c = lax.broadcasted_iota(jnp.int32, (2 * LANES, 2 * LANES), 1)
    suffix_w = jnp.where((c >= LANES) | (r > c), 1.0, 0.0).astype(BF16)
    acc_ref[...] = jnp.zeros(acc_ref.shape, F32)
    tail_ref[...] = jnp.zeros(tail_ref.shape, F32)

    def tile(j, masked):
        k0 = pl.multiple_of(j * tk, tk)
        kt = k_ref[0, pl.ds(k0, tk), :]
        vt = v_ref[0, pl.ds(k0, tk), :]
        if masked:
            qpos, kpos = _positions(tq, tk, q0, k0)
            strict = kpos < qpos
        heads = (0, 1)
        z = [_dot_t(qm[hh], kt) for hh in heads]
        ls, hi, lo, after = [None, None], [None, None], [None, None], [None, None]
        for hh in heads:
            ls[hh] = jnp.minimum(z[hh], 0.0) - jnp.log(1.0 + jnp.exp(-jnp.abs(z[hh])))
            la = ls[hh] - z[hh]
            if masked:
                la = jnp.where(strict, la, 0.0)
                ls[hh] = jnp.where(strict, ls[hh], NEG)
            hi[hh] = la.astype(BF16)
            lo[hh] = (la - hi[hh].astype(F32)).astype(BF16)
        for hh in heads:
            run = tail_ref[hh]
            blocks = [None] * (tk // LANES)
            for b in reversed(range(tk // LANES)):
                sl = slice(b * LANES, (b + 1) * LANES)
                res = _dot(jnp.concatenate([hi[hh][:, sl], lo[hh][:, sl]], axis=1), suffix_w)
                blocks[b] = res[:, :LANES] + run
                run = run + res[:, LANES:]
            tail_ref[hh] = run
            after[hh] = jnp.concatenate(blocks, axis=1)
        a = [jnp.exp(ls[hh] + after[hh]).astype(BF16) for hh in heads]
        for hh in heads:
            acc_ref[hh] += _dot(a[hh], vt)

    tile(diag, True)

    def body(i, _):
        tile(diag - 1 - i, False)
        return 0

    lax.fori_loop(0, diag, body, 0)
    o_ref[0] = _merge_pair(acc_ref[0], acc_ref[1]).astype(o_ref.dtype)


def _sb(q, k, v, *, tq, tk, q_off):
    B, T, _ = q.shape
    qspec, kspec = _pair_specs(tq, k.shape[1])
    return pl.pallas_call(
        functools.partial(_sb_kernel, tq=tq, tk=tk, q_off=q_off),
        grid=(B, N_PAIRS, T // tq),
        in_specs=[qspec, kspec, kspec],
        out_specs=qspec,
        out_shape=jax.ShapeDtypeStruct(q.shape, BF16),
        scratch_shapes=[pltpu.VMEM((2, tq, LANES), F32), pltpu.VMEM((2, tq, LANES), F32)],
        compiler_params=_cparams(("parallel", "parallel", "parallel")),
        name="sb_attn",
    )(q, k, v)


def _const_key(v):
    bits = int(np.float32(v).view(np.int32))
    return (bits ^ 0x7FFFFFFF) + 1 if bits < 0 else bits


def _float_key(x):
    bits = lax.bitcast_convert_type(x, jnp.int32)
    return jnp.where(bits < 0, jnp.bitwise_xor(bits, 0x7FFFFFFF) + 1, bits)


KEY_MIN = -2 ** 31
KEY_HALF_NEG = _const_key(0.5 * NEG)


KEY_MAX = 2 ** 31 - 1
KEY_BITS = 32
COUNT_CHAINS = 8


def _dsa_kernel(aq_ref, iq_ref, iwt_ref, k_ref, vt_ref, ik_ref, o_ref, key_ref, q8_ref, m_ref, acc_ref,
                *, tq, tk, q_off, n_rows, topk):
    q0 = q_off + pl.program_id(1) * tq
    n_tiles = (q0 + n_rows + tk - 1) // tk

    def tile_start(j):
        return pl.multiple_of(j * tk, LANES)

    iq = iq_ref[0]
    iwt = iwt_ref[0]
    qh = [_keep_head(iq[:, (h // 2) * LANES:(h // 2 + 1) * LANES], h % 2) for h in range(N_IDX)]
    q_chunk = jnp.right_shift(q0 + lax.broadcasted_iota(jnp.int32, (tk, tq), 1), CHUNK_SHIFT)

    def score_tile(j, _):
        k0 = tile_start(j)
        ik = ik_ref[0, pl.ds(k0, tk), :]
        sc = jnp.zeros((tk, tq), F32)
        for h in range(N_IDX):
            sc = sc + jnp.maximum(_dot_t(ik, qh[h]), 0.0) * iwt[h:h + 1, :]
        key_chunk = jnp.right_shift(k0 + lax.broadcasted_iota(jnp.int32, (tk, tq), 0), CHUNK_SHIFT)
        key_ref[pl.ds(k0, tk), :] = _float_key(jnp.where(key_chunk <= q_chunk, sc, NEG))
        return 0

    lax.fori_loop(0, n_tiles, score_tile, 0)

    def search():
        def step(_, cr):
            lo, hi, above_hi = cr
            mid = jnp.right_shift(lo, 1) + jnp.right_shift(hi, 1) + jnp.bitwise_and(jnp.bitwise_and(lo, hi), 1)

            def count_tile(j, cnt):
                above = jnp.where(key_ref[pl.ds(tile_start(j), tk), :] > mid, 1.0, 0.0)
                return cnt + jnp.sum(above.reshape((-1,) + cnt.shape), axis=0)

            cnt = lax.fori_loop(0, n_tiles, count_tile, jnp.zeros((COUNT_CHAINS, SUBLANES, tq), F32))
            cnt = jnp.sum(jnp.sum(cnt, axis=0), axis=0, keepdims=True)
            less = cnt < topk
            return jnp.where(less, lo, mid), jnp.where(less, mid, hi), jnp.where(less, cnt, above_hi)

        init = (jnp.full((1, tq), KEY_MIN, jnp.int32), jnp.full((1, tq), KEY_MAX, jnp.int32), jnp.zeros((1, tq), F32))
        _, hi, above_hi = lax.fori_loop(0, KEY_BITS, step, init)
        return hi, topk - above_hi

    thr, room = lax.cond(q0 + n_rows > topk, search,
                         lambda: (jnp.full((1, tq), KEY_MIN, jnp.int32), jnp.zeros((1, tq), F32)))
    row_ok = lax.broadcasted_iota(jnp.int32, (1, tq), 1) < n_rows
    room = jnp.where(row_ok, room, 0.0)

    @pl.when(jnp.max(room) > 0.0)
    def _():
        r = lax.broadcasted_iota(jnp.int32, (LANES, LANES), 0)
        c = lax.broadcasted_iota(jnp.int32, (LANES, LANES), 1)
        before = jnp.where(c < r, 1.0, 0.0).astype(BF16)

        def tie_tile(j, seen):
            k0 = tile_start(j)
            for b in range(tk // LANES):
                rows = pl.ds(k0 + b * LANES, LANES)
                keys = key_ref[rows, :]
                eq = keys == thr
                eqf = jnp.where(eq, 1.0, 0.0)
                rank = _dot(before, eqf.astype(BF16)) + seen
                key_ref[rows, :] = jnp.where(eq, jnp.where(rank < room, keys + 1, keys), keys)
                seen = seen + jnp.sum(eqf, axis=0, keepdims=True)
            return seen

        lax.fori_loop(0, n_tiles, tie_tile, jnp.zeros((1, tq), F32))

    aq = aq_ref[0]
    for h in range(N_HEADS):
        q8_ref[h * tq:(h + 1) * tq, :] = _keep_head(aq[:, (h // 2) * LANES:(h // 2 + 1) * LANES], h % 2)
    m_ref[...] = jnp.full(m_ref.shape, NEG, F32)
    acc_ref[...] = jnp.zeros(acc_ref.shape, F32)
    thr_sel = jnp.maximum(thr, KEY_HALF_NEG)

    def attend_tile(j, _):
        k0 = tile_start(j)
        kt = k_ref[0, pl.ds(k0, tk), :]
        v1t = vt_ref[0, :, pl.ds(k0, tk)]
        bias = jnp.where(key_ref[pl.ds(k0, tk), :] > thr_sel, 0.0, NEG)
        st = _dot_t(kt, q8_ref[...])
        st = jnp.concatenate([st[:, h * tq:(h + 1) * tq] + bias for h in range(N_HEADS)], axis=1)
        m_old = m_ref[...]
        m_new = jnp.maximum(m_old, jnp.max(st, axis=0, keepdims=True))
        p = jnp.exp(st - m_new).astype(BF16)
        acc_ref[...] = jnp.exp(m_old - m_new) * acc_ref[...] + _dot(v1t, p)
        m_ref[...] = m_new
        return 0

    lax.fori_loop(0, n_tiles, attend_tile, 0)
    for hp in range(N_PAIRS):
        pair = []
        for h in (2 * hp, 2 * hp + 1):
            a = acc_ref[:, h * tq:(h + 1) * tq]
            pair.append(a[:HEAD_DIM] / a[HEAD_DIM:])
        o_ref[0, :, hp * LANES:(hp + 1) * LANES] = jnp.concatenate(pair, axis=0).T.astype(o_ref.dtype)


def _dsa(aq, iq, iw, k2, v1t, ik2, *, tq, q_off, topk):
    B, T, _ = aq.shape
    Lp = k2.shape[1]
    tk = Lp if T <= LANES else next(t for t in (512, 384, 256, 128) if Lp % t == 0)
    Tp = -(-T // LANES) * LANES
    tq = max(tq, LANES)
    n_rows = min(T, tq)
    pad_t = lambda a: jnp.pad(a, ((0, 0), (0, Tp - T), (0, 0)))
    qspec = lambda w: pl.BlockSpec((1, tq, w), lambda b, qi: (b, qi, 0))
    kspec = pl.BlockSpec((1, Lp, LANES), lambda b, qi: (b, 0, 0))
    out = pl.pallas_call(
        functools.partial(_dsa_kernel, tq=tq, tk=tk, q_off=q_off, n_rows=n_rows, topk=topk),
        grid=(B, Tp // tq),
        in_specs=[qspec(BRANCH_W), qspec(N_IDX * IDX_DIM), pl.BlockSpec((1, SUBLANES, tq), lambda b, qi: (b, 0, qi)),
                  kspec, pl.BlockSpec((1, LANES, Lp), lambda b, qi: (b, 0, 0)), kspec],
        out_specs=qspec(BRANCH_W),
        out_shape=jax.ShapeDtypeStruct((B, Tp, BRANCH_W), BF16),
        scratch_shapes=[pltpu.VMEM((Lp, tq), jnp.int32), pltpu.VMEM((N_HEADS * tq, LANES), BF16),
                        pltpu.VMEM((1, N_HEADS * tq), F32), pltpu.VMEM((LANES, N_HEADS * tq), F32)],
        compiler_params=_cparams(("parallel", "parallel")),
        name="dsa_attn",
    )(pad_t(aq), pad_t(iq), jnp.swapaxes(pad_t(iw), 1, 2), k2, v1t, ik2)
    return out[:, :T]


def _merge_kernel(x_ref, oa_ref, ob_ref, oc_ref, g_ref, wg_ref, wb_ref, wo_ref, y_ref):
    x = x_ref[0]
    D = x.shape[-1]
    xn = _rms(x, g_ref[...]).astype(BF16)
    merged = jnp.zeros(x.shape, F32)
    for n, o_ref in enumerate((oa_ref, ob_ref, oc_ref)):
        gate = _sigmoid(_dot(xn, wg_ref[:, n * D:(n + 1) * D]))
        merged = merged + gate * _dot(o_ref[0], wb_ref[n])
    y_ref[0] = x + _dot(merged.astype(BF16), wo_ref[...])


def _merge(x, oa, ob, oc, g, wg, wb, wo, tm):
    B, T, D = x.shape
    tspec = lambda w_: pl.BlockSpec((1, tm, w_), lambda b, t: (b, t, 0))
    const = lambda a: pl.BlockSpec(a.shape, lambda b, t: (0,) * a.ndim)
    return pl.pallas_call(
        _merge_kernel,
        grid=(B, T // tm),
        in_specs=[tspec(D), tspec(BRANCH_W), tspec(BRANCH_W), tspec(BRANCH_W), const(g), const(wg), const(wb),
                  const(wo)],
        out_specs=tspec(D),
        out_shape=jax.ShapeDtypeStruct(x.shape, F32),
        compiler_params=_cparams(("parallel", "parallel")),
        name="merge_out",
    )(x, oa, ob, oc, g, wg, wb, wo)


def _ffn_kernel(x_ref, g_ref, past_ref, wu_ref, cw_ref, cb_ref, wd_ref, gf_ref, y_ref, st_ref,
                hbuf_ref, carry_ref, acc_ref, *, tm, final_norm):
    fc = FF_CHUNK
    d_ff = wd_ref.shape[0]
    n_chunks = d_ff // fc

    @pl.when(pl.program_id(1) == 0)
    def _():
        carry_ref[...] = past_ref[0]

    x = x_ref[0]
    xn = _rms(x, g_ref[...]).astype(BF16)
    acc_ref[...] = jnp.zeros(acc_ref.shape, F32)

    for c in range(n_chunks):
        hbuf = hbuf_ref.at[c % 2]
        both = lambda ref: jnp.concatenate([ref[:, c * fc:(c + 1) * fc], ref[:, d_ff + c * fc:d_ff + (c + 1) * fc]],
                                           axis=1)
        h = jnp.concatenate([_dot(xn, wu_ref[:, c * fc:(c + 1) * fc]),
                             _dot(xn, wu_ref[:, d_ff + c * fc:d_ff + (c + 1) * fc])], axis=1)
        hbuf[0:SUBLANES, :] = carry_ref[c]
        hbuf[SUBLANES:SUBLANES + tm, :] = h
        carry_ref[c] = h[tm - SUBLANES:tm, :]
        cw = both(cw_ref)
        hc = (both(cb_ref) + cw[0:1] * hbuf[SUBLANES - 2:SUBLANES - 2 + tm, :]
              + cw[1:2] * hbuf[SUBLANES - 1:SUBLANES - 1 + tm, :] + cw[2:3] * h)
        gate, up = hc[:, :fc], hc[:, fc:]
        act = gate * _sigmoid(gate) * up
        acc_ref[...] += _dot(act.astype(BF16), wd_ref[c * fc:(c + 1) * fc, :])
    st_ref[0] = carry_ref[...]
    y = x + acc_ref[...]
    if final_norm:
        y = _rms(y, gf_ref[...])
    y_ref[0] = y


def _ffn(x, g, past, wu, cw, cb, wd, gf, tm, final_norm):
    B, T, D = x.shape
    nc, fc2 = wd.shape[0] // FF_CHUNK, 2 * FF_CHUNK
    tspec = pl.BlockSpec((1, tm, D), lambda b, t: (b, t, 0))
    const = lambda a: pl.BlockSpec(a.shape, lambda b, t: (0,) * a.ndim, pipeline_mode=pl.Buffered(1))
    sspec = pl.BlockSpec((1, nc, SUBLANES, fc2), lambda b, t: (b, 0, 0, 0))
    return pl.pallas_call(
        functools.partial(_ffn_kernel, tm=tm, final_norm=final_norm),
        grid=(B, T // tm),
        in_specs=[tspec, const(g), sspec, const(wu), const(cw), const(cb), const(wd), const(gf)],
        out_specs=[tspec, sspec],
        out_shape=[jax.ShapeDtypeStruct(x.shape, F32), jax.ShapeDtypeStruct((B, nc, SUBLANES, fc2), F32)],
        scratch_shapes=[pltpu.VMEM((2, tm + SUBLANES, fc2), F32), pltpu.VMEM((nc, SUBLANES, fc2), F32),
                        pltpu.VMEM((tm, D), F32)],
        compiler_params=_cparams(("arbitrary", "arbitrary")),
        name="conv_ffn",
    )(x, g, past, wu, cw, cb, wd, gf)


def _rope_tables(pos):
    half = ROT_DIM // 2
    freq = ROPE_THETA ** (-jnp.arange(half, dtype=F32) / half)
    ang = pos.astype(F32)[:, None] * freq[None, :]
    lane = np.arange(LANES) % HEAD_DIM
    cos = jnp.cos(ang)[:, lane % half]
    sin = jnp.sin(ang)[:, lane % half]
    rc = jnp.where(lane < ROT_DIM, cos, 1.0)
    rs1 = jnp.where(lane < half, -sin, 0.0)
    rs2 = jnp.where((lane >= half) & (lane < ROT_DIM), sin, 0.0)
    return rc, rs1, rs2


def _pack_layer(l, norm_mix, w_in, b_f, w_branch, w_gate, w_out, norm_ffn, w_up, conv_w, conv_b, w_down):
    D = w_in.shape[1]
    sizes = (BRANCH_W, HEAD_DIM, HEAD_DIM, N_IDX * IDX_DIM, IDX_DIM, N_IDX) + (BRANCH_W,) * 6 + (N_HEADS,)
    offs = np.concatenate([[0], np.cumsum(sizes)])
    (a_q, a_k, a_v, i_q, i_k, i_w, b_q, b_k, b_v, c_q, c_k, c_v, c_f) = (
        w_in[l][:, offs[i]:offs[i + 1]] for i in range(len(sizes)))
    zeros = lambda n: jnp.zeros((D, n), w_in.dtype)
    w = jnp.concatenate([a_q, i_q, a_k, a_k, i_k, i_k, a_v, a_v,
                         i_w, zeros(CF_LANE - N_IDX), c_f, zeros(LANES - CF_LANE - N_HEADS),
                         b_q, b_k, b_v, c_q, c_k, c_v], axis=1).astype(BF16)
    bf_row = jnp.zeros((1, LANES), F32).at[0, CF_LANE:CF_LANE + N_HEADS].set(b_f[l])

    d_ff = w_down.shape[1]
    fc = FF_CHUNK
    nc = d_ff // fc
    assert nc * fc == d_ff
    halves = lambda a: jnp.concatenate([a[..., :d_ff].reshape(a.shape[:-1] + (nc, fc)),
                                        a[..., d_ff:].reshape(a.shape[:-1] + (nc, fc))], axis=-1)
    wu, cw, cb, wd = w_up[l].astype(BF16), conv_w[l], conv_b[l][None], w_down[l].astype(BF16)
    return dict(g_mix=norm_mix[l][None], w=w, bf_row=bf_row, wg=w_gate[l].astype(BF16),
                wb=w_branch[l].astype(BF16), wo=w_out[l].astype(BF16), g_ffn=norm_ffn[l][None],
                wu=wu, cw=cw, cb=cb, wd=wd, halves=halves, nc=nc, d_ff=d_ff)


def _conv_state_in(state, halves):
    st = jnp.moveaxis(halves(state), 2, 1)
    return jnp.pad(st, ((0, 0), (0, 0), (SUBLANES - (CONV_W - 1), 0), (0, 0)))


def _conv_state_out(st, d_ff):
    st = st[:, :, SUBLANES - (CONV_W - 1):, :]
    B, nc, r, fc2 = st.shape
    fc = fc2 // 2
    gate = jnp.moveaxis(st[..., :fc], 1, 2).reshape(B, r, d_ff)
    up = jnp.moveaxis(st[..., fc:], 1, 2).reshape(B, r, d_ff)
    return jnp.concatenate([gate, up], axis=-1)


def _pad_keys(a, Lp):
    return jnp.pad(a, ((0, 0), (0, Lp - a.shape[1]), (0, 0)))


def _group_layer(x, past, lw, gf, layer, depth, stacked, tiles):
    B, T, D = x.shape
    P = 0 if past is None else past[0].shape[1]
    L = P + T
    Lp = -(-L // KEY_TILE) * KEY_TILE
    topk = max(1, min(TOPK_MAX, L // 4))
    tables = _rope_tables(jnp.arange(P, L))

    stacked, (iw, aq16, iq16, ak16, avt16, ik16, bq16, bk16, bv16, cq16, ck16, cv16) = _project(
        x, lw["g_mix"], lw["w"], lw["bf_row"], tables, tiles["proj"], layer, depth, stacked)
    lf = stacked[-1][layer]

    if past is None:
        keys = (ak16, ik16, bk16, bv16, ck16, cv16)
        lf_all = lf
        conv_in = jnp.zeros((B, lw["nc"], SUBLANES, 2 * FF_CHUNK), F32)
    else:
        p_ak, p_av, p_ik, p_bk, p_bv, p_ck, p_cv, p_lf, p_conv = past
        dup = lambda a: jnp.concatenate([a, a], axis=-1).astype(BF16)
        flat = lambda a: a.reshape(B, P, BRANCH_W).astype(BF16)
        olds = (dup(p_ak), dup(p_ik), flat(p_bk), flat(p_bv), flat(p_ck), flat(p_cv))
        news = (ak16, ik16, bk16, bv16, ck16, cv16)
        keys = tuple(jnp.concatenate([o, n], axis=1) for o, n in zip(olds, news))
        old_vt = jnp.swapaxes(p_av, 1, 2).astype(BF16)
        avt16 = jnp.concatenate([jnp.concatenate([old_vt, jnp.ones_like(old_vt)], axis=1), avt16], axis=2)
        lf_all = jnp.concatenate([p_lf, lf], axis=1)
        conv_in = _conv_state_in(p_conv, lw["halves"])
    ak2, ik2, bkk, bvv, ckk, cvv = (_pad_keys(a, Lp) for a in keys)
    v1t = jnp.pad(avt16, ((0, 0), (0, 0), (0, Lp - L)))

    fk_t = _cumsum_time(jnp.swapaxes(_pad_keys(lf_all, Lp), 1, 2))
    fq = jnp.swapaxes(fk_t[:, :, P:L], 1, 2)

    o_a = _dsa(aq16, iq16, iw, ak2, v1t, ik2, tq=tiles["dsa"], q_off=P, topk=topk)
    o_b = _sb(bq16, bkk, bvv, tq=tiles["sb"][0], tk=tiles["sb"][1], q_off=P)
    o_c = _fox(cq16, ckk, cvv, fq, fk_t, tq=tiles["fox"][0], tk=tiles["fox"][1], q_off=P)

    x = _merge(x, o_a, o_b, o_c, lw["g_mix"], lw["wg"], lw["wb"], lw["wo"], tiles["merge"])
    x, st = _ffn(x, lw["g_ffn"], conv_in, lw["wu"], lw["cw"], lw["cb"], lw["wd"], gf, tiles["ffn"],
                 layer == depth - 1)
    return x, stacked, _conv_state_out(st, lw["d_ff"])


def _tiles(T, P):
    pick = lambda want: min(want, T)
    if T % 512 == 0 and P % 512 == 0:
        sb, fox = (256, 256), (256, 512)
    else:
        assert T <= KEY_TILE
        sb = fox = (T, -(-(P + T) // KEY_TILE) * KEY_TILE)
    return dict(proj=pick(512), dsa=pick(128), sb=sb, fox=fox, merge=pick(512), ffn=pick(512))


def kernel(x_prompt, x_sample, cache_a_k, cache_a_v, cache_a_idx_k, cache_b_k, cache_b_v, cache_c_k, cache_c_v,
           cache_c_logf, state_ffn_conv, norm_mix, w_in, b_f, w_branch, w_gate, w_out, norm_ffn, w_up, conv_w,
           conv_b, w_down, norm_final):
    depth = w_in.shape[0]
    caches = (cache_a_k, cache_a_v, cache_a_idx_k, cache_b_k, cache_b_v, cache_c_k, cache_c_v, cache_c_logf,
              state_ffn_conv)
    layers = [_pack_layer(l, norm_mix, w_in, b_f, w_branch, w_gate, w_out, norm_ffn, w_up, conv_w, conv_b, w_down)
              for l in range(depth)]
    gf = norm_final[None]

    def trunk(x, past):
        B, T, _ = x.shape
        tiles = _tiles(T, 0 if past is None else past[0].shape[2])
        stacked, conv_states = None, []
        for l in range(depth):
            lp = None if past is None else tuple(c[l] for c in past)
            x, stacked, conv_state = _group_layer(x, lp, layers[l], gf, l, depth, stacked, tiles)
            conv_states.append(conv_state)
        ak, av, ik, bk, bv, ck, cv, lf = stacked
        heads = lambda a: a.reshape(depth, B, T, N_HEADS, HEAD_DIM)
        return x, (ak, av, ik, heads(bk), heads(bv), heads(ck), heads(cv), lf, jnp.stack(conv_states, axis=0))

    y_prompt, p_state = trunk(x_prompt, None)
    y_sample, s_state = trunk(x_sample, caches)
    return (y_prompt, y_sample) + p_state + s_state
```

```python
import functools

import jax
import jax.numpy as jnp
import numpy as np
from jax import lax
from jax.experimental import pallas as pl
from jax.experimental.pallas import tpu as pltpu

F32 = jnp.float32
BF16 = jnp.bfloat16

LANES = 128
SUBLANES = 8
HEAD_DIM = 64
N_HEADS = 8
N_PAIRS = N_HEADS // 2
BRANCH_W = N_HEADS * HEAD_DIM
N_BRANCH = 3
N_IDX = 4
IDX_DIM = 64
CHUNK = 64
CHUNK_SHIFT = 6
TOPK_MAX = 256
ROT_DIM = HEAD_DIM // 4
ROPE_THETA = 500000.0
CONV_W = 3
EPS = 1e-6
NEG = -1e30
KEY_TILE = 128
FF_CHUNK = 256
VMEM_LIMIT = 56 * 1024 * 1024

ROPE_W = 1024
MISC_W = 256
CF_LANE = 8


def _cparams(sem):
    return pltpu.CompilerParams(dimension_semantics=sem, vmem_limit_bytes=VMEM_LIMIT)


def _rms(x, g):
    return x * lax.rsqrt(jnp.mean(x * x, axis=-1, keepdims=True) + EPS) * g


def _sigmoid(z):
    return 1.0 / (1.0 + jnp.exp(-z))


def _log_sigmoid(z):
    return jnp.minimum(z, 0.0) - jnp.log1p(jnp.exp(-jnp.abs(z)))


def _dot_t(a, b):
    return lax.dot_general(a, b, (((1,), (1,)), ((), ())), preferred_element_type=F32)


def _dot(a, b):
    return jnp.dot(a, b, preferred_element_type=F32)


def _keep_head(blk, which):
    lane = lax.broadcasted_iota(jnp.int32, blk.shape, blk.ndim - 1)
    keep = (lane >= HEAD_DIM) if which else (lane < HEAD_DIM)
    return jnp.where(keep, blk, jnp.zeros_like(blk))


def _merge_pair(o0, o1):
    lane = lax.broadcasted_iota(jnp.int32, o0.shape, o0.ndim - 1)
    return jnp.where(lane < HEAD_DIM, o0, o1)


N_CACHE_ROWS = 8


def _proj_kernel(x_ref, g_ref, w_ref, bf_ref, rc_ref, rs1_ref, rs2_ref, *refs):
    (ak_ref, av_ref, ik_ref, bk_ref, bv_ref, ck_ref, cv_ref, lf_ref, iw_ref,
     aq16, iq16, ak16, avt16, ik16, bq16, bk16, bv16, cq16, ck16, cv16) = refs[-(N_CACHE_ROWS + 12):]
    xn = _rms(x_ref[0], g_ref[...]).astype(BF16)
    rc, rs1, rs2 = rc_ref[...], rs1_ref[...], rs2_ref[...]

    def cols(c0, width):
        return _dot(xn, w_ref[:, c0:c0 + width])

    def group(blk, g):
        return blk[:, g * LANES:(g + 1) * LANES]

    def rope(blk):
        half = ROT_DIM // 2
        return blk * rc + pltpu.roll(blk, LANES - half, 1) * rs1 + pltpu.roll(blk, half, 1) * rs2

    wide = cols(0, BRANCH_W)
    for g in range(4):
        aq16[0, :, g * LANES:(g + 1) * LANES] = (rope(group(wide, g)) * HEAD_DIM ** -0.5).astype(BF16)
    wide = cols(BRANCH_W, ROPE_W - BRANCH_W)
    for g in range(2):
        iq16[0, :, g * LANES:(g + 1) * LANES] = (rope(group(wide, g)) * IDX_DIM ** -0.5).astype(BF16)
    blk = rope(group(wide, 2))
    ak_ref[0, 0] = blk[:, :HEAD_DIM]
    ak16[0] = blk.astype(BF16)
    blk = rope(group(wide, 3))
    ik_ref[0, 0] = blk[:, :IDX_DIM]
    ik16[0] = blk.astype(BF16)

    wide = cols(ROPE_W, MISC_W)
    blk = group(wide, 0)
    av_ref[0, 0] = blk[:, :HEAD_DIM]
    row = lax.broadcasted_iota(jnp.int32, (LANES, blk.shape[0]), 0)
    avt16[0] = jnp.where(row < HEAD_DIM, blk.T, 1.0).astype(BF16)
    blk = group(wide, 1)
    iw_ref[0] = blk[:, :SUBLANES] * N_IDX ** -0.5
    lf_ref[0, 0] = _log_sigmoid(blk[:, CF_LANE:CF_LANE + N_HEADS] + bf_ref[:, CF_LANE:CF_LANE + N_HEADS])

    base = ROPE_W + MISC_W
    outs = ((None, bq16), (bk_ref, bk16), (bv_ref, bv16), (None, cq16), (ck_ref, ck16), (cv_ref, cv16))
    for n, (o32, o16) in enumerate(outs):
        wide = cols(base + n * BRANCH_W, BRANCH_W)
        if o32 is None:
            o16[0] = (wide * HEAD_DIM ** -0.5).astype(BF16)
        else:
            o32[0, 0] = wide
            o16[0] = wide.astype(BF16)


def _project(x, g, w, bf_row, tables, tm, layer, depth, stacked):
    B, T, D = x.shape
    tok = lambda w_, dt: jax.ShapeDtypeStruct((B, T, w_), dt)
    tspec = lambda w_: pl.BlockSpec((1, tm, w_), lambda b, t: (b, t, 0))
    const = lambda a: pl.BlockSpec(a.shape, lambda b, t: (0,) * a.ndim, pipeline_mode=pl.Buffered(1))
    cache_w = (64, 64, 64, 512, 512, 512, 512, 8)
    cache_shapes = [jax.ShapeDtypeStruct((depth, B, T, w_), F32) for w_ in cache_w]
    cache_specs = [pl.BlockSpec((1, 1, tm, w_), lambda b, t: (layer, b, t, 0)) for w_ in cache_w]
    work = [(tok(w_, dt), tspec(w_)) for w_, dt in ((8, F32), (512, BF16), (256, BF16), (128, BF16))]
    work += [(jax.ShapeDtypeStruct((B, LANES, T), BF16), pl.BlockSpec((1, LANES, tm), lambda b, t: (b, 0, t)))]
    work += [(tok(w_, BF16), tspec(w_)) for w_ in (128,) + (512,) * 6]
    work_shapes, work_specs = [s for s, _ in work], [p for _, p in work]
    rspec = pl.BlockSpec((tm, LANES), lambda b, t: (t, 0))
    prev = () if stacked is None else tuple(stacked)
    n_in = 7
    outs = pl.pallas_call(
        _proj_kernel,
        grid=(B, T // tm),
        in_specs=[tspec(D), const(g), const(w), const(bf_row), rspec, rspec, rspec]
        + [pl.BlockSpec(memory_space=pl.ANY)] * len(prev),
        out_specs=cache_specs + work_specs,
        out_shape=cache_shapes + work_shapes,
        input_output_aliases={n_in + i: i for i in range(len(prev))},
        compiler_params=_cparams(("parallel", "parallel")),
        name="in_proj",
    )(x, g, w, bf_row, *tables, *prev)
    return outs[:N_CACHE_ROWS], outs[N_CACHE_ROWS:]


def _split3(x):
    hi = x.astype(BF16)
    r = x - hi.astype(F32)
    mid = r.astype(BF16)
    lo = (r - mid.astype(F32)).astype(BF16)
    return hi, mid, lo


def _cumsum_kernel(lf_ref, o_ref, *, blk):
    L = lf_ref.shape[-1]
    r = lax.broadcasted_iota(jnp.int32, (blk, blk), 0)
    c = lax.broadcasted_iota(jnp.int32, (blk, blk), 1)
    upper = jnp.where(r <= c, 1.0, 0.0).astype(BF16)
    carry = jnp.zeros((N_HEADS, 1), F32)
    for j in range(L // blk):
        hi, mid, lo = _split3(lf_ref[0, :, j * blk:(j + 1) * blk])
        f = _dot(hi, upper) + _dot(mid, upper) + _dot(lo, upper) + carry
        o_ref[0, :, j * blk:(j + 1) * blk] = f
        carry = f[:, blk - 1:blk]


def _cumsum_time(lf_t):
    B, H, L = lf_t.shape
    blk = LANES
    assert L % blk == 0
    spec = pl.BlockSpec((1, H, L), lambda b: (b, 0, 0))
    return pl.pallas_call(
        functools.partial(_cumsum_kernel, blk=blk),
        grid=(B,),
        in_specs=[spec],
        out_specs=spec,
        out_shape=jax.ShapeDtypeStruct((B, H, L), F32),
        compiler_params=_cparams(("parallel",)),
        name="logf_cumsum",
    )(lf_t)


def _head_column(blk, h):
    lane = lax.broadcasted_iota(jnp.int32, blk.shape, 1)
    return jnp.sum(jnp.where(lane == h, blk, 0.0), axis=1, keepdims=True)


def _positions(tq, tk, q0, k0):
    qpos = q0 + lax.broadcasted_iota(jnp.int32, (tq, tk), 0)
    kpos = k0 + lax.broadcasted_iota(jnp.int32, (tq, tk), 1)
    return qpos, kpos


def _pair_specs(tq, Lp):
    qspec = pl.BlockSpec((1, tq, LANES), lambda b, hp, qi: (b, qi, hp))
    kspec = pl.BlockSpec((1, Lp, LANES), lambda b, hp, qi: (b, 0, hp))
    return qspec, kspec


def _split_past(rest, has_past):
    return (rest[:2], rest[2:-2], rest[-2:]) if has_past else ((), rest, ())


def _key_views(k_ref, v_ref, past, bufs):
    if not past:
        return k_ref.at[0], v_ref.at[0]
    for old, new, buf in zip(past, (k_ref, v_ref), bufs):
        P, T = old.shape[1], new.shape[1]
        buf[0:P, :] = old[0].astype(BF16)
        buf[P:P + T, :] = new[0]
        if buf.shape[0] > P + T:
            buf[P + T:, :] = jnp.zeros((buf.shape[0] - P - T, LANES), BF16)
    return bufs


def _past_specs_and_bufs(past, T, Lp):
    if past is None:
        return [], [], ()
    P = past[0].shape[1]
    spec = pl.BlockSpec((1, P, LANES), lambda b, hp, qi: (b, 0, hp))
    return [spec, spec], [pltpu.VMEM((Lp, LANES), BF16)] * 2, tuple(past)


def _fox_kernel(q_ref, k_ref, v_ref, *rest, tq, tk, q_off, has_past):
    past, (fq_ref, fk_ref, o_ref, m_ref, acc_ref), bufs = _split_past(rest, has_past)
    keys, values = _key_views(k_ref, v_ref, past, bufs)
    hp = pl.program_id(1)
    q0 = q_off + pl.program_id(2) * tq
    diag = q0 // tk
    qp = q_ref[0]
    fq_blk = fq_ref[0]
    qm = (_keep_head(qp, 0), _keep_head(qp, 1))
    fq = (_head_column(fq_blk, hp * 2), _head_column(fq_blk, hp * 2 + 1))
    ones = jnp.ones((tk, LANES), BF16)
    m_ref[...] = jnp.full(m_ref.shape, NEG, F32)
    acc_ref[...] = jnp.zeros(acc_ref.shape, F32)

    def score(j, masked):
        k0 = pl.multiple_of(j * tk, tk)
        kt = keys[pl.ds(k0, tk), :]
        if masked:
            qpos, kpos = _positions(tq, tk, q0, k0)
            causal = kpos <= qpos
        out = []
        for hh in range(2):
            fk = fk_ref[0, hp * 2 + hh, :, pl.ds(k0, tk)]
            s = _dot_t(qm[hh], kt) + fq[hh] - fk
            out.append(jnp.where(causal, s, NEG) if masked else s)
        return tuple(out)

    def fold(j, scores):
        v1 = jnp.concatenate([values[pl.ds(pl.multiple_of(j * tk, tk), tk), :], ones], axis=1)
        m_old = [m_ref[hh] for hh in range(2)]
        m_new = [jnp.maximum(m_old[hh], jnp.max(scores[hh], axis=1, keepdims=True)) for hh in range(2)]
        p = [jnp.exp(scores[hh] - m_new[hh]).astype(BF16) for hh in range(2)]
        for hh in range(2):
            acc_ref[hh] = jnp.exp(m_old[hh] - m_new[hh]) * acc_ref[hh] + _dot(p[hh], v1)
            m_ref[hh] = m_new[hh]

    fold(diag, score(diag, True))

    def body(i, _):
        j = diag - 1 - i
        fold(j, score(j, False))
        return 0

    lax.fori_loop(0, diag, body, 0)
    o0 = acc_ref[0, :, :LANES] / acc_ref[0, :, LANES:]
    o1 = acc_ref[1, :, :LANES] / acc_ref[1, :, LANES:]
    o_ref[0] = _merge_pair(o0, o1).astype(o_ref.dtype)


def _fox(q, k, v, past, fq, fk_t, *, tq, tk, q_off):
    B, T, _ = q.shape
    Lp = fk_t.shape[-1]
    qspec, kspec = _pair_specs(tq, k.shape[1])
    past_specs, bufs, past = _past_specs_and_bufs(past, T, Lp)
    return pl.pallas_call(
        functools.partial(_fox_kernel, tq=tq, tk=tk, q_off=q_off, has_past=bool(past)),
        grid=(B, N_PAIRS, T // tq),
        in_specs=[qspec, kspec, kspec] + past_specs
        + [pl.BlockSpec((1, tq, N_HEADS), lambda b, hp, qi: (b, qi, 0)),
           pl.BlockSpec((1, N_HEADS, 1, Lp), lambda b, hp, qi: (b, 0, 0, 0))],
        out_specs=qspec,
        out_shape=jax.ShapeDtypeStruct(q.shape, BF16),
        scratch_shapes=[pltpu.VMEM((2, tq, 1), F32), pltpu.VMEM((2, tq, 2 * LANES), F32)] + bufs,
        compiler_params=_cparams(("parallel", "parallel", "parallel")),
        name="fox_attn",
    )(q, k, v, *past, fq, fk_t[:, :, None, :])


def _sb_kernel(q_ref, k_ref, v_ref, *rest, tq, tk, q_off, has_past):
    past, (o_ref, acc_ref, tail_ref), bufs = _split_past(rest, has_past)
    keys, values = _key_views(k_ref, v_ref, past, bufs)
    q0 = q_off + pl.program_id(2) * tq
    diag = q0 // tk
    qp = q_ref[0]
    qm = (_keep_head(qp, 0), _keep_head(qp, 1))
    r = jnp.bitwise_and(lax.broadcasted_iota(jnp.int32, (2 * LANES, 2 * LANES), 0), LANES - 1)
    c = lax.broadcasted_iota(jnp.int32, (2 * LANES, 2 * LANES), 1)
    suffix_w = jnp.where((c >= LANES) | (r > c), 1.0, 0.0).astype(BF16)
    acc_ref[...] = jnp.zeros(acc_ref.shape, F32)
    tail_ref[...] = jnp.zeros(tail_ref.shape, F32)

    def tile(j, masked):
        k0 = pl.multiple_of(j * tk, tk)
        kt = keys[pl.ds(k0, tk), :]
        vt = values[pl.ds(k0, tk), :]
        if masked:
            qpos, kpos = _positions(tq, tk, q0, k0)
            strict = kpos < qpos
        heads = (0, 1)
        z = [_dot_t(qm[hh], kt) for hh in heads]
        ls, hi, lo, after = [None, None], [None, None], [None, None], [None, None]
        for hh in heads:
            ls[hh] = jnp.minimum(z[hh], 0.0) - jnp.log(1.0 + jnp.exp(-jnp.abs(z[hh])))
            la = ls[hh] - z[hh]
            if masked:
                la = jnp.where(strict, la, 0.0)
                ls[hh] = jnp.where(strict, ls[hh], NEG)
            hi[hh] = la.astype(BF16)
            lo[hh] = (la - hi[hh].astype(F32)).astype(BF16)
        for hh in heads:
            run = tail_ref[hh]
            blocks = [None] * (tk // LANES)
            for b in reversed(range(tk // LANES)):
                sl = slice(b * LANES, (b + 1) * LANES)
                res = _dot(jnp.concatenate([hi[hh][:, sl], lo[hh][:, sl]], axis=1), suffix_w)
                blocks[b] = res[:, :LANES] + run
                run = run + res[:, LANES:]
            tail_ref[hh] = run
            after[hh] = jnp.concatenate(blocks, axis=1)
        a = [jnp.exp(ls[hh] + after[hh]).astype(BF16) for hh in heads]
        for hh in heads:
            acc_ref[hh] += _dot(a[hh], vt)

    tile(diag, True)

    def body(i, _):
        tile(diag - 1 - i, False)
        return 0

    lax.fori_loop(0, diag, body, 0)
    o_ref[0] = _merge_pair(acc_ref[0], acc_ref[1]).astype(o_ref.dtype)


def _sb(q, k, v, past, Lp, *, tq, tk, q_off):
    B, T, _ = q.shape
    qspec, kspec = _pair_specs(tq, k.shape[1])
    past_specs, bufs, past = _past_specs_and_bufs(past, T, Lp)
    return pl.pallas_call(
        functools.partial(_sb_kernel, tq=tq, tk=tk, q_off=q_off, has_past=bool(past)),
        grid=(B, N_PAIRS, T // tq),
        in_specs=[qspec, kspec, kspec] + past_specs,
        out_specs=qspec,
        out_shape=jax.ShapeDtypeStruct(q.shape, BF16),
        scratch_shapes=[pltpu.VMEM((2, tq, LANES), F32), pltpu.VMEM((2, tq, LANES), F32)] + bufs,
        compiler_params=_cparams(("parallel", "parallel", "parallel")),
        name="sb_attn",
    )(q, k, v, *past)


def _const_key(v):
    bits = int(np.float32(v).view(np.int32))
    return (bits ^ 0x7FFFFFFF) + 1 if bits < 0 else bits


def _float_key(x):
    bits = lax.bitcast_convert_type(x, jnp.int32)
    return jnp.where(bits < 0, jnp.bitwise_xor(bits, 0x7FFFFFFF) + 1, bits)


KEY_MIN = -2 ** 31
KEY_HALF_NEG = _const_key(0.5 * NEG)


KEY_MAX = 2 ** 31 - 1
KEY_BITS = 32
COUNT_CHAINS = 8


def _dsa_kernel(aq_ref, iq_ref, iwt_ref, k_ref, vt_ref, ik_ref, o_ref, key_ref, q8_ref, m_ref, acc_ref,
                *, tq, tk, q_off, n_rows, topk):
    q0 = q_off + pl.program_id(1) * tq
    n_tiles = (q0 + n_rows + tk - 1) // tk

    def tile_start(j):
        return pl.multiple_of(j * tk, LANES)

    iq = iq_ref[0]
    iwt = iwt_ref[0]
    qh = [_keep_head(iq[:, (h // 2) * LANES:(h // 2 + 1) * LANES], h % 2) for h in range(N_IDX)]
    q_chunk = jnp.right_shift(q0 + lax.broadcasted_iota(jnp.int32, (tk, tq), 1), CHUNK_SHIFT)

    def score_tile(j, _):
        k0 = tile_start(j)
        ik = ik_ref[0, pl.ds(k0, tk), :]
        sc = jnp.zeros((tk, tq), F32)
        for h in range(N_IDX):
            sc = sc + jnp.maximum(_dot_t(ik, qh[h]), 0.0) * iwt[h:h + 1, :]
        key_chunk = jnp.right_shift(k0 + lax.broadcasted_iota(jnp.int32, (tk, tq), 0), CHUNK_SHIFT)
        key_ref[pl.ds(k0, tk), :] = _float_key(jnp.where(key_chunk <= q_chunk, sc, NEG))
        return 0

    lax.fori_loop(0, n_tiles, score_tile, 0)

    def search():
        def step(_, cr):
            lo, hi, above_hi = cr
            mid = jnp.right_shift(lo, 1) + jnp.right_shift(hi, 1) + jnp.bitwise_and(jnp.bitwise_and(lo, hi), 1)

            def count_tile(j, cnt):
                above = jnp.where(key_ref[pl.ds(tile_start(j), tk), :] > mid, 1.0, 0.0)
                return cnt + jnp.sum(above.reshape((-1,) + cnt.shape), axis=0)

            cnt = lax.fori_loop(0, n_tiles, count_tile, jnp.zeros((COUNT_CHAINS, SUBLANES, tq), F32))
            cnt = jnp.sum(jnp.sum(cnt, axis=0), axis=0, keepdims=True)
            less = cnt < topk
            return jnp.where(less, lo, mid), jnp.where(less, mid, hi), jnp.where(less, cnt, above_hi)

        init = (jnp.full((1, tq), KEY_MIN, jnp.int32), jnp.full((1, tq), KEY_MAX, jnp.int32), jnp.zeros((1, tq), F32))
        _, hi, above_hi = lax.fori_loop(0, KEY_BITS, step, init)
        return hi, topk - above_hi

    thr, room = lax.cond(q0 + n_rows > topk, search,
                         lambda: (jnp.full((1, tq), KEY_MIN, jnp.int32), jnp.zeros((1, tq), F32)))
    row_ok = lax.broadcasted_iota(jnp.int32, (1, tq), 1) < n_rows
    room = jnp.where(row_ok, room, 0.0)

    @pl.when(jnp.max(room) > 0.0)
    def _():
        r = lax.broadcasted_iota(jnp.int32, (LANES, LANES), 0)
        c = lax.broadcasted_iota(jnp.int32, (LANES, LANES), 1)
        before = jnp.where(c < r, 1.0, 0.0).astype(BF16)

        def tie_tile(j, seen):
            k0 = tile_start(j)
            for b in range(tk // LANES):
                rows = pl.ds(k0 + b * LANES, LANES)
                keys = key_ref[rows, :]
                eq = keys == thr
                eqf = jnp.where(eq, 1.0, 0.0)
                rank = _dot(before, eqf.astype(BF16)) + seen
                key_ref[rows, :] = jnp.where(eq, jnp.where(rank < room, keys + 1, keys), keys)
                seen = seen + jnp.sum(eqf, axis=0, keepdims=True)
            return seen

        lax.fori_loop(0, n_tiles, tie_tile, jnp.zeros((1, tq), F32))

    aq = aq_ref[0]
    for h in range(N_HEADS):
        q8_ref[h * tq:(h + 1) * tq, :] = _keep_head(aq[:, (h // 2) * LANES:(h // 2 + 1) * LANES], h % 2)
    m_ref[...] = jnp.full(m_ref.shape, NEG, F32)
    acc_ref[...] = jnp.zeros(acc_ref.shape, F32)
    thr_sel = jnp.maximum(thr, KEY_HALF_NEG)

    def attend_tile(j, _):
        k0 = tile_start(j)
        kt = k_ref[0, pl.ds(k0, tk), :]
        v1t = vt_ref[0, :, pl.ds(k0, tk)]
        bias = jnp.where(key_ref[pl.ds(k0, tk), :] > thr_sel, 0.0, NEG)
        st = _dot_t(kt, q8_ref[...])
        st = jnp.concatenate([st[:, h * tq:(h + 1) * tq] + bias for h in range(N_HEADS)], axis=1)
        m_old = m_ref[...]
        m_new = jnp.maximum(m_old, jnp.max(st, axis=0, keepdims=True))
        p = jnp.exp(st - m_new).astype(BF16)
        acc_ref[...] = jnp.exp(m_old - m_new) * acc_ref[...] + _dot(v1t, p)
        m_ref[...] = m_new
        return 0

    lax.fori_loop(0, n_tiles, attend_tile, 0)
    for hp in range(N_PAIRS):
        pair = []
        for h in (2 * hp, 2 * hp + 1):
            a = acc_ref[:, h * tq:(h + 1) * tq]
            pair.append(a[:HEAD_DIM] / a[HEAD_DIM:])
        o_ref[0, :, hp * LANES:(hp + 1) * LANES] = jnp.concatenate(pair, axis=0).T.astype(o_ref.dtype)


def _dsa(aq, iq, iw, k2, v1t, ik2, *, tq, q_off, topk):
    B, T, _ = aq.shape
    Lp = k2.shape[1]
    tk = Lp if T <= LANES else next(t for t in (512, 384, 256, 128) if Lp % t == 0)
    Tp = -(-T // LANES) * LANES
    tq = max(tq, LANES)
    n_rows = min(T, tq)
    pad_t = lambda a: jnp.pad(a, ((0, 0), (0, Tp - T), (0, 0)))
    qspec = lambda w: pl.BlockSpec((1, tq, w), lambda b, qi: (b, qi, 0))
    kspec = pl.BlockSpec((1, Lp, LANES), lambda b, qi: (b, 0, 0))
    out = pl.pallas_call(
        functools.partial(_dsa_kernel, tq=tq, tk=tk, q_off=q_off, n_rows=n_rows, topk=topk),
        grid=(B, Tp // tq),
        in_specs=[qspec(BRANCH_W), qspec(N_IDX * IDX_DIM), pl.BlockSpec((1, SUBLANES, tq), lambda b, qi: (b, 0, qi)),
                  kspec, pl.BlockSpec((1, LANES, Lp), lambda b, qi: (b, 0, 0)), kspec],
        out_specs=qspec(BRANCH_W),
        out_shape=jax.ShapeDtypeStruct((B, Tp, BRANCH_W), BF16),
        scratch_shapes=[pltpu.VMEM((Lp, tq), jnp.int32), pltpu.VMEM((N_HEADS * tq, LANES), BF16),
                        pltpu.VMEM((1, N_HEADS * tq), F32), pltpu.VMEM((LANES, N_HEADS * tq), F32)],
        compiler_params=_cparams(("parallel", "parallel")),
        name="dsa_attn",
    )(pad_t(aq), pad_t(iq), jnp.swapaxes(pad_t(iw), 1, 2), k2, v1t, ik2)
    return out[:, :T]


def _merge_kernel(x_ref, oa_ref, ob_ref, oc_ref, g_ref, wg_ref, wb_ref, wo_ref, y_ref):
    x = x_ref[0]
    D = x.shape[-1]
    xn = _rms(x, g_ref[...]).astype(BF16)
    merged = jnp.zeros(x.shape, F32)
    for n, o_ref in enumerate((oa_ref, ob_ref, oc_ref)):
        gate = _sigmoid(_dot(xn, wg_ref[:, n * D:(n + 1) * D]))
        merged = merged + gate * _dot(o_ref[0], wb_ref[n])
    y_ref[0] = x + _dot(merged.astype(BF16), wo_ref[...])


def _merge(x, oa, ob, oc, g, wg, wb, wo, tm):
    B, T, D = x.shape
    tspec = lambda w_: pl.BlockSpec((1, tm, w_), lambda b, t: (b, t, 0))
    const = lambda a: pl.BlockSpec(a.shape, lambda b, t: (0,) * a.ndim)
    return pl.pallas_call(
        _merge_kernel,
        grid=(B, T // tm),
        in_specs=[tspec(D), tspec(BRANCH_W), tspec(BRANCH_W), tspec(BRANCH_W), const(g), const(wg), const(wb),
                  const(wo)],
        out_specs=tspec(D),
        out_shape=jax.ShapeDtypeStruct(x.shape, F32),
        compiler_params=_cparams(("parallel", "parallel")),
        name="merge_out",
    )(x, oa, ob, oc, g, wg, wb, wo)


def _ffn_kernel(x_ref, g_ref, past_ref, wu_ref, cw_ref, cb_ref, wd_ref, gf_ref, y_ref, st_ref,
                hbuf_ref, carry_ref, acc_ref, *, tm, final_norm):
    fc = FF_CHUNK
    d_ff = wd_ref.shape[0]
    n_chunks = d_ff // fc

    @pl.when(pl.program_id(1) == 0)
    def _():
        carry_ref[...] = past_ref[0]

    x = x_ref[0]
    xn = _rms(x, g_ref[...]).astype(BF16)
    acc_ref[...] = jnp.zeros(acc_ref.shape, F32)

    for c in range(n_chunks):
        hbuf = hbuf_ref.at[c % 2]
        both = lambda ref: jnp.concatenate([ref[:, c * fc:(c + 1) * fc], ref[:, d_ff + c * fc:d_ff + (c + 1) * fc]],
                                           axis=1)
        h = jnp.concatenate([_dot(xn, wu_ref[:, c * fc:(c + 1) * fc]),
                             _dot(xn, wu_ref[:, d_ff + c * fc:d_ff + (c + 1) * fc])], axis=1)
        hbuf[0:SUBLANES, :] = carry_ref[c]
        hbuf[SUBLANES:SUBLANES + tm, :] = h
        carry_ref[c] = h[tm - SUBLANES:tm, :]
        cw = both(cw_ref)
        hc = (both(cb_ref) + cw[0:1] * hbuf[SUBLANES - 2:SUBLANES - 2 + tm, :]
              + cw[1:2] * hbuf[SUBLANES - 1:SUBLANES - 1 + tm, :] + cw[2:3] * h)
        gate, up = hc[:, :fc], hc[:, fc:]
        act = gate * _sigmoid(gate) * up
        acc_ref[...] += _dot(act.astype(BF16), wd_ref[c * fc:(c + 1) * fc, :])
    st_ref[0] = carry_ref[...]
    y = x + acc_ref[...]
    if final_norm:
        y = _rms(y, gf_ref[...])
    y_ref[0] = y


def _ffn(x, g, past, wu, cw, cb, wd, gf, tm, final_norm):
    B, T, D = x.shape
    nc, fc2 = wd.shape[0] // FF_CHUNK, 2 * FF_CHUNK
    tspec = pl.BlockSpec((1, tm, D), lambda b, t: (b, t, 0))
    const = lambda a: pl.BlockSpec(a.shape, lambda b, t: (0,) * a.ndim, pipeline_mode=pl.Buffered(1))
    sspec = pl.BlockSpec((1, nc, SUBLANES, fc2), lambda b, t: (b, 0, 0, 0))
    return pl.pallas_call(
        functools.partial(_ffn_kernel, tm=tm, final_norm=final_norm),
        grid=(B, T // tm),
        in_specs=[tspec, const(g), sspec, const(wu), const(cw), const(cb), const(wd), const(gf)],
        out_specs=[tspec, sspec],
        out_shape=[jax.ShapeDtypeStruct(x.shape, F32), jax.ShapeDtypeStruct((B, nc, SUBLANES, fc2), F32)],
        scratch_shapes=[pltpu.VMEM((2, tm + SUBLANES, fc2), F32), pltpu.VMEM((nc, SUBLANES, fc2), F32),
                        pltpu.VMEM((tm, D), F32)],
        compiler_params=_cparams(("arbitrary", "arbitrary")),
        name="conv_ffn",
    )(x, g, past, wu, cw, cb, wd, gf)


def _rope_tables(pos):
    half = ROT_DIM // 2
    freq = ROPE_THETA ** (-jnp.arange(half, dtype=F32) / half)
    ang = pos.astype(F32)[:, None] * freq[None, :]
    lane = np.arange(LANES) % HEAD_DIM
    cos = jnp.cos(ang)[:, lane % half]
    sin = jnp.sin(ang)[:, lane % half]
    rc = jnp.where(lane < ROT_DIM, cos, 1.0)
    rs1 = jnp.where(lane < half, -sin, 0.0)
    rs2 = jnp.where((lane >= half) & (lane < ROT_DIM), sin, 0.0)
    return rc, rs1, rs2


IN_SIZES = (BRANCH_W, HEAD_DIM, HEAD_DIM, N_IDX * IDX_DIM, IDX_DIM, N_IDX) + (BRANCH_W,) * 6 + (N_HEADS,)
PACKED_W = ROPE_W + MISC_W + 6 * BRANCH_W


def _pack_w_kernel(w_ref, o_ref, s_ref):
    offs = np.concatenate([[0], np.cumsum(IN_SIZES)]).tolist()
    a_q, a_k, a_v, i_q, i_k, i_w, b_q = offs[:7]
    c_f = offs[12]
    src = lambda c0, width: w_ref[0, :, c0:c0 + width]
    s_ref[:, 0:BRANCH_W] = src(a_q, BRANCH_W)
    s_ref[:, BRANCH_W:BRANCH_W + 256] = src(i_q, 256)
    for dst, c0 in ((768, a_k), (896, i_k), (ROPE_W, a_v)):
        s_ref[:, dst:dst + HEAD_DIM] = src(c0, HEAD_DIM)
        s_ref[:, dst + HEAD_DIM:dst + 2 * HEAD_DIM] = src(c0, HEAD_DIM)
    misc = ROPE_W + LANES
    s_ref[:, misc:misc + LANES] = jnp.zeros((s_ref.shape[0], LANES), F32)
    s_ref[:, misc:misc + N_IDX] = src(i_w, N_IDX)
    s_ref[:, misc + CF_LANE:misc + CF_LANE + N_HEADS] = src(c_f, N_HEADS)
    s_ref[:, ROPE_W + MISC_W:PACKED_W] = src(b_q, 6 * BRANCH_W)
    o_ref[...] = s_ref[...].astype(BF16)


def _pack_w(w_in, l):
    D = w_in.shape[1]
    rows = 256
    return pl.pallas_call(
        _pack_w_kernel,
        grid=(D // rows,),
        in_specs=[pl.BlockSpec((1, rows, w_in.shape[2]), lambda r: (l, r, 0))],
        out_specs=pl.BlockSpec((rows, PACKED_W), lambda r: (r, 0)),
        out_shape=jax.ShapeDtypeStruct((D, PACKED_W), BF16),
        scratch_shapes=[pltpu.VMEM((rows, PACKED_W), F32)],
        compiler_params=_cparams(("parallel",)),
        name="pack_w_in",
    )(w_in)


def _pack_layer(l, norm_mix, w_in, b_f, w_branch, w_gate, w_out, norm_ffn, w_up, conv_w, conv_b, w_down):
    w = _pack_w(w_in, l)
    bf_row = jnp.zeros((1, LANES), F32).at[0, CF_LANE:CF_LANE + N_HEADS].set(b_f[l])

    d_ff = w_down.shape[1]
    fc = FF_CHUNK
    nc = d_ff // fc
    assert nc * fc == d_ff
    halves = lambda a: jnp.concatenate([a[..., :d_ff].reshape(a.shape[:-1] + (nc, fc)),
                                        a[..., d_ff:].reshape(a.shape[:-1] + (nc, fc))], axis=-1)
    wu, cw, cb, wd = w_up[l].astype(BF16), conv_w[l], conv_b[l][None], w_down[l].astype(BF16)
    return dict(g_mix=norm_mix[l][None], w=w, bf_row=bf_row, wg=w_gate[l].astype(BF16),
                wb=w_branch[l].astype(BF16), wo=w_out[l].astype(BF16), g_ffn=norm_ffn[l][None],
                wu=wu, cw=cw, cb=cb, wd=wd, halves=halves, nc=nc, d_ff=d_ff)


def _conv_state_in(state, halves):
    st = jnp.moveaxis(halves(state), 2, 1)
    return jnp.pad(st, ((0, 0), (0, 0), (SUBLANES - (CONV_W - 1), 0), (0, 0)))


def _conv_state_out(st, d_ff):
    st = st[:, :, SUBLANES - (CONV_W - 1):, :]
    B, nc, r, fc2 = st.shape
    fc = fc2 // 2
    gate = jnp.moveaxis(st[..., :fc], 1, 2).reshape(B, r, d_ff)
    up = jnp.moveaxis(st[..., fc:], 1, 2).reshape(B, r, d_ff)
    return jnp.concatenate([gate, up], axis=-1)


def _pad_keys(a, Lp):
    return jnp.pad(a, ((0, 0), (0, Lp - a.shape[1]), (0, 0)))


def _group_layer(x, past, lw, gf, layer, depth, stacked, tiles):
    B, T, D = x.shape
    P = 0 if past is None else past[0].shape[1]
    L = P + T
    Lp = -(-L // KEY_TILE) * KEY_TILE
    topk = max(1, min(TOPK_MAX, L // 4))
    tables = _rope_tables(jnp.arange(P, L))

    stacked, (iw, aq16, iq16, ak16, avt16, ik16, bq16, bk16, bv16, cq16, ck16, cv16) = _project(
        x, lw["g_mix"], lw["w"], lw["bf_row"], tables, tiles["proj"], layer, depth, stacked)
    lf = stacked[-1][layer]

    if past is None:
        keys = (ak16, ik16)
        lf_all = lf
        conv_in = jnp.zeros((B, lw["nc"], SUBLANES, 2 * FF_CHUNK), F32)
        past_b = past_c = None
    else:
        p_ak, p_av, p_ik, p_bk, p_bv, p_ck, p_cv, p_lf, p_conv = past
        dup = lambda a: jnp.concatenate([a, a], axis=-1).astype(BF16)
        keys = tuple(jnp.concatenate([o, n], axis=1) for o, n in zip((dup(p_ak), dup(p_ik)), (ak16, ik16)))
        old_vt = jnp.swapaxes(p_av, 1, 2).astype(BF16)
        avt16 = jnp.concatenate([jnp.concatenate([old_vt, jnp.ones_like(old_vt)], axis=1), avt16], axis=2)
        lf_all = jnp.concatenate([p_lf, lf], axis=1)
        conv_in = _conv_state_in(p_conv, lw["halves"])
        flat = lambda a: a.reshape(B, P, BRANCH_W)
        past_b, past_c = (flat(p_bk), flat(p_bv)), (flat(p_ck), flat(p_cv))
    ak2, ik2 = (_pad_keys(a, Lp) for a in keys)
    v1t = jnp.pad(avt16, ((0, 0), (0, 0), (0, Lp - L)))

    fk_t = _cumsum_time(jnp.swapaxes(_pad_keys(lf_all, Lp), 1, 2))
    fq = jnp.swapaxes(fk_t[:, :, P:L], 1, 2)

    o_a = _dsa(aq16, iq16, iw, ak2, v1t, ik2, tq=tiles["dsa"], q_off=P, topk=topk)
    o_b = _sb(bq16, bk16, bv16, past_b, Lp, tq=tiles["sb"][0], tk=tiles["sb"][1], q_off=P)
    o_c = _fox(cq16, ck16, cv16, past_c, fq, fk_t, tq=tiles["fox"][0], tk=tiles["fox"][1], q_off=P)

    x = _merge(x, o_a, o_b, o_c, lw["g_mix"], lw["wg"], lw["wb"], lw["wo"], tiles["merge"])
    x, st = _ffn(x, lw["g_ffn"], conv_in, lw["wu"], lw["cw"], lw["cb"], lw["wd"], gf, tiles["ffn"],
                 layer == depth - 1)
    return x, stacked, _conv_state_out(st, lw["d_ff"])


def _tiles(T, P):
    pick = lambda want: min(want, T)
    if T % 512 == 0 and P % 512 == 0:
        sb, fox = (256, 256), (256, 512)
    else:
        assert T <= KEY_TILE
        sb = fox = (T, -(-(P + T) // KEY_TILE) * KEY_TILE)
    return dict(proj=pick(512), dsa=pick(128), sb=sb, fox=fox, merge=pick(512), ffn=pick(512))


def kernel(x_prompt, x_sample, cache_a_k, cache_a_v, cache_a_idx_k, cache_b_k, cache_b_v, cache_c_k, cache_c_v,
           cache_c_logf, state_ffn_conv, norm_mix, w_in, b_f, w_branch, w_gate, w_out, norm_ffn, w_up, conv_w,
           conv_b, w_down, norm_final):
    depth = w_in.shape[0]
    caches = (cache_a_k, cache_a_v, cache_a_idx_k, cache_b_k, cache_b_v, cache_c_k, cache_c_v, cache_c_logf,
              state_ffn_conv)
    layers = [_pack_layer(l, norm_mix, w_in, b_f, w_branch, w_gate, w_out, norm_ffn, w_up, conv_w, conv_b, w_down)
              for l in range(depth)]
    gf = norm_final[None]

    def trunk(x, past):
        B, T, _ = x.shape
        tiles = _tiles(T, 0 if past is None else past[0].shape[2])
        stacked, conv_states = None, []
        for l in range(depth):
            lp = None if past is None else tuple(c[l] for c in past)
            x, stacked, conv_state = _group_layer(x, lp, layers[l], gf, l, depth, stacked, tiles)
            conv_states.append(conv_state)
        ak, av, ik, bk, bv, ck, cv, lf = stacked
        heads = lambda a: a.reshape(depth, B, T, N_HEADS, HEAD_DIM)
        return x, (ak, av, ik, heads(bk), heads(bv), heads(ck), heads(cv), lf, jnp.stack(conv_states, axis=0))

    y_prompt, p_state = trunk(x_prompt, None)
    y_sample, s_state = trunk(x_sample, caches)
    return (y_prompt, y_sample) + p_state + s_state
```

```python
import functools

import jax
import jax.numpy as jnp
import numpy as np
from jax import lax
from jax.experimental import pallas as pl
from jax.experimental.pallas import tpu as pltpu

F32 = jnp.float32
BF16 = jnp.bfloat16

LANES = 128
SUBLANES = 8
HEAD_DIM = 64
N_HEADS = 8
N_PAIRS = N_HEADS // 2
BRANCH_W = N_HEADS * HEAD_DIM
N_BRANCH = 3
N_IDX = 4
IDX_DIM = 64
CHUNK = 64
CHUNK_SHIFT = 6
TOPK_MAX = 256
ROT_DIM = HEAD_DIM // 4
ROPE_THETA = 500000.0
CONV_W = 3
EPS = 1e-6
NEG = -1e30
KEY_TILE = 128
FF_CHUNK = 256
VMEM_LIMIT = 56 * 1024 * 1024

ROPE_W = 1024
MISC_W = 256
CF_LANE = 8
LOG2E = 1.4426950408889634
Q_SCALE = HEAD_DIM ** -0.5 * LOG2E


def _cparams(sem):
    return pltpu.CompilerParams(dimension_semantics=sem, vmem_limit_bytes=VMEM_LIMIT)


def _rms(x, g):
    return x * lax.rsqrt(jnp.mean(x * x, axis=-1, keepdims=True) + EPS) * g


def _sigmoid(z):
    return 1.0 / (1.0 + jnp.exp(-z))


def _log_sigmoid(z):
    return jnp.minimum(z, 0.0) - jnp.log1p(jnp.exp(-jnp.abs(z)))


def _dot_t(a, b):
    return lax.dot_general(a, b, (((1,), (1,)), ((), ())), preferred_element_type=F32)


def _dot(a, b):
    return jnp.dot(a, b, preferred_element_type=F32)


def _keep_head(blk, which):
    lane = lax.broadcasted_iota(jnp.int32, blk.shape, blk.ndim - 1)
    keep = (lane >= HEAD_DIM) if which else (lane < HEAD_DIM)
    return jnp.where(keep, blk, jnp.zeros_like(blk))


def _merge_pair(o0, o1):
    lane = lax.broadcasted_iota(jnp.int32, o0.shape, o0.ndim - 1)
    return jnp.where(lane < HEAD_DIM, o0, o1)


N_CACHE_ROWS = 8


def _proj_kernel(x_ref, g_ref, w_ref, bf_ref, rc_ref, rs1_ref, rs2_ref, *refs):
    (ak_ref, av_ref, ik_ref, bk_ref, bv_ref, ck_ref, cv_ref, lf_ref, iw_ref,
     aq16, iq16, ak16, avt16, ik16, bq16, bk16, bv16, cq16, ck16, cv16) = refs[-(N_CACHE_ROWS + 12):]
    xn = _rms(x_ref[0], g_ref[...]).astype(BF16)
    rc, rs1, rs2 = rc_ref[...], rs1_ref[...], rs2_ref[...]

    def cols(c0, width):
        return _dot(xn, w_ref[:, c0:c0 + width])

    def group(blk, g):
        return blk[:, g * LANES:(g + 1) * LANES]

    def rope(blk):
        half = ROT_DIM // 2
        return blk * rc + pltpu.roll(blk, LANES - half, 1) * rs1 + pltpu.roll(blk, half, 1) * rs2

    wide = cols(0, BRANCH_W)
    for g in range(4):
        aq16[0, :, g * LANES:(g + 1) * LANES] = (rope(group(wide, g)) * Q_SCALE).astype(BF16)
    wide = cols(BRANCH_W, ROPE_W - BRANCH_W)
    for g in range(2):
        iq16[0, :, g * LANES:(g + 1) * LANES] = (rope(group(wide, g)) * IDX_DIM ** -0.5).astype(BF16)
    blk = rope(group(wide, 2))
    ak_ref[0, 0] = blk[:, :HEAD_DIM]
    ak16[0] = blk.astype(BF16)
    blk = rope(group(wide, 3))
    ik_ref[0, 0] = blk[:, :IDX_DIM]
    ik16[0] = blk.astype(BF16)

    wide = cols(ROPE_W, MISC_W)
    blk = group(wide, 0)
    av_ref[0, 0] = blk[:, :HEAD_DIM]
    row = lax.broadcasted_iota(jnp.int32, (LANES, blk.shape[0]), 0)
    avt16[0] = jnp.where(row < HEAD_DIM, blk.T, 1.0).astype(BF16)
    blk = group(wide, 1)
    iw_ref[0] = blk[:, :SUBLANES] * N_IDX ** -0.5
    lf_ref[0, 0] = _log_sigmoid(blk[:, CF_LANE:CF_LANE + N_HEADS] + bf_ref[:, CF_LANE:CF_LANE + N_HEADS])

    base = ROPE_W + MISC_W
    outs = ((None, bq16), (bk_ref, bk16), (bv_ref, bv16), (None, cq16), (ck_ref, ck16), (cv_ref, cv16))
    for n, (o32, o16) in enumerate(outs):
        wide = cols(base + n * BRANCH_W, BRANCH_W)
        if o32 is None:
            o16[0] = (wide * Q_SCALE).astype(BF16)
        else:
            o32[0, 0] = wide
            o16[0] = wide.astype(BF16)


def _project(x, g, w, bf_row, tables, tm, layer, depth, stacked):
    B, T, D = x.shape
    tok = lambda w_, dt: jax.ShapeDtypeStruct((B, T, w_), dt)
    tspec = lambda w_: pl.BlockSpec((1, tm, w_), lambda b, t: (b, t, 0))
    const = lambda a: pl.BlockSpec(a.shape, lambda b, t: (0,) * a.ndim, pipeline_mode=pl.Buffered(1))
    cache_w = (64, 64, 64, 512, 512, 512, 512, 8)
    cache_shapes = [jax.ShapeDtypeStruct((depth, B, T, w_), F32) for w_ in cache_w]
    cache_specs = [pl.BlockSpec((1, 1, tm, w_), lambda b, t: (layer, b, t, 0)) for w_ in cache_w]
    work = [(tok(w_, dt), tspec(w_)) for w_, dt in ((8, F32), (512, BF16), (256, BF16), (128, BF16))]
    work += [(jax.ShapeDtypeStruct((B, LANES, T), BF16), pl.BlockSpec((1, LANES, tm), lambda b, t: (b, 0, t)))]
    work += [(tok(w_, BF16), tspec(w_)) for w_ in (128,) + (512,) * 6]
    work_shapes, work_specs = [s for s, _ in work], [p for _, p in work]
    rspec = pl.BlockSpec((tm, LANES), lambda b, t: (t, 0))
    prev = () if stacked is None else tuple(stacked)
    n_in = 7
    outs = pl.pallas_call(
        _proj_kernel,
        grid=(B, T // tm),
        in_specs=[tspec(D), const(g), const(w), const(bf_row), rspec, rspec, rspec]
        + [pl.BlockSpec(memory_space=pl.ANY)] * len(prev),
        out_specs=cache_specs + work_specs,
        out_shape=cache_shapes + work_shapes,
        input_output_aliases={n_in + i: i for i in range(len(prev))},
        compiler_params=_cparams(("parallel", "parallel")),
        name="in_proj",
    )(x, g, w, bf_row, *tables, *prev)
    return outs[:N_CACHE_ROWS], outs[N_CACHE_ROWS:]


def _split3(x):
    hi = x.astype(BF16)
    r = x - hi.astype(F32)
    mid = r.astype(BF16)
    lo = (r - mid.astype(F32)).astype(BF16)
    return hi, mid, lo


def _cumsum_kernel(lf_ref, o_ref, *, blk):
    L = lf_ref.shape[-1]
    r = lax.broadcasted_iota(jnp.int32, (blk, blk), 0)
    c = lax.broadcasted_iota(jnp.int32, (blk, blk), 1)
    upper = jnp.where(r <= c, 1.0, 0.0).astype(BF16)
    carry = jnp.zeros((N_HEADS, 1), F32)
    for j in range(L // blk):
        hi, mid, lo = _split3(lf_ref[0, :, j * blk:(j + 1) * blk])
        f = _dot(hi, upper) + _dot(mid, upper) + _dot(lo, upper) + carry
        o_ref[0, :, j * blk:(j + 1) * blk] = f * LOG2E
        carry = f[:, blk - 1:blk]


def _cumsum_time(lf_t):
    B, H, L = lf_t.shape
    blk = LANES
    assert L % blk == 0
    spec = pl.BlockSpec((1, H, L), lambda b: (b, 0, 0))
    return pl.pallas_call(
        functools.partial(_cumsum_kernel, blk=blk),
        grid=(B,),
        in_specs=[spec],
        out_specs=spec,
        out_shape=jax.ShapeDtypeStruct((B, H, L), F32),
        compiler_params=_cparams(("parallel",)),
        name="logf_cumsum",
    )(lf_t)


def _head_column(blk, h):
    lane = lax.broadcasted_iota(jnp.int32, blk.shape, 1)
    return jnp.sum(jnp.where(lane == h, blk, 0.0), axis=1, keepdims=True)


def _positions(tq, tk, q0, k0):
    qpos = q0 + lax.broadcasted_iota(jnp.int32, (tq, tk), 0)
    kpos = k0 + lax.broadcasted_iota(jnp.int32, (tq, tk), 1)
    return qpos, kpos


def _pair_specs(tq, Lp):
    qspec = pl.BlockSpec((1, tq, LANES), lambda b, hp, qi: (b, qi, hp))
    kspec = pl.BlockSpec((1, Lp, LANES), lambda b, hp, qi: (b, 0, hp))
    return qspec, kspec


def _split_past(rest, has_past):
    return (rest[:2], rest[2:-2], rest[-2:]) if has_past else ((), rest, ())


def _key_views(k_ref, v_ref, past, bufs):
    if not past:
        return k_ref.at[0], v_ref.at[0]
    for old, new, buf in zip(past, (k_ref, v_ref), bufs):
        P, T = old.shape[1], new.shape[1]
        buf[0:P, :] = old[0].astype(BF16)
        buf[P:P + T, :] = new[0]
        if buf.shape[0] > P + T:
            buf[P + T:, :] = jnp.zeros((buf.shape[0] - P - T, LANES), BF16)
    return bufs


def _past_specs_and_bufs(past, T, Lp):
    if past is None:
        return [], [], ()
    P = past[0].shape[1]
    spec = pl.BlockSpec((1, P, LANES), lambda b, hp, qi: (b, 0, hp))
    return [spec, spec], [pltpu.VMEM((Lp, LANES), BF16)] * 2, tuple(past)


def _fox_kernel(q_ref, k_ref, v_ref, *rest, tq, tk, q_off, has_past):
    past, (fq_ref, fk_ref, o_ref, m_ref, acc_ref), bufs = _split_past(rest, has_past)
    keys, values = _key_views(k_ref, v_ref, past, bufs)
    hp = pl.program_id(1)
    q0 = q_off + pl.program_id(2) * tq
    diag = q0 // tk
    qp = q_ref[0]
    fq_blk = fq_ref[0]
    qm = (_keep_head(qp, 0), _keep_head(qp, 1))
    fq = (_head_column(fq_blk, hp * 2), _head_column(fq_blk, hp * 2 + 1))
    ones = jnp.ones((tk, LANES), BF16)
    m_ref[...] = jnp.full(m_ref.shape, NEG, F32)
    acc_ref[...] = jnp.zeros(acc_ref.shape, F32)

    def score(j, masked):
        k0 = pl.multiple_of(j * tk, tk)
        kt = keys[pl.ds(k0, tk), :]
        if masked:
            qpos, kpos = _positions(tq, tk, q0, k0)
            causal = kpos <= qpos
        out = []
        for hh in range(2):
            fk = fk_ref[0, hp * 2 + hh, :, pl.ds(k0, tk)]
            s = _dot_t(qm[hh], kt) + fq[hh] - fk
            out.append(jnp.where(causal, s, NEG) if masked else s)
        return tuple(out)

    def fold(j, scores):
        v1 = jnp.concatenate([values[pl.ds(pl.multiple_of(j * tk, tk), tk), :], ones], axis=1)
        m_old = [m_ref[hh] for hh in range(2)]
        m_new = [jnp.maximum(m_old[hh], jnp.max(scores[hh], axis=1, keepdims=True)) for hh in range(2)]
        p = [jnp.exp2(scores[hh] - m_new[hh]).astype(BF16) for hh in range(2)]
        for hh in range(2):
            acc_ref[hh] = jnp.exp2(m_old[hh] - m_new[hh]) * acc_ref[hh] + _dot(p[hh], v1)
            m_ref[hh] = m_new[hh]

    fold(diag, score(diag, True))

    def body(i, _):
        j = diag - 1 - i
        fold(j, score(j, False))
        return 0

    lax.fori_loop(0, diag, body, 0)
    o0 = acc_ref[0, :, :LANES] / acc_ref[0, :, LANES:]
    o1 = acc_ref[1, :, :LANES] / acc_ref[1, :, LANES:]
    o_ref[0] = _merge_pair(o0, o1).astype(o_ref.dtype)


def _fox(q, k, v, past, fq, fk_t, *, tq, tk, q_off):
    B, T, _ = q.shape
    Lp = fk_t.shape[-1]
    qspec, kspec = _pair_specs(tq, k.shape[1])
    past_specs, bufs, past = _past_specs_and_bufs(past, T, Lp)
    return pl.pallas_call(
        functools.partial(_fox_kernel, tq=tq, tk=tk, q_off=q_off, has_past=bool(past)),
        grid=(B, N_PAIRS, T // tq),
        in_specs=[qspec, kspec, kspec] + past_specs
        + [pl.BlockSpec((1, tq, N_HEADS), lambda b, hp, qi: (b, qi, 0)),
           pl.BlockSpec((1, N_HEADS, 1, Lp), lambda b, hp, qi: (b, 0, 0, 0))],
        out_specs=qspec,
        out_shape=jax.ShapeDtypeStruct(q.shape, BF16),
        scratch_shapes=[pltpu.VMEM((2, tq, 1), F32), pltpu.VMEM((2, tq, 2 * LANES), F32)] + bufs,
        compiler_params=_cparams(("parallel", "parallel", "parallel")),
        name="fox_attn",
    )(q, k, v, *past, fq, fk_t[:, :, None, :])


def _sb_kernel(q_ref, k_ref, v_ref, *rest, tq, tk, q_off, has_past):
    past, (o_ref, acc_ref, tail_ref), bufs = _split_past(rest, has_past)
    keys, values = _key_views(k_ref, v_ref, past, bufs)
    q0 = q_off + pl.program_id(2) * tq
    diag = q0 // tk
    qp = q_ref[0]
    qm = (_keep_head(qp, 0), _keep_head(qp, 1))
    r = lax.broadcasted_iota(jnp.int32, (LANES, 2 * LANES), 0)
    c = lax.broadcasted_iota(jnp.int32, (LANES, 2 * LANES), 1)
    suffix_w = jnp.where((c >= LANES) | (r > c), 1.0, 0.0).astype(BF16)
    acc_ref[...] = jnp.zeros(acc_ref.shape, F32)
    tail_ref[...] = jnp.zeros(tail_ref.shape, F32)

    def tile(j, masked):
        k0 = pl.multiple_of(j * tk, tk)
        kt = keys[pl.ds(k0, tk), :]
        vt = values[pl.ds(k0, tk), :]
        if masked:
            qpos, kpos = _positions(tq, tk, q0, k0)
            strict = kpos < qpos
        heads = (0, 1)
        z = [_dot_t(qm[hh], kt) for hh in heads]
        ls, la, after = [None, None], [None, None], [None, None]
        for hh in heads:
            ls[hh] = jnp.minimum(z[hh], 0.0) - jnp.log2(1.0 + jnp.exp2(-jnp.abs(z[hh])))
            la[hh] = ls[hh] - z[hh]
            if masked:
                la[hh] = jnp.where(strict, la[hh], 0.0)
                ls[hh] = jnp.where(strict, ls[hh], NEG)
            la[hh] = la[hh].astype(BF16)
        for hh in heads:
            run = tail_ref[hh]
            blocks = [None] * (tk // LANES)
            for b in reversed(range(tk // LANES)):
                res = _dot(la[hh][:, b * LANES:(b + 1) * LANES], suffix_w)
                blocks[b] = res[:, :LANES] + run
                run = run + res[:, LANES:]
            tail_ref[hh] = run
            after[hh] = jnp.concatenate(blocks, axis=1)
        a = [jnp.exp2(ls[hh] + after[hh]).astype(BF16) for hh in heads]
        for hh in heads:
            acc_ref[hh] += _dot(a[hh], vt)

    tile(diag, True)

    def body(i, _):
        tile(diag - 1 - i, False)
        return 0

    lax.fori_loop(0, diag, body, 0)
    o_ref[0] = _merge_pair(acc_ref[0], acc_ref[1]).astype(o_ref.dtype)


def _sb(q, k, v, past, Lp, *, tq, tk, q_off):
    B, T, _ = q.shape
    qspec, kspec = _pair_specs(tq, k.shape[1])
    past_specs, bufs, past = _past_specs_and_bufs(past, T, Lp)
    return pl.pallas_call(
        functools.partial(_sb_kernel, tq=tq, tk=tk, q_off=q_off, has_past=bool(past)),
        grid=(B, N_PAIRS, T // tq),
        in_specs=[qspec, kspec, kspec] + past_specs,
        out_specs=qspec,
        out_shape=jax.ShapeDtypeStruct(q.shape, BF16),
        scratch_shapes=[pltpu.VMEM((2, tq, LANES), F32), pltpu.VMEM((2, tq, LANES), F32)] + bufs,
        compiler_params=_cparams(("parallel", "parallel", "parallel")),
        name="sb_attn",
    )(q, k, v, *past)


def _const_key(v):
    bits = int(np.float32(v).view(np.int32))
    return (bits ^ 0x7FFFFFFF) + 1 if bits < 0 else bits


def _float_key(x):
    bits = lax.bitcast_convert_type(x, jnp.int32)
    return jnp.where(bits < 0, jnp.bitwise_xor(bits, 0x7FFFFFFF) + 1, bits)


KEY_MIN = -2 ** 31
KEY_HALF_NEG = _const_key(0.5 * NEG)


KEY_MAX = 2 ** 31 - 1
KEY_BITS = 32
COUNT_CHAINS = 8


def _dsa_kernel(aq_ref, iq_ref, iwt_ref, k_ref, vt_ref, ik_ref, o_ref, key_ref, q8_ref, m_ref, acc_ref,
                *, tq, tk, q_off, n_rows, topk):
    q0 = q_off + pl.program_id(1) * tq
    n_tiles = (q0 + n_rows + tk - 1) // tk

    def tile_start(j):
        return pl.multiple_of(j * tk, LANES)

    iq = iq_ref[0]
    iwt = iwt_ref[0]
    qh = [_keep_head(iq[:, (h // 2) * LANES:(h // 2 + 1) * LANES], h % 2) for h in range(N_IDX)]
    q_chunk = jnp.right_shift(q0 + lax.broadcasted_iota(jnp.int32, (tk, tq), 1), CHUNK_SHIFT)

    def score_tile(j, _):
        k0 = tile_start(j)
        ik = ik_ref[0, pl.ds(k0, tk), :]
        sc = jnp.zeros((tk, tq), F32)
        for h in range(N_IDX):
            sc = sc + jnp.maximum(_dot_t(ik, qh[h]), 0.0) * iwt[h:h + 1, :]
        key_chunk = jnp.right_shift(k0 + lax.broadcasted_iota(jnp.int32, (tk, tq), 0), CHUNK_SHIFT)
        key_ref[pl.ds(k0, tk), :] = _float_key(jnp.where(key_chunk <= q_chunk, sc, NEG))
        return 0

    lax.fori_loop(0, n_tiles, score_tile, 0)

    def search():
        def step(_, cr):
            lo, hi, above_hi = cr
            mid = jnp.right_shift(lo, 1) + jnp.right_shift(hi, 1) + jnp.bitwise_and(jnp.bitwise_and(lo, hi), 1)

            def count_tile(j, cnt):
                above = jnp.where(key_ref[pl.ds(tile_start(j), tk), :] > mid, 1.0, 0.0)
                return cnt + jnp.sum(above.reshape((-1,) + cnt.shape), axis=0)

            cnt = lax.fori_loop(0, n_tiles, count_tile, jnp.zeros((COUNT_CHAINS, SUBLANES, tq), F32))
            cnt = jnp.sum(jnp.sum(cnt, axis=0), axis=0, keepdims=True)
            less = cnt < topk
            return jnp.where(less, lo, mid), jnp.where(less, mid, hi), jnp.where(less, cnt, above_hi)

        init = (jnp.full((1, tq), KEY_MIN, jnp.int32), jnp.full((1, tq), KEY_MAX, jnp.int32), jnp.zeros((1, tq), F32))
        _, hi, above_hi = lax.fori_loop(0, KEY_BITS, step, init)
        return hi, topk - above_hi

    thr, room = lax.cond(q0 + n_rows > topk, search,
                         lambda: (jnp.full((1, tq), KEY_MIN, jnp.int32), jnp.zeros((1, tq), F32)))
    row_ok = lax.broadcasted_iota(jnp.int32, (1, tq), 1) < n_rows
    room = jnp.where(row_ok, room, 0.0)

    @pl.when(jnp.max(room) > 0.0)
    def _():
        r = lax.broadcasted_iota(jnp.int32, (LANES, LANES), 0)
        c = lax.broadcasted_iota(jnp.int32, (LANES, LANES), 1)
        before = jnp.where(c < r, 1.0, 0.0).astype(BF16)

        def tie_tile(j, seen):
            k0 = tile_start(j)
            for b in range(tk // LANES):
                rows = pl.ds(k0 + b * LANES, LANES)
                keys = key_ref[rows, :]
                eq = keys == thr
                eqf = jnp.where(eq, 1.0, 0.0)
                rank = _dot(before, eqf.astype(BF16)) + seen
                key_ref[rows, :] = jnp.where(eq, jnp.where(rank < room, keys + 1, keys), keys)
                seen = seen + jnp.sum(eqf, axis=0, keepdims=True)
            return seen

        lax.fori_loop(0, n_tiles, tie_tile, jnp.zeros((1, tq), F32))

    aq = aq_ref[0]
    for h in range(N_HEADS):
        q8_ref[h * tq:(h + 1) * tq, :] = _keep_head(aq[:, (h // 2) * LANES:(h // 2 + 1) * LANES], h % 2)
    m_ref[...] = jnp.full(m_ref.shape, NEG, F32)
    acc_ref[...] = jnp.zeros(acc_ref.shape, F32)
    thr_sel = jnp.maximum(thr, KEY_HALF_NEG)

    def attend_tile(j, _):
        k0 = tile_start(j)
        kt = k_ref[0, pl.ds(k0, tk), :]
        v1t = vt_ref[0, :, pl.ds(k0, tk)]
        bias = jnp.where(key_ref[pl.ds(k0, tk), :] > thr_sel, 0.0, NEG)
        st = _dot_t(kt, q8_ref[...])
        st = jnp.concatenate([st[:, h * tq:(h + 1) * tq] + bias for h in range(N_HEADS)], axis=1)
        m_old = m_ref[...]
        m_new = jnp.maximum(m_old, jnp.max(st, axis=0, keepdims=True))
        p = jnp.exp2(st - m_new).astype(BF16)
        acc_ref[...] = jnp.exp2(m_old - m_new) * acc_ref[...] + _dot(v1t, p)
        m_ref[...] = m_new
        return 0

    lax.fori_loop(0, n_tiles, attend_tile, 0)
    for hp in range(N_PAIRS):
        pair = []
        for h in (2 * hp, 2 * hp + 1):
            a = acc_ref[:, h * tq:(h + 1) * tq]
            pair.append(a[:HEAD_DIM] / a[HEAD_DIM:])
        o_ref[0, :, hp * LANES:(hp + 1) * LANES] = jnp.concatenate(pair, axis=0).T.astype(o_ref.dtype)


def _dsa(aq, iq, iw, k2, v1t, ik2, *, tq, q_off, topk):
    B, T, _ = aq.shape
    Lp = k2.shape[1]
    tk = Lp if T <= LANES else next(t for t in (512, 384, 256, 128) if Lp % t == 0)
    Tp = -(-T // LANES) * LANES
    tq = max(tq, LANES)
    n_rows = min(T, tq)
    pad_t = lambda a: jnp.pad(a, ((0, 0), (0, Tp - T), (0, 0)))
    qspec = lambda w: pl.BlockSpec((1, tq, w), lambda b, qi: (b, qi, 0))
    kspec = pl.BlockSpec((1, Lp, LANES), lambda b, qi: (b, 0, 0))
    out = pl.pallas_call(
        functools.partial(_dsa_kernel, tq=tq, tk=tk, q_off=q_off, n_rows=n_rows, topk=topk),
        grid=(B, Tp // tq),
        in_specs=[qspec(BRANCH_W), qspec(N_IDX * IDX_DIM), pl.BlockSpec((1, SUBLANES, tq), lambda b, qi: (b, 0, qi)),
                  kspec, pl.BlockSpec((1, LANES, Lp), lambda b, qi: (b, 0, 0)), kspec],
        out_specs=qspec(BRANCH_W),
        out_shape=jax.ShapeDtypeStruct((B, Tp, BRANCH_W), BF16),
        scratch_shapes=[pltpu.VMEM((Lp, tq), jnp.int32), pltpu.VMEM((N_HEADS * tq, LANES), BF16),
                        pltpu.VMEM((1, N_HEADS * tq), F32), pltpu.VMEM((LANES, N_HEADS * tq), F32)],
        compiler_params=_cparams(("parallel", "parallel")),
        name="dsa_attn",
    )(pad_t(aq), pad_t(iq), jnp.swapaxes(pad_t(iw), 1, 2), k2, v1t, ik2)
    return out[:, :T]


def _merge_kernel(x_ref, oa_ref, ob_ref, oc_ref, g_ref, wg_ref, wb_ref, wo_ref, y_ref):
    x = x_ref[0]
    D = x.shape[-1]
    xn = _rms(x, g_ref[...]).astype(BF16)
    merged = jnp.zeros(x.shape, F32)
    for n, o_ref in enumerate((oa_ref, ob_ref, oc_ref)):
        gate = _sigmoid(_dot(xn, wg_ref[:, n * D:(n + 1) * D]))
        merged = merged + gate * _dot(o_ref[0], wb_ref[n])
    y_ref[0] = x + _dot(merged.astype(BF16), wo_ref[...])


def _merge(x, oa, ob, oc, g, wg, wb, wo, tm):
    B, T, D = x.shape
    tspec = lambda w_: pl.BlockSpec((1, tm, w_), lambda b, t: (b, t, 0))
    const = lambda a: pl.BlockSpec(a.shape, lambda b, t: (0,) * a.ndim)
    return pl.pallas_call(
        _merge_kernel,
        grid=(B, T // tm),
        in_specs=[tspec(D), tspec(BRANCH_W), tspec(BRANCH_W), tspec(BRANCH_W), const(g), const(wg), const(wb),
                  const(wo)],
        out_specs=tspec(D),
        out_shape=jax.ShapeDtypeStruct(x.shape, F32),
        compiler_params=_cparams(("parallel", "parallel")),
        name="merge_out",
    )(x, oa, ob, oc, g, wg, wb, wo)


def _ffn_kernel(x_ref, g_ref, past_ref, wu_ref, cw_ref, cb_ref, wd_ref, gf_ref, y_ref, st_ref,
                hbuf_ref, carry_ref, acc_ref, *, tm, final_norm):
    fc = FF_CHUNK
    d_ff = wd_ref.shape[0]
    n_chunks = d_ff // fc

    @pl.when(pl.program_id(1) == 0)
    def _():
        carry_ref[...] = past_ref[0]

    x = x_ref[0]
    xn = _rms(x, g_ref[...]).astype(BF16)
    acc_ref[...] = jnp.zeros(acc_ref.shape, F32)

    for c in range(n_chunks):
        hbuf = hbuf_ref.at[c % 2]
        both = lambda ref: jnp.concatenate([ref[:, c * fc:(c + 1) * fc], ref[:, d_ff + c * fc:d_ff + (c + 1) * fc]],
                                           axis=1)
        h = jnp.concatenate([_dot(xn, wu_ref[:, c * fc:(c + 1) * fc]),
                             _dot(xn, wu_ref[:, d_ff + c * fc:d_ff + (c + 1) * fc])], axis=1)
        hbuf[0:SUBLANES, :] = carry_ref[c]
        hbuf[SUBLANES:SUBLANES + tm, :] = h
        carry_ref[c] = h[tm - SUBLANES:tm, :]
        cw = both(cw_ref)
        hc = (both(cb_ref) + cw[0:1] * hbuf[SUBLANES - 2:SUBLANES - 2 + tm, :]
              + cw[1:2] * hbuf[SUBLANES - 1:SUBLANES - 1 + tm, :] + cw[2:3] * h)
        gate, up = hc[:, :fc], hc[:, fc:]
        act = gate * _sigmoid(gate) * up
        acc_ref[...] += _dot(act.astype(BF16), wd_ref[c * fc:(c + 1) * fc, :])
    st_ref[0] = carry_ref[...]
    y = x + acc_ref[...]
    if final_norm:
        y = _rms(y, gf_ref[...])
    y_ref[0] = y


def _ffn(x, g, past, wu, cw, cb, wd, gf, tm, final_norm):
    B, T, D = x.shape
    nc, fc2 = wd.shape[0] // FF_CHUNK, 2 * FF_CHUNK
    tspec = pl.BlockSpec((1, tm, D), lambda b, t: (b, t, 0))
    const = lambda a: pl.BlockSpec(a.shape, lambda b, t: (0,) * a.ndim, pipeline_mode=pl.Buffered(1))
    sspec = pl.BlockSpec((1, nc, SUBLANES, fc2), lambda b, t: (b, 0, 0, 0))
    return pl.pallas_call(
        functools.partial(_ffn_kernel, tm=tm, final_norm=final_norm),
        grid=(B, T // tm),
        in_specs=[tspec, const(g), sspec, const(wu), const(cw), const(cb), const(wd), const(gf)],
        out_specs=[tspec, sspec],
        out_shape=[jax.ShapeDtypeStruct(x.shape, F32), jax.ShapeDtypeStruct((B, nc, SUBLANES, fc2), F32)],
        scratch_shapes=[pltpu.VMEM((2, tm + SUBLANES, fc2), F32), pltpu.VMEM((nc, SUBLANES, fc2), F32),
                        pltpu.VMEM((tm, D), F32)],
        compiler_params=_cparams(("arbitrary", "arbitrary")),
        name="conv_ffn",
    )(x, g, past, wu, cw, cb, wd, gf)


def _rope_tables(pos):
    half = ROT_DIM // 2
    freq = ROPE_THETA ** (-jnp.arange(half, dtype=F32) / half)
    ang = pos.astype(F32)[:, None] * freq[None, :]
    lane = np.arange(LANES) % HEAD_DIM
    cos = jnp.cos(ang)[:, lane % half]
    sin = jnp.sin(ang)[:, lane % half]
    rc = jnp.where(lane < ROT_DIM, cos, 1.0)
    rs1 = jnp.where(lane < half, -sin, 0.0)
    rs2 = jnp.where((lane >= half) & (lane < ROT_DIM), sin, 0.0)
    return rc, rs1, rs2


IN_SIZES = (BRANCH_W, HEAD_DIM, HEAD_DIM, N_IDX * IDX_DIM, IDX_DIM, N_IDX) + (BRANCH_W,) * 6 + (N_HEADS,)
PACKED_W = ROPE_W + MISC_W + 6 * BRANCH_W


def _pack_w_kernel(w_ref, o_ref, s_ref):
    offs = np.concatenate([[0], np.cumsum(IN_SIZES)]).tolist()
    a_q, a_k, a_v, i_q, i_k, i_w, b_q = offs[:7]
    c_f = offs[12]
    src = lambda c0, width: w_ref[0, :, c0:c0 + width]
    s_ref[:, 0:BRANCH_W] = src(a_q, BRANCH_W)
    s_ref[:, BRANCH_W:BRANCH_W + 256] = src(i_q, 256)
    for dst, c0 in ((768, a_k), (896, i_k), (ROPE_W, a_v)):
        s_ref[:, dst:dst + HEAD_DIM] = src(c0, HEAD_DIM)
        s_ref[:, dst + HEAD_DIM:dst + 2 * HEAD_DIM] = src(c0, HEAD_DIM)
    misc = ROPE_W + LANES
    s_ref[:, misc:misc + LANES] = jnp.zeros((s_ref.shape[0], LANES), F32)
    s_ref[:, misc:misc + N_IDX] = src(i_w, N_IDX)
    s_ref[:, misc + CF_LANE:misc + CF_LANE + N_HEADS] = src(c_f, N_HEADS)
    s_ref[:, ROPE_W + MISC_W:PACKED_W] = src(b_q, 6 * BRANCH_W)
    o_ref[...] = s_ref[...].astype(BF16)


def _pack_w(w_in, l):
    D = w_in.shape[1]
    rows = 256
    return pl.pallas_call(
        _pack_w_kernel,
        grid=(D // rows,),
        in_specs=[pl.BlockSpec((1, rows, w_in.shape[2]), lambda r: (l, r, 0))],
        out_specs=pl.BlockSpec((rows, PACKED_W), lambda r: (r, 0)),
        out_shape=jax.ShapeDtypeStruct((D, PACKED_W), BF16),
        scratch_shapes=[pltpu.VMEM((rows, PACKED_W), F32)],
        compiler_params=_cparams(("parallel",)),
        name="pack_w_in",
    )(w_in)


def _pack_layer(l, norm_mix, w_in, b_f, w_branch, w_gate, w_out, norm_ffn, w_up, conv_w, conv_b, w_down):
    w = _pack_w(w_in, l)
    bf_row = jnp.zeros((1, LANES), F32).at[0, CF_LANE:CF_LANE + N_HEADS].set(b_f[l])

    d_ff = w_down.shape[1]
    fc = FF_CHUNK
    nc = d_ff // fc
    assert nc * fc == d_ff
    halves = lambda a: jnp.concatenate([a[..., :d_ff].reshape(a.shape[:-1] + (nc, fc)),
                                        a[..., d_ff:].reshape(a.shape[:-1] + (nc, fc))], axis=-1)
    wu, cw, cb, wd = w_up[l].astype(BF16), conv_w[l], conv_b[l][None], w_down[l].astype(BF16)
    return dict(g_mix=norm_mix[l][None], w=w, bf_row=bf_row, wg=w_gate[l].astype(BF16),
                wb=w_branch[l].astype(BF16), wo=w_out[l].astype(BF16), g_ffn=norm_ffn[l][None],
                wu=wu, cw=cw, cb=cb, wd=wd, halves=halves, nc=nc, d_ff=d_ff)


def _conv_state_in(state, halves):
    st = jnp.moveaxis(halves(state), 2, 1)
    return jnp.pad(st, ((0, 0), (0, 0), (SUBLANES - (CONV_W - 1), 0), (0, 0)))


def _conv_state_out(st, d_ff):
    st = st[:, :, SUBLANES - (CONV_W - 1):, :]
    B, nc, r, fc2 = st.shape
    fc = fc2 // 2
    gate = jnp.moveaxis(st[..., :fc], 1, 2).reshape(B, r, d_ff)
    up = jnp.moveaxis(st[..., fc:], 1, 2).reshape(B, r, d_ff)
    return jnp.concatenate([gate, up], axis=-1)


def _pad_keys(a, Lp):
    return jnp.pad(a, ((0, 0), (0, Lp - a.shape[1]), (0, 0)))


def _group_layer(x, past, lw, gf, layer, depth, stacked, tiles):
    B, T, D = x.shape
    P = 0 if past is None else past[0].shape[1]
    L = P + T
    Lp = -(-L // KEY_TILE) * KEY_TILE
    topk = max(1, min(TOPK_MAX, L // 4))
    tables = _rope_tables(jnp.arange(P, L))

    stacked, (iw, aq16, iq16, ak16, avt16, ik16, bq16, bk16, bv16, cq16, ck16, cv16) = _project(
        x, lw["g_mix"], lw["w"], lw["bf_row"], tables, tiles["proj"], layer, depth, stacked)
    lf = stacked[-1][layer]

    if past is None:
        keys = (ak16, ik16)
        lf_all = lf
        conv_in = jnp.zeros((B, lw["nc"], SUBLANES, 2 * FF_CHUNK), F32)
        past_b = past_c = None
    else:
        p_ak, p_av, p_ik, p_bk, p_bv, p_ck, p_cv, p_lf, p_conv = past
        dup = lambda a: jnp.concatenate([a, a], axis=-1).astype(BF16)
        keys = tuple(jnp.concatenate([o, n], axis=1) for o, n in zip((dup(p_ak), dup(p_ik)), (ak16, ik16)))
        old_vt = jnp.swapaxes(p_av, 1, 2).astype(BF16)
        avt16 = jnp.concatenate([jnp.concatenate([old_vt, jnp.ones_like(old_vt)], axis=1), avt16], axis=2)
        lf_all = jnp.concatenate([p_lf, lf], axis=1)
        conv_in = _conv_state_in(p_conv, lw["halves"])
        flat = lambda a: a.reshape(B, P, BRANCH_W)
        past_b, past_c = (flat(p_bk), flat(p_bv)), (flat(p_ck), flat(p_cv))
    ak2, ik2 = (_pad_keys(a, Lp) for a in keys)
    v1t = jnp.pad(avt16, ((0, 0), (0, 0), (0, Lp - L)))

    fk_t = _cumsum_time(jnp.swapaxes(_pad_keys(lf_all, Lp), 1, 2))
    fq = jnp.swapaxes(fk_t[:, :, P:L], 1, 2)

    o_a = _dsa(aq16, iq16, iw, ak2, v1t, ik2, tq=tiles["dsa"], q_off=P, topk=topk)
    o_b = _sb(bq16, bk16, bv16, past_b, Lp, tq=tiles["sb"][0], tk=tiles["sb"][1], q_off=P)
    o_c = _fox(cq16, ck16, cv16, past_c, fq, fk_t, tq=tiles["fox"][0], tk=tiles["fox"][1], q_off=P)

    x = _merge(x, o_a, o_b, o_c, lw["g_mix"], lw["wg"], lw["wb"], lw["wo"], tiles["merge"])
    x, st = _ffn(x, lw["g_ffn"], conv_in, lw["wu"], lw["cw"], lw["cb"], lw["wd"], gf, tiles["ffn"],
                 layer == depth - 1)
    return x, stacked, _conv_state_out(st, lw["d_ff"])


def _tiles(T, P):
    pick = lambda want: min(want, T)
    if T % 512 == 0 and P % 512 == 0:
        sb, fox = (256, 256), (256, 512)
    else:
        assert T <= KEY_TILE
        sb = fox = (T, -(-(P + T) // KEY_TILE) * KEY_TILE)
    return dict(proj=pick(512), dsa=pick(128), sb=sb, fox=fox, merge=pick(512), ffn=pick(512))


def kernel(x_prompt, x_sample, cache_a_k, cache_a_v, cache_a_idx_k, cache_b_k, cache_b_v, cache_c_k, cache_c_v,
           cache_c_logf, state_ffn_conv, norm_mix, w_in, b_f, w_branch, w_gate, w_out, norm_ffn, w_up, conv_w,
           conv_b, w_down, norm_final):
    depth = w_in.shape[0]
    caches = (cache_a_k, cache_a_v, cache_a_idx_k, cache_b_k, cache_b_v, cache_c_k, cache_c_v, cache_c_logf,
              state_ffn_conv)
    layers = [_pack_layer(l, norm_mix, w_in, b_f, w_branch, w_gate, w_out, norm_ffn, w_up, conv_w, conv_b, w_down)
              for l in range(depth)]
    gf = norm_final[None]

    def trunk(x, past):
        B, T, _ = x.shape
        tiles = _tiles(T, 0 if past is None else past[0].shape[2])
        stacked, conv_states = None, []
        for l in range(depth):
            lp = None if past is None else tuple(c[l] for c in past)
            x, stacked, conv_state = _group_layer(x, lp, layers[l], gf, l, depth, stacked, tiles)
            conv_states.append(conv_state)
        ak, av, ik, bk, bv, ck, cv, lf = stacked
        heads = lambda a: a.reshape(depth, B, T, N_HEADS, HEAD_DIM)
        return x, (ak, av, ik, heads(bk), heads(bv), heads(ck), heads(cv), lf, jnp.stack(conv_states, axis=0))

    y_prompt, p_state = trunk(x_prompt, None)
    y_sample, s_state = trunk(x_sample, caches)
    return (y_prompt, y_sample) + p_state + s_state
```

```python
import functools

import jax
import jax.numpy as jnp
import numpy as np
from jax import lax
from jax.experimental import pallas as pl
from jax.experimental.pallas import tpu as pltpu

F32 = jnp.float32
BF16 = jnp.bfloat16

LANES = 128
SUBLANES = 8
HEAD_DIM = 64
N_HEADS = 8
N_PAIRS = N_HEADS // 2
BRANCH_W = N_HEADS * HEAD_DIM
N_BRANCH = 3
N_IDX = 4
IDX_DIM = 64
CHUNK = 64
CHUNK_SHIFT = 6
TOPK_MAX = 256
ROT_DIM = HEAD_DIM // 4
ROPE_THETA = 500000.0
CONV_W = 3
EPS = 1e-6
NEG = -1e30
KEY_TILE = 128
FF_CHUNK = 256
VMEM_LIMIT = 56 * 1024 * 1024

ROPE_W = 1024
MISC_W = 256
CF_LANE = 8
LOG2E = 1.4426950408889634
Q_SCALE = HEAD_DIM ** -0.5 * LOG2E


def _cparams(sem):
    return pltpu.CompilerParams(dimension_semantics=sem, vmem_limit_bytes=VMEM_LIMIT)


def _rms(x, g):
    return x * lax.rsqrt(jnp.mean(x * x, axis=-1, keepdims=True) + EPS) * g


def _sigmoid(z):
    return 1.0 / (1.0 + jnp.exp(-z))


def _log_sigmoid(z):
    return jnp.minimum(z, 0.0) - jnp.log1p(jnp.exp(-jnp.abs(z)))


def _dot_t(a, b):
    return lax.dot_general(a, b, (((1,), (1,)), ((), ())), preferred_element_type=F32)


def _dot(a, b):
    return jnp.dot(a, b, preferred_element_type=F32)


def _keep_head(blk, which):
    lane = lax.broadcasted_iota(jnp.int32, blk.shape, blk.ndim - 1)
    keep = (lane >= HEAD_DIM) if which else (lane < HEAD_DIM)
    return jnp.where(keep, blk, jnp.zeros_like(blk))


def _merge_pair(o0, o1):
    lane = lax.broadcasted_iota(jnp.int32, o0.shape, o0.ndim - 1)
    return jnp.where(lane < HEAD_DIM, o0, o1)


N_CACHE_ROWS = 8


def _proj_kernel(x_ref, g_ref, w_ref, bf_ref, rc_ref, rs1_ref, rs2_ref, *refs):
    (ak_ref, av_ref, ik_ref, bk_ref, bv_ref, ck_ref, cv_ref, lf_ref, iw_ref,
     aq16, iq16, ak16, avt16, ik16, bq16, bk16, bv16, cq16, ck16, cv16) = refs[-(N_CACHE_ROWS + 12):]
    xn = _rms(x_ref[0], g_ref[...]).astype(BF16)
    rc, rs1, rs2 = rc_ref[...], rs1_ref[...], rs2_ref[...]

    def cols(c0, width):
        return _dot(xn, w_ref[:, c0:c0 + width])

    def group(blk, g):
        return blk[:, g * LANES:(g + 1) * LANES]

    def rope(blk):
        half = ROT_DIM // 2
        return blk * rc + pltpu.roll(blk, LANES - half, 1) * rs1 + pltpu.roll(blk, half, 1) * rs2

    wide = cols(0, BRANCH_W)
    for g in range(4):
        aq16[0, :, g * LANES:(g + 1) * LANES] = (rope(group(wide, g)) * Q_SCALE).astype(BF16)
    wide = cols(BRANCH_W, ROPE_W - BRANCH_W)
    for g in range(2):
        iq16[0, :, g * LANES:(g + 1) * LANES] = (rope(group(wide, g)) * IDX_DIM ** -0.5).astype(BF16)
    blk = rope(group(wide, 2))
    ak_ref[0, 0] = blk[:, :HEAD_DIM]
    ak16[0] = blk.astype(BF16)
    blk = rope(group(wide, 3))
    ik_ref[0, 0] = blk[:, :IDX_DIM]
    ik16[0] = blk.astype(BF16)

    wide = cols(ROPE_W, MISC_W)
    blk = group(wide, 0)
    av_ref[0, 0] = blk[:, :HEAD_DIM]
    row = lax.broadcasted_iota(jnp.int32, (LANES, blk.shape[0]), 0)
    avt16[0] = jnp.where(row < HEAD_DIM, blk.T, 1.0).astype(BF16)
    blk = group(wide, 1)
    iw_ref[0] = blk[:, :SUBLANES] * N_IDX ** -0.5
    lf_ref[0, 0] = _log_sigmoid(blk[:, CF_LANE:CF_LANE + N_HEADS] + bf_ref[:, CF_LANE:CF_LANE + N_HEADS])

    base = ROPE_W + MISC_W
    outs = ((None, bq16), (bk_ref, bk16), (bv_ref, bv16), (None, cq16), (ck_ref, ck16), (cv_ref, cv16))
    for n, (o32, o16) in enumerate(outs):
        wide = cols(base + n * BRANCH_W, BRANCH_W)
        if o32 is None:
            o16[0] = (wide * Q_SCALE).astype(BF16)
        else:
            o32[0, 0] = wide
            o16[0] = wide.astype(BF16)


def _project(x, g, w, bf_row, tables, tm, layer, depth, stacked):
    B, T, D = x.shape
    tok = lambda w_, dt: jax.ShapeDtypeStruct((B, T, w_), dt)
    tspec = lambda w_: pl.BlockSpec((1, tm, w_), lambda b, t: (b, t, 0))
    const = lambda a: pl.BlockSpec(a.shape, lambda b, t: (0,) * a.ndim, pipeline_mode=pl.Buffered(1))
    cache_w = (64, 64, 64, 512, 512, 512, 512, 8)
    cache_shapes = [jax.ShapeDtypeStruct((depth, B, T, w_), F32) for w_ in cache_w]
    cache_specs = [pl.BlockSpec((1, 1, tm, w_), lambda b, t: (layer, b, t, 0)) for w_ in cache_w]
    work = [(tok(w_, dt), tspec(w_)) for w_, dt in ((8, F32), (512, BF16), (256, BF16), (128, BF16))]
    work += [(jax.ShapeDtypeStruct((B, LANES, T), BF16), pl.BlockSpec((1, LANES, tm), lambda b, t: (b, 0, t)))]
    work += [(tok(w_, BF16), tspec(w_)) for w_ in (128,) + (512,) * 6]
    work_shapes, work_specs = [s for s, _ in work], [p for _, p in work]
    rspec = pl.BlockSpec((tm, LANES), lambda b, t: (t, 0))
    prev = () if stacked is None else tuple(stacked)
    n_in = 7
    outs = pl.pallas_call(
        _proj_kernel,
        grid=(B, T // tm),
        in_specs=[tspec(D), const(g), const(w), const(bf_row), rspec, rspec, rspec]
        + [pl.BlockSpec(memory_space=pl.ANY)] * len(prev),
        out_specs=cache_specs + work_specs,
        out_shape=cache_shapes + work_shapes,
        input_output_aliases={n_in + i: i for i in range(len(prev))},
        compiler_params=_cparams(("parallel", "parallel")),
        name="in_proj",
    )(x, g, w, bf_row, *tables, *prev)
    return outs[:N_CACHE_ROWS], outs[N_CACHE_ROWS:]


def _split3(x):
    hi = x.astype(BF16)
    r = x - hi.astype(F32)
    mid = r.astype(BF16)
    lo = (r - mid.astype(F32)).astype(BF16)
    return hi, mid, lo


def _cumsum_kernel(lf_ref, o_ref, *, blk):
    L = lf_ref.shape[-1]
    r = lax.broadcasted_iota(jnp.int32, (blk, blk), 0)
    c = lax.broadcasted_iota(jnp.int32, (blk, blk), 1)
    upper = jnp.where(r <= c, 1.0, 0.0).astype(BF16)
    carry = jnp.zeros((N_HEADS, 1), F32)
    for j in range(L // blk):
        hi, mid, lo = _split3(lf_ref[0, :, j * blk:(j + 1) * blk])
        f = _dot(hi, upper) + _dot(mid, upper) + _dot(lo, upper) + carry
        o_ref[0, :, j * blk:(j + 1) * blk] = f * LOG2E
        carry = f[:, blk - 1:blk]


def _cumsum_time(lf_t):
    B, H, L = lf_t.shape
    blk = LANES
    assert L % blk == 0
    spec = pl.BlockSpec((1, H, L), lambda b: (b, 0, 0))
    return pl.pallas_call(
        functools.partial(_cumsum_kernel, blk=blk),
        grid=(B,),
        in_specs=[spec],
        out_specs=spec,
        out_shape=jax.ShapeDtypeStruct((B, H, L), F32),
        compiler_params=_cparams(("parallel",)),
        name="logf_cumsum",
    )(lf_t)


def _head_column(blk, h):
    lane = lax.broadcasted_iota(jnp.int32, blk.shape, 1)
    return jnp.sum(jnp.where(lane == h, blk, 0.0), axis=1, keepdims=True)


def _positions(tq, tk, q0, k0):
    qpos = q0 + lax.broadcasted_iota(jnp.int32, (tq, tk), 0)
    kpos = k0 + lax.broadcasted_iota(jnp.int32, (tq, tk), 1)
    return qpos, kpos


def _pair_specs(tq, Lp):
    qspec = pl.BlockSpec((1, tq, BRANCH_W), lambda b, qi: (b, qi, 0))
    kspec = pl.BlockSpec((1, Lp, BRANCH_W), lambda b, qi: (b, 0, 0))
    return qspec, kspec


def _pair_lanes(hp):
    return pl.ds(pl.multiple_of(hp * LANES, LANES), LANES)


def _split_past(rest, has_past):
    return (rest[:2], rest[2:-2], rest[-2:]) if has_past else ((), rest, ())


def _key_views(k_ref, v_ref, past, bufs):
    if not past:
        return k_ref.at[0], v_ref.at[0]
    for old, new, buf in zip(past, (k_ref, v_ref), bufs):
        P, T = old.shape[1], new.shape[1]
        buf[0:P, :] = old[0].astype(BF16)
        buf[P:P + T, :] = new[0]
        if buf.shape[0] > P + T:
            buf[P + T:, :] = jnp.zeros((buf.shape[0] - P - T, BRANCH_W), BF16)
    return bufs


def _past_specs_and_bufs(past, T, Lp):
    if past is None:
        return [], [], ()
    P = past[0].shape[1]
    spec = pl.BlockSpec((1, P, BRANCH_W), lambda b, qi: (b, 0, 0))
    return [spec, spec], [pltpu.VMEM((Lp, BRANCH_W), BF16)] * 2, tuple(past)


def _fox_kernel(q_ref, k_ref, v_ref, *rest, tq, tk, q_off, has_past):
    past, (fq_ref, fk_ref, o_ref, m_ref, acc_ref), bufs = _split_past(rest, has_past)
    keys, values = _key_views(k_ref, v_ref, past, bufs)
    q0 = q_off + pl.program_id(1) * tq
    diag = q0 // tk
    fq_blk = fq_ref[0]
    ones = jnp.ones((tk, LANES), BF16)

    def head_pair(hp, _):
        lanes = _pair_lanes(hp)
        qp = q_ref[0, :, lanes]
        qm = (_keep_head(qp, 0), _keep_head(qp, 1))
        fq = (_head_column(fq_blk, hp * 2), _head_column(fq_blk, hp * 2 + 1))
        m_ref[...] = jnp.full(m_ref.shape, NEG, F32)
        acc_ref[...] = jnp.zeros(acc_ref.shape, F32)

        def score(j, masked):
            k0 = pl.multiple_of(j * tk, tk)
            kt = keys[pl.ds(k0, tk), lanes]
            if masked:
                qpos, kpos = _positions(tq, tk, q0, k0)
                causal = kpos <= qpos
            out = []
            for hh in range(2):
                fk = fk_ref[0, hp * 2 + hh, :, pl.ds(k0, tk)]
                s = _dot_t(qm[hh], kt) + fq[hh] - fk
                out.append(jnp.where(causal, s, NEG) if masked else s)
            return tuple(out)

        def fold(j, scores):
            v1 = jnp.concatenate([values[pl.ds(pl.multiple_of(j * tk, tk), tk), lanes], ones], axis=1)
            m_old = [m_ref[hh] for hh in range(2)]
            m_new = [jnp.maximum(m_old[hh], jnp.max(scores[hh], axis=1, keepdims=True)) for hh in range(2)]
            p = [jnp.exp2(scores[hh] - m_new[hh]).astype(BF16) for hh in range(2)]
            for hh in range(2):
                acc_ref[hh] = jnp.exp2(m_old[hh] - m_new[hh]) * acc_ref[hh] + _dot(p[hh], v1)
                m_ref[hh] = m_new[hh]

        fold(diag, score(diag, True))

        def body(i, _):
            j = diag - 1 - i
            fold(j, score(j, False))
            return 0

        lax.fori_loop(0, diag, body, 0)
        o0 = acc_ref[0, :, :LANES] / acc_ref[0, :, LANES:]
        o1 = acc_ref[1, :, :LANES] / acc_ref[1, :, LANES:]
        o_ref[0, :, lanes] = _merge_pair(o0, o1).astype(o_ref.dtype)
        return 0

    lax.fori_loop(0, N_PAIRS, head_pair, 0)


def _fox(q, k, v, past, fq, fk_t, *, tq, tk, q_off):
    B, T, _ = q.shape
    Lp = fk_t.shape[-1]
    qspec, kspec = _pair_specs(tq, k.shape[1])
    past_specs, bufs, past = _past_specs_and_bufs(past, T, Lp)
    return pl.pallas_call(
        functools.partial(_fox_kernel, tq=tq, tk=tk, q_off=q_off, has_past=bool(past)),
        grid=(B, T // tq),
        in_specs=[qspec, kspec, kspec] + past_specs
        + [pl.BlockSpec((1, tq, N_HEADS), lambda b, qi: (b, qi, 0)),
           pl.BlockSpec((1, N_HEADS, 1, Lp), lambda b, qi: (b, 0, 0, 0))],
        out_specs=qspec,
        out_shape=jax.ShapeDtypeStruct(q.shape, BF16),
        scratch_shapes=[pltpu.VMEM((2, tq, 1), F32), pltpu.VMEM((2, tq, 2 * LANES), F32)] + bufs,
        compiler_params=_cparams(("parallel", "parallel")),
        name="fox_attn",
    )(q, k, v, *past, fq, fk_t[:, :, None, :])


def _sb_kernel(q_ref, k_ref, v_ref, *rest, tq, tk, q_off, has_past):
    past, (o_ref, acc_ref, tail_ref), bufs = _split_past(rest, has_past)
    keys, values = _key_views(k_ref, v_ref, past, bufs)
    q0 = q_off + pl.program_id(1) * tq
    diag = q0 // tk
    r = lax.broadcasted_iota(jnp.int32, (LANES, 2 * LANES), 0)
    c = lax.broadcasted_iota(jnp.int32, (LANES, 2 * LANES), 1)
    suffix_w = jnp.where((c >= LANES) | (r > c), 1.0, 0.0).astype(BF16)

    def head_pair(hp, _):
        lanes = _pair_lanes(hp)
        qp = q_ref[0, :, lanes]
        qm = (_keep_head(qp, 0), _keep_head(qp, 1))
        acc_ref[...] = jnp.zeros(acc_ref.shape, F32)
        tail_ref[...] = jnp.zeros(tail_ref.shape, F32)

        def tile(j, masked):
            k0 = pl.multiple_of(j * tk, tk)
            kt = keys[pl.ds(k0, tk), lanes]
            vt = values[pl.ds(k0, tk), lanes]
            if masked:
                qpos, kpos = _positions(tq, tk, q0, k0)
                strict = kpos < qpos
            heads = (0, 1)
            z = [_dot_t(qm[hh], kt) for hh in heads]
            ls, la, after = [None, None], [None, None], [None, None]
            for hh in heads:
                ls[hh] = jnp.minimum(z[hh], 0.0) - jnp.log2(1.0 + jnp.exp2(-jnp.abs(z[hh])))
                la[hh] = ls[hh] - z[hh]
                if masked:
                    la[hh] = jnp.where(strict, la[hh], 0.0)
                    ls[hh] = jnp.where(strict, ls[hh], NEG)
                la[hh] = la[hh].astype(BF16)
            for hh in heads:
                run = tail_ref[hh]
                blocks = [None] * (tk // LANES)
                for b in reversed(range(tk // LANES)):
                    res = _dot(la[hh][:, b * LANES:(b + 1) * LANES], suffix_w)
                    blocks[b] = res[:, :LANES] + run
                    run = run + res[:, LANES:]
                tail_ref[hh] = run
                after[hh] = jnp.concatenate(blocks, axis=1)
            a = [jnp.exp2(ls[hh] + after[hh]).astype(BF16) for hh in heads]
            for hh in heads:
                acc_ref[hh] += _dot(a[hh], vt)

        tile(diag, True)

        def body(i, _):
            tile(diag - 1 - i, False)
            return 0

        lax.fori_loop(0, diag, body, 0)
        o_ref[0, :, lanes] = _merge_pair(acc_ref[0], acc_ref[1]).astype(o_ref.dtype)
        return 0

    lax.fori_loop(0, N_PAIRS, head_pair, 0)


def _sb(q, k, v, past, Lp, *, tq, tk, q_off):
    B, T, _ = q.shape
    qspec, kspec = _pair_specs(tq, k.shape[1])
    past_specs, bufs, past = _past_specs_and_bufs(past, T, Lp)
    return pl.pallas_call(
        functools.partial(_sb_kernel, tq=tq, tk=tk, q_off=q_off, has_past=bool(past)),
        grid=(B, T // tq),
        in_specs=[qspec, kspec, kspec] + past_specs,
        out_specs=qspec,
        out_shape=jax.ShapeDtypeStruct(q.shape, BF16),
        scratch_shapes=[pltpu.VMEM((2, tq, LANES), F32), pltpu.VMEM((2, tq, LANES), F32)] + bufs,
        compiler_params=_cparams(("parallel", "parallel")),
        name="sb_attn",
    )(q, k, v, *past)


def _const_key(v):
    bits = int(np.float32(v).view(np.int32))
    return (bits ^ 0x7FFFFFFF) + 1 if bits < 0 else bits


def _float_key(x):
    bits = lax.bitcast_convert_type(x, jnp.int32)
    return jnp.where(bits < 0, jnp.bitwise_xor(bits, 0x7FFFFFFF) + 1, bits)


KEY_MIN = -2 ** 31
KEY_HALF_NEG = _const_key(0.5 * NEG)


KEY_MAX = 2 ** 31 - 1
KEY_BITS = 32
COUNT_CHAINS = 8


def _dsa_kernel(aq_ref, iq_ref, iwt_ref, k_ref, vt_ref, ik_ref, o_ref, key_ref, q8_ref, m_ref, acc_ref,
                *, tq, tk, q_off, n_rows, topk):
    q0 = q_off + pl.program_id(1) * tq
    n_tiles = (q0 + n_rows + tk - 1) // tk

    def tile_start(j):
        return pl.multiple_of(j * tk, LANES)

    iq = iq_ref[0]
    iwt = iwt_ref[0]
    qh = [_keep_head(iq[:, (h // 2) * LANES:(h // 2 + 1) * LANES], h % 2) for h in range(N_IDX)]
    q_chunk = jnp.right_shift(q0 + lax.broadcasted_iota(jnp.int32, (tk, tq), 1), CHUNK_SHIFT)

    def score_tile(j, _):
        k0 = tile_start(j)
        ik = ik_ref[0, pl.ds(k0, tk), :]
        sc = jnp.zeros((tk, tq), F32)
        for h in range(N_IDX):
            sc = sc + jnp.maximum(_dot_t(ik, qh[h]), 0.0) * iwt[h:h + 1, :]
        key_chunk = jnp.right_shift(k0 + lax.broadcasted_iota(jnp.int32, (tk, tq), 0), CHUNK_SHIFT)
        key_ref[pl.ds(k0, tk), :] = _float_key(jnp.where(key_chunk <= q_chunk, sc, NEG))
        return 0

    lax.fori_loop(0, n_tiles, score_tile, 0)

    def search():
        def step(_, cr):
            lo, hi, above_hi = cr
            mid = jnp.right_shift(lo, 1) + jnp.right_shift(hi, 1) + jnp.bitwise_and(jnp.bitwise_and(lo, hi), 1)

            def count_tile(j, cnt):
                above = jnp.where(key_ref[pl.ds(tile_start(j), tk), :] > mid, 1.0, 0.0)
                return cnt + jnp.sum(above.reshape((-1,) + cnt.shape), axis=0)

            cnt = lax.fori_loop(0, n_tiles, count_tile, jnp.zeros((COUNT_CHAINS, SUBLANES, tq), F32))
            cnt = jnp.sum(jnp.sum(cnt, axis=0), axis=0, keepdims=True)
            less = cnt < topk
            return jnp.where(less, lo, mid), jnp.where(less, mid, hi), jnp.where(less, cnt, above_hi)

        init = (jnp.full((1, tq), KEY_MIN, jnp.int32), jnp.full((1, tq), KEY_MAX, jnp.int32), jnp.zeros((1, tq), F32))
        _, hi, above_hi = lax.fori_loop(0, KEY_BITS, step, init)
        return hi, topk - above_hi

    thr, room = lax.cond(q0 + n_rows > topk, search,
                         lambda: (jnp.full((1, tq), KEY_MIN, jnp.int32), jnp.zeros((1, tq), F32)))
    row_ok = lax.broadcasted_iota(jnp.int32, (1, tq), 1) < n_rows
    room = jnp.where(row_ok, room, 0.0)

    @pl.when(jnp.max(room) > 0.0)
    def _():
        r = lax.broadcasted_iota(jnp.int32, (LANES, LANES), 0)
        c = lax.broadcasted_iota(jnp.int32, (LANES, LANES), 1)
        before = jnp.where(c < r, 1.0, 0.0).astype(BF16)

        def tie_tile(j, seen):
            k0 = tile_start(j)
            for b in range(tk // LANES):
                rows = pl.ds(k0 + b * LANES, LANES)
                keys = key_ref[rows, :]
                eq = keys == thr
                eqf = jnp.where(eq, 1.0, 0.0)
                rank = _dot(before, eqf.astype(BF16)) + seen
                key_ref[rows, :] = jnp.where(eq, jnp.where(rank < room, keys + 1, keys), keys)
                seen = seen + jnp.sum(eqf, axis=0, keepdims=True)
            return seen

        lax.fori_loop(0, n_tiles, tie_tile, jnp.zeros((1, tq), F32))

    aq = aq_ref[0]
    for h in range(N_HEADS):
        q8_ref[h * tq:(h + 1) * tq, :] = _keep_head(aq[:, (h // 2) * LANES:(h // 2 + 1) * LANES], h % 2)
    m_ref[...] = jnp.full(m_ref.shape, NEG, F32)
    acc_ref[...] = jnp.zeros(acc_ref.shape, F32)
    thr_sel = jnp.maximum(thr, KEY_HALF_NEG)

    def attend_tile(j, _):
        k0 = tile_start(j)
        kt = k_ref[0, pl.ds(k0, tk), :]
        v1t = vt_ref[0, :, pl.ds(k0, tk)]
        bias = jnp.where(key_ref[pl.ds(k0, tk), :] > thr_sel, 0.0, NEG)
        st = _dot_t(kt, q8_ref[...])
        st = jnp.concatenate([st[:, h * tq:(h + 1) * tq] + bias for h in range(N_HEADS)], axis=1)
        m_old = m_ref[...]
        m_new = jnp.maximum(m_old, jnp.max(st, axis=0, keepdims=True))
        p = jnp.exp2(st - m_new).astype(BF16)
        acc_ref[...] = jnp.exp2(m_old - m_new) * acc_ref[...] + _dot(v1t, p)
        m_ref[...] = m_new
        return 0

    lax.fori_loop(0, n_tiles, attend_tile, 0)
    for hp in range(N_PAIRS):
        pair = []
        for h in (2 * hp, 2 * hp + 1):
            a = acc_ref[:, h * tq:(h + 1) * tq]
            pair.append(a[:HEAD_DIM] / a[HEAD_DIM:])
        o_ref[0, :, hp * LANES:(hp + 1) * LANES] = jnp.concatenate(pair, axis=0).T.astype(o_ref.dtype)


def _dsa(aq, iq, iw, k2, v1t, ik2, *, tq, q_off, topk):
    B, T, _ = aq.shape
    Lp = k2.shape[1]
    tk = Lp if T <= LANES else next(t for t in (512, 384, 256, 128) if Lp % t == 0)
    Tp = -(-T // LANES) * LANES
    tq = max(tq, LANES)
    n_rows = min(T, tq)
    pad_t = lambda a: jnp.pad(a, ((0, 0), (0, Tp - T), (0, 0)))
    qspec = lambda w: pl.BlockSpec((1, tq, w), lambda b, qi: (b, qi, 0))
    kspec = pl.BlockSpec((1, Lp, LANES), lambda b, qi: (b, 0, 0))
    out = pl.pallas_call(
        functools.partial(_dsa_kernel, tq=tq, tk=tk, q_off=q_off, n_rows=n_rows, topk=topk),
        grid=(B, Tp // tq),
        in_specs=[qspec(BRANCH_W), qspec(N_IDX * IDX_DIM), pl.BlockSpec((1, SUBLANES, tq), lambda b, qi: (b, 0, qi)),
                  kspec, pl.BlockSpec((1, LANES, Lp), lambda b, qi: (b, 0, 0)), kspec],
        out_specs=qspec(BRANCH_W),
        out_shape=jax.ShapeDtypeStruct((B, Tp, BRANCH_W), BF16),
        scratch_shapes=[pltpu.VMEM((Lp, tq), jnp.int32), pltpu.VMEM((N_HEADS * tq, LANES), BF16),
                        pltpu.VMEM((1, N_HEADS * tq), F32), pltpu.VMEM((LANES, N_HEADS * tq), F32)],
        compiler_params=_cparams(("parallel", "parallel")),
        name="dsa_attn",
    )(pad_t(aq), pad_t(iq), jnp.swapaxes(pad_t(iw), 1, 2), k2, v1t, ik2)
    return out[:, :T]


def _merge_kernel(x_ref, oa_ref, ob_ref, oc_ref, g_ref, wg_ref, wb_ref, wo_ref, y_ref):
    x = x_ref[0]
    D = x.shape[-1]
    xn = _rms(x, g_ref[...]).astype(BF16)
    merged = jnp.zeros(x.shape, F32)
    for n, o_ref in enumerate((oa_ref, ob_ref, oc_ref)):
        gate = _sigmoid(_dot(xn, wg_ref[:, n * D:(n + 1) * D]))
        merged = merged + gate * _dot(o_ref[0], wb_ref[n])
    y_ref[0] = x + _dot(merged.astype(BF16), wo_ref[...])


def _merge(x, oa, ob, oc, g, wg, wb, wo, tm):
    B, T, D = x.shape
    tspec = lambda w_: pl.BlockSpec((1, tm, w_), lambda b, t: (b, t, 0))
    const = lambda a: pl.BlockSpec(a.shape, lambda b, t: (0,) * a.ndim)
    return pl.pallas_call(
        _merge_kernel,
        grid=(B, T // tm),
        in_specs=[tspec(D), tspec(BRANCH_W), tspec(BRANCH_W), tspec(BRANCH_W), const(g), const(wg), const(wb),
                  const(wo)],
        out_specs=tspec(D),
        out_shape=jax.ShapeDtypeStruct(x.shape, F32),
        compiler_params=_cparams(("parallel", "parallel")),
        name="merge_out",
    )(x, oa, ob, oc, g, wg, wb, wo)


def _ffn_kernel(x_ref, g_ref, past_ref, wu_ref, cw_ref, cb_ref, wd_ref, gf_ref, y_ref, st_ref,
                hbuf_ref, carry_ref, acc_ref, *, tm, final_norm):
    fc = FF_CHUNK
    d_ff = wd_ref.shape[0]
    n_chunks = d_ff // fc

    @pl.when(pl.program_id(1) == 0)
    def _():
        carry_ref[...] = past_ref[0]

    x = x_ref[0]
    xn = _rms(x, g_ref[...]).astype(BF16)
    acc_ref[...] = jnp.zeros(acc_ref.shape, F32)

    for c in range(n_chunks):
        hbuf = hbuf_ref.at[c % 2]
        both = lambda ref: jnp.concatenate([ref[:, c * fc:(c + 1) * fc], ref[:, d_ff + c * fc:d_ff + (c + 1) * fc]],
                                           axis=1)
        h = jnp.concatenate([_dot(xn, wu_ref[:, c * fc:(c + 1) * fc]),
                             _dot(xn, wu_ref[:, d_ff + c * fc:d_ff + (c + 1) * fc])], axis=1)
        hbuf[0:SUBLANES, :] = carry_ref[c]
        hbuf[SUBLANES:SUBLANES + tm, :] = h
        carry_ref[c] = h[tm - SUBLANES:tm, :]
        cw = both(cw_ref)
        hc = (both(cb_ref) + cw[0:1] * hbuf[SUBLANES - 2:SUBLANES - 2 + tm, :]
              + cw[1:2] * hbuf[SUBLANES - 1:SUBLANES - 1 + tm, :] + cw[2:3] * h)
        gate, up = hc[:, :fc], hc[:, fc:]
        act = gate * _sigmoid(gate) * up
        acc_ref[...] += _dot(act.astype(BF16), wd_ref[c * fc:(c + 1) * fc, :])
    st_ref[0] = carry_ref[...]
    y = x + acc_ref[...]
    if final_norm:
        y = _rms(y, gf_ref[...])
    y_ref[0] = y


def _ffn(x, g, past, wu, cw, cb, wd, gf, tm, final_norm):
    B, T, D = x.shape
    nc, fc2 = wd.shape[0] // FF_CHUNK, 2 * FF_CHUNK
    tspec = pl.BlockSpec((1, tm, D), lambda b, t: (b, t, 0))
    const = lambda a: pl.BlockSpec(a.shape, lambda b, t: (0,) * a.ndim, pipeline_mode=pl.Buffered(1))
    sspec = pl.BlockSpec((1, nc, SUBLANES, fc2), lambda b, t: (b, 0, 0, 0))
    return pl.pallas_call(
        functools.partial(_ffn_kernel, tm=tm, final_norm=final_norm),
        grid=(B, T // tm),
        in_specs=[tspec, const(g), sspec, const(wu), const(cw), const(cb), const(wd), const(gf)],
        out_specs=[tspec, sspec],
        out_shape=[jax.ShapeDtypeStruct(x.shape, F32), jax.ShapeDtypeStruct((B, nc, SUBLANES, fc2), F32)],
        scratch_shapes=[pltpu.VMEM((2, tm + SUBLANES, fc2), F32), pltpu.VMEM((nc, SUBLANES, fc2), F32),
                        pltpu.VMEM((tm, D), F32)],
        compiler_params=_cparams(("arbitrary", "arbitrary")),
        name="conv_ffn",
    )(x, g, past, wu, cw, cb, wd, gf)


def _rope_tables(pos):
    half = ROT_DIM // 2
    freq = ROPE_THETA ** (-jnp.arange(half, dtype=F32) / half)
    ang = pos.astype(F32)[:, None] * freq[None, :]
    lane = np.arange(LANES) % HEAD_DIM
    cos = jnp.cos(ang)[:, lane % half]
    sin = jnp.sin(ang)[:, lane % half]
    rc = jnp.where(lane < ROT_DIM, cos, 1.0)
    rs1 = jnp.where(lane < half, -sin, 0.0)
    rs2 = jnp.where((lane >= half) & (lane < ROT_DIM), sin, 0.0)
    return rc, rs1, rs2


IN_SIZES = (BRANCH_W, HEAD_DIM, HEAD_DIM, N_IDX * IDX_DIM, IDX_DIM, N_IDX) + (BRANCH_W,) * 6 + (N_HEADS,)
PACKED_W = ROPE_W + MISC_W + 6 * BRANCH_W


def _pack_w_kernel(w_ref, o_ref, s_ref):
    offs = np.concatenate([[0], np.cumsum(IN_SIZES)]).tolist()
    a_q, a_k, a_v, i_q, i_k, i_w, b_q = offs[:7]
    c_f = offs[12]
    src = lambda c0, width: w_ref[0, :, c0:c0 + width]
    s_ref[:, 0:BRANCH_W] = src(a_q, BRANCH_W)
    s_ref[:, BRANCH_W:BRANCH_W + 256] = src(i_q, 256)
    for dst, c0 in ((768, a_k), (896, i_k), (ROPE_W, a_v)):
        s_ref[:, dst:dst + HEAD_DIM] = src(c0, HEAD_DIM)
        s_ref[:, dst + HEAD_DIM:dst + 2 * HEAD_DIM] = src(c0, HEAD_DIM)
    misc = ROPE_W + LANES
    s_ref[:, misc:misc + LANES] = jnp.zeros((s_ref.shape[0], LANES), F32)
    s_ref[:, misc:misc + N_IDX] = src(i_w, N_IDX)
    s_ref[:, misc + CF_LANE:misc + CF_LANE + N_HEADS] = src(c_f, N_HEADS)
    s_ref[:, ROPE_W + MISC_W:PACKED_W] = src(b_q, 6 * BRANCH_W)
    o_ref[...] = s_ref[...].astype(BF16)


def _pack_w(w_in, l):
    D = w_in.shape[1]
    rows = 256
    return pl.pallas_call(
        _pack_w_kernel,
        grid=(D // rows,),
        in_specs=[pl.BlockSpec((1, rows, w_in.shape[2]), lambda r: (l, r, 0))],
        out_specs=pl.BlockSpec((rows, PACKED_W), lambda r: (r, 0)),
        out_shape=jax.ShapeDtypeStruct((D, PACKED_W), BF16),
        scratch_shapes=[pltpu.VMEM((rows, PACKED_W), F32)],
        compiler_params=_cparams(("parallel",)),
        name="pack_w_in",
    )(w_in)


def _pack_layer(l, norm_mix, w_in, b_f, w_branch, w_gate, w_out, norm_ffn, w_up, conv_w, conv_b, w_down):
    w = _pack_w(w_in, l)
    bf_row = jnp.zeros((1, LANES), F32).at[0, CF_LANE:CF_LANE + N_HEADS].set(b_f[l])

    d_ff = w_down.shape[1]
    fc = FF_CHUNK
    nc = d_ff // fc
    assert nc * fc == d_ff
    halves = lambda a: jnp.concatenate([a[..., :d_ff].reshape(a.shape[:-1] + (nc, fc)),
                                        a[..., d_ff:].reshape(a.shape[:-1] + (nc, fc))], axis=-1)
    wu, cw, cb, wd = w_up[l].astype(BF16), conv_w[l], conv_b[l][None], w_down[l].astype(BF16)
    return dict(g_mix=norm_mix[l][None], w=w, bf_row=bf_row, wg=w_gate[l].astype(BF16),
                wb=w_branch[l].astype(BF16), wo=w_out[l].astype(BF16), g_ffn=norm_ffn[l][None],
                wu=wu, cw=cw, cb=cb, wd=wd, halves=halves, nc=nc, d_ff=d_ff)


def _conv_state_in(state, halves):
    st = jnp.moveaxis(halves(state), 2, 1)
    return jnp.pad(st, ((0, 0), (0, 0), (SUBLANES - (CONV_W - 1), 0), (0, 0)))


def _conv_state_out(st, d_ff):
    st = st[:, :, SUBLANES - (CONV_W - 1):, :]
    B, nc, r, fc2 = st.shape
    fc = fc2 // 2
    gate = jnp.moveaxis(st[..., :fc], 1, 2).reshape(B, r, d_ff)
    up = jnp.moveaxis(st[..., fc:], 1, 2).reshape(B, r, d_ff)
    return jnp.concatenate([gate, up], axis=-1)


def _pad_keys(a, Lp):
    return jnp.pad(a, ((0, 0), (0, Lp - a.shape[1]), (0, 0)))


def _group_layer(x, past, lw, gf, layer, depth, stacked, tiles):
    B, T, D = x.shape
    P = 0 if past is None else past[0].shape[1]
    L = P + T
    Lp = -(-L // KEY_TILE) * KEY_TILE
    topk = max(1, min(TOPK_MAX, L // 4))
    tables = _rope_tables(jnp.arange(P, L))

    stacked, (iw, aq16, iq16, ak16, avt16, ik16, bq16, bk16, bv16, cq16, ck16, cv16) = _project(
        x, lw["g_mix"], lw["w"], lw["bf_row"], tables, tiles["proj"], layer, depth, stacked)
    lf = stacked[-1][layer]

    if past is None:
        keys = (ak16, ik16)
        lf_all = lf
        conv_in = jnp.zeros((B, lw["nc"], SUBLANES, 2 * FF_CHUNK), F32)
        past_b = past_c = None
    else:
        p_ak, p_av, p_ik, p_bk, p_bv, p_ck, p_cv, p_lf, p_conv = past
        dup = lambda a: jnp.concatenate([a, a], axis=-1).astype(BF16)
        keys = tuple(jnp.concatenate([o, n], axis=1) for o, n in zip((dup(p_ak), dup(p_ik)), (ak16, ik16)))
        old_vt = jnp.swapaxes(p_av, 1, 2).astype(BF16)
        avt16 = jnp.concatenate([jnp.concatenate([old_vt, jnp.ones_like(old_vt)], axis=1), avt16], axis=2)
        lf_all = jnp.concatenate([p_lf, lf], axis=1)
        conv_in = _conv_state_in(p_conv, lw["halves"])
        flat = lambda a: a.reshape(B, P, BRANCH_W)
        past_b, past_c = (flat(p_bk), flat(p_bv)), (flat(p_ck), flat(p_cv))
    ak2, ik2 = (_pad_keys(a, Lp) for a in keys)
    v1t = jnp.pad(avt16, ((0, 0), (0, 0), (0, Lp - L)))

    fk_t = _cumsum_time(jnp.swapaxes(_pad_keys(lf_all, Lp), 1, 2))
    fq = jnp.swapaxes(fk_t[:, :, P:L], 1, 2)

    o_a = _dsa(aq16, iq16, iw, ak2, v1t, ik2, tq=tiles["dsa"], q_off=P, topk=topk)
    o_b = _sb(bq16, bk16, bv16, past_b, Lp, tq=tiles["sb"][0], tk=tiles["sb"][1], q_off=P)
    o_c = _fox(cq16, ck16, cv16, past_c, fq, fk_t, tq=tiles["fox"][0], tk=tiles["fox"][1], q_off=P)

    x = _merge(x, o_a, o_b, o_c, lw["g_mix"], lw["wg"], lw["wb"], lw["wo"], tiles["merge"])
    x, st = _ffn(x, lw["g_ffn"], conv_in, lw["wu"], lw["cw"], lw["cb"], lw["wd"], gf, tiles["ffn"],
                 layer == depth - 1)
    return x, stacked, _conv_state_out(st, lw["d_ff"])


def _tiles(T, P):
    pick = lambda want: min(want, T)
    if T % 512 == 0 and P % 512 == 0:
        sb, fox = (256, 256), (256, 512)
    else:
        assert T <= KEY_TILE
        sb = fox = (T, -(-(P + T) // KEY_TILE) * KEY_TILE)
    return dict(proj=pick(512), dsa=pick(128), sb=sb, fox=fox, merge=pick(512), ffn=pick(512))


def kernel(x_prompt, x_sample, cache_a_k, cache_a_v, cache_a_idx_k, cache_b_k, cache_b_v, cache_c_k, cache_c_v,
           cache_c_logf, state_ffn_conv, norm_mix, w_in, b_f, w_branch, w_gate, w_out, norm_ffn, w_up, conv_w,
           conv_b, w_down, norm_final):
    depth = w_in.shape[0]
    caches = (cache_a_k, cache_a_v, cache_a_idx_k, cache_b_k, cache_b_v, cache_c_k, cache_c_v, cache_c_logf,
              state_ffn_conv)
    layers = [_pack_layer(l, norm_mix, w_in, b_f, w_branch, w_gate, w_out, norm_ffn, w_up, conv_w, conv_b, w_down)
              for l in range(depth)]
    gf = norm_final[None]

    def trunk(x, past):
        B, T, _ = x.shape
        tiles = _tiles(T, 0 if past is None else past[0].shape[2])
        stacked, conv_states = None, []
        for l in range(depth):
            lp = None if past is None else tuple(c[l] for c in past)
            x, stacked, conv_state = _group_layer(x, lp, layers[l], gf, l, depth, stacked, tiles)
            conv_states.append(conv_state)
        ak, av, ik, bk, bv, ck, cv, lf = stacked
        heads = lambda a: a.reshape(depth, B, T, N_HEADS, HEAD_DIM)
        return x, (ak, av, ik, heads(bk), heads(bv), heads(ck), heads(cv), lf, jnp.stack(conv_states, axis=0))

    y_prompt, p_state = trunk(x_prompt, None)
    y_sample, s_state = trunk(x_sample, caches)
    return (y_prompt, y_sample) + p_state + s_state
```

```python
import functools

import jax
import jax.numpy as jnp
import numpy as np
from jax import lax
from jax.experimental import pallas as pl
from jax.experimental.pallas import tpu as pltpu

F32 = jnp.float32
BF16 = jnp.bfloat16

LANES = 128
SUBLANES = 8
HEAD_DIM = 64
N_HEADS = 8
N_PAIRS = N_HEADS // 2
BRANCH_W = N_HEADS * HEAD_DIM
N_BRANCH = 3
N_IDX = 4
IDX_DIM = 64
CHUNK = 64
CHUNK_SHIFT = 6
TOPK_MAX = 256
ROT_DIM = HEAD_DIM // 4
ROPE_THETA = 500000.0
CONV_W = 3
EPS = 1e-6
NEG = -1e30
KEY_TILE = 128
FF_CHUNK = 256
VMEM_LIMIT = 56 * 1024 * 1024

ROPE_W = 1024
MISC_W = 256
CF_LANE = 8
LOG2E = 1.4426950408889634
Q_SCALE = HEAD_DIM ** -0.5 * LOG2E


def _cparams(sem):
    return pltpu.CompilerParams(dimension_semantics=sem, vmem_limit_bytes=VMEM_LIMIT)


def _rms(x, g):
    return x * lax.rsqrt(jnp.mean(x * x, axis=-1, keepdims=True) + EPS) * g


def _sigmoid(z):
    return 1.0 / (1.0 + jnp.exp(-z))


def _log_sigmoid(z):
    return jnp.minimum(z, 0.0) - jnp.log1p(jnp.exp(-jnp.abs(z)))


def _dot_t(a, b):
    return lax.dot_general(a, b, (((1,), (1,)), ((), ())), preferred_element_type=F32)


def _dot(a, b):
    return jnp.dot(a, b, preferred_element_type=F32)


def _keep_head(blk, which):
    lane = lax.broadcasted_iota(jnp.int32, blk.shape, blk.ndim - 1)
    keep = (lane >= HEAD_DIM) if which else (lane < HEAD_DIM)
    return jnp.where(keep, blk, jnp.zeros_like(blk))


def _merge_pair(o0, o1):
    lane = lax.broadcasted_iota(jnp.int32, o0.shape, o0.ndim - 1)
    return jnp.where(lane < HEAD_DIM, o0, o1)


N_CACHE_ROWS = 8


def _proj_kernel(x_ref, g_ref, w_ref, bf_ref, rc_ref, rs1_ref, rs2_ref, *refs):
    (ak_ref, av_ref, ik_ref, bk_ref, bv_ref, ck_ref, cv_ref, lf_ref, iw_ref,
     aq16, iq16, ak16, avt16, ik16, bq16, bk16, bv16, cq16, ck16, cv16) = refs[-(N_CACHE_ROWS + 12):]
    xn = _rms(x_ref[0], g_ref[...]).astype(BF16)
    rc, rs1, rs2 = rc_ref[...], rs1_ref[...], rs2_ref[...]

    def cols(c0, width):
        return _dot(xn, w_ref[:, c0:c0 + width])

    def group(blk, g):
        return blk[:, g * LANES:(g + 1) * LANES]

    def rope(blk):
        half = ROT_DIM // 2
        return blk * rc + pltpu.roll(blk, LANES - half, 1) * rs1 + pltpu.roll(blk, half, 1) * rs2

    wide = cols(0, BRANCH_W)
    for g in range(4):
        aq16[0, :, g * LANES:(g + 1) * LANES] = (rope(group(wide, g)) * Q_SCALE).astype(BF16)
    wide = cols(BRANCH_W, ROPE_W - BRANCH_W)
    for g in range(2):
        iq16[0, :, g * LANES:(g + 1) * LANES] = (rope(group(wide, g)) * IDX_DIM ** -0.5).astype(BF16)
    blk = rope(group(wide, 2))
    ak_ref[0, 0] = blk[:, :HEAD_DIM]
    ak16[0] = blk.astype(BF16)
    blk = rope(group(wide, 3))
    ik_ref[0, 0] = blk[:, :IDX_DIM]
    ik16[0] = blk.astype(BF16)

    wide = cols(ROPE_W, MISC_W)
    blk = group(wide, 0)
    av_ref[0, 0] = blk[:, :HEAD_DIM]
    row = lax.broadcasted_iota(jnp.int32, (LANES, blk.shape[0]), 0)
    avt16[0] = jnp.where(row < HEAD_DIM, blk.T, 1.0).astype(BF16)
    blk = group(wide, 1)
    iw_ref[0] = blk[:, :SUBLANES] * N_IDX ** -0.5
    lf_ref[0, 0] = _log_sigmoid(blk[:, CF_LANE:CF_LANE + N_HEADS] + bf_ref[:, CF_LANE:CF_LANE + N_HEADS])

    base = ROPE_W + MISC_W
    outs = ((None, bq16), (bk_ref, bk16), (bv_ref, bv16), (None, cq16), (ck_ref, ck16), (cv_ref, cv16))
    for n, (o32, o16) in enumerate(outs):
        wide = cols(base + n * BRANCH_W, BRANCH_W)
        if o32 is None:
            o16[0] = (wide * Q_SCALE).astype(BF16)
        else:
            o32[0, 0] = wide
            o16[0] = wide.astype(BF16)


def _project(x, g, w, bf_row, tables, tm, layer, depth, stacked):
    B, T, D = x.shape
    tok = lambda w_, dt: jax.ShapeDtypeStruct((B, T, w_), dt)
    tspec = lambda w_: pl.BlockSpec((1, tm, w_), lambda b, t: (b, t, 0))
    const = lambda a: pl.BlockSpec(a.shape, lambda b, t: (0,) * a.ndim, pipeline_mode=pl.Buffered(1))
    cache_w = (64, 64, 64, 512, 512, 512, 512, 8)
    cache_shapes = [jax.ShapeDtypeStruct((depth, B, T, w_), F32) for w_ in cache_w]
    cache_specs = [pl.BlockSpec((1, 1, tm, w_), lambda b, t: (layer, b, t, 0)) for w_ in cache_w]
    work = [(tok(w_, dt), tspec(w_)) for w_, dt in ((8, F32), (512, BF16), (256, BF16), (128, BF16))]
    work += [(jax.ShapeDtypeStruct((B, LANES, T), BF16), pl.BlockSpec((1, LANES, tm), lambda b, t: (b, 0, t)))]
    work += [(tok(w_, BF16), tspec(w_)) for w_ in (128,) + (512,) * 6]
    work_shapes, work_specs = [s for s, _ in work], [p for _, p in work]
    rspec = pl.BlockSpec((tm, LANES), lambda b, t: (t, 0))
    prev = () if stacked is None else tuple(stacked)
    n_in = 7
    outs = pl.pallas_call(
        _proj_kernel,
        grid=(B, T // tm),
        in_specs=[tspec(D), const(g), const(w), const(bf_row), rspec, rspec, rspec]
        + [pl.BlockSpec(memory_space=pl.ANY)] * len(prev),
        out_specs=cache_specs + work_specs,
        out_shape=cache_shapes + work_shapes,
        input_output_aliases={n_in + i: i for i in range(len(prev))},
        compiler_params=_cparams(("parallel", "parallel")),
        name="in_proj",
    )(x, g, w, bf_row, *tables, *prev)
    return outs[:N_CACHE_ROWS], outs[N_CACHE_ROWS:]


def _split3(x):
    hi = x.astype(BF16)
    r = x - hi.astype(F32)
    mid = r.astype(BF16)
    lo = (r - mid.astype(F32)).astype(BF16)
    return hi, mid, lo


def _cumsum_kernel(lf_ref, o_ref, *, blk):
    L = lf_ref.shape[-1]
    r = lax.broadcasted_iota(jnp.int32, (blk, blk), 0)
    c = lax.broadcasted_iota(jnp.int32, (blk, blk), 1)
    upper = jnp.where(r <= c, 1.0, 0.0).astype(BF16)
    carry = jnp.zeros((N_HEADS, 1), F32)
    for j in range(L // blk):
        hi, mid, lo = _split3(lf_ref[0, :, j * blk:(j + 1) * blk])
        f = _dot(hi, upper) + _dot(mid, upper) + _dot(lo, upper) + carry
        o_ref[0, :, j * blk:(j + 1) * blk] = f * LOG2E
        carry = f[:, blk - 1:blk]


def _cumsum_time(lf_t):
    B, H, L = lf_t.shape
    blk = LANES
    assert L % blk == 0
    spec = pl.BlockSpec((1, H, L), lambda b: (b, 0, 0))
    return pl.pallas_call(
        functools.partial(_cumsum_kernel, blk=blk),
        grid=(B,),
        in_specs=[spec],
        out_specs=spec,
        out_shape=jax.ShapeDtypeStruct((B, H, L), F32),
        compiler_params=_cparams(("parallel",)),
        name="logf_cumsum",
    )(lf_t)


def _head_column(blk, h):
    lane = lax.broadcasted_iota(jnp.int32, blk.shape, 1)
    return jnp.sum(jnp.where(lane == h, blk, 0.0), axis=1, keepdims=True)


def _positions(tq, tk, q0, k0):
    qpos = q0 + lax.broadcasted_iota(jnp.int32, (tq, tk), 0)
    kpos = k0 + lax.broadcasted_iota(jnp.int32, (tq, tk), 1)
    return qpos, kpos


def _pair_specs(tq, Lp):
    qspec = pl.BlockSpec((1, tq, BRANCH_W), lambda b, qi: (b, qi, 0))
    kspec = pl.BlockSpec((1, Lp, BRANCH_W), lambda b, qi: (b, 0, 0))
    return qspec, kspec


def _pair_lanes(hp):
    return pl.ds(pl.multiple_of(hp * LANES, LANES), LANES)


def _split_past(rest, has_past):
    return (rest[:2], rest[2:-2], rest[-2:]) if has_past else ((), rest, ())


def _key_ops(k_ref, v_ref, past, bufs, tk):
    if not past:
        def qk(qmat, hp, k0):
            return _dot_t(qmat, k_ref[0, pl.ds(k0, tk), _pair_lanes(hp)])

        def pv(p, hp, k0, ones):
            v = v_ref[0, pl.ds(k0, tk), _pair_lanes(hp)]
            return _dot(p, v if ones is None else jnp.concatenate([v, ones], axis=1))

        return qk, pv

    P, T = past[0].shape[-1], k_ref.shape[1]
    for new, buf in zip((k_ref, v_ref), bufs):
        buf[0:T, :] = new[0]
        if buf.shape[0] > T:
            buf[T:, :] = jnp.zeros((buf.shape[0] - T, BRANCH_W), BF16)

    def qk(qmat, hp, k0):
        return jnp.concatenate([_dot(qmat, past[0][0, hp].astype(BF16)), _dot_t(qmat, bufs[0][:, _pair_lanes(hp)])],
                               axis=1)

    def pv(p, hp, k0, ones):
        vt, v_tail = past[1][0, hp].astype(BF16), bufs[1][:, _pair_lanes(hp)]
        if ones is not None:
            vt = jnp.concatenate([vt, jnp.ones(vt.shape, BF16)], axis=0)
            v_tail = jnp.concatenate([v_tail, ones[:v_tail.shape[0]]], axis=1)
        return _dot_t(p[:, :P], vt) + _dot(p[:, P:], v_tail)

    return qk, pv


def _past_specs_and_bufs(past, T, Lp):
    if past is None:
        return [], [], ()
    layer, pk, pv = past
    P = pk.shape[-1]
    assert P % LANES == 0 and pk.shape[2:4] == (N_PAIRS, LANES)
    spec = pl.BlockSpec((None, 1, N_PAIRS, LANES, P), lambda b, qi: (layer, b, 0, 0, 0))
    return [spec, spec], [pltpu.VMEM((Lp - P, BRANCH_W), BF16)] * 2, (pk, pv)


def _fox_kernel(q_ref, k_ref, v_ref, *rest, tq, tk, q_off, has_past):
    past, (fq_ref, fk_ref, o_ref, m_ref, acc_ref), bufs = _split_past(rest, has_past)
    qk, pv = _key_ops(k_ref, v_ref, past, bufs, tk)
    q0 = q_off + pl.program_id(1) * tq
    diag = q0 // tk
    fq_blk = fq_ref[0]
    ones = jnp.ones((tk, LANES), BF16)

    def head_pair(hp, _):
        lanes = _pair_lanes(hp)
        qp = q_ref[0, :, lanes]
        qm = (_keep_head(qp, 0), _keep_head(qp, 1))
        fq = (_head_column(fq_blk, hp * 2), _head_column(fq_blk, hp * 2 + 1))
        m_ref[...] = jnp.full(m_ref.shape, NEG, F32)
        acc_ref[...] = jnp.zeros(acc_ref.shape, F32)

        def score(j, masked):
            k0 = pl.multiple_of(j * tk, tk)
            if masked:
                qpos, kpos = _positions(tq, tk, q0, k0)
                causal = kpos <= qpos
            out = []
            for hh in range(2):
                fk = fk_ref[0, hp * 2 + hh, :, pl.ds(k0, tk)]
                s = qk(qm[hh], hp, k0) + fq[hh] - fk
                out.append(jnp.where(causal, s, NEG) if masked else s)
            return tuple(out)

        def fold(j, scores):
            k0 = pl.multiple_of(j * tk, tk)
            m_old = [m_ref[hh] for hh in range(2)]
            m_new = [jnp.maximum(m_old[hh], jnp.max(scores[hh], axis=1, keepdims=True)) for hh in range(2)]
            p = [jnp.exp2(scores[hh] - m_new[hh]).astype(BF16) for hh in range(2)]
            for hh in range(2):
                acc_ref[hh] = jnp.exp2(m_old[hh] - m_new[hh]) * acc_ref[hh] + pv(p[hh], hp, k0, ones)
                m_ref[hh] = m_new[hh]

        fold(diag, score(diag, True))

        def body(i, _):
            j = diag - 1 - i
            fold(j, score(j, False))
            return 0

        lax.fori_loop(0, diag, body, 0)
        o0 = acc_ref[0, :, :LANES] / acc_ref[0, :, LANES:]
        o1 = acc_ref[1, :, :LANES] / acc_ref[1, :, LANES:]
        o_ref[0, :, lanes] = _merge_pair(o0, o1).astype(o_ref.dtype)
        return 0

    lax.fori_loop(0, N_PAIRS, head_pair, 0)


def _fox(q, k, v, past, fq, fk_t, *, tq, tk, q_off):
    B, T, _ = q.shape
    Lp = fk_t.shape[-1]
    qspec, kspec = _pair_specs(tq, k.shape[1])
    past_specs, bufs, past = _past_specs_and_bufs(past, T, Lp)
    return pl.pallas_call(
        functools.partial(_fox_kernel, tq=tq, tk=tk, q_off=q_off, has_past=bool(past)),
        grid=(B, T // tq),
        in_specs=[qspec, kspec, kspec] + past_specs
        + [pl.BlockSpec((1, tq, N_HEADS), lambda b, qi: (b, qi, 0)),
           pl.BlockSpec((1, N_HEADS, 1, Lp), lambda b, qi: (b, 0, 0, 0))],
        out_specs=qspec,
        out_shape=jax.ShapeDtypeStruct(q.shape, BF16),
        scratch_shapes=[pltpu.VMEM((2, tq, 1), F32), pltpu.VMEM((2, tq, 2 * LANES), F32)] + bufs,
        compiler_params=_cparams(("parallel", "parallel")),
        name="fox_attn",
    )(q, k, v, *past, fq, fk_t[:, :, None, :])


def _sb_kernel(q_ref, k_ref, v_ref, *rest, tq, tk, q_off, has_past):
    past, (o_ref, acc_ref, tail_ref), bufs = _split_past(rest, has_past)
    qk, pv = _key_ops(k_ref, v_ref, past, bufs, tk)
    q0 = q_off + pl.program_id(1) * tq
    diag = q0 // tk
    r = lax.broadcasted_iota(jnp.int32, (LANES, 2 * LANES), 0)
    c = lax.broadcasted_iota(jnp.int32, (LANES, 2 * LANES), 1)
    suffix_w = jnp.where((c >= LANES) | (r > c), 1.0, 0.0).astype(BF16)

    def head_pair(hp, _):
        lanes = _pair_lanes(hp)
        qp = q_ref[0, :, lanes]
        qm = (_keep_head(qp, 0), _keep_head(qp, 1))
        acc_ref[...] = jnp.zeros(acc_ref.shape, F32)
        tail_ref[...] = jnp.zeros(tail_ref.shape, F32)

        def tile(j, masked):
            k0 = pl.multiple_of(j * tk, tk)
            if masked:
                qpos, kpos = _positions(tq, tk, q0, k0)
                strict = kpos < qpos
            heads = (0, 1)
            z = [qk(qm[hh], hp, k0) for hh in heads]
            ls, la, after = [None, None], [None, None], [None, None]
            for hh in heads:
                ls[hh] = jnp.minimum(z[hh], 0.0) - jnp.log2(1.0 + jnp.exp2(-jnp.abs(z[hh])))
                la[hh] = ls[hh] - z[hh]
                if masked:
                    la[hh] = jnp.where(strict, la[hh], 0.0)
                    ls[hh] = jnp.where(strict, ls[hh], NEG)
                la[hh] = la[hh].astype(BF16)
            for hh in heads:
                run = tail_ref[hh]
                blocks = [None] * (tk // LANES)
                for b in reversed(range(tk // LANES)):
                    res = _dot(la[hh][:, b * LANES:(b + 1) * LANES], suffix_w)
                    blocks[b] = res[:, :LANES] + run
                    run = run + res[:, LANES:]
                tail_ref[hh] = run
                after[hh] = jnp.concatenate(blocks, axis=1)
            a = [jnp.exp2(ls[hh] + after[hh]).astype(BF16) for hh in heads]
            for hh in heads:
                acc_ref[hh] += pv(a[hh], hp, k0, None)

        tile(diag, True)

        def body(i, _):
            tile(diag - 1 - i, False)
            return 0

        lax.fori_loop(0, diag, body, 0)
        o_ref[0, :, lanes] = _merge_pair(acc_ref[0], acc_ref[1]).astype(o_ref.dtype)
        return 0

    lax.fori_loop(0, N_PAIRS, head_pair, 0)


def _sb(q, k, v, past, Lp, *, tq, tk, q_off):
    B, T, _ = q.shape
    qspec, kspec = _pair_specs(tq, k.shape[1])
    past_specs, bufs, past = _past_specs_and_bufs(past, T, Lp)
    return pl.pallas_call(
        functools.partial(_sb_kernel, tq=tq, tk=tk, q_off=q_off, has_past=bool(past)),
        grid=(B, T // tq),
        in_specs=[qspec, kspec, kspec] + past_specs,
        out_specs=qspec,
        out_shape=jax.ShapeDtypeStruct(q.shape, BF16),
        scratch_shapes=[pltpu.VMEM((2, tq, LANES), F32), pltpu.VMEM((2, tq, LANES), F32)] + bufs,
        compiler_params=_cparams(("parallel", "parallel")),
        name="sb_attn",
    )(q, k, v, *past)


def _const_key(v):
    bits = int(np.float32(v).view(np.int32))
    return (bits ^ 0x7FFFFFFF) + 1 if bits < 0 else bits


def _float_key(x):
    bits = lax.bitcast_convert_type(x, jnp.int32)
    return jnp.where(bits < 0, jnp.bitwise_xor(bits, 0x7FFFFFFF) + 1, bits)


KEY_MIN = -2 ** 31
KEY_HALF_NEG = _const_key(0.5 * NEG)


KEY_MAX = 2 ** 31 - 1
KEY_BITS = 32
COUNT_CHAINS = 8


def _dsa_kernel(aq_ref, iq_ref, iwt_ref, k_ref, vt_ref, ik_ref, o_ref, key_ref, q8_ref, m_ref, acc_ref,
                *, tq, tk, q_off, n_rows, topk):
    q0 = q_off + pl.program_id(1) * tq
    n_tiles = (q0 + n_rows + tk - 1) // tk

    def tile_start(j):
        return pl.multiple_of(j * tk, LANES)

    iq = iq_ref[0]
    iwt = iwt_ref[0]
    qh = [_keep_head(iq[:, (h // 2) * LANES:(h // 2 + 1) * LANES], h % 2) for h in range(N_IDX)]
    q_chunk = jnp.right_shift(q0 + lax.broadcasted_iota(jnp.int32, (tk, tq), 1), CHUNK_SHIFT)

    def score_tile(j, _):
        k0 = tile_start(j)
        ik = ik_ref[0, pl.ds(k0, tk), :]
        sc = jnp.zeros((tk, tq), F32)
        for h in range(N_IDX):
            sc = sc + jnp.maximum(_dot_t(ik, qh[h]), 0.0) * iwt[h:h + 1, :]
        key_chunk = jnp.right_shift(k0 + lax.broadcasted_iota(jnp.int32, (tk, tq), 0), CHUNK_SHIFT)
        key_ref[pl.ds(k0, tk), :] = _float_key(jnp.where(key_chunk <= q_chunk, sc, NEG))
        return 0

    lax.fori_loop(0, n_tiles, score_tile, 0)

    def search():
        def step(_, cr):
            lo, hi, above_hi = cr
            mid = jnp.right_shift(lo, 1) + jnp.right_shift(hi, 1) + jnp.bitwise_and(jnp.bitwise_and(lo, hi), 1)

            def count_tile(j, cnt):
                above = jnp.where(key_ref[pl.ds(tile_start(j), tk), :] > mid, 1.0, 0.0)
                return cnt + jnp.sum(above.reshape((-1,) + cnt.shape), axis=0)

            cnt = lax.fori_loop(0, n_tiles, count_tile, jnp.zeros((COUNT_CHAINS, SUBLANES, tq), F32))
            cnt = jnp.sum(jnp.sum(cnt, axis=0), axis=0, keepdims=True)
            less = cnt < topk
            return jnp.where(less, lo, mid), jnp.where(less, mid, hi), jnp.where(less, cnt, above_hi)

        init = (jnp.full((1, tq), KEY_MIN, jnp.int32), jnp.full((1, tq), KEY_MAX, jnp.int32), jnp.zeros((1, tq), F32))
        _, hi, above_hi = lax.fori_loop(0, KEY_BITS, step, init)
        return hi, topk - above_hi

    thr, room = lax.cond(q0 + n_rows > topk, search,
                         lambda: (jnp.full((1, tq), KEY_MIN, jnp.int32), jnp.zeros((1, tq), F32)))
    row_ok = lax.broadcasted_iota(jnp.int32, (1, tq), 1) < n_rows
    room = jnp.where(row_ok, room, 0.0)

    @pl.when(jnp.max(room) > 0.0)
    def _():
        r = lax.broadcasted_iota(jnp.int32, (LANES, LANES), 0)
        c = lax.broadcasted_iota(jnp.int32, (LANES, LANES), 1)
        before = jnp.where(c < r, 1.0, 0.0).astype(BF16)

        def tie_tile(j, seen):
            k0 = tile_start(j)
            for b in range(tk // LANES):
                rows = pl.ds(k0 + b * LANES, LANES)
                keys = key_ref[rows, :]
                eq = keys == thr
                eqf = jnp.where(eq, 1.0, 0.0)
                rank = _dot(before, eqf.astype(BF16)) + seen
                key_ref[rows, :] = jnp.where(eq, jnp.where(rank < room, keys + 1, keys), keys)
                seen = seen + jnp.sum(eqf, axis=0, keepdims=True)
            return seen

        lax.fori_loop(0, n_tiles, tie_tile, jnp.zeros((1, tq), F32))

    aq = aq_ref[0]
    for h in range(N_HEADS):
        q8_ref[h * tq:(h + 1) * tq, :] = _keep_head(aq[:, (h // 2) * LANES:(h // 2 + 1) * LANES], h % 2)
    m_ref[...] = jnp.full(m_ref.shape, NEG, F32)
    acc_ref[...] = jnp.zeros(acc_ref.shape, F32)
    thr_sel = jnp.maximum(thr, KEY_HALF_NEG)

    def attend_tile(j, _):
        k0 = tile_start(j)
        kt = k_ref[0, pl.ds(k0, tk), :]
        v1t = vt_ref[0, :, pl.ds(k0, tk)]
        bias = jnp.where(key_ref[pl.ds(k0, tk), :] > thr_sel, 0.0, NEG)
        st = _dot_t(kt, q8_ref[...])
        st = jnp.concatenate([st[:, h * tq:(h + 1) * tq] + bias for h in range(N_HEADS)], axis=1)
        m_old = m_ref[...]
        m_new = jnp.maximum(m_old, jnp.max(st, axis=0, keepdims=True))
        p = jnp.exp2(st - m_new).astype(BF16)
        acc_ref[...] = jnp.exp2(m_old - m_new) * acc_ref[...] + _dot(v1t, p)
        m_ref[...] = m_new
        return 0

    lax.fori_loop(0, n_tiles, attend_tile, 0)
    for hp in range(N_PAIRS):
        pair = []
        for h in (2 * hp, 2 * hp + 1):
            a = acc_ref[:, h * tq:(h + 1) * tq]
            pair.append(a[:HEAD_DIM] / a[HEAD_DIM:])
        o_ref[0, :, hp * LANES:(hp + 1) * LANES] = jnp.concatenate(pair, axis=0).T.astype(o_ref.dtype)


def _dsa(aq, iq, iw, k2, v1t, ik2, *, tq, q_off, topk):
    B, T, _ = aq.shape
    Lp = k2.shape[1]
    tk = Lp if T <= LANES else next(t for t in (512, 384, 256, 128) if Lp % t == 0)
    Tp = -(-T // LANES) * LANES
    tq = max(tq, LANES)
    n_rows = min(T, tq)
    pad_t = lambda a: jnp.pad(a, ((0, 0), (0, Tp - T), (0, 0)))
    qspec = lambda w: pl.BlockSpec((1, tq, w), lambda b, qi: (b, qi, 0))
    kspec = pl.BlockSpec((1, Lp, LANES), lambda b, qi: (b, 0, 0))
    out = pl.pallas_call(
        functools.partial(_dsa_kernel, tq=tq, tk=tk, q_off=q_off, n_rows=n_rows, topk=topk),
        grid=(B, Tp // tq),
        in_specs=[qspec(BRANCH_W), qspec(N_IDX * IDX_DIM), pl.BlockSpec((1, SUBLANES, tq), lambda b, qi: (b, 0, qi)),
                  kspec, pl.BlockSpec((1, LANES, Lp), lambda b, qi: (b, 0, 0)), kspec],
        out_specs=qspec(BRANCH_W),
        out_shape=jax.ShapeDtypeStruct((B, Tp, BRANCH_W), BF16),
        scratch_shapes=[pltpu.VMEM((Lp, tq), jnp.int32), pltpu.VMEM((N_HEADS * tq, LANES), BF16),
                        pltpu.VMEM((1, N_HEADS * tq), F32), pltpu.VMEM((LANES, N_HEADS * tq), F32)],
        compiler_params=_cparams(("parallel", "parallel")),
        name="dsa_attn",
    )(pad_t(aq), pad_t(iq), jnp.swapaxes(pad_t(iw), 1, 2), k2, v1t, ik2)
    return out[:, :T]


def _merge_kernel(x_ref, oa_ref, ob_ref, oc_ref, g_ref, wg_ref, wb_ref, wo_ref, y_ref):
    x = x_ref[0]
    D = x.shape[-1]
    xn = _rms(x, g_ref[...]).astype(BF16)
    merged = jnp.zeros(x.shape, F32)
    for n, o_ref in enumerate((oa_ref, ob_ref, oc_ref)):
        gate = _sigmoid(_dot(xn, wg_ref[:, n * D:(n + 1) * D]))
        merged = merged + gate * _dot(o_ref[0], wb_ref[n])
    y_ref[0] = x + _dot(merged.astype(BF16), wo_ref[...])


def _merge(x, oa, ob, oc, g, wg, wb, wo, tm):
    B, T, D = x.shape
    tspec = lambda w_: pl.BlockSpec((1, tm, w_), lambda b, t: (b, t, 0))
    const = lambda a: pl.BlockSpec(a.shape, lambda b, t: (0,) * a.ndim)
    return pl.pallas_call(
        _merge_kernel,
        grid=(B, T // tm),
        in_specs=[tspec(D), tspec(BRANCH_W), tspec(BRANCH_W), tspec(BRANCH_W), const(g), const(wg), const(wb),
                  const(wo)],
        out_specs=tspec(D),
        out_shape=jax.ShapeDtypeStruct(x.shape, F32),
        compiler_params=_cparams(("parallel", "parallel")),
        name="merge_out",
    )(x, oa, ob, oc, g, wg, wb, wo)


def _ffn_kernel(x_ref, g_ref, past_ref, wu_ref, cw_ref, cb_ref, wd_ref, gf_ref, y_ref, st_ref,
                hbuf_ref, carry_ref, acc_ref, *, tm, final_norm):
    fc = FF_CHUNK
    d_ff = wd_ref.shape[0]
    n_chunks = d_ff // fc

    @pl.when(pl.program_id(1) == 0)
    def _():
        carry_ref[...] = past_ref[0]

    x = x_ref[0]
    xn = _rms(x, g_ref[...]).astype(BF16)
    acc_ref[...] = jnp.zeros(acc_ref.shape, F32)

    for c in range(n_chunks):
        hbuf = hbuf_ref.at[c % 2]
        both = lambda ref: jnp.concatenate([ref[:, c * fc:(c + 1) * fc], ref[:, d_ff + c * fc:d_ff + (c + 1) * fc]],
                                           axis=1)
        h = jnp.concatenate([_dot(xn, wu_ref[:, c * fc:(c + 1) * fc]),
                             _dot(xn, wu_ref[:, d_ff + c * fc:d_ff + (c + 1) * fc])], axis=1)
        hbuf[0:SUBLANES, :] = carry_ref[c]
        hbuf[SUBLANES:SUBLANES + tm, :] = h
        carry_ref[c] = h[tm - SUBLANES:tm, :]
        cw = both(cw_ref)
        hc = (both(cb_ref) + cw[0:1] * hbuf[SUBLANES - 2:SUBLANES - 2 + tm, :]
              + cw[1:2] * hbuf[SUBLANES - 1:SUBLANES - 1 + tm, :] + cw[2:3] * h)
        gate, up = hc[:, :fc], hc[:, fc:]
        act = gate * _sigmoid(gate) * up
        acc_ref[...] += _dot(act.astype(BF16), wd_ref[c * fc:(c + 1) * fc, :])
    st_ref[0] = carry_ref[...]
    y = x + acc_ref[...]
    if final_norm:
        y = _rms(y, gf_ref[...])
    y_ref[0] = y


def _ffn(x, g, past, wu, cw, cb, wd, gf, tm, final_norm):
    B, T, D = x.shape
    nc, fc2 = wd.shape[0] // FF_CHUNK, 2 * FF_CHUNK
    tspec = pl.BlockSpec((1, tm, D), lambda b, t: (b, t, 0))
    const = lambda a: pl.BlockSpec(a.shape, lambda b, t: (0,) * a.ndim, pipeline_mode=pl.Buffered(1))
    sspec = pl.BlockSpec((1, nc, SUBLANES, fc2), lambda b, t: (b, 0, 0, 0))
    return pl.pallas_call(
        functools.partial(_ffn_kernel, tm=tm, final_norm=final_norm),
        grid=(B, T // tm),
        in_specs=[tspec, const(g), sspec, const(wu), const(cw), const(cb), const(wd), const(gf)],
        out_specs=[tspec, sspec],
        out_shape=[jax.ShapeDtypeStruct(x.shape, F32), jax.ShapeDtypeStruct((B, nc, SUBLANES, fc2), F32)],
        scratch_shapes=[pltpu.VMEM((2, tm + SUBLANES, fc2), F32), pltpu.VMEM((nc, SUBLANES, fc2), F32),
                        pltpu.VMEM((tm, D), F32)],
        compiler_params=_cparams(("arbitrary", "arbitrary")),
        name="conv_ffn",
    )(x, g, past, wu, cw, cb, wd, gf)


def _rope_tables(pos):
    half = ROT_DIM // 2
    freq = ROPE_THETA ** (-jnp.arange(half, dtype=F32) / half)
    ang = pos.astype(F32)[:, None] * freq[None, :]
    lane = np.arange(LANES) % HEAD_DIM
    cos = jnp.cos(ang)[:, lane % half]
    sin = jnp.sin(ang)[:, lane % half]
    rc = jnp.where(lane < ROT_DIM, cos, 1.0)
    rs1 = jnp.where(lane < half, -sin, 0.0)
    rs2 = jnp.where((lane >= half) & (lane < ROT_DIM), sin, 0.0)
    return rc, rs1, rs2


IN_SIZES = (BRANCH_W, HEAD_DIM, HEAD_DIM, N_IDX * IDX_DIM, IDX_DIM, N_IDX) + (BRANCH_W,) * 6 + (N_HEADS,)
PACKED_W = ROPE_W + MISC_W + 6 * BRANCH_W


def _pack_w_kernel(w_ref, o_ref, s_ref):
    offs = np.concatenate([[0], np.cumsum(IN_SIZES)]).tolist()
    a_q, a_k, a_v, i_q, i_k, i_w, b_q = offs[:7]
    c_f = offs[12]
    src = lambda c0, width: w_ref[0, :, c0:c0 + width]
    s_ref[:, 0:BRANCH_W] = src(a_q, BRANCH_W)
    s_ref[:, BRANCH_W:BRANCH_W + 256] = src(i_q, 256)
    for dst, c0 in ((768, a_k), (896, i_k), (ROPE_W, a_v)):
        s_ref[:, dst:dst + HEAD_DIM] = src(c0, HEAD_DIM)
        s_ref[:, dst + HEAD_DIM:dst + 2 * HEAD_DIM] = src(c0, HEAD_DIM)
    misc = ROPE_W + LANES
    s_ref[:, misc:misc + LANES] = jnp.zeros((s_ref.shape[0], LANES), F32)
    s_ref[:, misc:misc + N_IDX] = src(i_w, N_IDX)
    s_ref[:, misc + CF_LANE:misc + CF_LANE + N_HEADS] = src(c_f, N_HEADS)
    s_ref[:, ROPE_W + MISC_W:PACKED_W] = src(b_q, 6 * BRANCH_W)
    o_ref[...] = s_ref[...].astype(BF16)


def _pack_w(w_in, l):
    D = w_in.shape[1]
    rows = 256
    return pl.pallas_call(
        _pack_w_kernel,
        grid=(D // rows,),
        in_specs=[pl.BlockSpec((1, rows, w_in.shape[2]), lambda r: (l, r, 0))],
        out_specs=pl.BlockSpec((rows, PACKED_W), lambda r: (r, 0)),
        out_shape=jax.ShapeDtypeStruct((D, PACKED_W), BF16),
        scratch_shapes=[pltpu.VMEM((rows, PACKED_W), F32)],
        compiler_params=_cparams(("parallel",)),
        name="pack_w_in",
    )(w_in)


def _pack_layer(l, norm_mix, w_in, b_f, w_branch, w_gate, w_out, norm_ffn, w_up, conv_w, conv_b, w_down):
    w = _pack_w(w_in, l)
    bf_row = jnp.zeros((1, LANES), F32).at[0, CF_LANE:CF_LANE + N_HEADS].set(b_f[l])

    d_ff = w_down.shape[1]
    fc = FF_CHUNK
    nc = d_ff // fc
    assert nc * fc == d_ff
    halves = lambda a: jnp.concatenate([a[..., :d_ff].reshape(a.shape[:-1] + (nc, fc)),
                                        a[..., d_ff:].reshape(a.shape[:-1] + (nc, fc))], axis=-1)
    wu, cw, cb, wd = w_up[l].astype(BF16), conv_w[l], conv_b[l][None], w_down[l].astype(BF16)
    return dict(g_mix=norm_mix[l][None], w=w, bf_row=bf_row, wg=w_gate[l].astype(BF16),
                wb=w_branch[l].astype(BF16), wo=w_out[l].astype(BF16), g_ffn=norm_ffn[l][None],
                wu=wu, cw=cw, cb=cb, wd=wd, halves=halves, nc=nc, d_ff=d_ff)


def _conv_state_in(state, halves):
    st = jnp.moveaxis(halves(state), 2, 1)
    return jnp.pad(st, ((0, 0), (0, 0), (SUBLANES - (CONV_W - 1), 0), (0, 0)))


def _conv_state_out(st, d_ff):
    st = st[:, :, SUBLANES - (CONV_W - 1):, :]
    B, nc, r, fc2 = st.shape
    fc = fc2 // 2
    gate = jnp.moveaxis(st[..., :fc], 1, 2).reshape(B, r, d_ff)
    up = jnp.moveaxis(st[..., fc:], 1, 2).reshape(B, r, d_ff)
    return jnp.concatenate([gate, up], axis=-1)


def _pad_keys(a, Lp):
    return jnp.pad(a, ((0, 0), (0, Lp - a.shape[1]), (0, 0)))


def _group_layer(x, past, lw, gf, layer, depth, stacked, tiles):
    B, T, D = x.shape
    P = 0 if past is None else past[0].shape[1]
    L = P + T
    Lp = -(-L // KEY_TILE) * KEY_TILE
    topk = max(1, min(TOPK_MAX, L // 4))
    tables = _rope_tables(jnp.arange(P, L))

    stacked, (iw, aq16, iq16, ak16, avt16, ik16, bq16, bk16, bv16, cq16, ck16, cv16) = _project(
        x, lw["g_mix"], lw["w"], lw["bf_row"], tables, tiles["proj"], layer, depth, stacked)
    lf = stacked[-1][layer]

    if past is None:
        keys = (ak16, ik16)
        lf_all = lf
        conv_in = jnp.zeros((B, lw["nc"], SUBLANES, 2 * FF_CHUNK), F32)
        past_b = past_c = None
    else:
        p_ak, p_av, p_ik, p_bk, p_bv, p_ck, p_cv, p_lf, p_conv = past
        dup = lambda a: jnp.concatenate([a, a], axis=-1).astype(BF16)
        keys = tuple(jnp.concatenate([o, n], axis=1) for o, n in zip((dup(p_ak), dup(p_ik)), (ak16, ik16)))
        old_vt = jnp.swapaxes(p_av, 1, 2).astype(BF16)
        avt16 = jnp.concatenate([jnp.concatenate([old_vt, jnp.ones_like(old_vt)], axis=1), avt16], axis=2)
        lf_all = jnp.concatenate([p_lf, lf], axis=1)
        conv_in = _conv_state_in(p_conv, lw["halves"])
        past_b, past_c = (layer, p_bk, p_bv), (layer, p_ck, p_cv)
    ak2, ik2 = (_pad_keys(a, Lp) for a in keys)
    v1t = jnp.pad(avt16, ((0, 0), (0, 0), (0, Lp - L)))

    fk_t = _cumsum_time(jnp.swapaxes(_pad_keys(lf_all, Lp), 1, 2))
    fq = jnp.swapaxes(fk_t[:, :, P:L], 1, 2)

    o_a = _dsa(aq16, iq16, iw, ak2, v1t, ik2, tq=tiles["dsa"], q_off=P, topk=topk)
    o_b = _sb(bq16, bk16, bv16, past_b, Lp, tq=tiles["sb"][0], tk=tiles["sb"][1], q_off=P)
    o_c = _fox(cq16, ck16, cv16, past_c, fq, fk_t, tq=tiles["fox"][0], tk=tiles["fox"][1], q_off=P)

    x = _merge(x, o_a, o_b, o_c, lw["g_mix"], lw["wg"], lw["wb"], lw["wo"], tiles["merge"])
    x, st = _ffn(x, lw["g_ffn"], conv_in, lw["wu"], lw["cw"], lw["cb"], lw["wd"], gf, tiles["ffn"],
                 layer == depth - 1)
    return x, stacked, _conv_state_out(st, lw["d_ff"])


def _tiles(T, P):
    pick = lambda want: min(want, T)
    if T % 512 == 0 and P % 512 == 0:
        sb, fox = (256, 256), (256, 512)
    else:
        assert T <= KEY_TILE
        sb = fox = (T, -(-(P + T) // KEY_TILE) * KEY_TILE)
    return dict(proj=pick(512), dsa=pick(128), sb=sb, fox=fox, merge=pick(512), ffn=pick(512))


def kernel(x_prompt, x_sample, cache_a_k, cache_a_v, cache_a_idx_k, cache_b_k, cache_b_v, cache_c_k, cache_c_v,
           cache_c_logf, state_ffn_conv, norm_mix, w_in, b_f, w_branch, w_gate, w_out, norm_ffn, w_up, conv_w,
           conv_b, w_down, norm_final):
    depth = w_in.shape[0]
    caches = (cache_a_k, cache_a_v, cache_a_idx_k, cache_b_k, cache_b_v, cache_c_k, cache_c_v, cache_c_logf,
              state_ffn_conv)
    layers = [_pack_layer(l, norm_mix, w_in, b_f, w_branch, w_gate, w_out, norm_ffn, w_up, conv_w, conv_b, w_down)
              for l in range(depth)]
    gf = norm_final[None]

    def trunk(x, past):
        B, T, _ = x.shape
        tiles = _tiles(T, 0 if past is None else past[0].shape[2])
        stacked, conv_states = None, []
        if past is not None:
            P = past[0].shape[2]
            wide = lambda c: jnp.transpose(c, (0, 1, 3, 4, 2)).reshape(depth, B, N_PAIRS, LANES, P)
        for l in range(depth):
            lp = None if past is None else tuple(wide(c) if 3 <= i <= 6 else c[l] for i, c in enumerate(past))
            x, stacked, conv_state = _group_layer(x, lp, layers[l], gf, l, depth, stacked, tiles)
            conv_states.append(conv_state)
        ak, av, ik, bk, bv, ck, cv, lf = stacked
        heads = lambda a: a.reshape(depth, B, T, N_HEADS, HEAD_DIM)
        return x, (ak, av, ik, heads(bk), heads(bv), heads(ck), heads(cv), lf, jnp.stack(conv_states, axis=0))

    y_prompt, p_state = trunk(x_prompt, None)
    y_sample, s_state = trunk(x_sample, caches)
    return (y_prompt, y_sample) + p_state + s_state
```

```python
import functools

import jax
import jax.numpy as jnp
import numpy as np
from jax import lax
from jax.experimental import pallas as pl
from jax.experimental.pallas import tpu as pltpu

F32 = jnp.float32
BF16 = jnp.bfloat16

LANES = 128
SUBLANES = 8
HEAD_DIM = 64
N_HEADS = 8
N_PAIRS = N_HEADS // 2
BRANCH_W = N_HEADS * HEAD_DIM
N_BRANCH = 3
N_IDX = 4
IDX_DIM = 64
CHUNK = 64
CHUNK_SHIFT = 6
TOPK_MAX = 256
ROT_DIM = HEAD_DIM // 4
ROPE_THETA = 500000.0
CONV_W = 3
EPS = 1e-6
NEG = -1e30
KEY_TILE = 128
FF_CHUNK = 256
VMEM_LIMIT = 56 * 1024 * 1024

ROPE_W = 1024
MISC_W = 256
CF_LANE = 8
LOG2E = 1.4426950408889634
Q_SCALE = HEAD_DIM ** -0.5 * LOG2E


def _cparams(sem):
    return pltpu.CompilerParams(dimension_semantics=sem, vmem_limit_bytes=VMEM_LIMIT)


def _rms(x, g):
    return x * lax.rsqrt(jnp.mean(x * x, axis=-1, keepdims=True) + EPS) * g


def _sigmoid(z):
    return 1.0 / (1.0 + jnp.exp(-z))


def _log_sigmoid(z):
    return jnp.minimum(z, 0.0) - jnp.log1p(jnp.exp(-jnp.abs(z)))


def _dot_t(a, b):
    return lax.dot_general(a, b, (((1,), (1,)), ((), ())), preferred_element_type=F32)


def _dot(a, b):
    return jnp.dot(a, b, preferred_element_type=F32)


def _keep_head(blk, which):
    lane = lax.broadcasted_iota(jnp.int32, blk.shape, blk.ndim - 1)
    keep = (lane >= HEAD_DIM) if which else (lane < HEAD_DIM)
    return jnp.where(keep, blk, jnp.zeros_like(blk))


def _merge_pair(o0, o1):
    lane = lax.broadcasted_iota(jnp.int32, o0.shape, o0.ndim - 1)
    return jnp.where(lane < HEAD_DIM, o0, o1)


N_CACHE_ROWS = 8


def _proj_kernel(x_ref, g_ref, w_ref, bf_ref, rc_ref, rs1_ref, rs2_ref, *refs):
    (ak_ref, av_ref, ik_ref, bk_ref, bv_ref, ck_ref, cv_ref, lf_ref, iw_ref,
     aq16, iq16, ak16, avt16, ik16, bq16, bk16, bv16, cq16, ck16, cv16) = refs[-(N_CACHE_ROWS + 12):]
    xn = _rms(x_ref[0], g_ref[...]).astype(BF16)
    rc, rs1, rs2 = rc_ref[...], rs1_ref[...], rs2_ref[...]

    def cols(c0, width):
        return _dot(xn, w_ref[:, c0:c0 + width])

    def group(blk, g):
        return blk[:, g * LANES:(g + 1) * LANES]

    def rope(blk):
        half = ROT_DIM // 2
        return blk * rc + pltpu.roll(blk, LANES - half, 1) * rs1 + pltpu.roll(blk, half, 1) * rs2

    wide = cols(0, BRANCH_W)
    for g in range(4):
        aq16[0, :, g * LANES:(g + 1) * LANES] = (rope(group(wide, g)) * Q_SCALE).astype(BF16)
    wide = cols(BRANCH_W, ROPE_W - BRANCH_W)
    for g in range(2):
        iq16[0, :, g * LANES:(g + 1) * LANES] = (rope(group(wide, g)) * IDX_DIM ** -0.5).astype(BF16)
    blk = rope(group(wide, 2))
    ak_ref[0, 0] = blk[:, :HEAD_DIM]
    ak16[0] = blk.astype(BF16)
    blk = rope(group(wide, 3))
    ik_ref[0, 0] = blk[:, :IDX_DIM]
    ik16[0] = blk.astype(BF16)

    wide = cols(ROPE_W, MISC_W)
    blk = group(wide, 0)
    av_ref[0, 0] = blk[:, :HEAD_DIM]
    row = lax.broadcasted_iota(jnp.int32, (LANES, blk.shape[0]), 0)
    avt16[0] = jnp.where(row < HEAD_DIM, blk.T, 1.0).astype(BF16)
    blk = group(wide, 1)
    iw_ref[0] = blk[:, :SUBLANES] * N_IDX ** -0.5
    lf_ref[0, 0] = _log_sigmoid(blk[:, CF_LANE:CF_LANE + N_HEADS] + bf_ref[:, CF_LANE:CF_LANE + N_HEADS])

    base = ROPE_W + MISC_W
    outs = ((None, bq16), (bk_ref, bk16), (bv_ref, bv16), (None, cq16), (ck_ref, ck16), (cv_ref, cv16))
    for n, (o32, o16) in enumerate(outs):
        wide = cols(base + n * BRANCH_W, BRANCH_W)
        if o32 is None:
            o16[0] = (wide * Q_SCALE).astype(BF16)
        else:
            o32[0, 0] = wide
            o16[0] = wide.astype(BF16)


def _project(x, g, w, bf_row, tables, tm, layer, depth, stacked):
    B, T, D = x.shape
    tok = lambda w_, dt: jax.ShapeDtypeStruct((B, T, w_), dt)
    tspec = lambda w_: pl.BlockSpec((1, tm, w_), lambda b, t: (b, t, 0))
    const = lambda a: pl.BlockSpec(a.shape, lambda b, t: (0,) * a.ndim, pipeline_mode=pl.Buffered(1))
    cache_w = (64, 64, 64, 512, 512, 512, 512, 8)
    cache_shapes = [jax.ShapeDtypeStruct((depth, B, T, w_), F32) for w_ in cache_w]
    cache_specs = [pl.BlockSpec((1, 1, tm, w_), lambda b, t: (layer, b, t, 0)) for w_ in cache_w]
    work = [(tok(w_, dt), tspec(w_)) for w_, dt in ((8, F32), (512, BF16), (256, BF16), (128, BF16))]
    work += [(jax.ShapeDtypeStruct((B, LANES, T), BF16), pl.BlockSpec((1, LANES, tm), lambda b, t: (b, 0, t)))]
    work += [(tok(w_, BF16), tspec(w_)) for w_ in (128,) + (512,) * 6]
    work_shapes, work_specs = [s for s, _ in work], [p for _, p in work]
    rspec = pl.BlockSpec((tm, LANES), lambda b, t: (t, 0))
    prev = () if stacked is None else tuple(stacked)
    n_in = 7
    outs = pl.pallas_call(
        _proj_kernel,
        grid=(B, T // tm),
        in_specs=[tspec(D), const(g), const(w), const(bf_row), rspec, rspec, rspec]
        + [pl.BlockSpec(memory_space=pl.ANY)] * len(prev),
        out_specs=cache_specs + work_specs,
        out_shape=cache_shapes + work_shapes,
        input_output_aliases={n_in + i: i for i in range(len(prev))},
        compiler_params=_cparams(("parallel", "parallel")),
        name="in_proj",
    )(x, g, w, bf_row, *tables, *prev)
    return outs[:N_CACHE_ROWS], outs[N_CACHE_ROWS:]


def _split3(x):
    hi = x.astype(BF16)
    r = x - hi.astype(F32)
    mid = r.astype(BF16)
    lo = (r - mid.astype(F32)).astype(BF16)
    return hi, mid, lo


def _cumsum_kernel(lf_ref, o_ref, *, blk):
    L = lf_ref.shape[-1]
    r = lax.broadcasted_iota(jnp.int32, (blk, blk), 0)
    c = lax.broadcasted_iota(jnp.int32, (blk, blk), 1)
    upper = jnp.where(r <= c, 1.0, 0.0).astype(BF16)
    carry = jnp.zeros((N_HEADS, 1), F32)
    for j in range(L // blk):
        hi, mid, lo = _split3(lf_ref[0, :, j * blk:(j + 1) * blk])
        f = _dot(hi, upper) + _dot(mid, upper) + _dot(lo, upper) + carry
        o_ref[0, :, j * blk:(j + 1) * blk] = f * LOG2E
        carry = f[:, blk - 1:blk]


def _cumsum_time(lf_t):
    B, H, L = lf_t.shape
    blk = LANES
    assert L % blk == 0
    spec = pl.BlockSpec((1, H, L), lambda b: (b, 0, 0))
    return pl.pallas_call(
        functools.partial(_cumsum_kernel, blk=blk),
        grid=(B,),
        in_specs=[spec],
        out_specs=spec,
        out_shape=jax.ShapeDtypeStruct((B, H, L), F32),
        compiler_params=_cparams(("parallel",)),
        name="logf_cumsum",
    )(lf_t)


def _head_column(blk, h):
    lane = lax.broadcasted_iota(jnp.int32, blk.shape, 1)
    return jnp.sum(jnp.where(lane == h, blk, 0.0), axis=1, keepdims=True)


def _positions(tq, tk, q0, k0):
    qpos = q0 + lax.broadcasted_iota(jnp.int32, (tq, tk), 0)
    kpos = k0 + lax.broadcasted_iota(jnp.int32, (tq, tk), 1)
    return qpos, kpos


def _pair_specs(tq, Lp):
    qspec = pl.BlockSpec((1, tq, BRANCH_W), lambda b, qi: (b, qi, 0))
    kspec = pl.BlockSpec((1, Lp, BRANCH_W), lambda b, qi: (b, 0, 0))
    return qspec, kspec


def _pair_lanes(hp):
    return pl.ds(pl.multiple_of(hp * LANES, LANES), LANES)


def _split_past(rest, has_past):
    return (rest[:2], rest[2:-2], rest[-2:]) if has_past else ((), rest, ())


def _key_ops(k_ref, v_ref, past, bufs, tk):
    if not past:
        def qk(qmat, hp, k0):
            return _dot_t(qmat, k_ref[0, pl.ds(k0, tk), _pair_lanes(hp)])

        def pv(p, hp, k0, ones):
            v = v_ref[0, pl.ds(k0, tk), _pair_lanes(hp)]
            return _dot(p, v if ones is None else jnp.concatenate([v, ones], axis=1))

        return qk, pv

    P, T = past[0].shape[-1], k_ref.shape[1]
    for new, buf in zip((k_ref, v_ref), bufs):
        buf[0:T, :] = new[0]
        if buf.shape[0] > T:
            buf[T:, :] = jnp.zeros((buf.shape[0] - T, BRANCH_W), BF16)

    def qk(qmat, hp, k0):
        return jnp.concatenate([_dot(qmat, past[0][0, hp].astype(BF16)), _dot_t(qmat, bufs[0][:, _pair_lanes(hp)])],
                               axis=1)

    def pv(p, hp, k0, ones):
        vt, v_tail = past[1][0, hp].astype(BF16), bufs[1][:, _pair_lanes(hp)]
        if ones is not None:
            vt = jnp.concatenate([vt, jnp.ones(vt.shape, BF16)], axis=0)
            v_tail = jnp.concatenate([v_tail, ones[:v_tail.shape[0]]], axis=1)
        return _dot_t(p[:, :P], vt) + _dot(p[:, P:], v_tail)

    return qk, pv


def _past_specs_and_bufs(past, T, Lp):
    if past is None:
        return [], [], ()
    layer, pk, pv = past
    P = pk.shape[-1]
    assert P % LANES == 0 and pk.shape[2:4] == (N_PAIRS, LANES)
    spec = pl.BlockSpec((None, 1, N_PAIRS, LANES, P), lambda b, qi: (layer, b, 0, 0, 0))
    return [spec, spec], [pltpu.VMEM((Lp - P, BRANCH_W), BF16)] * 2, (pk, pv)


def _fox_kernel(q_ref, k_ref, v_ref, *rest, tq, tk, q_off, has_past):
    past, (fq_ref, fk_ref, o_ref, m_ref, acc_ref), bufs = _split_past(rest, has_past)
    qk, pv = _key_ops(k_ref, v_ref, past, bufs, tk)
    q0 = q_off + pl.program_id(1) * tq
    diag = q0 // tk
    fq_blk = fq_ref[0]
    ones = jnp.ones((tk, LANES), BF16)

    def head_pair(hp, _):
        lanes = _pair_lanes(hp)
        qp = q_ref[0, :, lanes]
        qm = (_keep_head(qp, 0), _keep_head(qp, 1))
        fq = (_head_column(fq_blk, hp * 2), _head_column(fq_blk, hp * 2 + 1))
        m_ref[...] = jnp.full(m_ref.shape, NEG, F32)
        acc_ref[...] = jnp.zeros(acc_ref.shape, F32)

        def score(j, masked):
            k0 = pl.multiple_of(j * tk, tk)
            if masked:
                qpos, kpos = _positions(tq, tk, q0, k0)
                causal = kpos <= qpos
            out = []
            for hh in range(2):
                fk = fk_ref[0, hp * 2 + hh, :, pl.ds(k0, tk)]
                s = qk(qm[hh], hp, k0) + fq[hh] - fk
                out.append(jnp.where(causal, s, NEG) if masked else s)
            return tuple(out)

        def fold(j, scores):
            k0 = pl.multiple_of(j * tk, tk)
            m_old = [m_ref[hh] for hh in range(2)]
            m_new = [jnp.maximum(m_old[hh], jnp.max(scores[hh], axis=1, keepdims=True)) for hh in range(2)]
            p = [jnp.exp2(scores[hh] - m_new[hh]).astype(BF16) for hh in range(2)]
            for hh in range(2):
                acc_ref[hh] = jnp.exp2(m_old[hh] - m_new[hh]) * acc_ref[hh] + pv(p[hh], hp, k0, ones)
                m_ref[hh] = m_new[hh]

        fold(diag, score(diag, True))

        def body(i, _):
            j = diag - 1 - i
            fold(j, score(j, False))
            return 0

        lax.fori_loop(0, diag, body, 0)
        o0 = acc_ref[0, :, :LANES] / acc_ref[0, :, LANES:]
        o1 = acc_ref[1, :, :LANES] / acc_ref[1, :, LANES:]
        o_ref[0, :, lanes] = _merge_pair(o0, o1).astype(o_ref.dtype)
        return 0

    lax.fori_loop(0, N_PAIRS, head_pair, 0)


def _fox(q, k, v, past, fq, fk_t, *, tq, tk, q_off):
    B, T, _ = q.shape
    Lp = fk_t.shape[-1]
    qspec, kspec = _pair_specs(tq, k.shape[1])
    past_specs, bufs, past = _past_specs_and_bufs(past, T, Lp)
    return pl.pallas_call(
        functools.partial(_fox_kernel, tq=tq, tk=tk, q_off=q_off, has_past=bool(past)),
        grid=(B, T // tq),
        in_specs=[qspec, kspec, kspec] + past_specs
        + [pl.BlockSpec((1, tq, N_HEADS), lambda b, qi: (b, qi, 0)),
           pl.BlockSpec((1, N_HEADS, 1, Lp), lambda b, qi: (b, 0, 0, 0))],
        out_specs=qspec,
        out_shape=jax.ShapeDtypeStruct(q.shape, BF16),
        scratch_shapes=[pltpu.VMEM((2, tq, 1), F32), pltpu.VMEM((2, tq, 2 * LANES), F32)] + bufs,
        compiler_params=_cparams(("parallel", "parallel")),
        name="fox_attn",
    )(q, k, v, *past, fq, fk_t[:, :, None, :])


def _sb_kernel(q_ref, k_ref, v_ref, *rest, tq, tk, q_off, has_past):
    past, (o_ref, acc_ref, tail_ref), bufs = _split_past(rest, has_past)
    qk, pv = _key_ops(k_ref, v_ref, past, bufs, tk)
    q0 = q_off + pl.program_id(1) * tq
    diag = q0 // tk
    r = lax.broadcasted_iota(jnp.int32, (LANES, 2 * LANES), 0)
    c = lax.broadcasted_iota(jnp.int32, (LANES, 2 * LANES), 1)
    suffix_w = jnp.where((c >= LANES) | (r > c), 1.0, 0.0).astype(BF16)

    def head_pair(hp, _):
        lanes = _pair_lanes(hp)
        qp = q_ref[0, :, lanes]
        qm = (_keep_head(qp, 0), _keep_head(qp, 1))
        acc_ref[...] = jnp.zeros(acc_ref.shape, F32)
        tail_ref[...] = jnp.zeros(tail_ref.shape, F32)

        def tile(j, masked):
            k0 = pl.multiple_of(j * tk, tk)
            if masked:
                qpos, kpos = _positions(tq, tk, q0, k0)
                strict = kpos < qpos
            heads = (0, 1)
            z = [qk(qm[hh], hp, k0) for hh in heads]
            ls, la, after = [None, None], [None, None], [None, None]
            for hh in heads:
                ls[hh] = jnp.minimum(z[hh], 0.0) - jnp.log2(1.0 + jnp.exp2(-jnp.abs(z[hh])))
                la[hh] = ls[hh] - z[hh]
                if masked:
                    la[hh] = jnp.where(strict, la[hh], 0.0)
                    ls[hh] = jnp.where(strict, ls[hh], NEG)
                la[hh] = la[hh].astype(BF16)
            for hh in heads:
                run = tail_ref[hh]
                blocks = [None] * (tk // LANES)
                for b in reversed(range(tk // LANES)):
                    res = _dot(la[hh][:, b * LANES:(b + 1) * LANES], suffix_w)
                    blocks[b] = res[:, :LANES] + run
                    run = run + res[:, LANES:]
                tail_ref[hh] = run
                after[hh] = jnp.concatenate(blocks, axis=1)
            a = [jnp.exp2(ls[hh] + after[hh]).astype(BF16) for hh in heads]
            for hh in heads:
                acc_ref[hh] += pv(a[hh], hp, k0, None)

        tile(diag, True)

        def body(i, _):
            tile(diag - 1 - i, False)
            return 0

        lax.fori_loop(0, diag, body, 0)
        o_ref[0, :, lanes] = _merge_pair(acc_ref[0], acc_ref[1]).astype(o_ref.dtype)
        return 0

    lax.fori_loop(0, N_PAIRS, head_pair, 0)


def _sb(q, k, v, past, Lp, *, tq, tk, q_off):
    B, T, _ = q.shape
    qspec, kspec = _pair_specs(tq, k.shape[1])
    past_specs, bufs, past = _past_specs_and_bufs(past, T, Lp)
    return pl.pallas_call(
        functools.partial(_sb_kernel, tq=tq, tk=tk, q_off=q_off, has_past=bool(past)),
        grid=(B, T // tq),
        in_specs=[qspec, kspec, kspec] + past_specs,
        out_specs=qspec,
        out_shape=jax.ShapeDtypeStruct(q.shape, BF16),
        scratch_shapes=[pltpu.VMEM((2, tq, LANES), F32), pltpu.VMEM((2, tq, LANES), F32)] + bufs,
        compiler_params=_cparams(("parallel", "parallel")),
        name="sb_attn",
    )(q, k, v, *past)


def _const_key(v):
    bits = int(np.float32(v).view(np.int32))
    return (bits ^ 0x7FFFFFFF) + 1 if bits < 0 else bits


def _float_key(x):
    bits = lax.bitcast_convert_type(x, jnp.int32)
    return jnp.where(bits < 0, jnp.bitwise_xor(bits, 0x7FFFFFFF) + 1, bits)


KEY_MIN = -2 ** 31
KEY_HALF_NEG = _const_key(0.5 * NEG)


HALF_BITS = 16
HALF_MAX = 2 ** 15 - 1
COUNT_CHAINS = 8


def _dsa_kernel(aq_ref, iq_ref, iwt_ref, k_ref, vt_ref, ik_ref, o_ref, key_ref, hi_ref, lo_ref, q8_ref, m_ref,
                acc_ref, *, tq, tk, q_off, n_rows, topk):
    q0 = q_off + pl.program_id(1) * tq
    n_tiles = (q0 + n_rows + tk - 1) // tk

    def tile_start(j):
        return pl.multiple_of(j * tk, LANES)

    iq = iq_ref[0]
    iwt = iwt_ref[0]
    qh = [_keep_head(iq[:, (h // 2) * LANES:(h // 2 + 1) * LANES], h % 2) for h in range(N_IDX)]
    q_chunk = jnp.right_shift(q0 + lax.broadcasted_iota(jnp.int32, (tk, tq), 1), CHUNK_SHIFT)

    def score_tile(j, _):
        k0 = tile_start(j)
        ik = ik_ref[0, pl.ds(k0, tk), :]
        sc = jnp.zeros((tk, tq), F32)
        for h in range(N_IDX):
            sc = sc + jnp.maximum(_dot_t(ik, qh[h]), 0.0) * iwt[h:h + 1, :]
        key_chunk = jnp.right_shift(k0 + lax.broadcasted_iota(jnp.int32, (tk, tq), 0), CHUNK_SHIFT)
        keys = _float_key(jnp.where(key_chunk <= q_chunk, sc, NEG))
        key_ref[pl.ds(k0, tk), :] = keys
        hi_ref[pl.ds(k0, tk), :] = jnp.right_shift(keys, HALF_BITS).astype(jnp.int16)
        lo_ref[pl.ds(k0, tk), :] = (jnp.bitwise_and(keys, HALF_MAX * 2 + 1) - (HALF_MAX + 1)).astype(jnp.int16)
        return 0

    lax.fori_loop(0, n_tiles, score_tile, 0)

    def search():
        one, zero = jnp.ones((), jnp.int16), jnp.zeros((), jnp.int16)

        def bisect(half_ref, base):
            def step(_, cr):
                lo, hi, above_hi = cr
                mid = jnp.right_shift(lo + hi, 1)
                mid16 = mid.astype(jnp.int16)

                def count_tile(j, cnt):
                    above = jnp.where(half_ref[pl.ds(tile_start(j), tk), :] > mid16, one, zero)
                    parts = above.reshape((-1,) + cnt.shape)
                    for i in range(parts.shape[0]):
                        cnt = cnt + parts[i]
                    return cnt

                cnt = lax.fori_loop(0, n_tiles, count_tile, jnp.zeros((COUNT_CHAINS, 2 * SUBLANES, tq), jnp.int16))
                cnt = base + jnp.sum(jnp.sum(cnt.astype(jnp.int32).astype(F32), axis=0), axis=0, keepdims=True)
                active = lo + 1 < hi
                down = jnp.logical_and(active, cnt < topk)
                up = jnp.logical_and(active, cnt >= topk)
                return jnp.where(up, mid, lo), jnp.where(down, mid, hi), jnp.where(down, cnt, above_hi)

            init = (jnp.full((1, tq), -HALF_MAX - 2, jnp.int32), jnp.full((1, tq), HALF_MAX, jnp.int32), base)
            _, hi, above_hi = lax.fori_loop(0, HALF_BITS, step, init)
            return hi, above_hi

        thr_hi, above = bisect(hi_ref, jnp.zeros((1, tq), F32))
        thr_hi16 = thr_hi.astype(jnp.int16)

        def park(j, _):
            rows = pl.ds(tile_start(j), tk)
            lo_ref[rows, :] = jnp.where(hi_ref[rows, :] == thr_hi16, lo_ref[rows, :], jnp.int16(-HALF_MAX - 1))
            return 0

        lax.fori_loop(0, n_tiles, park, 0)
        thr_lo, above = bisect(lo_ref, above)
        return jnp.left_shift(thr_hi, HALF_BITS) + (thr_lo + (HALF_MAX + 1)), topk - above

    thr, room = lax.cond(q0 + n_rows > topk, search,
                         lambda: (jnp.full((1, tq), KEY_MIN, jnp.int32), jnp.zeros((1, tq), F32)))
    row_ok = lax.broadcasted_iota(jnp.int32, (1, tq), 1) < n_rows
    room = jnp.where(row_ok, room, 0.0)

    @pl.when(jnp.max(room) > 0.0)
    def _():
        r = lax.broadcasted_iota(jnp.int32, (LANES, LANES), 0)
        c = lax.broadcasted_iota(jnp.int32, (LANES, LANES), 1)
        before = jnp.where(c < r, 1.0, 0.0).astype(BF16)

        def tie_tile(j, seen):
            k0 = tile_start(j)
            for b in range(tk // LANES):
                rows = pl.ds(k0 + b * LANES, LANES)
                keys = key_ref[rows, :]
                eq = keys == thr
                eqf = jnp.where(eq, 1.0, 0.0)
                rank = _dot(before, eqf.astype(BF16)) + seen
                key_ref[rows, :] = jnp.where(eq, jnp.where(rank < room, keys + 1, keys), keys)
                seen = seen + jnp.sum(eqf, axis=0, keepdims=True)
            return seen

        lax.fori_loop(0, n_tiles, tie_tile, jnp.zeros((1, tq), F32))

    aq = aq_ref[0]
    for h in range(N_HEADS):
        q8_ref[h * tq:(h + 1) * tq, :] = _keep_head(aq[:, (h // 2) * LANES:(h // 2 + 1) * LANES], h % 2)
    m_ref[...] = jnp.full(m_ref.shape, NEG, F32)
    acc_ref[...] = jnp.zeros(acc_ref.shape, F32)
    thr_sel = jnp.maximum(thr, KEY_HALF_NEG)

    def attend_tile(j, _):
        k0 = tile_start(j)
        kt = k_ref[0, pl.ds(k0, tk), :]
        v1t = vt_ref[0, :, pl.ds(k0, tk)]
        bias = jnp.where(key_ref[pl.ds(k0, tk), :] > thr_sel, 0.0, NEG)
        st = _dot_t(kt, q8_ref[...])
        st = jnp.concatenate([st[:, h * tq:(h + 1) * tq] + bias for h in range(N_HEADS)], axis=1)
        m_old = m_ref[...]
        m_new = jnp.maximum(m_old, jnp.max(st, axis=0, keepdims=True))
        p = jnp.exp2(st - m_new).astype(BF16)
        acc_ref[...] = jnp.exp2(m_old - m_new) * acc_ref[...] + _dot(v1t, p)
        m_ref[...] = m_new
        return 0

    lax.fori_loop(0, n_tiles, attend_tile, 0)
    for hp in range(N_PAIRS):
        pair = []
        for h in (2 * hp, 2 * hp + 1):
            a = acc_ref[:, h * tq:(h + 1) * tq]
            pair.append(a[:HEAD_DIM] / a[HEAD_DIM:])
        o_ref[0, :, hp * LANES:(hp + 1) * LANES] = jnp.concatenate(pair, axis=0).T.astype(o_ref.dtype)


def _dsa(aq, iq, iw, k2, v1t, ik2, *, tq, q_off, topk):
    B, T, _ = aq.shape
    Lp = k2.shape[1]
    tk = Lp if T <= LANES else next(t for t in (512, 384, 256, 128) if Lp % t == 0)
    Tp = -(-T // LANES) * LANES
    tq = max(tq, LANES)
    n_rows = min(T, tq)
    pad_t = lambda a: jnp.pad(a, ((0, 0), (0, Tp - T), (0, 0)))
    qspec = lambda w: pl.BlockSpec((1, tq, w), lambda b, qi: (b, qi, 0))
    kspec = pl.BlockSpec((1, Lp, LANES), lambda b, qi: (b, 0, 0))
    out = pl.pallas_call(
        functools.partial(_dsa_kernel, tq=tq, tk=tk, q_off=q_off, n_rows=n_rows, topk=topk),
        grid=(B, Tp // tq),
        in_specs=[qspec(BRANCH_W), qspec(N_IDX * IDX_DIM), pl.BlockSpec((1, SUBLANES, tq), lambda b, qi: (b, 0, qi)),
                  kspec, pl.BlockSpec((1, LANES, Lp), lambda b, qi: (b, 0, 0)), kspec],
        out_specs=qspec(BRANCH_W),
        out_shape=jax.ShapeDtypeStruct((B, Tp, BRANCH_W), BF16),
        scratch_shapes=[pltpu.VMEM((Lp, tq), jnp.int32), pltpu.VMEM((Lp, tq), jnp.int16),
                        pltpu.VMEM((Lp, tq), jnp.int16), pltpu.VMEM((N_HEADS * tq, LANES), BF16),
                        pltpu.VMEM((1, N_HEADS * tq), F32), pltpu.VMEM((LANES, N_HEADS * tq), F32)],
        compiler_params=_cparams(("parallel", "parallel")),
        name="dsa_attn",
    )(pad_t(aq), pad_t(iq), jnp.swapaxes(pad_t(iw), 1, 2), k2, v1t, ik2)
    return out[:, :T]


def _merge_kernel(x_ref, oa_ref, ob_ref, oc_ref, g_ref, wg_ref, wb_ref, wo_ref, y_ref):
    x = x_ref[0]
    D = x.shape[-1]
    xn = _rms(x, g_ref[...]).astype(BF16)
    merged = jnp.zeros(x.shape, F32)
    for n, o_ref in enumerate((oa_ref, ob_ref, oc_ref)):
        gate = _sigmoid(_dot(xn, wg_ref[:, n * D:(n + 1) * D]))
        merged = merged + gate * _dot(o_ref[0], wb_ref[n])
    y_ref[0] = x + _dot(merged.astype(BF16), wo_ref[...])


def _merge(x, oa, ob, oc, g, wg, wb, wo, tm):
    B, T, D = x.shape
    tspec = lambda w_: pl.BlockSpec((1, tm, w_), lambda b, t: (b, t, 0))
    const = lambda a: pl.BlockSpec(a.shape, lambda b, t: (0,) * a.ndim)
    return pl.pallas_call(
        _merge_kernel,
        grid=(B, T // tm),
        in_specs=[tspec(D), tspec(BRANCH_W), tspec(BRANCH_W), tspec(BRANCH_W), const(g), const(wg), const(wb),
                  const(wo)],
        out_specs=tspec(D),
        out_shape=jax.ShapeDtypeStruct(x.shape, F32),
        compiler_params=_cparams(("parallel", "parallel")),
        name="merge_out",
    )(x, oa, ob, oc, g, wg, wb, wo)


def _ffn_kernel(x_ref, g_ref, past_ref, wu_ref, cw_ref, cb_ref, wd_ref, gf_ref, y_ref, st_ref,
                hbuf_ref, carry_ref, acc_ref, *, tm, final_norm):
    fc = FF_CHUNK
    d_ff = wd_ref.shape[0]
    n_chunks = d_ff // fc

    @pl.when(pl.program_id(1) == 0)
    def _():
        carry_ref[...] = past_ref[0]

    x = x_ref[0]
    xn = _rms(x, g_ref[...]).astype(BF16)
    acc_ref[...] = jnp.zeros(acc_ref.shape, F32)

    for c in range(n_chunks):
        hbuf = hbuf_ref.at[c % 2]
        both = lambda ref: jnp.concatenate([ref[:, c * fc:(c + 1) * fc], ref[:, d_ff + c * fc:d_ff + (c + 1) * fc]],
                                           axis=1)
        h = jnp.concatenate([_dot(xn, wu_ref[:, c * fc:(c + 1) * fc]),
                             _dot(xn, wu_ref[:, d_ff + c * fc:d_ff + (c + 1) * fc])], axis=1)
        hbuf[0:SUBLANES, :] = carry_ref[c]
        hbuf[SUBLANES:SUBLANES + tm, :] = h
        carry_ref[c] = h[tm - SUBLANES:tm, :]
        cw = both(cw_ref)
        hc = (both(cb_ref) + cw[0:1] * hbuf[SUBLANES - 2:SUBLANES - 2 + tm, :]
              + cw[1:2] * hbuf[SUBLANES - 1:SUBLANES - 1 + tm, :] + cw[2:3] * h)
        gate, up = hc[:, :fc], hc[:, fc:]
        act = gate * _sigmoid(gate) * up
        acc_ref[...] += _dot(act.astype(BF16), wd_ref[c * fc:(c + 1) * fc, :])
    st_ref[0] = carry_ref[...]
    y = x + acc_ref[...]
    if final_norm:
        y = _rms(y, gf_ref[...])
    y_ref[0] = y


def _ffn(x, g, past, wu, cw, cb, wd, gf, tm, final_norm):
    B, T, D = x.shape
    nc, fc2 = wd.shape[0] // FF_CHUNK, 2 * FF_CHUNK
    tspec = pl.BlockSpec((1, tm, D), lambda b, t: (b, t, 0))
    const = lambda a: pl.BlockSpec(a.shape, lambda b, t: (0,) * a.ndim, pipeline_mode=pl.Buffered(1))
    sspec = pl.BlockSpec((1, nc, SUBLANES, fc2), lambda b, t: (b, 0, 0, 0))
    return pl.pallas_call(
        functools.partial(_ffn_kernel, tm=tm, final_norm=final_norm),
        grid=(B, T // tm),
        in_specs=[tspec, const(g), sspec, const(wu), const(cw), const(cb), const(wd), const(gf)],
        out_specs=[tspec, sspec],
        out_shape=[jax.ShapeDtypeStruct(x.shape, F32), jax.ShapeDtypeStruct((B, nc, SUBLANES, fc2), F32)],
        scratch_shapes=[pltpu.VMEM((2, tm + SUBLANES, fc2), F32), pltpu.VMEM((nc, SUBLANES, fc2), F32),
                        pltpu.VMEM((tm, D), F32)],
        compiler_params=_cparams(("arbitrary", "arbitrary")),
        name="conv_ffn",
    )(x, g, past, wu, cw, cb, wd, gf)


def _rope_tables(pos):
    half = ROT_DIM // 2
    freq = ROPE_THETA ** (-jnp.arange(half, dtype=F32) / half)
    ang = pos.astype(F32)[:, None] * freq[None, :]
    lane = np.arange(LANES) % HEAD_DIM
    cos = jnp.cos(ang)[:, lane % half]
    sin = jnp.sin(ang)[:, lane % half]
    rc = jnp.where(lane < ROT_DIM, cos, 1.0)
    rs1 = jnp.where(lane < half, -sin, 0.0)
    rs2 = jnp.where((lane >= half) & (lane < ROT_DIM), sin, 0.0)
    return rc, rs1, rs2


IN_SIZES = (BRANCH_W, HEAD_DIM, HEAD_DIM, N_IDX * IDX_DIM, IDX_DIM, N_IDX) + (BRANCH_W,) * 6 + (N_HEADS,)
PACKED_W = ROPE_W + MISC_W + 6 * BRANCH_W


def _pack_w_kernel(w_ref, o_ref, s_ref):
    offs = np.concatenate([[0], np.cumsum(IN_SIZES)]).tolist()
    a_q, a_k, a_v, i_q, i_k, i_w, b_q = offs[:7]
    c_f = offs[12]
    src = lambda c0, width: w_ref[0, :, c0:c0 + width]
    s_ref[:, 0:BRANCH_W] = src(a_q, BRANCH_W)
    s_ref[:, BRANCH_W:BRANCH_W + 256] = src(i_q, 256)
    for dst, c0 in ((768, a_k), (896, i_k), (ROPE_W, a_v)):
        s_ref[:, dst:dst + HEAD_DIM] = src(c0, HEAD_DIM)
        s_ref[:, dst + HEAD_DIM:dst + 2 * HEAD_DIM] = src(c0, HEAD_DIM)
    misc = ROPE_W + LANES
    s_ref[:, misc:misc + LANES] = jnp.zeros((s_ref.shape[0], LANES), F32)
    s_ref[:, misc:misc + N_IDX] = src(i_w, N_IDX)
    s_ref[:, misc + CF_LANE:misc + CF_LANE + N_HEADS] = src(c_f, N_HEADS)
    s_ref[:, ROPE_W + MISC_W:PACKED_W] = src(b_q, 6 * BRANCH_W)
    o_ref[...] = s_ref[...].astype(BF16)


def _pack_w(w_in, l):
    D = w_in.shape[1]
    rows = 256
    return pl.pallas_call(
        _pack_w_kernel,
        grid=(D // rows,),
        in_specs=[pl.BlockSpec((1, rows, w_in.shape[2]), lambda r: (l, r, 0))],
        out_specs=pl.BlockSpec((rows, PACKED_W), lambda r: (r, 0)),
        out_shape=jax.ShapeDtypeStruct((D, PACKED_W), BF16),
        scratch_shapes=[pltpu.VMEM((rows, PACKED_W), F32)],
        compiler_params=_cparams(("parallel",)),
        name="pack_w_in",
    )(w_in)


def _pack_layer(l, norm_mix, w_in, b_f, w_branch, w_gate, w_out, norm_ffn, w_up, conv_w, conv_b, w_down):
    w = _pack_w(w_in, l)
    bf_row = jnp.zeros((1, LANES), F32).at[0, CF_LANE:CF_LANE + N_HEADS].set(b_f[l])

    d_ff = w_down.shape[1]
    fc = FF_CHUNK
    nc = d_ff // fc
    assert nc * fc == d_ff
    halves = lambda a: jnp.concatenate([a[..., :d_ff].reshape(a.shape[:-1] + (nc, fc)),
                                        a[..., d_ff:].reshape(a.shape[:-1] + (nc, fc))], axis=-1)
    wu, cw, cb, wd = w_up[l].astype(BF16), conv_w[l], conv_b[l][None], w_down[l].astype(BF16)
    return dict(g_mix=norm_mix[l][None], w=w, bf_row=bf_row, wg=w_gate[l].astype(BF16),
                wb=w_branch[l].astype(BF16), wo=w_out[l].astype(BF16), g_ffn=norm_ffn[l][None],
                wu=wu, cw=cw, cb=cb, wd=wd, halves=halves, nc=nc, d_ff=d_ff)


def _conv_state_in(state, halves):
    st = jnp.moveaxis(halves(state), 2, 1)
    return jnp.pad(st, ((0, 0), (0, 0), (SUBLANES - (CONV_W - 1), 0), (0, 0)))


def _conv_state_out(st, d_ff):
    st = st[:, :, SUBLANES - (CONV_W - 1):, :]
    B, nc, r, fc2 = st.shape
    fc = fc2 // 2
    gate = jnp.moveaxis(st[..., :fc], 1, 2).reshape(B, r, d_ff)
    up = jnp.moveaxis(st[..., fc:], 1, 2).reshape(B, r, d_ff)
    return jnp.concatenate([gate, up], axis=-1)


def _pad_keys(a, Lp):
    return jnp.pad(a, ((0, 0), (0, Lp - a.shape[1]), (0, 0)))


def _group_layer(x, past, lw, gf, layer, depth, stacked, tiles):
    B, T, D = x.shape
    P = 0 if past is None else past[0].shape[1]
    L = P + T
    Lp = -(-L // KEY_TILE) * KEY_TILE
    topk = max(1, min(TOPK_MAX, L // 4))
    tables = _rope_tables(jnp.arange(P, L))

    stacked, (iw, aq16, iq16, ak16, avt16, ik16, bq16, bk16, bv16, cq16, ck16, cv16) = _project(
        x, lw["g_mix"], lw["w"], lw["bf_row"], tables, tiles["proj"], layer, depth, stacked)
    lf = stacked[-1][layer]

    if past is None:
        keys = (ak16, ik16)
        lf_all = lf
        conv_in = jnp.zeros((B, lw["nc"], SUBLANES, 2 * FF_CHUNK), F32)
        past_b = past_c = None
    else:
        p_ak, p_av, p_ik, p_bk, p_bv, p_ck, p_cv, p_lf, p_conv = past
        dup = lambda a: jnp.concatenate([a, a], axis=-1).astype(BF16)
        keys = tuple(jnp.concatenate([o, n], axis=1) for o, n in zip((dup(p_ak), dup(p_ik)), (ak16, ik16)))
        old_vt = jnp.swapaxes(p_av, 1, 2).astype(BF16)
        avt16 = jnp.concatenate([jnp.concatenate([old_vt, jnp.ones_like(old_vt)], axis=1), avt16], axis=2)
        lf_all = jnp.concatenate([p_lf, lf], axis=1)
        conv_in = _conv_state_in(p_conv, lw["halves"])
        past_b, past_c = (layer, p_bk, p_bv), (layer, p_ck, p_cv)
    ak2, ik2 = (_pad_keys(a, Lp) for a in keys)
    v1t = jnp.pad(avt16, ((0, 0), (0, 0), (0, Lp - L)))

    fk_t = _cumsum_time(jnp.swapaxes(_pad_keys(lf_all, Lp), 1, 2))
    fq = jnp.swapaxes(fk_t[:, :, P:L], 1, 2)

    o_a = _dsa(aq16, iq16, iw, ak2, v1t, ik2, tq=tiles["dsa"], q_off=P, topk=topk)
    o_b = _sb(bq16, bk16, bv16, past_b, Lp, tq=tiles["sb"][0], tk=tiles["sb"][1], q_off=P)
    o_c = _fox(cq16, ck16, cv16, past_c, fq, fk_t, tq=tiles["fox"][0], tk=tiles["fox"][1], q_off=P)

    x = _merge(x, o_a, o_b, o_c, lw["g_mix"], lw["wg"], lw["wb"], lw["wo"], tiles["merge"])
    x, st = _ffn(x, lw["g_ffn"], conv_in, lw["wu"], lw["cw"], lw["cb"], lw["wd"], gf, tiles["ffn"],
                 layer == depth - 1)
    return x, stacked, _conv_state_out(st, lw["d_ff"])


def _tiles(T, P):
    pick = lambda want: min(want, T)
    if T % 512 == 0 and P % 512 == 0:
        sb, fox = (256, 256), (256, 512)
    else:
        assert T <= KEY_TILE
        sb = fox = (T, -(-(P + T) // KEY_TILE) * KEY_TILE)
    return dict(proj=pick(512), dsa=pick(128), sb=sb, fox=fox, merge=pick(512), ffn=pick(512))


def kernel(x_prompt, x_sample, cache_a_k, cache_a_v, cache_a_idx_k, cache_b_k, cache_b_v, cache_c_k, cache_c_v,
           cache_c_logf, state_ffn_conv, norm_mix, w_in, b_f, w_branch, w_gate, w_out, norm_ffn, w_up, conv_w,
           conv_b, w_down, norm_final):
    depth = w_in.shape[0]
    caches = (cache_a_k, cache_a_v, cache_a_idx_k, cache_b_k, cache_b_v, cache_c_k, cache_c_v, cache_c_logf,
              state_ffn_conv)
    layers = [_pack_layer(l, norm_mix, w_in, b_f, w_branch, w_gate, w_out, norm_ffn, w_up, conv_w, conv_b, w_down)
              for l in range(depth)]
    gf = norm_final[None]

    def trunk(x, past):
        B, T, _ = x.shape
        tiles = _tiles(T, 0 if past is None else past[0].shape[2])
        stacked, conv_states = None, []
        if past is not None:
            P = past[0].shape[2]
            wide = lambda c: jnp.transpose(c, (0, 1, 3, 4, 2)).reshape(depth, B, N_PAIRS, LANES, P)
        for l in range(depth):
            lp = None if past is None else tuple(wide(c) if 3 <= i <= 6 else c[l] for i, c in enumerate(past))
            x, stacked, conv_state = _group_layer(x, lp, layers[l], gf, l, depth, stacked, tiles)
            conv_states.append(conv_state)
        ak, av, ik, bk, bv, ck, cv, lf = stacked
        heads = lambda a: a.reshape(depth, B, T, N_HEADS, HEAD_DIM)
        return x, (ak, av, ik, heads(bk), heads(bv), heads(ck), heads(cv), lf, jnp.stack(conv_states, axis=0))

    y_prompt, p_state = trunk(x_prompt, None)
    y_sample, s_state = trunk(x_sample, caches)
    return (y_prompt, y_sample) + p_state + s_state
```

```python
import functools

import jax
import jax.numpy as jnp
import numpy as np
from jax import lax
from jax.experimental import pallas as pl
from jax.experimental.pallas import tpu as pltpu

F32 = jnp.float32
BF16 = jnp.bfloat16

LANES = 128
SUBLANES = 8
HEAD_DIM = 64
N_HEADS = 8
N_PAIRS = N_HEADS // 2
BRANCH_W = N_HEADS * HEAD_DIM
N_BRANCH = 3
N_IDX = 4
IDX_DIM = 64
CHUNK = 64
CHUNK_SHIFT = 6
TOPK_MAX = 256
ROT_DIM = HEAD_DIM // 4
ROPE_THETA = 500000.0
CONV_W = 3
EPS = 1e-6
NEG = -1e30
KEY_TILE = 128
FF_CHUNK = 256
VMEM_LIMIT = 56 * 1024 * 1024

ROPE_W = 1024
MISC_W = 256
CF_LANE = 8
LOG2E = 1.4426950408889634
Q_SCALE = HEAD_DIM ** -0.5 * LOG2E


def _cparams(sem):
    return pltpu.CompilerParams(dimension_semantics=sem, vmem_limit_bytes=VMEM_LIMIT)


def _rms(x, g):
    return x * lax.rsqrt(jnp.mean(x * x, axis=-1, keepdims=True) + EPS) * g


def _sigmoid(z):
    return 1.0 / (1.0 + jnp.exp(-z))


def _log_sigmoid(z):
    return jnp.minimum(z, 0.0) - jnp.log1p(jnp.exp(-jnp.abs(z)))


def _dot_t(a, b):
    return lax.dot_general(a, b, (((1,), (1,)), ((), ())), preferred_element_type=F32)


def _dot(a, b):
    return jnp.dot(a, b, preferred_element_type=F32)


def _keep_head(blk, which):
    lane = lax.broadcasted_iota(jnp.int32, blk.shape, blk.ndim - 1)
    keep = (lane >= HEAD_DIM) if which else (lane < HEAD_DIM)
    return jnp.where(keep, blk, jnp.zeros_like(blk))


def _merge_pair(o0, o1):
    lane = lax.broadcasted_iota(jnp.int32, o0.shape, o0.ndim - 1)
    return jnp.where(lane < HEAD_DIM, o0, o1)


N_CACHE_ROWS = 8


def _proj_kernel(x_ref, g_ref, w_ref, bf_ref, rc_ref, rs1_ref, rs2_ref, *refs):
    (ak_ref, av_ref, ik_ref, bk_ref, bv_ref, ck_ref, cv_ref, lf_ref, iw_ref,
     aq16, iq16, ak16, avt16, ik16, bq16, bk16, bv16, cq16, ck16, cv16) = refs[-(N_CACHE_ROWS + 12):]
    xn = _rms(x_ref[0], g_ref[...]).astype(BF16)
    rc, rs1, rs2 = rc_ref[...], rs1_ref[...], rs2_ref[...]

    def cols(c0, width):
        return _dot(xn, w_ref[:, c0:c0 + width])

    def group(blk, g):
        return blk[:, g * LANES:(g + 1) * LANES]

    def rope(blk):
        half = ROT_DIM // 2
        return blk * rc + pltpu.roll(blk, LANES - half, 1) * rs1 + pltpu.roll(blk, half, 1) * rs2

    wide = cols(0, BRANCH_W)
    for g in range(4):
        aq16[0, :, g * LANES:(g + 1) * LANES] = (rope(group(wide, g)) * Q_SCALE).astype(BF16)
    wide = cols(BRANCH_W, ROPE_W - BRANCH_W)
    for g in range(2):
        iq16[0, :, g * LANES:(g + 1) * LANES] = (rope(group(wide, g)) * IDX_DIM ** -0.5).astype(BF16)
    blk = rope(group(wide, 2))
    ak_ref[0, 0] = blk[:, :HEAD_DIM]
    ak16[0] = blk.astype(BF16)
    blk = rope(group(wide, 3))
    ik_ref[0, 0] = blk[:, :IDX_DIM]
    ik16[0] = blk.astype(BF16)

    wide = cols(ROPE_W, MISC_W)
    blk = group(wide, 0)
    av_ref[0, 0] = blk[:, :HEAD_DIM]
    row = lax.broadcasted_iota(jnp.int32, (LANES, blk.shape[0]), 0)
    avt16[0] = jnp.where(row < HEAD_DIM, blk.T, 1.0).astype(BF16)
    blk = group(wide, 1)
    iw_ref[0] = blk[:, :SUBLANES] * N_IDX ** -0.5
    lf_ref[0, 0] = _log_sigmoid(blk[:, CF_LANE:CF_LANE + N_HEADS] + bf_ref[:, CF_LANE:CF_LANE + N_HEADS])

    base = ROPE_W + MISC_W
    outs = ((None, bq16), (bk_ref, bk16), (bv_ref, bv16), (None, cq16), (ck_ref, ck16), (cv_ref, cv16))
    for n, (o32, o16) in enumerate(outs):
        wide = cols(base + n * BRANCH_W, BRANCH_W)
        if o32 is None:
            o16[0] = (wide * Q_SCALE).astype(BF16)
        else:
            o32[0, 0] = wide
            o16[0] = wide.astype(BF16)


def _project(x, g, w, bf_row, tables, tm, layer, depth, stacked):
    B, T, D = x.shape
    tok = lambda w_, dt: jax.ShapeDtypeStruct((B, T, w_), dt)
    tspec = lambda w_: pl.BlockSpec((1, tm, w_), lambda b, t: (b, t, 0))
    const = lambda a: pl.BlockSpec(a.shape, lambda b, t: (0,) * a.ndim, pipeline_mode=pl.Buffered(1))
    cache_w = (64, 64, 64, 512, 512, 512, 512, 8)
    cache_shapes = [jax.ShapeDtypeStruct((depth, B, T, w_), F32) for w_ in cache_w]
    cache_specs = [pl.BlockSpec((1, 1, tm, w_), lambda b, t: (layer, b, t, 0)) for w_ in cache_w]
    work = [(tok(w_, dt), tspec(w_)) for w_, dt in ((8, F32), (512, BF16), (256, BF16), (128, BF16))]
    work += [(jax.ShapeDtypeStruct((B, LANES, T), BF16), pl.BlockSpec((1, LANES, tm), lambda b, t: (b, 0, t)))]
    work += [(tok(w_, BF16), tspec(w_)) for w_ in (128,) + (512,) * 6]
    work_shapes, work_specs = [s for s, _ in work], [p for _, p in work]
    rspec = pl.BlockSpec((tm, LANES), lambda b, t: (t, 0))
    prev = () if stacked is None else tuple(stacked)
    n_in = 7
    outs = pl.pallas_call(
        _proj_kernel,
        grid=(B, T // tm),
        in_specs=[tspec(D), const(g), const(w), const(bf_row), rspec, rspec, rspec]
        + [pl.BlockSpec(memory_space=pl.ANY)] * len(prev),
        out_specs=cache_specs + work_specs,
        out_shape=cache_shapes + work_shapes,
        input_output_aliases={n_in + i: i for i in range(len(prev))},
        compiler_params=_cparams(("parallel", "parallel")),
        name="in_proj",
    )(x, g, w, bf_row, *tables, *prev)
    return outs[:N_CACHE_ROWS], outs[N_CACHE_ROWS:]


def _split3(x):
    hi = x.astype(BF16)
    r = x - hi.astype(F32)
    mid = r.astype(BF16)
    lo = (r - mid.astype(F32)).astype(BF16)
    return hi, mid, lo


def _cumsum_kernel(lf_ref, o_ref, *, blk):
    L = lf_ref.shape[-1]
    r = lax.broadcasted_iota(jnp.int32, (blk, blk), 0)
    c = lax.broadcasted_iota(jnp.int32, (blk, blk), 1)
    upper = jnp.where(r <= c, 1.0, 0.0).astype(BF16)
    carry = jnp.zeros((N_HEADS, 1), F32)
    for j in range(L // blk):
        hi, mid, lo = _split3(lf_ref[0, :, j * blk:(j + 1) * blk])
        f = _dot(hi, upper) + _dot(mid, upper) + _dot(lo, upper) + carry
        o_ref[0, :, j * blk:(j + 1) * blk] = f * LOG2E
        carry = f[:, blk - 1:blk]


def _cumsum_time(lf_t):
    B, H, L = lf_t.shape
    blk = LANES
    assert L % blk == 0
    spec = pl.BlockSpec((1, H, L), lambda b: (b, 0, 0))
    return pl.pallas_call(
        functools.partial(_cumsum_kernel, blk=blk),
        grid=(B,),
        in_specs=[spec],
        out_specs=spec,
        out_shape=jax.ShapeDtypeStruct((B, H, L), F32),
        compiler_params=_cparams(("parallel",)),
        name="logf_cumsum",
    )(lf_t)


def _head_column(blk, h):
    lane = lax.broadcasted_iota(jnp.int32, blk.shape, 1)
    return jnp.sum(jnp.where(lane == h, blk, 0.0), axis=1, keepdims=True)


def _positions(tq, tk, q0, k0):
    qpos = q0 + lax.broadcasted_iota(jnp.int32, (tq, tk), 0)
    kpos = k0 + lax.broadcasted_iota(jnp.int32, (tq, tk), 1)
    return qpos, kpos


def _pair_specs(tq, Lp):
    qspec = pl.BlockSpec((1, tq, BRANCH_W), lambda b, qi: (b, qi, 0))
    kspec = pl.BlockSpec((1, Lp, BRANCH_W), lambda b, qi: (b, 0, 0))
    return qspec, kspec


def _pair_lanes(hp):
    return pl.ds(pl.multiple_of(hp * LANES, LANES), LANES)


def _split_past(rest, has_past):
    return (rest[:2], rest[2:-2], rest[-2:]) if has_past else ((), rest, ())


def _key_ops(k_ref, v_ref, past, bufs, tk):
    if not past:
        def qk(qmat, hp, k0):
            return _dot_t(qmat, k_ref[0, pl.ds(k0, tk), _pair_lanes(hp)])

        def pv(p, hp, k0, ones):
            v = v_ref[0, pl.ds(k0, tk), _pair_lanes(hp)]
            return _dot(p, v if ones is None else jnp.concatenate([v, ones], axis=1))

        return qk, pv

    P, T = past[0].shape[-1], k_ref.shape[1]
    for new, buf in zip((k_ref, v_ref), bufs):
        buf[0:T, :] = new[0]
        if buf.shape[0] > T:
            buf[T:, :] = jnp.zeros((buf.shape[0] - T, BRANCH_W), BF16)

    def qk(qmat, hp, k0):
        return jnp.concatenate([_dot(qmat, past[0][0, hp].astype(BF16)), _dot_t(qmat, bufs[0][:, _pair_lanes(hp)])],
                               axis=1)

    def pv(p, hp, k0, ones):
        vt, v_tail = past[1][0, hp].astype(BF16), bufs[1][:, _pair_lanes(hp)]
        if ones is not None:
            vt = jnp.concatenate([vt, jnp.ones(vt.shape, BF16)], axis=0)
            v_tail = jnp.concatenate([v_tail, ones[:v_tail.shape[0]]], axis=1)
        return _dot_t(p[:, :P], vt) + _dot(p[:, P:], v_tail)

    return qk, pv


def _past_specs_and_bufs(past, T, Lp):
    if past is None:
        return [], [], ()
    layer, pk, pv = past
    P = pk.shape[-1]
    assert P % LANES == 0 and pk.shape[2:4] == (N_PAIRS, LANES)
    spec = pl.BlockSpec((None, 1, N_PAIRS, LANES, P), lambda b, qi: (layer, b, 0, 0, 0))
    return [spec, spec], [pltpu.VMEM((Lp - P, BRANCH_W), BF16)] * 2, (pk, pv)


def _fox_kernel(q_ref, k_ref, v_ref, *rest, tq, tk, q_off, has_past):
    past, (fq_ref, fk_ref, o_ref, m_ref, acc_ref), bufs = _split_past(rest, has_past)
    qk, pv = _key_ops(k_ref, v_ref, past, bufs, tk)
    q0 = q_off + pl.program_id(1) * tq
    diag = q0 // tk
    fq_blk = fq_ref[0]
    ones = jnp.ones((tk, LANES), BF16)

    def head_pair(hp, _):
        lanes = _pair_lanes(hp)
        qp = q_ref[0, :, lanes]
        qm = (_keep_head(qp, 0), _keep_head(qp, 1))
        fq = (_head_column(fq_blk, hp * 2), _head_column(fq_blk, hp * 2 + 1))
        m_ref[...] = jnp.full(m_ref.shape, NEG, F32)
        acc_ref[...] = jnp.zeros(acc_ref.shape, F32)

        def score(j, masked):
            k0 = pl.multiple_of(j * tk, tk)
            if masked:
                qpos, kpos = _positions(tq, tk, q0, k0)
                causal = kpos <= qpos
            out = []
            for hh in range(2):
                fk = fk_ref[0, hp * 2 + hh, :, pl.ds(k0, tk)]
                s = qk(qm[hh], hp, k0) + fq[hh] - fk
                out.append(jnp.where(causal, s, NEG) if masked else s)
            return tuple(out)

        def fold(j, scores):
            k0 = pl.multiple_of(j * tk, tk)
            m_old = [m_ref[hh] for hh in range(2)]
            m_new = [jnp.maximum(m_old[hh], jnp.max(scores[hh], axis=1, keepdims=True)) for hh in range(2)]
            p = [jnp.exp2(scores[hh] - m_new[hh]).astype(BF16) for hh in range(2)]
            for hh in range(2):
                acc_ref[hh] = jnp.exp2(m_old[hh] - m_new[hh]) * acc_ref[hh] + pv(p[hh], hp, k0, ones)
                m_ref[hh] = m_new[hh]

        fold(diag, score(diag, True))

        def body(i, _):
            j = diag - 1 - i
            fold(j, score(j, False))
            return 0

        lax.fori_loop(0, diag, body, 0)
        o0 = acc_ref[0, :, :LANES] / acc_ref[0, :, LANES:]
        o1 = acc_ref[1, :, :LANES] / acc_ref[1, :, LANES:]
        o_ref[0, :, lanes] = _merge_pair(o0, o1).astype(o_ref.dtype)
        return 0

    lax.fori_loop(0, N_PAIRS, head_pair, 0)


def _fox(q, k, v, past, fq, fk_t, *, tq, tk, q_off):
    B, T, _ = q.shape
    Lp = fk_t.shape[-1]
    qspec, kspec = _pair_specs(tq, k.shape[1])
    past_specs, bufs, past = _past_specs_and_bufs(past, T, Lp)
    return pl.pallas_call(
        functools.partial(_fox_kernel, tq=tq, tk=tk, q_off=q_off, has_past=bool(past)),
        grid=(B, T // tq),
        in_specs=[qspec, kspec, kspec] + past_specs
        + [pl.BlockSpec((1, tq, N_HEADS), lambda b, qi: (b, qi, 0)),
           pl.BlockSpec((1, N_HEADS, 1, Lp), lambda b, qi: (b, 0, 0, 0))],
        out_specs=qspec,
        out_shape=jax.ShapeDtypeStruct(q.shape, BF16),
        scratch_shapes=[pltpu.VMEM((2, tq, 1), F32), pltpu.VMEM((2, tq, 2 * LANES), F32)] + bufs,
        compiler_params=_cparams(("parallel", "parallel")),
        name="fox_attn",
    )(q, k, v, *past, fq, fk_t[:, :, None, :])


def _sb_kernel(q_ref, k_ref, v_ref, *rest, tq, tk, q_off, has_past):
    past, (o_ref, acc_ref, tail_ref), bufs = _split_past(rest, has_past)
    qk, pv = _key_ops(k_ref, v_ref, past, bufs, tk)
    q0 = q_off + pl.program_id(1) * tq
    diag = q0 // tk
    r = lax.broadcasted_iota(jnp.int32, (LANES, 2 * LANES), 0)
    c = lax.broadcasted_iota(jnp.int32, (LANES, 2 * LANES), 1)
    suffix_w = jnp.where((c >= LANES) | (r > c), 1.0, 0.0).astype(BF16)

    def head_pair(hp, _):
        lanes = _pair_lanes(hp)
        qp = q_ref[0, :, lanes]
        qm = (_keep_head(qp, 0), _keep_head(qp, 1))
        acc_ref[...] = jnp.zeros(acc_ref.shape, F32)
        tail_ref[...] = jnp.zeros(tail_ref.shape, F32)

        def tile(j, masked):
            k0 = pl.multiple_of(j * tk, tk)
            if masked:
                qpos, kpos = _positions(tq, tk, q0, k0)
                strict = kpos < qpos
            heads = (0, 1)
            z = [qk(qm[hh], hp, k0) for hh in heads]
            ls, la, after = [None, None], [None, None], [None, None]
            for hh in heads:
                ls[hh] = jnp.minimum(z[hh], 0.0) - jnp.log2(1.0 + jnp.exp2(-jnp.abs(z[hh])))
                la[hh] = ls[hh] - z[hh]
                if masked:
                    la[hh] = jnp.where(strict, la[hh], 0.0)
                    ls[hh] = jnp.where(strict, ls[hh], NEG)
                la[hh] = la[hh].astype(BF16)
            for hh in heads:
                run = tail_ref[hh]
                blocks = [None] * (tk // LANES)
                for b in reversed(range(tk // LANES)):
                    res = _dot(la[hh][:, b * LANES:(b + 1) * LANES], suffix_w)
                    blocks[b] = res[:, :LANES] + run
                    run = run + res[:, LANES:]
                tail_ref[hh] = run
                after[hh] = jnp.concatenate(blocks, axis=1)
            a = [jnp.exp2(ls[hh] + after[hh]).astype(BF16) for hh in heads]
            for hh in heads:
                acc_ref[hh] += pv(a[hh], hp, k0, None)

        tile(diag, True)

        def body(i, _):
            tile(diag - 1 - i, False)
            return 0

        lax.fori_loop(0, diag, body, 0)
        o_ref[0, :, lanes] = _merge_pair(acc_ref[0], acc_ref[1]).astype(o_ref.dtype)
        return 0

    lax.fori_loop(0, N_PAIRS, head_pair, 0)


def _sb(q, k, v, past, Lp, *, tq, tk, q_off):
    B, T, _ = q.shape
    qspec, kspec = _pair_specs(tq, k.shape[1])
    past_specs, bufs, past = _past_specs_and_bufs(past, T, Lp)
    return pl.pallas_call(
        functools.partial(_sb_kernel, tq=tq, tk=tk, q_off=q_off, has_past=bool(past)),
        grid=(B, T // tq),
        in_specs=[qspec, kspec, kspec] + past_specs,
        out_specs=qspec,
        out_shape=jax.ShapeDtypeStruct(q.shape, BF16),
        scratch_shapes=[pltpu.VMEM((2, tq, LANES), F32), pltpu.VMEM((2, tq, LANES), F32)] + bufs,
        compiler_params=_cparams(("parallel", "parallel")),
        name="sb_attn",
    )(q, k, v, *past)


def _const_key(v):
    bits = int(np.float32(v).view(np.int32))
    return (bits ^ 0x7FFFFFFF) + 1 if bits < 0 else bits


def _float_key(x):
    bits = lax.bitcast_convert_type(x, jnp.int32)
    return jnp.where(bits < 0, jnp.bitwise_xor(bits, 0x7FFFFFFF) + 1, bits)


KEY_MIN = -2 ** 31
KEY_HALF_NEG = _const_key(0.5 * NEG)


KEY_MAX = 2 ** 31 - 1
KEY_BITS = 32
COUNT_CHAINS = 8


def _dsa_kernel(aq_ref, iq_ref, iwt_ref, k_ref, vt_ref, ik_ref, o_ref, key_ref, q8_ref, m_ref, acc_ref,
                *, tq, tk, q_off, n_rows, topk):
    q0 = q_off + pl.program_id(1) * tq
    n_tiles = (q0 + n_rows + tk - 1) // tk

    def tile_start(j):
        return pl.multiple_of(j * tk, LANES)

    iq = iq_ref[0]
    iwt = iwt_ref[0]
    qh = [_keep_head(iq[:, (h // 2) * LANES:(h // 2 + 1) * LANES], h % 2) for h in range(N_IDX)]
    q_chunk = jnp.right_shift(q0 + lax.broadcasted_iota(jnp.int32, (tk, tq), 1), CHUNK_SHIFT)

    def score_tile(j, _):
        k0 = tile_start(j)
        ik = ik_ref[0, pl.ds(k0, tk), :]
        sc = jnp.zeros((tk, tq), F32)
        for h in range(N_IDX):
            sc = sc + jnp.maximum(_dot_t(ik, qh[h]), 0.0) * iwt[h:h + 1, :]
        key_chunk = jnp.right_shift(k0 + lax.broadcasted_iota(jnp.int32, (tk, tq), 0), CHUNK_SHIFT)
        key_ref[pl.ds(k0, tk), :] = _float_key(jnp.where(key_chunk <= q_chunk, sc, NEG))
        return 0

    lax.fori_loop(0, n_tiles, score_tile, 0)

    def search():
        def step(_, cr):
            lo, hi, above_hi = cr
            mid = jnp.right_shift(lo, 1) + jnp.right_shift(hi, 1) + jnp.bitwise_and(jnp.bitwise_and(lo, hi), 1)

            def count_tile(j, cnt):
                above = jnp.where(key_ref[pl.ds(tile_start(j), tk), :] > mid, 1.0, 0.0)
                return cnt + jnp.sum(above.reshape((-1,) + cnt.shape), axis=0)

            cnt = lax.fori_loop(0, n_tiles, count_tile, jnp.zeros((COUNT_CHAINS, SUBLANES, tq), F32))
            cnt = jnp.sum(jnp.sum(cnt, axis=0), axis=0, keepdims=True)
            less = cnt < topk
            return jnp.where(less, lo, mid), jnp.where(less, mid, hi), jnp.where(less, cnt, above_hi)

        init = (jnp.full((1, tq), KEY_MIN, jnp.int32), jnp.full((1, tq), KEY_MAX, jnp.int32), jnp.zeros((1, tq), F32))
        _, hi, above_hi = lax.fori_loop(0, KEY_BITS, step, init)
        return hi, topk - above_hi

    thr, room = lax.cond(q0 + n_rows > topk, search,
                         lambda: (jnp.full((1, tq), KEY_MIN, jnp.int32), jnp.zeros((1, tq), F32)))
    row_ok = lax.broadcasted_iota(jnp.int32, (1, tq), 1) < n_rows
    room = jnp.where(row_ok, room, 0.0)

    @pl.when(jnp.max(room) > 0.0)
    def _():
        r = lax.broadcasted_iota(jnp.int32, (LANES, LANES), 0)
        c = lax.broadcasted_iota(jnp.int32, (LANES, LANES), 1)
        before = jnp.where(c < r, 1.0, 0.0).astype(BF16)

        def tie_tile(j, seen):
            k0 = tile_start(j)
            for b in range(tk // LANES):
                rows = pl.ds(k0 + b * LANES, LANES)
                keys = key_ref[rows, :]
                eq = keys == thr
                eqf = jnp.where(eq, 1.0, 0.0)
                rank = _dot(before, eqf.astype(BF16)) + seen
                key_ref[rows, :] = jnp.where(eq, jnp.where(rank < room, keys + 1, keys), keys)
                seen = seen + jnp.sum(eqf, axis=0, keepdims=True)
            return seen

        lax.fori_loop(0, n_tiles, tie_tile, jnp.zeros((1, tq), F32))

    aq = aq_ref[0]
    for h in range(N_HEADS):
        q8_ref[h * tq:(h + 1) * tq, :] = _keep_head(aq[:, (h // 2) * LANES:(h // 2 + 1) * LANES], h % 2)
    m_ref[...] = jnp.full(m_ref.shape, NEG, F32)
    acc_ref[...] = jnp.zeros(acc_ref.shape, F32)
    thr_sel = jnp.maximum(thr, KEY_HALF_NEG)

    def attend_tile(j, _):
        k0 = tile_start(j)
        kt = k_ref[0, pl.ds(k0, tk), :]
        v1t = vt_ref[0, :, pl.ds(k0, tk)]
        bias = jnp.where(key_ref[pl.ds(k0, tk), :] > thr_sel, 0.0, NEG)
        st = _dot_t(kt, q8_ref[...])
        st = jnp.concatenate([st[:, h * tq:(h + 1) * tq] + bias for h in range(N_HEADS)], axis=1)
        m_old = m_ref[...]
        m_new = jnp.maximum(m_old, jnp.max(st, axis=0, keepdims=True))
        p = jnp.exp2(st - m_new).astype(BF16)
        acc_ref[...] = jnp.exp2(m_old - m_new) * acc_ref[...] + _dot(v1t, p)
        m_ref[...] = m_new
        return 0

    lax.fori_loop(0, n_tiles, attend_tile, 0)
    for hp in range(N_PAIRS):
        pair = []
        for h in (2 * hp, 2 * hp + 1):
            a = acc_ref[:, h * tq:(h + 1) * tq]
            pair.append(a[:HEAD_DIM] / a[HEAD_DIM:])
        o_ref[0, :, hp * LANES:(hp + 1) * LANES] = jnp.concatenate(pair, axis=0).T.astype(o_ref.dtype)


def _dsa(aq, iq, iw, k2, v1t, ik2, *, tq, q_off, topk):
    B, T, _ = aq.shape
    Lp = k2.shape[1]
    tk = Lp if T <= LANES else next(t for t in (512, 384, 256, 128) if Lp % t == 0)
    Tp = -(-T // LANES) * LANES
    tq = max(tq, LANES)
    n_rows = min(T, tq)
    pad_t = lambda a: jnp.pad(a, ((0, 0), (0, Tp - T), (0, 0)))
    qspec = lambda w: pl.BlockSpec((1, tq, w), lambda b, qi: (b, qi, 0))
    kspec = pl.BlockSpec((1, Lp, LANES), lambda b, qi: (b, 0, 0))
    out = pl.pallas_call(
        functools.partial(_dsa_kernel, tq=tq, tk=tk, q_off=q_off, n_rows=n_rows, topk=topk),
        grid=(B, Tp // tq),
        in_specs=[qspec(BRANCH_W), qspec(N_IDX * IDX_DIM), pl.BlockSpec((1, SUBLANES, tq), lambda b, qi: (b, 0, qi)),
                  kspec, pl.BlockSpec((1, LANES, Lp), lambda b, qi: (b, 0, 0)), kspec],
        out_specs=qspec(BRANCH_W),
        out_shape=jax.ShapeDtypeStruct((B, Tp, BRANCH_W), BF16),
        scratch_shapes=[pltpu.VMEM((Lp, tq), jnp.int32), pltpu.VMEM((N_HEADS * tq, LANES), BF16),
                        pltpu.VMEM((1, N_HEADS * tq), F32), pltpu.VMEM((LANES, N_HEADS * tq), F32)],
        compiler_params=_cparams(("parallel", "parallel")),
        name="dsa_attn",
    )(pad_t(aq), pad_t(iq), jnp.swapaxes(pad_t(iw), 1, 2), k2, v1t, ik2)
    return out[:, :T]


def _merge_kernel(x_ref, oa_ref, ob_ref, oc_ref, g_ref, wg_ref, wb_ref, wo_ref, y_ref):
    x = x_ref[0]
    D = x.shape[-1]
    xn = _rms(x, g_ref[...]).astype(BF16)
    merged = jnp.zeros(x.shape, F32)
    for n, o_ref in enumerate((oa_ref, ob_ref, oc_ref)):
        gate = _sigmoid(_dot(xn, wg_ref[:, n * D:(n + 1) * D]))
        merged = merged + gate * _dot(o_ref[0], wb_ref[n])
    y_ref[0] = x + _dot(merged.astype(BF16), wo_ref[...])


def _merge(x, oa, ob, oc, g, wg, wb, wo, tm):
    B, T, D = x.shape
    tspec = lambda w_: pl.BlockSpec((1, tm, w_), lambda b, t: (b, t, 0))
    const = lambda a: pl.BlockSpec(a.shape, lambda b, t: (0,) * a.ndim)
    return pl.pallas_call(
        _merge_kernel,
        grid=(B, T // tm),
        in_specs=[tspec(D), tspec(BRANCH_W), tspec(BRANCH_W), tspec(BRANCH_W), const(g), const(wg), const(wb),
                  const(wo)],
        out_specs=tspec(D),
        out_shape=jax.ShapeDtypeStruct(x.shape, F32),
        compiler_params=_cparams(("parallel", "parallel")),
        name="merge_out",
    )(x, oa, ob, oc, g, wg, wb, wo)


def _ffn_kernel(x_ref, g_ref, past_ref, wu_ref, cw_ref, cb_ref, wd_ref, gf_ref, y_ref, st_ref,
                hbuf_ref, carry_ref, acc_ref, *, tm, final_norm):
    fc = FF_CHUNK
    d_ff = wd_ref.shape[0]
    n_chunks = d_ff // fc

    @pl.when(pl.program_id(1) == 0)
    def _():
        carry_ref[...] = past_ref[0]

    x = x_ref[0]
    xn = _rms(x, g_ref[...]).astype(BF16)
    acc_ref[...] = jnp.zeros(acc_ref.shape, F32)

    def up_proj(c):
        return jnp.concatenate([_dot(xn, wu_ref[:, c * fc:(c + 1) * fc]),
                                _dot(xn, wu_ref[:, d_ff + c * fc:d_ff + (c + 1) * fc])], axis=1)

    h_next = up_proj(0)
    for c in range(n_chunks):
        hbuf = hbuf_ref.at[c % 2]
        both = lambda ref: jnp.concatenate([ref[:, c * fc:(c + 1) * fc], ref[:, d_ff + c * fc:d_ff + (c + 1) * fc]],
                                           axis=1)
        h = h_next
        if c + 1 < n_chunks:
            h_next = up_proj(c + 1)
        hbuf[0:SUBLANES, :] = carry_ref[c]
        hbuf[SUBLANES:SUBLANES + tm, :] = h
        carry_ref[c] = h[tm - SUBLANES:tm, :]
        cw = both(cw_ref)
        hc = (both(cb_ref) + cw[0:1] * hbuf[SUBLANES - 2:SUBLANES - 2 + tm, :]
              + cw[1:2] * hbuf[SUBLANES - 1:SUBLANES - 1 + tm, :] + cw[2:3] * h)
        gate, up = hc[:, :fc], hc[:, fc:]
        act = gate * _sigmoid(gate) * up
        acc_ref[...] += _dot(act.astype(BF16), wd_ref[c * fc:(c + 1) * fc, :])
    st_ref[0] = carry_ref[...]
    y = x + acc_ref[...]
    if final_norm:
        y = _rms(y, gf_ref[...])
    y_ref[0] = y


def _ffn(x, g, past, wu, cw, cb, wd, gf, tm, final_norm):
    B, T, D = x.shape
    nc, fc2 = wd.shape[0] // FF_CHUNK, 2 * FF_CHUNK
    tspec = pl.BlockSpec((1, tm, D), lambda b, t: (b, t, 0))
    const = lambda a: pl.BlockSpec(a.shape, lambda b, t: (0,) * a.ndim, pipeline_mode=pl.Buffered(1))
    sspec = pl.BlockSpec((1, nc, SUBLANES, fc2), lambda b, t: (b, 0, 0, 0))
    return pl.pallas_call(
        functools.partial(_ffn_kernel, tm=tm, final_norm=final_norm),
        grid=(B, T // tm),
        in_specs=[tspec, const(g), sspec, const(wu), const(cw), const(cb), const(wd), const(gf)],
        out_specs=[tspec, sspec],
        out_shape=[jax.ShapeDtypeStruct(x.shape, F32), jax.ShapeDtypeStruct((B, nc, SUBLANES, fc2), F32)],
        scratch_shapes=[pltpu.VMEM((2, tm + SUBLANES, fc2), F32), pltpu.VMEM((nc, SUBLANES, fc2), F32),
                        pltpu.VMEM((tm, D), F32)],
        compiler_params=_cparams(("arbitrary", "arbitrary")),
        name="conv_ffn",
    )(x, g, past, wu, cw, cb, wd, gf)


def _rope_tables(pos):
    half = ROT_DIM // 2
    freq = ROPE_THETA ** (-jnp.arange(half, dtype=F32) / half)
    ang = pos.astype(F32)[:, None] * freq[None, :]
    lane = np.arange(LANES) % HEAD_DIM
    cos = jnp.cos(ang)[:, lane % half]
    sin = jnp.sin(ang)[:, lane % half]
    rc = jnp.where(lane < ROT_DIM, cos, 1.0)
    rs1 = jnp.where(lane < half, -sin, 0.0)
    rs2 = jnp.where((lane >= half) & (lane < ROT_DIM), sin, 0.0)
    return rc, rs1, rs2


IN_SIZES = (BRANCH_W, HEAD_DIM, HEAD_DIM, N_IDX * IDX_DIM, IDX_DIM, N_IDX) + (BRANCH_W,) * 6 + (N_HEADS,)
PACKED_W = ROPE_W + MISC_W + 6 * BRANCH_W


def _pack_w_kernel(w_ref, o_ref, s_ref):
    offs = np.concatenate([[0], np.cumsum(IN_SIZES)]).tolist()
    a_q, a_k, a_v, i_q, i_k, i_w, b_q = offs[:7]
    c_f = offs[12]
    src = lambda c0, width: w_ref[0, :, c0:c0 + width]
    s_ref[:, 0:BRANCH_W] = src(a_q, BRANCH_W)
    s_ref[:, BRANCH_W:BRANCH_W + 256] = src(i_q, 256)
    for dst, c0 in ((768, a_k), (896, i_k), (ROPE_W, a_v)):
        s_ref[:, dst:dst + HEAD_DIM] = src(c0, HEAD_DIM)
        s_ref[:, dst + HEAD_DIM:dst + 2 * HEAD_DIM] = src(c0, HEAD_DIM)
    misc = ROPE_W + LANES
    s_ref[:, misc:misc + LANES] = jnp.zeros((s_ref.shape[0], LANES), F32)
    s_ref[:, misc:misc + N_IDX] = src(i_w, N_IDX)
    s_ref[:, misc + CF_LANE:misc + CF_LANE + N_HEADS] = src(c_f, N_HEADS)
    s_ref[:, ROPE_W + MISC_W:PACKED_W] = src(b_q, 6 * BRANCH_W)
    o_ref[...] = s_ref[...].astype(BF16)


def _pack_w(w_in, l):
    D = w_in.shape[1]
    rows = 256
    return pl.pallas_call(
        _pack_w_kernel,
        grid=(D // rows,),
        in_specs=[pl.BlockSpec((1, rows, w_in.shape[2]), lambda r: (l, r, 0))],
        out_specs=pl.BlockSpec((rows, PACKED_W), lambda r: (r, 0)),
        out_shape=jax.ShapeDtypeStruct((D, PACKED_W), BF16),
        scratch_shapes=[pltpu.VMEM((rows, PACKED_W), F32)],
        compiler_params=_cparams(("parallel",)),
        name="pack_w_in",
    )(w_in)


def _pack_layer(l, norm_mix, w_in, b_f, w_branch, w_gate, w_out, norm_ffn, w_up, conv_w, conv_b, w_down):
    w = _pack_w(w_in, l)
    bf_row = jnp.zeros((1, LANES), F32).at[0, CF_LANE:CF_LANE + N_HEADS].set(b_f[l])

    d_ff = w_down.shape[1]
    fc = FF_CHUNK
    nc = d_ff // fc
    assert nc * fc == d_ff
    halves = lambda a: jnp.concatenate([a[..., :d_ff].reshape(a.shape[:-1] + (nc, fc)),
                                        a[..., d_ff:].reshape(a.shape[:-1] + (nc, fc))], axis=-1)
    wu, cw, cb, wd = w_up[l].astype(BF16), conv_w[l], conv_b[l][None], w_down[l].astype(BF16)
    return dict(g_mix=norm_mix[l][None], w=w, bf_row=bf_row, wg=w_gate[l].astype(BF16),
                wb=w_branch[l].astype(BF16), wo=w_out[l].astype(BF16), g_ffn=norm_ffn[l][None],
                wu=wu, cw=cw, cb=cb, wd=wd, halves=halves, nc=nc, d_ff=d_ff)


def _conv_state_in(state, halves):
    st = jnp.moveaxis(halves(state), 2, 1)
    return jnp.pad(st, ((0, 0), (0, 0), (SUBLANES - (CONV_W - 1), 0), (0, 0)))


def _conv_state_out(st, d_ff):
    st = st[:, :, SUBLANES - (CONV_W - 1):, :]
    B, nc, r, fc2 = st.shape
    fc = fc2 // 2
    gate = jnp.moveaxis(st[..., :fc], 1, 2).reshape(B, r, d_ff)
    up = jnp.moveaxis(st[..., fc:], 1, 2).reshape(B, r, d_ff)
    return jnp.concatenate([gate, up], axis=-1)


def _pad_keys(a, Lp):
    return jnp.pad(a, ((0, 0), (0, Lp - a.shape[1]), (0, 0)))


def _group_layer(x, past, lw, gf, layer, depth, stacked, tiles):
    B, T, D = x.shape
    P = 0 if past is None else past[0].shape[1]
    L = P + T
    Lp = -(-L // KEY_TILE) * KEY_TILE
    topk = max(1, min(TOPK_MAX, L // 4))
    tables = _rope_tables(jnp.arange(P, L))

    stacked, (iw, aq16, iq16, ak16, avt16, ik16, bq16, bk16, bv16, cq16, ck16, cv16) = _project(
        x, lw["g_mix"], lw["w"], lw["bf_row"], tables, tiles["proj"], layer, depth, stacked)
    lf = stacked[-1][layer]

    if past is None:
        keys = (ak16, ik16)
        lf_all = lf
        conv_in = jnp.zeros((B, lw["nc"], SUBLANES, 2 * FF_CHUNK), F32)
        past_b = past_c = None
    else:
        p_ak, p_av, p_ik, p_bk, p_bv, p_ck, p_cv, p_lf, p_conv = past
        dup = lambda a: jnp.concatenate([a, a], axis=-1).astype(BF16)
        keys = tuple(jnp.concatenate([o, n], axis=1) for o, n in zip((dup(p_ak), dup(p_ik)), (ak16, ik16)))
        old_vt = jnp.swapaxes(p_av, 1, 2).astype(BF16)
        avt16 = jnp.concatenate([jnp.concatenate([old_vt, jnp.ones_like(old_vt)], axis=1), avt16], axis=2)
        lf_all = jnp.concatenate([p_lf, lf], axis=1)
        conv_in = _conv_state_in(p_conv, lw["halves"])
        past_b, past_c = (layer, p_bk, p_bv), (layer, p_ck, p_cv)
    ak2, ik2 = (_pad_keys(a, Lp) for a in keys)
    v1t = jnp.pad(avt16, ((0, 0), (0, 0), (0, Lp - L)))

    fk_t = _cumsum_time(jnp.swapaxes(_pad_keys(lf_all, Lp), 1, 2))
    fq = jnp.swapaxes(fk_t[:, :, P:L], 1, 2)

    o_a = _dsa(aq16, iq16, iw, ak2, v1t, ik2, tq=tiles["dsa"], q_off=P, topk=topk)
    o_b = _sb(bq16, bk16, bv16, past_b, Lp, tq=tiles["sb"][0], tk=tiles["sb"][1], q_off=P)
    o_c = _fox(cq16, ck16, cv16, past_c, fq, fk_t, tq=tiles["fox"][0], tk=tiles["fox"][1], q_off=P)

    x = _merge(x, o_a, o_b, o_c, lw["g_mix"], lw["wg"], lw["wb"], lw["wo"], tiles["merge"])
    x, st = _ffn(x, lw["g_ffn"], conv_in, lw["wu"], lw["cw"], lw["cb"], lw["wd"], gf, tiles["ffn"],
                 layer == depth - 1)
    return x, stacked, _conv_state_out(st, lw["d_ff"])


def _tiles(T, P):
    pick = lambda want: min(want, T)
    if T % 512 == 0 and P % 512 == 0:
        sb, fox = (256, 256), (256, 512)
    else:
        assert T <= KEY_TILE
        sb = fox = (T, -(-(P + T) // KEY_TILE) * KEY_TILE)
    return dict(proj=pick(512), dsa=pick(128), sb=sb, fox=fox, merge=pick(512), ffn=pick(512))


def kernel(x_prompt, x_sample, cache_a_k, cache_a_v, cache_a_idx_k, cache_b_k, cache_b_v, cache_c_k, cache_c_v,
           cache_c_logf, state_ffn_conv, norm_mix, w_in, b_f, w_branch, w_gate, w_out, norm_ffn, w_up, conv_w,
           conv_b, w_down, norm_final):
    depth = w_in.shape[0]
    caches = (cache_a_k, cache_a_v, cache_a_idx_k, cache_b_k, cache_b_v, cache_c_k, cache_c_v, cache_c_logf,
              state_ffn_conv)
    layers = [_pack_layer(l, norm_mix, w_in, b_f, w_branch, w_gate, w_out, norm_ffn, w_up, conv_w, conv_b, w_down)
              for l in range(depth)]
    gf = norm_final[None]

    def trunk(x, past):
        B, T, _ = x.shape
        tiles = _tiles(T, 0 if past is None else past[0].shape[2])
        stacked, conv_states = None, []
        if past is not None:
            P = past[0].shape[2]
            wide = lambda c: jnp.transpose(c, (0, 1, 3, 4, 2)).reshape(depth, B, N_PAIRS, LANES, P)
        for l in range(depth):
            lp = None if past is None else tuple(wide(c) if 3 <= i <= 6 else c[l] for i, c in enumerate(past))
            x, stacked, conv_state = _group_layer(x, lp, layers[l], gf, l, depth, stacked, tiles)
            conv_states.append(conv_state)
        ak, av, ik, bk, bv, ck, cv, lf = stacked
        heads = lambda a: a.reshape(depth, B, T, N_HEADS, HEAD_DIM)
        return x, (ak, av, ik, heads(bk), heads(bv), heads(ck), heads(cv), lf, jnp.stack(conv_states, axis=0))

    y_prompt, p_state = trunk(x_prompt, None)
    y_sample, s_state = trunk(x_sample, caches)
    return (y_prompt, y_sample) + p_state + s_state
```

```python
import functools

import jax
import jax.numpy as jnp
import numpy as np
from jax import lax
from jax.experimental import pallas as pl
from jax.experimental.pallas import tpu as pltpu

F32 = jnp.float32
BF16 = jnp.bfloat16

LANES = 128
SUBLANES = 8
HEAD_DIM = 64
N_HEADS = 8
N_PAIRS = N_HEADS // 2
BRANCH_W = N_HEADS * HEAD_DIM
N_BRANCH = 3
N_IDX = 4
IDX_DIM = 64
CHUNK = 64
CHUNK_SHIFT = 6
TOPK_MAX = 256
ROT_DIM = HEAD_DIM // 4
ROPE_THETA = 500000.0
CONV_W = 3
EPS = 1e-6
NEG = -1e30
KEY_TILE = 128
FF_CHUNK = 256
VMEM_LIMIT = 56 * 1024 * 1024

ROPE_W = 1024
MISC_W = 256
CF_LANE = 8
LOG2E = 1.4426950408889634
Q_SCALE = HEAD_DIM ** -0.5 * LOG2E


def _cparams(sem):
    return pltpu.CompilerParams(dimension_semantics=sem, vmem_limit_bytes=VMEM_LIMIT)


def _rms(x, g):
    return x * lax.rsqrt(jnp.mean(x * x, axis=-1, keepdims=True) + EPS) * g


def _sigmoid(z):
    return 1.0 / (1.0 + jnp.exp(-z))


def _log_sigmoid(z):
    return jnp.minimum(z, 0.0) - jnp.log1p(jnp.exp(-jnp.abs(z)))


def _dot_t(a, b):
    return lax.dot_general(a, b, (((1,), (1,)), ((), ())), preferred_element_type=F32)


def _dot(a, b):
    return jnp.dot(a, b, preferred_element_type=F32)


def _keep_head(blk, which):
    lane = lax.broadcasted_iota(jnp.int32, blk.shape, blk.ndim - 1)
    keep = (lane >= HEAD_DIM) if which else (lane < HEAD_DIM)
    return jnp.where(keep, blk, jnp.zeros_like(blk))


def _merge_pair(o0, o1):
    lane = lax.broadcasted_iota(jnp.int32, o0.shape, o0.ndim - 1)
    return jnp.where(lane < HEAD_DIM, o0, o1)


N_CACHE_ROWS = 8


def _proj_kernel(x_ref, g_ref, w_ref, bf_ref, rc_ref, rs1_ref, rs2_ref, *refs):
    (ak_ref, av_ref, ik_ref, bk_ref, bv_ref, ck_ref, cv_ref, lf_ref, iw_ref,
     aq16, iq16, ak16, avt16, ik16, bq16, bk16, bv16, cq16, ck16, cv16) = refs[-(N_CACHE_ROWS + 12):]
    xn = _rms(x_ref[0], g_ref[...]).astype(BF16)
    rc, rs1, rs2 = rc_ref[...], rs1_ref[...], rs2_ref[...]

    def cols(c0, width):
        return _dot(xn, w_ref[:, c0:c0 + width])

    def group(blk, g):
        return blk[:, g * LANES:(g + 1) * LANES]

    def rope(blk):
        half = ROT_DIM // 2
        return blk * rc + pltpu.roll(blk, LANES - half, 1) * rs1 + pltpu.roll(blk, half, 1) * rs2

    wide = cols(0, BRANCH_W)
    for g in range(4):
        aq16[0, :, g * LANES:(g + 1) * LANES] = (rope(group(wide, g)) * Q_SCALE).astype(BF16)
    wide = cols(BRANCH_W, ROPE_W - BRANCH_W)
    for g in range(2):
        iq16[0, :, g * LANES:(g + 1) * LANES] = (rope(group(wide, g)) * IDX_DIM ** -0.5).astype(BF16)
    blk = rope(group(wide, 2))
    ak_ref[0, 0] = blk[:, :HEAD_DIM]
    ak16[0] = blk.astype(BF16)
    blk = rope(group(wide, 3))
    ik_ref[0, 0] = blk[:, :IDX_DIM]
    ik16[0] = blk.astype(BF16)

    wide = cols(ROPE_W, MISC_W)
    blk = group(wide, 0)
    av_ref[0, 0] = blk[:, :HEAD_DIM]
    row = lax.broadcasted_iota(jnp.int32, (LANES, blk.shape[0]), 0)
    avt16[0] = jnp.where(row < HEAD_DIM, blk.T, 1.0).astype(BF16)
    blk = group(wide, 1)
    iw_ref[0] = blk[:, :SUBLANES] * N_IDX ** -0.5
    lf_ref[0, 0] = _log_sigmoid(blk[:, CF_LANE:CF_LANE + N_HEADS] + bf_ref[:, CF_LANE:CF_LANE + N_HEADS])

    base = ROPE_W + MISC_W
    outs = ((None, bq16), (bk_ref, bk16), (bv_ref, bv16), (None, cq16), (ck_ref, ck16), (cv_ref, cv16))
    for n, (o32, o16) in enumerate(outs):
        wide = cols(base + n * BRANCH_W, BRANCH_W)
        if o32 is None:
            o16[0] = (wide * Q_SCALE).astype(BF16)
        else:
            o32[0, 0] = wide
            o16[0] = wide.astype(BF16)


def _project(x, g, w, bf_row, tables, tm, layer, depth, stacked):
    B, T, D = x.shape
    tok = lambda w_, dt: jax.ShapeDtypeStruct((B, T, w_), dt)
    tspec = lambda w_: pl.BlockSpec((1, tm, w_), lambda b, t: (b, t, 0))
    const = lambda a: pl.BlockSpec(a.shape, lambda b, t: (0,) * a.ndim, pipeline_mode=pl.Buffered(1))
    cache_w = (64, 64, 64, 512, 512, 512, 512, 8)
    cache_shapes = [jax.ShapeDtypeStruct((depth, B, T, w_), F32) for w_ in cache_w]
    cache_specs = [pl.BlockSpec((1, 1, tm, w_), lambda b, t: (layer, b, t, 0)) for w_ in cache_w]
    work = [(tok(w_, dt), tspec(w_)) for w_, dt in ((8, F32), (512, BF16), (256, BF16), (128, BF16))]
    work += [(jax.ShapeDtypeStruct((B, LANES, T), BF16), pl.BlockSpec((1, LANES, tm), lambda b, t: (b, 0, t)))]
    work += [(tok(w_, BF16), tspec(w_)) for w_ in (128,) + (512,) * 6]
    work_shapes, work_specs = [s for s, _ in work], [p for _, p in work]
    rspec = pl.BlockSpec((tm, LANES), lambda b, t: (t, 0))
    prev = () if stacked is None else tuple(stacked)
    n_in = 7
    outs = pl.pallas_call(
        _proj_kernel,
        grid=(B, T // tm),
        in_specs=[tspec(D), const(g), const(w), const(bf_row), rspec, rspec, rspec]
        + [pl.BlockSpec(memory_space=pl.ANY)] * len(prev),
        out_specs=cache_specs + work_specs,
        out_shape=cache_shapes + work_shapes,
        input_output_aliases={n_in + i: i for i in range(len(prev))},
        compiler_params=_cparams(("parallel", "parallel")),
        name="in_proj",
    )(x, g, w, bf_row, *tables, *prev)
    return outs[:N_CACHE_ROWS], outs[N_CACHE_ROWS:]


def _split3(x):
    hi = x.astype(BF16)
    r = x - hi.astype(F32)
    mid = r.astype(BF16)
    lo = (r - mid.astype(F32)).astype(BF16)
    return hi, mid, lo


def _cumsum_kernel(lf_ref, o_ref, *, blk):
    L = lf_ref.shape[-1]
    r = lax.broadcasted_iota(jnp.int32, (blk, blk), 0)
    c = lax.broadcasted_iota(jnp.int32, (blk, blk), 1)
    upper = jnp.where(r <= c, 1.0, 0.0).astype(BF16)
    carry = jnp.zeros((N_HEADS, 1), F32)
    for j in range(L // blk):
        hi, mid, lo = _split3(lf_ref[0, :, j * blk:(j + 1) * blk])
        f = _dot(hi, upper) + _dot(mid, upper) + _dot(lo, upper) + carry
        o_ref[0, :, j * blk:(j + 1) * blk] = f * LOG2E
        carry = f[:, blk - 1:blk]


def _cumsum_time(lf_t):
    B, H, L = lf_t.shape
    blk = LANES
    assert L % blk == 0
    spec = pl.BlockSpec((1, H, L), lambda b: (b, 0, 0))
    return pl.pallas_call(
        functools.partial(_cumsum_kernel, blk=blk),
        grid=(B,),
        in_specs=[spec],
        out_specs=spec,
        out_shape=jax.ShapeDtypeStruct((B, H, L), F32),
        compiler_params=_cparams(("parallel",)),
        name="logf_cumsum",
    )(lf_t)


def _head_column(blk, h):
    lane = lax.broadcasted_iota(jnp.int32, blk.shape, 1)
    return jnp.sum(jnp.where(lane == h, blk, 0.0), axis=1, keepdims=True)


def _positions(tq, tk, q0, k0):
    qpos = q0 + lax.broadcasted_iota(jnp.int32, (tq, tk), 0)
    kpos = k0 + lax.broadcasted_iota(jnp.int32, (tq, tk), 1)
    return qpos, kpos


def _pair_specs(tq, Lp):
    qspec = pl.BlockSpec((1, tq, BRANCH_W), lambda b, qi: (b, qi, 0))
    kspec = pl.BlockSpec((1, Lp, BRANCH_W), lambda b, qi: (b, 0, 0))
    return qspec, kspec


def _pair_lanes(hp):
    return pl.ds(pl.multiple_of(hp * LANES, LANES), LANES)


def _split_past(rest, has_past):
    return (rest[:2], rest[2:-2], rest[-2:]) if has_past else ((), rest, ())


def _key_ops(k_ref, v_ref, past, bufs, tk):
    if not past:
        def qk(qmat, hp, k0):
            return _dot_t(qmat, k_ref[0, pl.ds(k0, tk), _pair_lanes(hp)])

        def pv(p, hp, k0, ones):
            v = v_ref[0, pl.ds(k0, tk), _pair_lanes(hp)]
            return _dot(p, v if ones is None else jnp.concatenate([v, ones], axis=1))

        return qk, pv

    P, T = past[0].shape[-1], k_ref.shape[1]
    for new, buf in zip((k_ref, v_ref), bufs):
        buf[0:T, :] = new[0]
        if buf.shape[0] > T:
            buf[T:, :] = jnp.zeros((buf.shape[0] - T, BRANCH_W), BF16)

    def qk(qmat, hp, k0):
        return jnp.concatenate([_dot(qmat, past[0][0, hp].astype(BF16)), _dot_t(qmat, bufs[0][:, _pair_lanes(hp)])],
                               axis=1)

    def pv(p, hp, k0, ones):
        vt, v_tail = past[1][0, hp].astype(BF16), bufs[1][:, _pair_lanes(hp)]
        if ones is not None:
            vt = jnp.concatenate([vt, jnp.ones(vt.shape, BF16)], axis=0)
            v_tail = jnp.concatenate([v_tail, ones[:v_tail.shape[0]]], axis=1)
        return _dot_t(p[:, :P], vt) + _dot(p[:, P:], v_tail)

    return qk, pv


def _past_specs_and_bufs(past, T, Lp):
    if past is None:
        return [], [], ()
    layer, pk, pv = past
    P = pk.shape[-1]
    assert P % LANES == 0 and pk.shape[2:4] == (N_PAIRS, LANES)
    spec = pl.BlockSpec((None, 1, N_PAIRS, LANES, P), lambda b, qi: (layer, b, 0, 0, 0))
    return [spec, spec], [pltpu.VMEM((Lp - P, BRANCH_W), BF16)] * 2, (pk, pv)


def _fox_kernel(q_ref, k_ref, v_ref, *rest, tq, tk, q_off, has_past):
    past, (fq_ref, fk_ref, o_ref, m_ref, acc_ref), bufs = _split_past(rest, has_past)
    qk, pv = _key_ops(k_ref, v_ref, past, bufs, tk)
    q0 = q_off + pl.program_id(1) * tq
    diag = q0 // tk
    fq_blk = fq_ref[0]
    ones = jnp.ones((tk, LANES), BF16)

    def head_pair(hp, _):
        lanes = _pair_lanes(hp)
        qp = q_ref[0, :, lanes]
        qm = (_keep_head(qp, 0), _keep_head(qp, 1))
        fq = (_head_column(fq_blk, hp * 2), _head_column(fq_blk, hp * 2 + 1))
        m_ref[...] = jnp.full(m_ref.shape, NEG, F32)
        acc_ref[...] = jnp.zeros(acc_ref.shape, F32)

        def tiles(js, masked):
            k0 = [j * tk if isinstance(j, int) else pl.multiple_of(j * tk, tk) for j in js]
            chains = [(t, hh) for t in range(len(js)) for hh in (0, 1)]
            if masked:
                qpos, kpos = _positions(tq, tk, q0, k0[0])
                causal = kpos <= qpos
            s = {}
            for t, hh in chains:
                fk = fk_ref[0, hp * 2 + hh, :, pl.ds(k0[t], tk)]
                s[t, hh] = qk(qm[hh], hp, k0[t]) + fq[hh] - fk
                if masked and t == 0:
                    s[t, hh] = jnp.where(causal, s[t, hh], NEG)
            m_old = [m_ref[hh] for hh in (0, 1)]
            m_new = list(m_old)
            for t, hh in chains:
                m_new[hh] = jnp.maximum(m_new[hh], jnp.max(s[t, hh], axis=1, keepdims=True))
            p = {c: jnp.exp2(s[c] - m_new[c[1]]).astype(BF16) for c in chains}
            for hh in (0, 1):
                upd = jnp.exp2(m_old[hh] - m_new[hh]) * acc_ref[hh]
                for t in range(len(js)):
                    upd = upd + pv(p[t, hh], hp, k0[t], ones)
                acc_ref[hh] = upd
                m_ref[hh] = m_new[hh]

        tiles([diag], True)

        def body(i, _):
            j = diag - 1 - 2 * i
            tiles([j, j - 1], False)
            return 0

        lax.fori_loop(0, diag // 2, body, 0)

        @pl.when(diag % 2 == 1)
        def _():
            tiles([0], False)
        o0 = acc_ref[0, :, :LANES] / acc_ref[0, :, LANES:]
        o1 = acc_ref[1, :, :LANES] / acc_ref[1, :, LANES:]
        o_ref[0, :, lanes] = _merge_pair(o0, o1).astype(o_ref.dtype)
        return 0

    lax.fori_loop(0, N_PAIRS, head_pair, 0)


def _fox(q, k, v, past, fq, fk_t, *, tq, tk, q_off):
    B, T, _ = q.shape
    Lp = fk_t.shape[-1]
    qspec, kspec = _pair_specs(tq, k.shape[1])
    past_specs, bufs, past = _past_specs_and_bufs(past, T, Lp)
    return pl.pallas_call(
        functools.partial(_fox_kernel, tq=tq, tk=tk, q_off=q_off, has_past=bool(past)),
        grid=(B, T // tq),
        in_specs=[qspec, kspec, kspec] + past_specs
        + [pl.BlockSpec((1, tq, N_HEADS), lambda b, qi: (b, qi, 0)),
           pl.BlockSpec((1, N_HEADS, 1, Lp), lambda b, qi: (b, 0, 0, 0))],
        out_specs=qspec,
        out_shape=jax.ShapeDtypeStruct(q.shape, BF16),
        scratch_shapes=[pltpu.VMEM((2, tq, 1), F32), pltpu.VMEM((2, tq, 2 * LANES), F32)] + bufs,
        compiler_params=_cparams(("parallel", "parallel")),
        name="fox_attn",
    )(q, k, v, *past, fq, fk_t[:, :, None, :])


def _sb_kernel(q_ref, k_ref, v_ref, *rest, tq, tk, q_off, has_past):
    past, (o_ref, acc_ref, tail_ref), bufs = _split_past(rest, has_past)
    qk, pv = _key_ops(k_ref, v_ref, past, bufs, tk)
    q0 = q_off + pl.program_id(1) * tq
    diag = q0 // tk
    r = lax.broadcasted_iota(jnp.int32, (LANES, 2 * LANES), 0)
    c = lax.broadcasted_iota(jnp.int32, (LANES, 2 * LANES), 1)
    suffix_w = jnp.where((c >= LANES) | (r > c), 1.0, 0.0).astype(BF16)

    def head_pair(hp, _):
        lanes = _pair_lanes(hp)
        qp = q_ref[0, :, lanes]
        qm = (_keep_head(qp, 0), _keep_head(qp, 1))
        acc_ref[...] = jnp.zeros(acc_ref.shape, F32)
        tail_ref[...] = jnp.zeros(tail_ref.shape, F32)

        def tiles(js, masked):
            k0 = [j * tk if isinstance(j, int) else pl.multiple_of(j * tk, tk) for j in js]
            chains = [(t, hh) for t in range(len(js)) for hh in (0, 1)]
            if masked:
                qpos, kpos = _positions(tq, tk, q0, k0[0])
                strict = kpos < qpos
            z = {c: qk(qm[c[1]], hp, k0[c[0]]) for c in chains}
            ls, la, after = {}, {}, {}
            for c in chains:
                ls[c] = jnp.minimum(z[c], 0.0) - jnp.log2(1.0 + jnp.exp2(-jnp.abs(z[c])))
                la[c] = ls[c] - z[c]
                if masked and c[0] == 0:
                    la[c] = jnp.where(strict, la[c], 0.0)
                    ls[c] = jnp.where(strict, ls[c], NEG)
                la[c] = la[c].astype(BF16)
            for hh in (0, 1):
                run = tail_ref[hh]
                for t in range(len(js)):
                    blocks = [None] * (tk // LANES)
                    for b in reversed(range(tk // LANES)):
                        res = _dot(la[t, hh][:, b * LANES:(b + 1) * LANES], suffix_w)
                        blocks[b] = res[:, :LANES] + run
                        run = run + res[:, LANES:]
                    after[t, hh] = jnp.concatenate(blocks, axis=1)
                tail_ref[hh] = run
            a = {c: jnp.exp2(ls[c] + after[c]).astype(BF16) for c in chains}
            for c in chains:
                acc_ref[c[1]] += pv(a[c], hp, k0[c[0]], None)

        tiles([diag], True)

        def body(i, _):
            j = diag - 1 - 2 * i
            tiles([j, j - 1], False)
            return 0

        lax.fori_loop(0, diag // 2, body, 0)

        @pl.when(diag % 2 == 1)
        def _():
            tiles([0], False)
        o_ref[0, :, lanes] = _merge_pair(acc_ref[0], acc_ref[1]).astype(o_ref.dtype)
        return 0

    lax.fori_loop(0, N_PAIRS, head_pair, 0)


def _sb(q, k, v, past, Lp, *, tq, tk, q_off):
    B, T, _ = q.shape
    qspec, kspec = _pair_specs(tq, k.shape[1])
    past_specs, bufs, past = _past_specs_and_bufs(past, T, Lp)
    return pl.pallas_call(
        functools.partial(_sb_kernel, tq=tq, tk=tk, q_off=q_off, has_past=bool(past)),
        grid=(B, T // tq),
        in_specs=[qspec, kspec, kspec] + past_specs,
        out_specs=qspec,
        out_shape=jax.ShapeDtypeStruct(q.shape, BF16),
        scratch_shapes=[pltpu.VMEM((2, tq, LANES), F32), pltpu.VMEM((2, tq, LANES), F32)] + bufs,
        compiler_params=_cparams(("parallel", "parallel")),
        name="sb_attn",
    )(q, k, v, *past)


def _const_key(v):
    bits = int(np.float32(v).view(np.int32))
    return (bits ^ 0x7FFFFFFF) + 1 if bits < 0 else bits


def _float_key(x):
    bits = lax.bitcast_convert_type(x, jnp.int32)
    return jnp.where(bits < 0, jnp.bitwise_xor(bits, 0x7FFFFFFF) + 1, bits)


KEY_MIN = -2 ** 31
KEY_HALF_NEG = _const_key(0.5 * NEG)


KEY_MAX = 2 ** 31 - 1
KEY_BITS = 32
COUNT_CHAINS = 8


def _dsa_kernel(aq_ref, iq_ref, iwt_ref, k_ref, vt_ref, ik_ref, o_ref, key_ref, q8_ref, m_ref, acc_ref,
                *, tq, tk, q_off, n_rows, topk):
    q0 = q_off + pl.program_id(1) * tq
    n_tiles = (q0 + n_rows + tk - 1) // tk

    def tile_start(j):
        return pl.multiple_of(j * tk, LANES)

    iq = iq_ref[0]
    iwt = iwt_ref[0]
    qh = [_keep_head(iq[:, (h // 2) * LANES:(h // 2 + 1) * LANES], h % 2) for h in range(N_IDX)]
    q_chunk = jnp.right_shift(q0 + lax.broadcasted_iota(jnp.int32, (tk, tq), 1), CHUNK_SHIFT)

    def score_tile(j, _):
        k0 = tile_start(j)
        ik = ik_ref[0, pl.ds(k0, tk), :]
        sc = jnp.zeros((tk, tq), F32)
        for h in range(N_IDX):
            sc = sc + jnp.maximum(_dot_t(ik, qh[h]), 0.0) * iwt[h:h + 1, :]
        key_chunk = jnp.right_shift(k0 + lax.broadcasted_iota(jnp.int32, (tk, tq), 0), CHUNK_SHIFT)
        key_ref[pl.ds(k0, tk), :] = _float_key(jnp.where(key_chunk <= q_chunk, sc, NEG))
        return 0

    lax.fori_loop(0, n_tiles, score_tile, 0)

    def search():
        def step(_, cr):
            lo, hi, above_hi = cr
            mid = jnp.right_shift(lo, 1) + jnp.right_shift(hi, 1) + jnp.bitwise_and(jnp.bitwise_and(lo, hi), 1)

            def count_tile(j, cnt):
                above = jnp.where(key_ref[pl.ds(tile_start(j), tk), :] > mid, 1.0, 0.0)
                return cnt + jnp.sum(above.reshape((-1,) + cnt.shape), axis=0)

            cnt = lax.fori_loop(0, n_tiles, count_tile, jnp.zeros((COUNT_CHAINS, SUBLANES, tq), F32))
            cnt = jnp.sum(jnp.sum(cnt, axis=0), axis=0, keepdims=True)
            less = cnt < topk
            return jnp.where(less, lo, mid), jnp.where(less, mid, hi), jnp.where(less, cnt, above_hi)

        init = (jnp.full((1, tq), KEY_MIN, jnp.int32), jnp.full((1, tq), KEY_MAX, jnp.int32), jnp.zeros((1, tq), F32))
        _, hi, above_hi = lax.fori_loop(0, KEY_BITS, step, init)
        return hi, topk - above_hi

    thr, room = lax.cond(q0 + n_rows > topk, search,
                         lambda: (jnp.full((1, tq), KEY_MIN, jnp.int32), jnp.zeros((1, tq), F32)))
    row_ok = lax.broadcasted_iota(jnp.int32, (1, tq), 1) < n_rows
    room = jnp.where(row_ok, room, 0.0)

    @pl.when(jnp.max(room) > 0.0)
    def _():
        r = lax.broadcasted_iota(jnp.int32, (LANES, LANES), 0)
        c = lax.broadcasted_iota(jnp.int32, (LANES, LANES), 1)
        before = jnp.where(c < r, 1.0, 0.0).astype(BF16)

        def tie_tile(j, seen):
            k0 = tile_start(j)
            for b in range(tk // LANES):
                rows = pl.ds(k0 + b * LANES, LANES)
                keys = key_ref[rows, :]
                eq = keys == thr
                eqf = jnp.where(eq, 1.0, 0.0)
                rank = _dot(before, eqf.astype(BF16)) + seen
                key_ref[rows, :] = jnp.where(eq, jnp.where(rank < room, keys + 1, keys), keys)
                seen = seen + jnp.sum(eqf, axis=0, keepdims=True)
            return seen

        lax.fori_loop(0, n_tiles, tie_tile, jnp.zeros((1, tq), F32))

    aq = aq_ref[0]
    for h in range(N_HEADS):
        q8_ref[h * tq:(h + 1) * tq, :] = _keep_head(aq[:, (h // 2) * LANES:(h // 2 + 1) * LANES], h % 2)
    m_ref[...] = jnp.full(m_ref.shape, NEG, F32)
    acc_ref[...] = jnp.zeros(acc_ref.shape, F32)
    thr_sel = jnp.maximum(thr, KEY_HALF_NEG)

    def attend_tile(j, _):
        k0 = tile_start(j)
        kt = k_ref[0, pl.ds(k0, tk), :]
        v1t = vt_ref[0, :, pl.ds(k0, tk)]
        bias = jnp.where(key_ref[pl.ds(k0, tk), :] > thr_sel, 0.0, NEG)
        st = _dot_t(kt, q8_ref[...])
        st = jnp.concatenate([st[:, h * tq:(h + 1) * tq] + bias for h in range(N_HEADS)], axis=1)
        m_old = m_ref[...]
        m_new = jnp.maximum(m_old, jnp.max(st, axis=0, keepdims=True))
        p = jnp.exp2(st - m_new).astype(BF16)
        acc_ref[...] = jnp.exp2(m_old - m_new) * acc_ref[...] + _dot(v1t, p)
        m_ref[...] = m_new
        return 0

    lax.fori_loop(0, n_tiles, attend_tile, 0)
    for hp in range(N_PAIRS):
        pair = []
        for h in (2 * hp, 2 * hp + 1):
            a = acc_ref[:, h * tq:(h + 1) * tq]
            pair.append(a[:HEAD_DIM] / a[HEAD_DIM:])
        o_ref[0, :, hp * LANES:(hp + 1) * LANES] = jnp.concatenate(pair, axis=0).T.astype(o_ref.dtype)


def _dsa(aq, iq, iw, k2, v1t, ik2, *, tq, q_off, topk):
    B, T, _ = aq.shape
    Lp = k2.shape[1]
    tk = Lp if T <= LANES else next(t for t in (512, 384, 256, 128) if Lp % t == 0)
    Tp = -(-T // LANES) * LANES
    tq = max(tq, LANES)
    n_rows = min(T, tq)
    pad_t = lambda a: jnp.pad(a, ((0, 0), (0, Tp - T), (0, 0)))
    qspec = lambda w: pl.BlockSpec((1, tq, w), lambda b, qi: (b, qi, 0))
    kspec = pl.BlockSpec((1, Lp, LANES), lambda b, qi: (b, 0, 0))
    out = pl.pallas_call(
        functools.partial(_dsa_kernel, tq=tq, tk=tk, q_off=q_off, n_rows=n_rows, topk=topk),
        grid=(B, Tp // tq),
        in_specs=[qspec(BRANCH_W), qspec(N_IDX * IDX_DIM), pl.BlockSpec((1, SUBLANES, tq), lambda b, qi: (b, 0, qi)),
                  kspec, pl.BlockSpec((1, LANES, Lp), lambda b, qi: (b, 0, 0)), kspec],
        out_specs=qspec(BRANCH_W),
        out_shape=jax.ShapeDtypeStruct((B, Tp, BRANCH_W), BF16),
        scratch_shapes=[pltpu.VMEM((Lp, tq), jnp.int32), pltpu.VMEM((N_HEADS * tq, LANES), BF16),
                        pltpu.VMEM((1, N_HEADS * tq), F32), pltpu.VMEM((LANES, N_HEADS * tq), F32)],
        compiler_params=_cparams(("parallel", "parallel")),
        name="dsa_attn",
    )(pad_t(aq), pad_t(iq), jnp.swapaxes(pad_t(iw), 1, 2), k2, v1t, ik2)
    return out[:, :T]


def _merge_kernel(x_ref, oa_ref, ob_ref, oc_ref, g_ref, wg_ref, wb_ref, wo_ref, y_ref):
    x = x_ref[0]
    D = x.shape[-1]
    xn = _rms(x, g_ref[...]).astype(BF16)
    merged = jnp.zeros(x.shape, F32)
    for n, o_ref in enumerate((oa_ref, ob_ref, oc_ref)):
        gate = _sigmoid(_dot(xn, wg_ref[:, n * D:(n + 1) * D]))
        merged = merged + gate * _dot(o_ref[0], wb_ref[n])
    y_ref[0] = x + _dot(merged.astype(BF16), wo_ref[...])


def _merge(x, oa, ob, oc, g, wg, wb, wo, tm):
    B, T, D = x.shape
    tspec = lambda w_: pl.BlockSpec((1, tm, w_), lambda b, t: (b, t, 0))
    const = lambda a: pl.BlockSpec(a.shape, lambda b, t: (0,) * a.ndim)
    return pl.pallas_call(
        _merge_kernel,
        grid=(B, T // tm),
        in_specs=[tspec(D), tspec(BRANCH_W), tspec(BRANCH_W), tspec(BRANCH_W), const(g), const(wg), const(wb),
                  const(wo)],
        out_specs=tspec(D),
        out_shape=jax.ShapeDtypeStruct(x.shape, F32),
        compiler_params=_cparams(("parallel", "parallel")),
        name="merge_out",
    )(x, oa, ob, oc, g, wg, wb, wo)


def _ffn_kernel(x_ref, g_ref, past_ref, wu_ref, cw_ref, cb_ref, wd_ref, gf_ref, y_ref, st_ref,
                hbuf_ref, carry_ref, acc_ref, *, tm, final_norm):
    fc = FF_CHUNK
    d_ff = wd_ref.shape[0]
    n_chunks = d_ff // fc

    @pl.when(pl.program_id(1) == 0)
    def _():
        carry_ref[...] = past_ref[0]

    x = x_ref[0]
    xn = _rms(x, g_ref[...]).astype(BF16)
    acc_ref[...] = jnp.zeros(acc_ref.shape, F32)

    def up_proj(c):
        return jnp.concatenate([_dot(xn, wu_ref[:, c * fc:(c + 1) * fc]),
                                _dot(xn, wu_ref[:, d_ff + c * fc:d_ff + (c + 1) * fc])], axis=1)

    h_next = up_proj(0)
    for c in range(n_chunks):
        hbuf = hbuf_ref.at[c % 2]
        both = lambda ref: jnp.concatenate([ref[:, c * fc:(c + 1) * fc], ref[:, d_ff + c * fc:d_ff + (c + 1) * fc]],
                                           axis=1)
        h = h_next
        if c + 1 < n_chunks:
            h_next = up_proj(c + 1)
        hbuf[0:SUBLANES, :] = carry_ref[c]
        hbuf[SUBLANES:SUBLANES + tm, :] = h
        carry_ref[c] = h[tm - SUBLANES:tm, :]
        cw = both(cw_ref)
        hc = (both(cb_ref) + cw[0:1] * hbuf[SUBLANES - 2:SUBLANES - 2 + tm, :]
              + cw[1:2] * hbuf[SUBLANES - 1:SUBLANES - 1 + tm, :] + cw[2:3] * h)
        gate, up = hc[:, :fc], hc[:, fc:]
        act = gate * _sigmoid(gate) * up
        acc_ref[...] += _dot(act.astype(BF16), wd_ref[c * fc:(c + 1) * fc, :])
    st_ref[0] = carry_ref[...]
    y = x + acc_ref[...]
    if final_norm:
        y = _rms(y, gf_ref[...])
    y_ref[0] = y


def _ffn(x, g, past, wu, cw, cb, wd, gf, tm, final_norm):
    B, T, D = x.shape
    nc, fc2 = wd.shape[0] // FF_CHUNK, 2 * FF_CHUNK
    tspec = pl.BlockSpec((1, tm, D), lambda b, t: (b, t, 0))
    const = lambda a: pl.BlockSpec(a.shape, lambda b, t: (0,) * a.ndim, pipeline_mode=pl.Buffered(1))
    sspec = pl.BlockSpec((1, nc, SUBLANES, fc2), lambda b, t: (b, 0, 0, 0))
    return pl.pallas_call(
        functools.partial(_ffn_kernel, tm=tm, final_norm=final_norm),
        grid=(B, T // tm),
        in_specs=[tspec, const(g), sspec, const(wu), const(cw), const(cb), const(wd), const(gf)],
        out_specs=[tspec, sspec],
        out_shape=[jax.ShapeDtypeStruct(x.shape, F32), jax.ShapeDtypeStruct((B, nc, SUBLANES, fc2), F32)],
        scratch_shapes=[pltpu.VMEM((2, tm + SUBLANES, fc2), F32), pltpu.VMEM((nc, SUBLANES, fc2), F32),
                        pltpu.VMEM((tm, D), F32)],
        compiler_params=_cparams(("arbitrary", "arbitrary")),
        name="conv_ffn",
    )(x, g, past, wu, cw, cb, wd, gf)


def _rope_tables(pos):
    half = ROT_DIM // 2
    freq = ROPE_THETA ** (-jnp.arange(half, dtype=F32) / half)
    ang = pos.astype(F32)[:, None] * freq[None, :]
    lane = np.arange(LANES) % HEAD_DIM
    cos = jnp.cos(ang)[:, lane % half]
    sin = jnp.sin(ang)[:, lane % half]
    rc = jnp.where(lane < ROT_DIM, cos, 1.0)
    rs1 = jnp.where(lane < half, -sin, 0.0)
    rs2 = jnp.where((lane >= half) & (lane < ROT_DIM), sin, 0.0)
    return rc, rs1, rs2


IN_SIZES = (BRANCH_W, HEAD_DIM, HEAD_DIM, N_IDX * IDX_DIM, IDX_DIM, N_IDX) + (BRANCH_W,) * 6 + (N_HEADS,)
PACKED_W = ROPE_W + MISC_W + 6 * BRANCH_W


def _pack_w_kernel(w_ref, o_ref, s_ref):
    offs = np.concatenate([[0], np.cumsum(IN_SIZES)]).tolist()
    a_q, a_k, a_v, i_q, i_k, i_w, b_q = offs[:7]
    c_f = offs[12]
    src = lambda c0, width: w_ref[0, :, c0:c0 + width]
    s_ref[:, 0:BRANCH_W] = src(a_q, BRANCH_W)
    s_ref[:, BRANCH_W:BRANCH_W + 256] = src(i_q, 256)
    for dst, c0 in ((768, a_k), (896, i_k), (ROPE_W, a_v)):
        s_ref[:, dst:dst + HEAD_DIM] = src(c0, HEAD_DIM)
        s_ref[:, dst + HEAD_DIM:dst + 2 * HEAD_DIM] = src(c0, HEAD_DIM)
    misc = ROPE_W + LANES
    s_ref[:, misc:misc + LANES] = jnp.zeros((s_ref.shape[0], LANES), F32)
    s_ref[:, misc:misc + N_IDX] = src(i_w, N_IDX)
    s_ref[:, misc + CF_LANE:misc + CF_LANE + N_HEADS] = src(c_f, N_HEADS)
    s_ref[:, ROPE_W + MISC_W:PACKED_W] = src(b_q, 6 * BRANCH_W)
    o_ref[...] = s_ref[...].astype(BF16)


def _pack_w(w_in, l):
    D = w_in.shape[1]
    rows = 256
    return pl.pallas_call(
        _pack_w_kernel,
        grid=(D // rows,),
        in_specs=[pl.BlockSpec((1, rows, w_in.shape[2]), lambda r: (l, r, 0))],
        out_specs=pl.BlockSpec((rows, PACKED_W), lambda r: (r, 0)),
        out_shape=jax.ShapeDtypeStruct((D, PACKED_W), BF16),
        scratch_shapes=[pltpu.VMEM((rows, PACKED_W), F32)],
        compiler_params=_cparams(("parallel",)),
        name="pack_w_in",
    )(w_in)


def _pack_layer(l, norm_mix, w_in, b_f, w_branch, w_gate, w_out, norm_ffn, w_up, conv_w, conv_b, w_down):
    w = _pack_w(w_in, l)
    bf_row = jnp.zeros((1, LANES), F32).at[0, CF_LANE:CF_LANE + N_HEADS].set(b_f[l])

    d_ff = w_down.shape[1]
    fc = FF_CHUNK
    nc = d_ff // fc
    assert nc * fc == d_ff
    halves = lambda a: jnp.concatenate([a[..., :d_ff].reshape(a.shape[:-1] + (nc, fc)),
                                        a[..., d_ff:].reshape(a.shape[:-1] + (nc, fc))], axis=-1)
    wu, cw, cb, wd = w_up[l].astype(BF16), conv_w[l], conv_b[l][None], w_down[l].astype(BF16)
    return dict(g_mix=norm_mix[l][None], w=w, bf_row=bf_row, wg=w_gate[l].astype(BF16),
                wb=w_branch[l].astype(BF16), wo=w_out[l].astype(BF16), g_ffn=norm_ffn[l][None],
                wu=wu, cw=cw, cb=cb, wd=wd, halves=halves, nc=nc, d_ff=d_ff)


def _conv_state_in(state, halves):
    st = jnp.moveaxis(halves(state), 2, 1)
    return jnp.pad(st, ((0, 0), (0, 0), (SUBLANES - (CONV_W - 1), 0), (0, 0)))


def _conv_state_out(st, d_ff):
    st = st[:, :, SUBLANES - (CONV_W - 1):, :]
    B, nc, r, fc2 = st.shape
    fc = fc2 // 2
    gate = jnp.moveaxis(st[..., :fc], 1, 2).reshape(B, r, d_ff)
    up = jnp.moveaxis(st[..., fc:], 1, 2).reshape(B, r, d_ff)
    return jnp.concatenate([gate, up], axis=-1)


def _pad_keys(a, Lp):
    return jnp.pad(a, ((0, 0), (0, Lp - a.shape[1]), (0, 0)))


def _group_layer(x, past, lw, gf, layer, depth, stacked, tiles):
    B, T, D = x.shape
    P = 0 if past is None else past[0].shape[1]
    L = P + T
    Lp = -(-L // KEY_TILE) * KEY_TILE
    topk = max(1, min(TOPK_MAX, L // 4))
    tables = _rope_tables(jnp.arange(P, L))

    stacked, (iw, aq16, iq16, ak16, avt16, ik16, bq16, bk16, bv16, cq16, ck16, cv16) = _project(
        x, lw["g_mix"], lw["w"], lw["bf_row"], tables, tiles["proj"], layer, depth, stacked)
    lf = stacked[-1][layer]

    if past is None:
        keys = (ak16, ik16)
        lf_all = lf
        conv_in = jnp.zeros((B, lw["nc"], SUBLANES, 2 * FF_CHUNK), F32)
        past_b = past_c = None
    else:
        p_ak, p_av, p_ik, p_bk, p_bv, p_ck, p_cv, p_lf, p_conv = past
        dup = lambda a: jnp.concatenate([a, a], axis=-1).astype(BF16)
        keys = tuple(jnp.concatenate([o, n], axis=1) for o, n in zip((dup(p_ak), dup(p_ik)), (ak16, ik16)))
        old_vt = jnp.swapaxes(p_av, 1, 2).astype(BF16)
        avt16 = jnp.concatenate([jnp.concatenate([old_vt, jnp.ones_like(old_vt)], axis=1), avt16], axis=2)
        lf_all = jnp.concatenate([p_lf, lf], axis=1)
        conv_in = _conv_state_in(p_conv, lw["halves"])
        past_b, past_c = (layer, p_bk, p_bv), (layer, p_ck, p_cv)
    ak2, ik2 = (_pad_keys(a, Lp) for a in keys)
    v1t = jnp.pad(avt16, ((0, 0), (0, 0), (0, Lp - L)))

    fk_t = _cumsum_time(jnp.swapaxes(_pad_keys(lf_all, Lp), 1, 2))
    fq = jnp.swapaxes(fk_t[:, :, P:L], 1, 2)

    o_a = _dsa(aq16, iq16, iw, ak2, v1t, ik2, tq=tiles["dsa"], q_off=P, topk=topk)
    o_b = _sb(bq16, bk16, bv16, past_b, Lp, tq=tiles["sb"][0], tk=tiles["sb"][1], q_off=P)
    o_c = _fox(cq16, ck16, cv16, past_c, fq, fk_t, tq=tiles["fox"][0], tk=tiles["fox"][1], q_off=P)

    x = _merge(x, o_a, o_b, o_c, lw["g_mix"], lw["wg"], lw["wb"], lw["wo"], tiles["merge"])
    x, st = _ffn(x, lw["g_ffn"], conv_in, lw["wu"], lw["cw"], lw["cb"], lw["wd"], gf, tiles["ffn"],
                 layer == depth - 1)
    return x, stacked, _conv_state_out(st, lw["d_ff"])


def _tiles(T, P):
    pick = lambda want: min(want, T)
    if T % 512 == 0 and P % 512 == 0:
        sb, fox = (256, 256), (256, 512)
    else:
        assert T <= KEY_TILE
        sb = fox = (T, -(-(P + T) // KEY_TILE) * KEY_TILE)
    return dict(proj=pick(512), dsa=pick(128), sb=sb, fox=fox, merge=pick(512), ffn=pick(512))


def kernel(x_prompt, x_sample, cache_a_k, cache_a_v, cache_a_idx_k, cache_b_k, cache_b_v, cache_c_k, cache_c_v,
           cache_c_logf, state_ffn_conv, norm_mix, w_in, b_f, w_branch, w_gate, w_out, norm_ffn, w_up, conv_w,
           conv_b, w_down, norm_final):
    depth = w_in.shape[0]
    caches = (cache_a_k, cache_a_v, cache_a_idx_k, cache_b_k, cache_b_v, cache_c_k, cache_c_v, cache_c_logf,
              state_ffn_conv)
    layers = [_pack_layer(l, norm_mix, w_in, b_f, w_branch, w_gate, w_out, norm_ffn, w_up, conv_w, conv_b, w_down)
              for l in range(depth)]
    gf = norm_final[None]

    def trunk(x, past):
        B, T, _ = x.shape
        tiles = _tiles(T, 0 if past is None else past[0].shape[2])
        stacked, conv_states = None, []
        if past is not None:
            P = past[0].shape[2]
            wide = lambda c: jnp.transpose(c, (0, 1, 3, 4, 2)).reshape(depth, B, N_PAIRS, LANES, P)
        for l in range(depth):
            lp = None if past is None else tuple(wide(c) if 3 <= i <= 6 else c[l] for i, c in enumerate(past))
            x, stacked, conv_state = _group_layer(x, lp, layers[l], gf, l, depth, stacked, tiles)
            conv_states.append(conv_state)
        ak, av, ik, bk, bv, ck, cv, lf = stacked
        heads = lambda a: a.reshape(depth, B, T, N_HEADS, HEAD_DIM)
        return x, (ak, av, ik, heads(bk), heads(bv), heads(ck), heads(cv), lf, jnp.stack(conv_states, axis=0))

    y_prompt, p_state = trunk(x_prompt, None)
    y_sample, s_state = trunk(x_sample, caches)
    return (y_prompt, y_sample) + p_state + s_state
```

```python
import functools

import jax
import jax.numpy as jnp
import numpy as np
from jax import lax
from jax.experimental import pallas as pl
from jax.experimental.pallas import tpu as pltpu

F32 = jnp.float32
BF16 = jnp.bfloat16

LANES = 128
SUBLANES = 8
HEAD_DIM = 64
N_HEADS = 8
N_PAIRS = N_HEADS // 2
BRANCH_W = N_HEADS * HEAD_DIM
N_BRANCH = 3
N_IDX = 4
IDX_DIM = 64
CHUNK = 64
CHUNK_SHIFT = 6
TOPK_MAX = 256
ROT_DIM = HEAD_DIM // 4
ROPE_THETA = 500000.0
CONV_W = 3
EPS = 1e-6
NEG = -1e30
KEY_TILE = 128
FF_CHUNK = 256
VMEM_LIMIT = 56 * 1024 * 1024

ROPE_W = 1024
MISC_W = 256
CF_LANE = 8
LOG2E = 1.4426950408889634
Q_SCALE = HEAD_DIM ** -0.5 * LOG2E


def _cparams(sem):
    return pltpu.CompilerParams(dimension_semantics=sem, vmem_limit_bytes=VMEM_LIMIT)


def _rms(x, g):
    return x * lax.rsqrt(jnp.mean(x * x, axis=-1, keepdims=True) + EPS) * g


def _sigmoid(z):
    return 1.0 / (1.0 + jnp.exp(-z))


def _log_sigmoid(z):
    return jnp.minimum(z, 0.0) - jnp.log1p(jnp.exp(-jnp.abs(z)))


def _dot_t(a, b):
    return lax.dot_general(a, b, (((1,), (1,)), ((), ())), preferred_element_type=F32)


def _dot(a, b):
    return jnp.dot(a, b, preferred_element_type=F32)


def _keep_head(blk, which):
    lane = lax.broadcasted_iota(jnp.int32, blk.shape, blk.ndim - 1)
    keep = (lane >= HEAD_DIM) if which else (lane < HEAD_DIM)
    return jnp.where(keep, blk, jnp.zeros_like(blk))


def _merge_pair(o0, o1):
    lane = lax.broadcasted_iota(jnp.int32, o0.shape, o0.ndim - 1)
    return jnp.where(lane < HEAD_DIM, o0, o1)


N_CACHE_ROWS = 8


def _proj_kernel(x_ref, g_ref, w_ref, bf_ref, rc_ref, rs1_ref, rs2_ref, *refs):
    (ak_ref, av_ref, ik_ref, bk_ref, bv_ref, ck_ref, cv_ref, lf_ref, iw_ref,
     aq16, iq16, ak16, avt16, ik16, bq16, bk16, bv16, cq16, ck16, cv16) = refs[-(N_CACHE_ROWS + 12):]
    xn = _rms(x_ref[0], g_ref[...]).astype(BF16)
    rc, rs1, rs2 = rc_ref[...], rs1_ref[...], rs2_ref[...]

    def cols(c0, width):
        return _dot(xn, w_ref[:, c0:c0 + width])

    def group(blk, g):
        return blk[:, g * LANES:(g + 1) * LANES]

    def rope(blk):
        half = ROT_DIM // 2
        return blk * rc + pltpu.roll(blk, LANES - half, 1) * rs1 + pltpu.roll(blk, half, 1) * rs2

    wide = cols(0, BRANCH_W)
    for g in range(4):
        aq16[0, :, g * LANES:(g + 1) * LANES] = (rope(group(wide, g)) * Q_SCALE).astype(BF16)
    wide = cols(BRANCH_W, ROPE_W - BRANCH_W)
    for g in range(2):
        iq16[0, :, g * LANES:(g + 1) * LANES] = (rope(group(wide, g)) * IDX_DIM ** -0.5).astype(BF16)
    blk = rope(group(wide, 2))
    ak_ref[0, 0] = blk[:, :HEAD_DIM]
    ak16[0] = blk.astype(BF16)
    blk = rope(group(wide, 3))
    ik_ref[0, 0] = blk[:, :IDX_DIM]
    ik16[0] = blk.astype(BF16)

    wide = cols(ROPE_W, MISC_W)
    blk = group(wide, 0)
    av_ref[0, 0] = blk[:, :HEAD_DIM]
    row = lax.broadcasted_iota(jnp.int32, (LANES, blk.shape[0]), 0)
    avt16[0] = jnp.where(row < HEAD_DIM, blk.T, 1.0).astype(BF16)
    blk = group(wide, 1)
    iw_ref[0] = blk[:, :SUBLANES] * N_IDX ** -0.5
    lf_ref[0, 0] = _log_sigmoid(blk[:, CF_LANE:CF_LANE + N_HEADS] + bf_ref[:, CF_LANE:CF_LANE + N_HEADS])

    base = ROPE_W + MISC_W
    outs = ((None, bq16), (bk_ref, bk16), (bv_ref, bv16), (None, cq16), (ck_ref, ck16), (cv_ref, cv16))
    for n, (o32, o16) in enumerate(outs):
        wide = cols(base + n * BRANCH_W, BRANCH_W)
        if o32 is None:
            o16[0] = (wide * Q_SCALE).astype(BF16)
        else:
            o32[0, 0] = wide
            o16[0] = wide.astype(BF16)


def _project(x, g, w, bf_row, tables, tm, layer, depth, stacked):
    B, T, D = x.shape
    tok = lambda w_, dt: jax.ShapeDtypeStruct((B, T, w_), dt)
    tspec = lambda w_: pl.BlockSpec((1, tm, w_), lambda b, t: (b, t, 0))
    const = lambda a: pl.BlockSpec(a.shape, lambda b, t: (0,) * a.ndim, pipeline_mode=pl.Buffered(1))
    cache_w = (64, 64, 64, 512, 512, 512, 512, 8)
    cache_shapes = [jax.ShapeDtypeStruct((depth, B, T, w_), F32) for w_ in cache_w]
    cache_specs = [pl.BlockSpec((1, 1, tm, w_), lambda b, t: (layer, b, t, 0)) for w_ in cache_w]
    work = [(tok(w_, dt), tspec(w_)) for w_, dt in ((8, F32), (512, BF16), (256, BF16), (128, BF16))]
    work += [(jax.ShapeDtypeStruct((B, LANES, T), BF16), pl.BlockSpec((1, LANES, tm), lambda b, t: (b, 0, t)))]
    work += [(tok(w_, BF16), tspec(w_)) for w_ in (128,) + (512,) * 6]
    work_shapes, work_specs = [s for s, _ in work], [p for _, p in work]
    rspec = pl.BlockSpec((tm, LANES), lambda b, t: (t, 0))
    prev = () if stacked is None else tuple(stacked)
    n_in = 7
    outs = pl.pallas_call(
        _proj_kernel,
        grid=(B, T // tm),
        in_specs=[tspec(D), const(g), const(w), const(bf_row), rspec, rspec, rspec]
        + [pl.BlockSpec(memory_space=pl.ANY)] * len(prev),
        out_specs=cache_specs + work_specs,
        out_shape=cache_shapes + work_shapes,
        input_output_aliases={n_in + i: i for i in range(len(prev))},
        compiler_params=_cparams(("parallel", "parallel")),
        name="in_proj",
    )(x, g, w, bf_row, *tables, *prev)
    return outs[:N_CACHE_ROWS], outs[N_CACHE_ROWS:]


def _split3(x):
    hi = x.astype(BF16)
    r = x - hi.astype(F32)
    mid = r.astype(BF16)
    lo = (r - mid.astype(F32)).astype(BF16)
    return hi, mid, lo


def _cumsum_kernel(lf_ref, o_ref, *, blk):
    L = lf_ref.shape[-1]
    r = lax.broadcasted_iota(jnp.int32, (blk, blk), 0)
    c = lax.broadcasted_iota(jnp.int32, (blk, blk), 1)
    upper = jnp.where(r <= c, 1.0, 0.0).astype(BF16)
    carry = jnp.zeros((N_HEADS, 1), F32)
    for j in range(L // blk):
        hi, mid, lo = _split3(lf_ref[0, :, j * blk:(j + 1) * blk])
        f = _dot(hi, upper) + _dot(mid, upper) + _dot(lo, upper) + carry
        o_ref[0, :, j * blk:(j + 1) * blk] = f * LOG2E
        carry = f[:, blk - 1:blk]


def _cumsum_time(lf_t):
    B, H, L = lf_t.shape
    blk = LANES
    assert L % blk == 0
    spec = pl.BlockSpec((1, H, L), lambda b: (b, 0, 0))
    return pl.pallas_call(
        functools.partial(_cumsum_kernel, blk=blk),
        grid=(B,),
        in_specs=[spec],
        out_specs=spec,
        out_shape=jax.ShapeDtypeStruct((B, H, L), F32),
        compiler_params=_cparams(("parallel",)),
        name="logf_cumsum",
    )(lf_t)


def _head_column(blk, h):
    lane = lax.broadcasted_iota(jnp.int32, blk.shape, 1)
    return jnp.sum(jnp.where(lane == h, blk, 0.0), axis=1, keepdims=True)


def _positions(tq, tk, q0, k0):
    qpos = q0 + lax.broadcasted_iota(jnp.int32, (tq, tk), 0)
    kpos = k0 + lax.broadcasted_iota(jnp.int32, (tq, tk), 1)
    return qpos, kpos


def _pair_specs(tq, Lp):
    qspec = pl.BlockSpec((1, tq, BRANCH_W), lambda b, qi: (b, qi, 0))
    kspec = pl.BlockSpec((1, Lp, BRANCH_W), lambda b, qi: (b, 0, 0))
    return qspec, kspec


def _pair_lanes(hp):
    return pl.ds(pl.multiple_of(hp * LANES, LANES), LANES)


def _split_past(rest, has_past):
    return (rest[:2], rest[2:-2], rest[-2:]) if has_past else ((), rest, ())


def _key_ops(k_ref, v_ref, past, bufs, tk):
    if not past:
        def qk(qmat, hp, k0):
            return _dot_t(qmat, k_ref[0, pl.ds(k0, tk), _pair_lanes(hp)])

        def pv(p, hp, k0, ones):
            v = v_ref[0, pl.ds(k0, tk), _pair_lanes(hp)]
            return _dot(p, v if ones is None else jnp.concatenate([v, ones], axis=1))

        return qk, pv

    P, T = past[0].shape[-1], k_ref.shape[1]
    for new, buf in zip((k_ref, v_ref), bufs):
        buf[0:T, :] = new[0]
        if buf.shape[0] > T:
            buf[T:, :] = jnp.zeros((buf.shape[0] - T, BRANCH_W), BF16)

    def qk(qmat, hp, k0):
        return jnp.concatenate([_dot(qmat, past[0][0, hp].astype(BF16)), _dot_t(qmat, bufs[0][:, _pair_lanes(hp)])],
                               axis=1)

    def pv(p, hp, k0, ones):
        vt, v_tail = past[1][0, hp].astype(BF16), bufs[1][:, _pair_lanes(hp)]
        if ones is not None:
            vt = jnp.concatenate([vt, jnp.ones(vt.shape, BF16)], axis=0)
            v_tail = jnp.concatenate([v_tail, ones[:v_tail.shape[0]]], axis=1)
        return _dot_t(p[:, :P], vt) + _dot(p[:, P:], v_tail)

    return qk, pv


def _past_specs_and_bufs(past, T, Lp):
    if past is None:
        return [], [], ()
    layer, pk, pv = past
    P = pk.shape[-1]
    assert P % LANES == 0 and pk.shape[2:4] == (N_PAIRS, LANES)
    spec = pl.BlockSpec((None, 1, N_PAIRS, LANES, P), lambda b, qi: (layer, b, 0, 0, 0))
    return [spec, spec], [pltpu.VMEM((Lp - P, BRANCH_W), BF16)] * 2, (pk, pv)


def _fox_kernel(q_ref, k_ref, v_ref, *rest, tq, tk, q_off, has_past):
    past, (fq_ref, fk_ref, o_ref, m_ref, acc_ref), bufs = _split_past(rest, has_past)
    qk, pv = _key_ops(k_ref, v_ref, past, bufs, tk)
    q0 = q_off + pl.program_id(1) * tq
    diag = q0 // tk
    fq_blk = fq_ref[0]
    ones = jnp.ones((tk, LANES), BF16)

    group = acc_ref.shape[0]

    def head_group(g, _):
        hps = [g * (group // 2) + pp for pp in range(group // 2)]
        lanes = [_pair_lanes(hp) for hp in hps]
        qm = [_keep_head(q_ref[0, :, lanes[hh // 2]], hh % 2) for hh in range(group)]
        fq = [_head_column(fq_blk, g * group + hh) for hh in range(group)]
        m_ref[...] = jnp.full(m_ref.shape, NEG, F32)
        acc_ref[...] = jnp.zeros(acc_ref.shape, F32)

        def tiles(js, masked):
            k0 = [j * tk if isinstance(j, int) else pl.multiple_of(j * tk, tk) for j in js]
            chains = [(t, hh) for t in range(len(js)) for hh in range(group)]
            if masked:
                qpos, kpos = _positions(tq, tk, q0, k0[0])
                causal = kpos <= qpos
            s = {}
            for t, hh in chains:
                fk = fk_ref[0, g * group + hh, :, pl.ds(k0[t], tk)]
                s[t, hh] = qk(qm[hh], hps[hh // 2], k0[t]) + fq[hh] - fk
                if masked and t == 0:
                    s[t, hh] = jnp.where(causal, s[t, hh], NEG)
            m_old = [m_ref[hh] for hh in range(group)]
            m_new = list(m_old)
            for t, hh in chains:
                m_new[hh] = jnp.maximum(m_new[hh], jnp.max(s[t, hh], axis=1, keepdims=True))
            p = {c: jnp.exp2(s[c] - m_new[c[1]]).astype(BF16) for c in chains}
            for hh in range(group):
                upd = jnp.exp2(m_old[hh] - m_new[hh]) * acc_ref[hh]
                for t in range(len(js)):
                    upd = upd + pv(p[t, hh], hps[hh // 2], k0[t], ones)
                acc_ref[hh] = upd
                m_ref[hh] = m_new[hh]

        tiles([diag], True)

        def body(i, _):
            j = diag - 1 - 2 * i
            tiles([j, j - 1], False)
            return 0

        lax.fori_loop(0, diag // 2, body, 0)

        @pl.when(diag % 2 == 1)
        def _():
            tiles([0], False)
        for pp in range(group // 2):
            o0 = acc_ref[2 * pp, :, :LANES] / acc_ref[2 * pp, :, LANES:]
            o1 = acc_ref[2 * pp + 1, :, :LANES] / acc_ref[2 * pp + 1, :, LANES:]
            o_ref[0, :, lanes[pp]] = _merge_pair(o0, o1).astype(o_ref.dtype)
        return 0

    lax.fori_loop(0, N_HEADS // group, head_group, 0)


def _fox(q, k, v, past, fq, fk_t, *, tq, tk, q_off):
    B, T, _ = q.shape
    Lp = fk_t.shape[-1]
    qspec, kspec = _pair_specs(tq, k.shape[1])
    past_specs, bufs, past = _past_specs_and_bufs(past, T, Lp)
    return pl.pallas_call(
        functools.partial(_fox_kernel, tq=tq, tk=tk, q_off=q_off, has_past=bool(past)),
        grid=(B, T // tq),
        in_specs=[qspec, kspec, kspec] + past_specs
        + [pl.BlockSpec((1, tq, N_HEADS), lambda b, qi: (b, qi, 0)),
           pl.BlockSpec((1, N_HEADS, 1, Lp), lambda b, qi: (b, 0, 0, 0))],
        out_specs=qspec,
        out_shape=jax.ShapeDtypeStruct(q.shape, BF16),
        scratch_shapes=[pltpu.VMEM((FOX_GROUP, tq, 1), F32), pltpu.VMEM((FOX_GROUP, tq, 2 * LANES), F32)] + bufs,
        compiler_params=_cparams(("parallel", "parallel")),
        name="fox_attn",
    )(q, k, v, *past, fq, fk_t[:, :, None, :])


def _sb_kernel(q_ref, k_ref, v_ref, *rest, tq, tk, q_off, has_past):
    past, (o_ref, acc_ref, tail_ref), bufs = _split_past(rest, has_past)
    qk, pv = _key_ops(k_ref, v_ref, past, bufs, tk)
    q0 = q_off + pl.program_id(1) * tq
    diag = q0 // tk
    r = lax.broadcasted_iota(jnp.int32, (LANES, 2 * LANES), 0)
    c = lax.broadcasted_iota(jnp.int32, (LANES, 2 * LANES), 1)
    suffix_w = jnp.where((c >= LANES) | (r > c), 1.0, 0.0).astype(BF16)

    group = acc_ref.shape[0]

    def head_group(g, _):
        hps = [g * (group // 2) + pp for pp in range(group // 2)]
        lanes = [_pair_lanes(hp) for hp in hps]
        qm = [_keep_head(q_ref[0, :, lanes[hh // 2]], hh % 2) for hh in range(group)]
        acc_ref[...] = jnp.zeros(acc_ref.shape, F32)
        tail_ref[...] = jnp.zeros(tail_ref.shape, F32)

        def tiles(js, masked):
            k0 = [j * tk if isinstance(j, int) else pl.multiple_of(j * tk, tk) for j in js]
            chains = [(t, hh) for t in range(len(js)) for hh in range(group)]
            if masked:
                qpos, kpos = _positions(tq, tk, q0, k0[0])
                strict = kpos < qpos
            z = {c: qk(qm[c[1]], hps[c[1] // 2], k0[c[0]]) for c in chains}
            ls, la, after = {}, {}, {}
            for c in chains:
                ls[c] = jnp.minimum(z[c], 0.0) - jnp.log2(1.0 + jnp.exp2(-jnp.abs(z[c])))
                la[c] = ls[c] - z[c]
                if masked and c[0] == 0:
                    la[c] = jnp.where(strict, la[c], 0.0)
                    ls[c] = jnp.where(strict, ls[c], NEG)
                la[c] = la[c].astype(BF16)
            for hh in range(group):
                run = tail_ref[hh]
                for t in range(len(js)):
                    blocks = [None] * (tk // LANES)
                    for b in reversed(range(tk // LANES)):
                        res = _dot(la[t, hh][:, b * LANES:(b + 1) * LANES], suffix_w)
                        blocks[b] = res[:, :LANES] + run
                        run = run + res[:, LANES:]
                    after[t, hh] = jnp.concatenate(blocks, axis=1)
                tail_ref[hh] = run
            a = {c: jnp.exp2(ls[c] + after[c]).astype(BF16) for c in chains}
            for c in chains:
                acc_ref[c[1]] += pv(a[c], hps[c[1] // 2], k0[c[0]], None)

        tiles([diag], True)

        def body(i, _):
            j = diag - 1 - 2 * i
            tiles([j, j - 1], False)
            return 0

        lax.fori_loop(0, diag // 2, body, 0)

        @pl.when(diag % 2 == 1)
        def _():
            tiles([0], False)
        for pp in range(group // 2):
            o_ref[0, :, lanes[pp]] = _merge_pair(acc_ref[2 * pp], acc_ref[2 * pp + 1]).astype(o_ref.dtype)
        return 0

    lax.fori_loop(0, N_HEADS // group, head_group, 0)


def _sb(q, k, v, past, Lp, *, tq, tk, q_off):
    B, T, _ = q.shape
    qspec, kspec = _pair_specs(tq, k.shape[1])
    past_specs, bufs, past = _past_specs_and_bufs(past, T, Lp)
    return pl.pallas_call(
        functools.partial(_sb_kernel, tq=tq, tk=tk, q_off=q_off, has_past=bool(past)),
        grid=(B, T // tq),
        in_specs=[qspec, kspec, kspec] + past_specs,
        out_specs=qspec,
        out_shape=jax.ShapeDtypeStruct(q.shape, BF16),
        scratch_shapes=[pltpu.VMEM((SB_GROUP, tq, LANES), F32), pltpu.VMEM((SB_GROUP, tq, LANES), F32)] + bufs,
        compiler_params=_cparams(("parallel", "parallel")),
        name="sb_attn",
    )(q, k, v, *past)


def _const_key(v):
    bits = int(np.float32(v).view(np.int32))
    return (bits ^ 0x7FFFFFFF) + 1 if bits < 0 else bits


def _float_key(x):
    bits = lax.bitcast_convert_type(x, jnp.int32)
    return jnp.where(bits < 0, jnp.bitwise_xor(bits, 0x7FFFFFFF) + 1, bits)


KEY_MIN = -2 ** 31
KEY_HALF_NEG = _const_key(0.5 * NEG)


KEY_MAX = 2 ** 31 - 1
KEY_BITS = 32
COUNT_CHAINS = 8
SB_GROUP = 4
FOX_GROUP = 2


def _dsa_kernel(aq_ref, iq_ref, iwt_ref, k_ref, vt_ref, ik_ref, o_ref, key_ref, q8_ref, m_ref, acc_ref,
                *, tq, tk, q_off, n_rows, topk):
    q0 = q_off + pl.program_id(1) * tq
    n_tiles = (q0 + n_rows + tk - 1) // tk

    def tile_start(j):
        return pl.multiple_of(j * tk, LANES)

    iq = iq_ref[0]
    iwt = iwt_ref[0]
    qh = [_keep_head(iq[:, (h // 2) * LANES:(h // 2 + 1) * LANES], h % 2) for h in range(N_IDX)]
    q_chunk = jnp.right_shift(q0 + lax.broadcasted_iota(jnp.int32, (tk, tq), 1), CHUNK_SHIFT)

    def score_tile(j, _):
        k0 = tile_start(j)
        ik = ik_ref[0, pl.ds(k0, tk), :]
        sc = jnp.zeros((tk, tq), F32)
        for h in range(N_IDX):
            sc = sc + jnp.maximum(_dot_t(ik, qh[h]), 0.0) * iwt[h:h + 1, :]
        key_chunk = jnp.right_shift(k0 + lax.broadcasted_iota(jnp.int32, (tk, tq), 0), CHUNK_SHIFT)
        key_ref[pl.ds(k0, tk), :] = _float_key(jnp.where(key_chunk <= q_chunk, sc, NEG))
        return 0

    lax.fori_loop(0, n_tiles, score_tile, 0)

    def search():
        def step(_, cr):
            lo, hi, above_hi = cr
            mid = jnp.right_shift(lo, 1) + jnp.right_shift(hi, 1) + jnp.bitwise_and(jnp.bitwise_and(lo, hi), 1)

            def count_tile(j, cnt):
                above = jnp.where(key_ref[pl.ds(tile_start(j), tk), :] > mid, 1.0, 0.0)
                return cnt + jnp.sum(above.reshape((-1,) + cnt.shape), axis=0)

            cnt = lax.fori_loop(0, n_tiles, count_tile, jnp.zeros((COUNT_CHAINS, SUBLANES, tq), F32))
            cnt = jnp.sum(jnp.sum(cnt, axis=0), axis=0, keepdims=True)
            less = cnt < topk
            return jnp.where(less, lo, mid), jnp.where(less, mid, hi), jnp.where(less, cnt, above_hi)

        init = (jnp.full((1, tq), KEY_MIN, jnp.int32), jnp.full((1, tq), KEY_MAX, jnp.int32), jnp.zeros((1, tq), F32))
        _, hi, above_hi = lax.fori_loop(0, KEY_BITS, step, init)
        return hi, topk - above_hi

    thr, room = lax.cond(q0 + n_rows > topk, search,
                         lambda: (jnp.full((1, tq), KEY_MIN, jnp.int32), jnp.zeros((1, tq), F32)))
    row_ok = lax.broadcasted_iota(jnp.int32, (1, tq), 1) < n_rows
    room = jnp.where(row_ok, room, 0.0)

    @pl.when(jnp.max(room) > 0.0)
    def _():
        r = lax.broadcasted_iota(jnp.int32, (LANES, LANES), 0)
        c = lax.broadcasted_iota(jnp.int32, (LANES, LANES), 1)
        before = jnp.where(c < r, 1.0, 0.0).astype(BF16)

        def tie_tile(j, seen):
            k0 = tile_start(j)
            for b in range(tk // LANES):
                rows = pl.ds(k0 + b * LANES, LANES)
                keys = key_ref[rows, :]
                eq = keys == thr
                eqf = jnp.where(eq, 1.0, 0.0)
                rank = _dot(before, eqf.astype(BF16)) + seen
                key_ref[rows, :] = jnp.where(eq, jnp.where(rank < room, keys + 1, keys), keys)
                seen = seen + jnp.sum(eqf, axis=0, keepdims=True)
            return seen

        lax.fori_loop(0, n_tiles, tie_tile, jnp.zeros((1, tq), F32))

    aq = aq_ref[0]
    for h in range(N_HEADS):
        q8_ref[h * tq:(h + 1) * tq, :] = _keep_head(aq[:, (h // 2) * LANES:(h // 2 + 1) * LANES], h % 2)
    m_ref[...] = jnp.full(m_ref.shape, NEG, F32)
    acc_ref[...] = jnp.zeros(acc_ref.shape, F32)
    thr_sel = jnp.maximum(thr, KEY_HALF_NEG)

    def attend_tile(j, _):
        k0 = tile_start(j)
        kt = k_ref[0, pl.ds(k0, tk), :]
        v1t = vt_ref[0, :, pl.ds(k0, tk)]
        bias = jnp.where(key_ref[pl.ds(k0, tk), :] > thr_sel, 0.0, NEG)
        groups = [slice(g * (N_HEADS // 2) * tq, (g + 1) * (N_HEADS // 2) * tq) for g in (0, 1)]
        st = [_dot_t(kt, q8_ref[cols, :]) for cols in groups]
        st = [jnp.concatenate([s[:, h * tq:(h + 1) * tq] + bias for h in range(N_HEADS // 2)], axis=1) for s in st]
        m_old = [m_ref[:, cols] for cols in groups]
        m_new = [jnp.maximum(mo, jnp.max(s, axis=0, keepdims=True)) for mo, s in zip(m_old, st)]
        p = [jnp.exp2(s - mn).astype(BF16) for s, mn in zip(st, m_new)]
        for g, cols in enumerate(groups):
            acc_ref[:, cols] = jnp.exp2(m_old[g] - m_new[g]) * acc_ref[:, cols] + _dot(v1t, p[g])
            m_ref[:, cols] = m_new[g]
        return 0

    lax.fori_loop(0, n_tiles, attend_tile, 0)
    for hp in range(N_PAIRS):
        pair = []
        for h in (2 * hp, 2 * hp + 1):
            a = acc_ref[:, h * tq:(h + 1) * tq]
            pair.append(a[:HEAD_DIM] / a[HEAD_DIM:])
        o_ref[0, :, hp * LANES:(hp + 1) * LANES] = jnp.concatenate(pair, axis=0).T.astype(o_ref.dtype)


def _dsa(aq, iq, iw, k2, v1t, ik2, *, tq, q_off, topk):
    B, T, _ = aq.shape
    Lp = k2.shape[1]
    tk = Lp if T <= LANES else next(t for t in (512, 384, 256, 128) if Lp % t == 0)
    Tp = -(-T // LANES) * LANES
    tq = max(tq, LANES)
    n_rows = min(T, tq)
    pad_t = lambda a: jnp.pad(a, ((0, 0), (0, Tp - T), (0, 0)))
    qspec = lambda w: pl.BlockSpec((1, tq, w), lambda b, qi: (b, qi, 0))
    kspec = pl.BlockSpec((1, Lp, LANES), lambda b, qi: (b, 0, 0))
    out = pl.pallas_call(
        functools.partial(_dsa_kernel, tq=tq, tk=tk, q_off=q_off, n_rows=n_rows, topk=topk),
        grid=(B, Tp // tq),
        in_specs=[qspec(BRANCH_W), qspec(N_IDX * IDX_DIM), pl.BlockSpec((1, SUBLANES, tq), lambda b, qi: (b, 0, qi)),
                  kspec, pl.BlockSpec((1, LANES, Lp), lambda b, qi: (b, 0, 0)), kspec],
        out_specs=qspec(BRANCH_W),
        out_shape=jax.ShapeDtypeStruct((B, Tp, BRANCH_W), BF16),
        scratch_shapes=[pltpu.VMEM((Lp, tq), jnp.int32), pltpu.VMEM((N_HEADS * tq, LANES), BF16),
                        pltpu.VMEM((1, N_HEADS * tq), F32), pltpu.VMEM((LANES, N_HEADS * tq), F32)],
        compiler_params=_cparams(("parallel", "parallel")),
        name="dsa_attn",
    )(pad_t(aq), pad_t(iq), jnp.swapaxes(pad_t(iw), 1, 2), k2, v1t, ik2)
    return out[:, :T]


def _merge_kernel(x_ref, oa_ref, ob_ref, oc_ref, g_ref, wg_ref, wb_ref, wo_ref, y_ref):
    x = x_ref[0]
    D = x.shape[-1]
    xn = _rms(x, g_ref[...]).astype(BF16)
    merged = jnp.zeros(x.shape, F32)
    for n, o_ref in enumerate((oa_ref, ob_ref, oc_ref)):
        gate = _sigmoid(_dot(xn, wg_ref[:, n * D:(n + 1) * D]))
        merged = merged + gate * _dot(o_ref[0], wb_ref[n])
    y_ref[0] = x + _dot(merged.astype(BF16), wo_ref[...])


def _merge(x, oa, ob, oc, g, wg, wb, wo, tm):
    B, T, D = x.shape
    tspec = lambda w_: pl.BlockSpec((1, tm, w_), lambda b, t: (b, t, 0))
    const = lambda a: pl.BlockSpec(a.shape, lambda b, t: (0,) * a.ndim)
    return pl.pallas_call(
        _merge_kernel,
        grid=(B, T // tm),
        in_specs=[tspec(D), tspec(BRANCH_W), tspec(BRANCH_W), tspec(BRANCH_W), const(g), const(wg), const(wb),
                  const(wo)],
        out_specs=tspec(D),
        out_shape=jax.ShapeDtypeStruct(x.shape, F32),
        compiler_params=_cparams(("parallel", "parallel")),
        name="merge_out",
    )(x, oa, ob, oc, g, wg, wb, wo)


def _ffn_kernel(x_ref, g_ref, past_ref, wu_ref, cw_ref, cb_ref, wd_ref, gf_ref, y_ref, st_ref,
                hbuf_ref, carry_ref, acc_ref, *, tm, final_norm):
    fc = FF_CHUNK
    d_ff = wd_ref.shape[0]
    n_chunks = d_ff // fc

    @pl.when(pl.program_id(1) == 0)
    def _():
        carry_ref[...] = past_ref[0]

    x = x_ref[0]
    xn = _rms(x, g_ref[...]).astype(BF16)
    acc_ref[...] = jnp.zeros(acc_ref.shape, F32)

    def up_proj(c):
        return jnp.concatenate([_dot(xn, wu_ref[:, c * fc:(c + 1) * fc]),
                                _dot(xn, wu_ref[:, d_ff + c * fc:d_ff + (c + 1) * fc])], axis=1)

    h_next = up_proj(0)
    for c in range(n_chunks):
        hbuf = hbuf_ref.at[c % 2]
        both = lambda ref: jnp.concatenate([ref[:, c * fc:(c + 1) * fc], ref[:, d_ff + c * fc:d_ff + (c + 1) * fc]],
                                           axis=1)
        h = h_next
        if c + 1 < n_chunks:
            h_next = up_proj(c + 1)
        hbuf[0:SUBLANES, :] = carry_ref[c]
        hbuf[SUBLANES:SUBLANES + tm, :] = h
        carry_ref[c] = h[tm - SUBLANES:tm, :]
        cw = both(cw_ref)
        hc = (both(cb_ref) + cw[0:1] * hbuf[SUBLANES - 2:SUBLANES - 2 + tm, :]
              + cw[1:2] * hbuf[SUBLANES - 1:SUBLANES - 1 + tm, :] + cw[2:3] * h)
        gate, up = hc[:, :fc], hc[:, fc:]
        act = gate * _sigmoid(gate) * up
        acc_ref[...] += _dot(act.astype(BF16), wd_ref[c * fc:(c + 1) * fc, :])
    st_ref[0] = carry_ref[...]
    y = x + acc_ref[...]
    if final_norm:
        y = _rms(y, gf_ref[...])
    y_ref[0] = y


def _ffn(x, g, past, wu, cw, cb, wd, gf, tm, final_norm):
    B, T, D = x.shape
    nc, fc2 = wd.shape[0] // FF_CHUNK, 2 * FF_CHUNK
    tspec = pl.BlockSpec((1, tm, D), lambda b, t: (b, t, 0))
    const = lambda a: pl.BlockSpec(a.shape, lambda b, t: (0,) * a.ndim, pipeline_mode=pl.Buffered(1))
    sspec = pl.BlockSpec((1, nc, SUBLANES, fc2), lambda b, t: (b, 0, 0, 0))
    return pl.pallas_call(
        functools.partial(_ffn_kernel, tm=tm, final_norm=final_norm),
        grid=(B, T // tm),
        in_specs=[tspec, const(g), sspec, const(wu), const(cw), const(cb), const(wd), const(gf)],
        out_specs=[tspec, sspec],
        out_shape=[jax.ShapeDtypeStruct(x.shape, F32), jax.ShapeDtypeStruct((B, nc, SUBLANES, fc2), F32)],
        scratch_shapes=[pltpu.VMEM((2, tm + SUBLANES, fc2), F32), pltpu.VMEM((nc, SUBLANES, fc2), F32),
                        pltpu.VMEM((tm, D), F32)],
        compiler_params=_cparams(("arbitrary", "arbitrary")),
        name="conv_ffn",
    )(x, g, past, wu, cw, cb, wd, gf)


def _rope_tables(pos):
    half = ROT_DIM // 2
    freq = ROPE_THETA ** (-jnp.arange(half, dtype=F32) / half)
    ang = pos.astype(F32)[:, None] * freq[None, :]
    lane = np.arange(LANES) % HEAD_DIM
    cos = jnp.cos(ang)[:, lane % half]
    sin = jnp.sin(ang)[:, lane % half]
    rc = jnp.where(lane < ROT_DIM, cos, 1.0)
    rs1 = jnp.where(lane < half, -sin, 0.0)
    rs2 = jnp.where((lane >= half) & (lane < ROT_DIM), sin, 0.0)
    return rc, rs1, rs2


IN_SIZES = (BRANCH_W, HEAD_DIM, HEAD_DIM, N_IDX * IDX_DIM, IDX_DIM, N_IDX) + (BRANCH_W,) * 6 + (N_HEADS,)
PACKED_W = ROPE_W + MISC_W + 6 * BRANCH_W


def _pack_w_kernel(w_ref, o_ref, s_ref):
    offs = np.concatenate([[0], np.cumsum(IN_SIZES)]).tolist()
    a_q, a_k, a_v, i_q, i_k, i_w, b_q = offs[:7]
    c_f = offs[12]
    src = lambda c0, width: w_ref[0, :, c0:c0 + width]
    s_ref[:, 0:BRANCH_W] = src(a_q, BRANCH_W)
    s_ref[:, BRANCH_W:BRANCH_W + 256] = src(i_q, 256)
    for dst, c0 in ((768, a_k), (896, i_k), (ROPE_W, a_v)):
        s_ref[:, dst:dst + HEAD_DIM] = src(c0, HEAD_DIM)
        s_ref[:, dst + HEAD_DIM:dst + 2 * HEAD_DIM] = src(c0, HEAD_DIM)
    misc = ROPE_W + LANES
    s_ref[:, misc:misc + LANES] = jnp.zeros((s_ref.shape[0], LANES), F32)
    s_ref[:, misc:misc + N_IDX] = src(i_w, N_IDX)
    s_ref[:, misc + CF_LANE:misc + CF_LANE + N_HEADS] = src(c_f, N_HEADS)
    s_ref[:, ROPE_W + MISC_W:PACKED_W] = src(b_q, 6 * BRANCH_W)
    o_ref[...] = s_ref[...].astype(BF16)


def _pack_w(w_in, l):
    D = w_in.shape[1]
    rows = 256
    return pl.pallas_call(
        _pack_w_kernel,
        grid=(D // rows,),
        in_specs=[pl.BlockSpec((1, rows, w_in.shape[2]), lambda r: (l, r, 0))],
        out_specs=pl.BlockSpec((rows, PACKED_W), lambda r: (r, 0)),
        out_shape=jax.ShapeDtypeStruct((D, PACKED_W), BF16),
        scratch_shapes=[pltpu.VMEM((rows, PACKED_W), F32)],
        compiler_params=_cparams(("parallel",)),
        name="pack_w_in",
    )(w_in)


def _pack_layer(l, norm_mix, w_in, b_f, w_branch, w_gate, w_out, norm_ffn, w_up, conv_w, conv_b, w_down):
    w = _pack_w(w_in, l)
    bf_row = jnp.zeros((1, LANES), F32).at[0, CF_LANE:CF_LANE + N_HEADS].set(b_f[l])

    d_ff = w_down.shape[1]
    fc = FF_CHUNK
    nc = d_ff // fc
    assert nc * fc == d_ff
    halves = lambda a: jnp.concatenate([a[..., :d_ff].reshape(a.shape[:-1] + (nc, fc)),
                                        a[..., d_ff:].reshape(a.shape[:-1] + (nc, fc))], axis=-1)
    wu, cw, cb, wd = w_up[l].astype(BF16), conv_w[l], conv_b[l][None], w_down[l].astype(BF16)
    return dict(g_mix=norm_mix[l][None], w=w, bf_row=bf_row, wg=w_gate[l].astype(BF16),
                wb=w_branch[l].astype(BF16), wo=w_out[l].astype(BF16), g_ffn=norm_ffn[l][None],
                wu=wu, cw=cw, cb=cb, wd=wd, halves=halves, nc=nc, d_ff=d_ff)


def _conv_state_in(state, halves):
    st = jnp.moveaxis(halves(state), 2, 1)
    return jnp.pad(st, ((0, 0), (0, 0), (SUBLANES - (CONV_W - 1), 0), (0, 0)))


def _conv_state_out(st, d_ff):
    st = st[:, :, SUBLANES - (CONV_W - 1):, :]
    B, nc, r, fc2 = st.shape
    fc = fc2 // 2
    gate = jnp.moveaxis(st[..., :fc], 1, 2).reshape(B, r, d_ff)
    up = jnp.moveaxis(st[..., fc:], 1, 2).reshape(B, r, d_ff)
    return jnp.concatenate([gate, up], axis=-1)


def _pad_keys(a, Lp):
    return jnp.pad(a, ((0, 0), (0, Lp - a.shape[1]), (0, 0)))


def _group_layer(x, past, lw, gf, layer, depth, stacked, tiles):
    B, T, D = x.shape
    P = 0 if past is None else past[0].shape[1]
    L = P + T
    Lp = -(-L // KEY_TILE) * KEY_TILE
    topk = max(1, min(TOPK_MAX, L // 4))
    tables = _rope_tables(jnp.arange(P, L))

    stacked, (iw, aq16, iq16, ak16, avt16, ik16, bq16, bk16, bv16, cq16, ck16, cv16) = _project(
        x, lw["g_mix"], lw["w"], lw["bf_row"], tables, tiles["proj"], layer, depth, stacked)
    lf = stacked[-1][layer]

    if past is None:
        keys = (ak16, ik16)
        lf_all = lf
        conv_in = jnp.zeros((B, lw["nc"], SUBLANES, 2 * FF_CHUNK), F32)
        past_b = past_c = None
    else:
        p_ak, p_av, p_ik, p_bk, p_bv, p_ck, p_cv, p_lf, p_conv = past
        dup = lambda a: jnp.concatenate([a, a], axis=-1).astype(BF16)
        keys = tuple(jnp.concatenate([o, n], axis=1) for o, n in zip((dup(p_ak), dup(p_ik)), (ak16, ik16)))
        old_vt = jnp.swapaxes(p_av, 1, 2).astype(BF16)
        avt16 = jnp.concatenate([jnp.concatenate([old_vt, jnp.ones_like(old_vt)], axis=1), avt16], axis=2)
        lf_all = jnp.concatenate([p_lf, lf], axis=1)
        conv_in = _conv_state_in(p_conv, lw["halves"])
        past_b, past_c = (layer, p_bk, p_bv), (layer, p_ck, p_cv)
    ak2, ik2 = (_pad_keys(a, Lp) for a in keys)
    v1t = jnp.pad(avt16, ((0, 0), (0, 0), (0, Lp - L)))

    fk_t = _cumsum_time(jnp.swapaxes(_pad_keys(lf_all, Lp), 1, 2))
    fq = jnp.swapaxes(fk_t[:, :, P:L], 1, 2)

    o_a = _dsa(aq16, iq16, iw, ak2, v1t, ik2, tq=tiles["dsa"], q_off=P, topk=topk)
    o_b = _sb(bq16, bk16, bv16, past_b, Lp, tq=tiles["sb"][0], tk=tiles["sb"][1], q_off=P)
    o_c = _fox(cq16, ck16, cv16, past_c, fq, fk_t, tq=tiles["fox"][0], tk=tiles["fox"][1], q_off=P)

    x = _merge(x, o_a, o_b, o_c, lw["g_mix"], lw["wg"], lw["wb"], lw["wo"], tiles["merge"])
    x, st = _ffn(x, lw["g_ffn"], conv_in, lw["wu"], lw["cw"], lw["cb"], lw["wd"], gf, tiles["ffn"],
                 layer == depth - 1)
    return x, stacked, _conv_state_out(st, lw["d_ff"])


def _tiles(T, P):
    pick = lambda want: min(want, T)
    if T % 512 == 0 and P % 512 == 0:
        sb, fox = (256, 256), (256, 512)
    else:
        assert T <= KEY_TILE
        sb = fox = (T, -(-(P + T) // KEY_TILE) * KEY_TILE)
    return dict(proj=pick(512), dsa=pick(128), sb=sb, fox=fox, merge=pick(512), ffn=pick(512))


def kernel(x_prompt, x_sample, cache_a_k, cache_a_v, cache_a_idx_k, cache_b_k, cache_b_v, cache_c_k, cache_c_v,
           cache_c_logf, state_ffn_conv, norm_mix, w_in, b_f, w_branch, w_gate, w_out, norm_ffn, w_up, conv_w,
           conv_b, w_down, norm_final):
    depth = w_in.shape[0]
    caches = (cache_a_k, cache_a_v, cache_a_idx_k, cache_b_k, cache_b_v, cache_c_k, cache_c_v, cache_c_logf,
              state_ffn_conv)
    layers = [_pack_layer(l, norm_mix, w_in, b_f, w_branch, w_gate, w_out, norm_ffn, w_up, conv_w, conv_b, w_down)
              for l in range(depth)]
    gf = norm_final[None]

    def trunk(x, past):
        B, T, _ = x.shape
        tiles = _tiles(T, 0 if past is None else past[0].shape[2])
        stacked, conv_states = None, []
        if past is not None:
            P = past[0].shape[2]
            wide = lambda c: jnp.transpose(c, (0, 1, 3, 4, 2)).reshape(depth, B, N_PAIRS, LANES, P)
        for l in range(depth):
            lp = None if past is None else tuple(wide(c) if 3 <= i <= 6 else c[l] for i, c in enumerate(past))
            x, stacked, conv_state = _group_layer(x, lp, layers[l], gf, l, depth, stacked, tiles)
            conv_states.append(conv_state)
        ak, av, ik, bk, bv, ck, cv, lf = stacked
        heads = lambda a: a.reshape(depth, B, T, N_HEADS, HEAD_DIM)
        return x, (ak, av, ik, heads(bk), heads(bv), heads(ck), heads(cv), lf, jnp.stack(conv_states, axis=0))

    y_prompt, p_state = trunk(x_prompt, None)
    y_sample, s_state = trunk(x_sample, caches)
    return (y_prompt, y_sample) + p_state + s_state
```

```python
import functools

import jax
import jax.numpy as jnp
import numpy as np
from jax import lax
from jax.experimental import pallas as pl
from jax.experimental.pallas import tpu as pltpu

F32 = jnp.float32
BF16 = jnp.bfloat16

LANES = 128
SUBLANES = 8
HEAD_DIM = 64
N_HEADS = 8
N_PAIRS = N_HEADS // 2
BRANCH_W = N_HEADS * HEAD_DIM
N_BRANCH = 3
N_IDX = 4
IDX_DIM = 64
CHUNK = 64
CHUNK_SHIFT = 6
TOPK_MAX = 256
ROT_DIM = HEAD_DIM // 4
ROPE_THETA = 500000.0
CONV_W = 3
EPS = 1e-6
NEG = -1e30
KEY_TILE = 128
FF_CHUNK = 256
VMEM_LIMIT = 56 * 1024 * 1024

ROPE_W = 1024
MISC_W = 256
CF_LANE = 8
LOG2E = 1.4426950408889634
Q_SCALE = HEAD_DIM ** -0.5 * LOG2E


def _cparams(sem):
    return pltpu.CompilerParams(dimension_semantics=sem, vmem_limit_bytes=VMEM_LIMIT)


def _rms(x, g):
    return x * lax.rsqrt(jnp.mean(x * x, axis=-1, keepdims=True) + EPS) * g


def _sigmoid(z):
    return 1.0 / (1.0 + jnp.exp(-z))


def _log_sigmoid(z):
    return jnp.minimum(z, 0.0) - jnp.log1p(jnp.exp(-jnp.abs(z)))


def _dot_t(a, b):
    return lax.dot_general(a, b, (((1,), (1,)), ((), ())), preferred_element_type=F32)


def _dot(a, b):
    return jnp.dot(a, b, preferred_element_type=F32)


def _keep_head(blk, which):
    lane = lax.broadcasted_iota(jnp.int32, blk.shape, blk.ndim - 1)
    keep = (lane >= HEAD_DIM) if which else (lane < HEAD_DIM)
    return jnp.where(keep, blk, jnp.zeros_like(blk))


def _merge_pair(o0, o1):
    lane = lax.broadcasted_iota(jnp.int32, o0.shape, o0.ndim - 1)
    return jnp.where(lane < HEAD_DIM, o0, o1)


N_CACHE_ROWS = 8


def _proj_kernel(x_ref, g_ref, w_ref, bf_ref, rc_ref, rs1_ref, rs2_ref, *refs):
    (ak_ref, av_ref, ik_ref, bk_ref, bv_ref, ck_ref, cv_ref, lf_ref, iw_ref,
     aq16, iq16, ak16, avt16, ik16, bq16, bk16, bv16, cq16, ck16, cv16) = refs[-(N_CACHE_ROWS + 12):]
    xn = _rms(x_ref[0], g_ref[...]).astype(BF16)
    rc, rs1, rs2 = rc_ref[...], rs1_ref[...], rs2_ref[...]

    def cols(c0, width):
        return _dot(xn, w_ref[:, c0:c0 + width])

    def group(blk, g):
        return blk[:, g * LANES:(g + 1) * LANES]

    def rope(blk):
        half = ROT_DIM // 2
        return blk * rc + pltpu.roll(blk, LANES - half, 1) * rs1 + pltpu.roll(blk, half, 1) * rs2

    wide = cols(0, BRANCH_W)
    for g in range(4):
        aq16[0, :, g * LANES:(g + 1) * LANES] = (rope(group(wide, g)) * Q_SCALE).astype(BF16)
    wide = cols(BRANCH_W, ROPE_W - BRANCH_W)
    for g in range(2):
        iq16[0, :, g * LANES:(g + 1) * LANES] = (rope(group(wide, g)) * IDX_DIM ** -0.5).astype(BF16)
    blk = rope(group(wide, 2))
    ak_ref[0, 0] = blk[:, :HEAD_DIM]
    ak16[0] = blk.astype(BF16)
    blk = rope(group(wide, 3))
    ik_ref[0, 0] = blk[:, :IDX_DIM]
    ik16[0] = blk.astype(BF16)

    wide = cols(ROPE_W, MISC_W)
    blk = group(wide, 0)
    av_ref[0, 0] = blk[:, :HEAD_DIM]
    row = lax.broadcasted_iota(jnp.int32, (LANES, blk.shape[0]), 0)
    avt16[0] = jnp.where(row < HEAD_DIM, blk.T, 1.0).astype(BF16)
    blk = group(wide, 1)
    iw_ref[0] = blk[:, :SUBLANES] * N_IDX ** -0.5
    lf_ref[0, 0] = _log_sigmoid(blk[:, CF_LANE:CF_LANE + N_HEADS] + bf_ref[:, CF_LANE:CF_LANE + N_HEADS])

    base = ROPE_W + MISC_W
    outs = ((None, bq16), (bk_ref, bk16), (bv_ref, bv16), (None, cq16), (ck_ref, ck16), (cv_ref, cv16))
    for n, (o32, o16) in enumerate(outs):
        wide = cols(base + n * BRANCH_W, BRANCH_W)
        if o32 is None:
            o16[0] = (wide * Q_SCALE).astype(BF16)
        else:
            o32[0, 0] = wide
            o16[0] = wide.astype(BF16)


def _project(x, g, w, bf_row, tables, tm, layer, depth, stacked):
    B, T, D = x.shape
    tok = lambda w_, dt: jax.ShapeDtypeStruct((B, T, w_), dt)
    tspec = lambda w_: pl.BlockSpec((1, tm, w_), lambda b, t: (b, t, 0))
    const = lambda a: pl.BlockSpec(a.shape, lambda b, t: (0,) * a.ndim, pipeline_mode=pl.Buffered(1))
    cache_w = (64, 64, 64, 512, 512, 512, 512, 8)
    cache_shapes = [jax.ShapeDtypeStruct((depth, B, T, w_), F32) for w_ in cache_w]
    cache_specs = [pl.BlockSpec((1, 1, tm, w_), lambda b, t: (layer, b, t, 0)) for w_ in cache_w]
    work = [(tok(w_, dt), tspec(w_)) for w_, dt in ((8, F32), (512, BF16), (256, BF16), (128, BF16))]
    work += [(jax.ShapeDtypeStruct((B, LANES, T), BF16), pl.BlockSpec((1, LANES, tm), lambda b, t: (b, 0, t)))]
    work += [(tok(w_, BF16), tspec(w_)) for w_ in (128,) + (512,) * 6]
    work_shapes, work_specs = [s for s, _ in work], [p for _, p in work]
    rspec = pl.BlockSpec((tm, LANES), lambda b, t: (t, 0))
    prev = () if stacked is None else tuple(stacked)
    n_in = 7
    outs = pl.pallas_call(
        _proj_kernel,
        grid=(B, T // tm),
        in_specs=[tspec(D), const(g), const(w), const(bf_row), rspec, rspec, rspec]
        + [pl.BlockSpec(memory_space=pl.ANY)] * len(prev),
        out_specs=cache_specs + work_specs,
        out_shape=cache_shapes + work_shapes,
        input_output_aliases={n_in + i: i for i in range(len(prev))},
        compiler_params=_cparams(("parallel", "parallel")),
        name="in_proj",
    )(x, g, w, bf_row, *tables, *prev)
    return outs[:N_CACHE_ROWS], outs[N_CACHE_ROWS:]


def _split3(x):
    hi = x.astype(BF16)
    r = x - hi.astype(F32)
    mid = r.astype(BF16)
    lo = (r - mid.astype(F32)).astype(BF16)
    return hi, mid, lo


def _cumsum_kernel(lf_ref, o_ref, *, blk):
    L = lf_ref.shape[-1]
    r = lax.broadcasted_iota(jnp.int32, (blk, blk), 0)
    c = lax.broadcasted_iota(jnp.int32, (blk, blk), 1)
    upper = jnp.where(r <= c, 1.0, 0.0).astype(BF16)
    carry = jnp.zeros((N_HEADS, 1), F32)
    for j in range(L // blk):
        hi, mid, lo = _split3(lf_ref[0, :, j * blk:(j + 1) * blk])
        f = _dot(hi, upper) + _dot(mid, upper) + _dot(lo, upper) + carry
        o_ref[0, :, j * blk:(j + 1) * blk] = f * LOG2E
        carry = f[:, blk - 1:blk]


def _cumsum_time(lf_t):
    B, H, L = lf_t.shape
    blk = LANES
    assert L % blk == 0
    spec = pl.BlockSpec((1, H, L), lambda b: (b, 0, 0))
    return pl.pallas_call(
        functools.partial(_cumsum_kernel, blk=blk),
        grid=(B,),
        in_specs=[spec],
        out_specs=spec,
        out_shape=jax.ShapeDtypeStruct((B, H, L), F32),
        compiler_params=_cparams(("parallel",)),
        name="logf_cumsum",
    )(lf_t)


def _head_column(blk, h):
    lane = lax.broadcasted_iota(jnp.int32, blk.shape, 1)
    return jnp.sum(jnp.where(lane == h, blk, 0.0), axis=1, keepdims=True)


def _positions(tq, tk, q0, k0):
    qpos = q0 + lax.broadcasted_iota(jnp.int32, (tq, tk), 0)
    kpos = k0 + lax.broadcasted_iota(jnp.int32, (tq, tk), 1)
    return qpos, kpos


def _pair_specs(tq, Lp):
    qspec = pl.BlockSpec((1, tq, BRANCH_W), lambda b, qi: (b, qi, 0))
    kspec = pl.BlockSpec((1, Lp, BRANCH_W), lambda b, qi: (b, 0, 0))
    return qspec, kspec


def _pair_lanes(hp):
    return pl.ds(pl.multiple_of(hp * LANES, LANES), LANES)


def _split_past(rest, has_past):
    return (rest[:2], rest[2:-2], rest[-2:]) if has_past else ((), rest, ())


def _key_ops(k_ref, v_ref, past, bufs, tk):
    if not past:
        def qk(qmat, hp, k0):
            return _dot_t(qmat, k_ref[0, pl.ds(k0, tk), _pair_lanes(hp)])

        def pv(p, hp, k0, ones):
            v = v_ref[0, pl.ds(k0, tk), _pair_lanes(hp)]
            return _dot(p, v if ones is None else jnp.concatenate([v, ones], axis=1))

        return qk, pv

    P, T = past[0].shape[-1], k_ref.shape[1]
    for new, buf in zip((k_ref, v_ref), bufs):
        buf[0:T, :] = new[0]
        if buf.shape[0] > T:
            buf[T:, :] = jnp.zeros((buf.shape[0] - T, BRANCH_W), BF16)

    def qk(qmat, hp, k0):
        return jnp.concatenate([_dot(qmat, past[0][0, hp].astype(BF16)), _dot_t(qmat, bufs[0][:, _pair_lanes(hp)])],
                               axis=1)

    def pv(p, hp, k0, ones):
        vt, v_tail = past[1][0, hp].astype(BF16), bufs[1][:, _pair_lanes(hp)]
        if ones is not None:
            vt = jnp.concatenate([vt, jnp.ones(vt.shape, BF16)], axis=0)
            v_tail = jnp.concatenate([v_tail, ones[:v_tail.shape[0]]], axis=1)
        return _dot_t(p[:, :P], vt) + _dot(p[:, P:], v_tail)

    return qk, pv


def _past_specs_and_bufs(past, T, Lp):
    if past is None:
        return [], [], ()
    layer, pk, pv = past
    P = pk.shape[-1]
    assert P % LANES == 0 and pk.shape[2:4] == (N_PAIRS, LANES)
    spec = pl.BlockSpec((None, 1, N_PAIRS, LANES, P), lambda b, qi: (layer, b, 0, 0, 0))
    return [spec, spec], [pltpu.VMEM((Lp - P, BRANCH_W), BF16)] * 2, (pk, pv)


def _fox_kernel(q_ref, k_ref, v_ref, *rest, tq, tk, q_off, has_past):
    past, (fq_ref, fk_ref, o_ref, m_ref, acc_ref), bufs = _split_past(rest, has_past)
    qk, pv = _key_ops(k_ref, v_ref, past, bufs, tk)
    q0 = q_off + pl.program_id(1) * tq
    diag = q0 // tk
    fq_blk = fq_ref[0]
    ones = jnp.ones((tk, LANES), BF16)

    group = acc_ref.shape[0]

    def head_group(g, _):
        hps = [g * (group // 2) + pp for pp in range(group // 2)]
        lanes = [_pair_lanes(hp) for hp in hps]
        qm = [_keep_head(q_ref[0, :, lanes[hh // 2]], hh % 2) for hh in range(group)]
        fq = [_head_column(fq_blk, g * group + hh) for hh in range(group)]
        m_ref[...] = jnp.full(m_ref.shape, NEG, F32)
        acc_ref[...] = jnp.zeros(acc_ref.shape, F32)

        def tiles(js, masked):
            k0 = [j * tk if isinstance(j, int) else pl.multiple_of(j * tk, tk) for j in js]
            chains = [(t, hh) for t in range(len(js)) for hh in range(group)]
            if masked:
                qpos, kpos = _positions(tq, tk, q0, k0[0])
                causal = kpos <= qpos
            s = {}
            for t, hh in chains:
                fk = fk_ref[0, g * group + hh, :, pl.ds(k0[t], tk)]
                s[t, hh] = qk(qm[hh], hps[hh // 2], k0[t]) + fq[hh] - fk
                if masked and t == 0:
                    s[t, hh] = jnp.where(causal, s[t, hh], NEG)
            m_old = [m_ref[hh] for hh in range(group)]
            m_new = list(m_old)
            for t, hh in chains:
                m_new[hh] = jnp.maximum(m_new[hh], jnp.max(s[t, hh], axis=1, keepdims=True))
            p = {c: jnp.exp2(s[c] - m_new[c[1]]).astype(BF16) for c in chains}
            for hh in range(group):
                upd = jnp.exp2(m_old[hh] - m_new[hh]) * acc_ref[hh]
                for t in range(len(js)):
                    upd = upd + pv(p[t, hh], hps[hh // 2], k0[t], ones)
                acc_ref[hh] = upd
                m_ref[hh] = m_new[hh]

        tiles([diag], True)

        def body(i, _):
            j = diag - 1 - 2 * i
            tiles([j, j - 1], False)
            return 0

        lax.fori_loop(0, diag // 2, body, 0)

        @pl.when(diag % 2 == 1)
        def _():
            tiles([0], False)
        for pp in range(group // 2):
            o0 = acc_ref[2 * pp, :, :LANES] / acc_ref[2 * pp, :, LANES:]
            o1 = acc_ref[2 * pp + 1, :, :LANES] / acc_ref[2 * pp + 1, :, LANES:]
            o_ref[0, :, lanes[pp]] = _merge_pair(o0, o1).astype(o_ref.dtype)
        return 0

    lax.fori_loop(0, N_HEADS // group, head_group, 0)


def _fox(q, k, v, past, fq, fk_t, *, tq, tk, q_off):
    B, T, _ = q.shape
    Lp = fk_t.shape[-1]
    qspec, kspec = _pair_specs(tq, k.shape[1])
    past_specs, bufs, past = _past_specs_and_bufs(past, T, Lp)
    return pl.pallas_call(
        functools.partial(_fox_kernel, tq=tq, tk=tk, q_off=q_off, has_past=bool(past)),
        grid=(B, T // tq),
        in_specs=[qspec, kspec, kspec] + past_specs
        + [pl.BlockSpec((1, tq, N_HEADS), lambda b, qi: (b, qi, 0)),
           pl.BlockSpec((1, N_HEADS, 1, Lp), lambda b, qi: (b, 0, 0, 0))],
        out_specs=qspec,
        out_shape=jax.ShapeDtypeStruct(q.shape, BF16),
        scratch_shapes=[pltpu.VMEM((FOX_GROUP, tq, 1), F32), pltpu.VMEM((FOX_GROUP, tq, 2 * LANES), F32)] + bufs,
        compiler_params=_cparams(("parallel", "parallel")),
        name="fox_attn",
    )(q, k, v, *past, fq, fk_t[:, :, None, :])


def _sb_kernel(q_ref, k_ref, v_ref, *rest, tq, tk, q_off, has_past):
    past, (o_ref, acc_ref, tail_ref), bufs = _split_past(rest, has_past)
    qk, pv = _key_ops(k_ref, v_ref, past, bufs, tk)
    q0 = q_off + pl.program_id(1) * tq
    diag = q0 // tk
    r = lax.broadcasted_iota(jnp.int32, (LANES, 2 * LANES), 0)
    c = lax.broadcasted_iota(jnp.int32, (LANES, 2 * LANES), 1)
    suffix_w = jnp.where((c >= LANES) | (r > c), 1.0, 0.0).astype(BF16)

    group = acc_ref.shape[0]

    def head_group(g, _):
        hps = [g * (group // 2) + pp for pp in range(group // 2)]
        lanes = [_pair_lanes(hp) for hp in hps]
        qm = [_keep_head(q_ref[0, :, lanes[hh // 2]], hh % 2) for hh in range(group)]
        acc_ref[...] = jnp.zeros(acc_ref.shape, F32)
        tail_ref[...] = jnp.zeros(tail_ref.shape, F32)

        def tiles(js, masked):
            k0 = [j * tk if isinstance(j, int) else pl.multiple_of(j * tk, tk) for j in js]
            chains = [(t, hh) for t in range(len(js)) for hh in range(group)]
            if masked:
                qpos, kpos = _positions(tq, tk, q0, k0[0])
                strict = kpos < qpos
            z = {c: qk(qm[c[1]], hps[c[1] // 2], k0[c[0]]) for c in chains}
            ls, la, after = {}, {}, {}
            for c in chains:
                ls[c] = jnp.minimum(z[c], 0.0) - jnp.log2(1.0 + jnp.exp2(-jnp.abs(z[c])))
                la[c] = ls[c] - z[c]
                if masked and c[0] == 0:
                    la[c] = jnp.where(strict, la[c], 0.0)
                    ls[c] = jnp.where(strict, ls[c], NEG)
                la[c] = la[c].astype(BF16)
            for hh in range(group):
                run = tail_ref[hh]
                for t in range(len(js)):
                    blocks = [None] * (tk // LANES)
                    for b in reversed(range(tk // LANES)):
                        res = _dot(la[t, hh][:, b * LANES:(b + 1) * LANES], suffix_w)
                        blocks[b] = res[:, :LANES] + run
                        run = run + res[:, LANES:]
                    after[t, hh] = jnp.concatenate(blocks, axis=1)
                tail_ref[hh] = run
            a = {c: jnp.exp2(ls[c] + after[c]).astype(BF16) for c in chains}
            for c in chains:
                acc_ref[c[1]] += pv(a[c], hps[c[1] // 2], k0[c[0]], None)

        tiles([diag], True)

        def body(i, _):
            j = diag - 1 - 2 * i
            tiles([j, j - 1], False)
            return 0

        lax.fori_loop(0, diag // 2, body, 0)

        @pl.when(diag % 2 == 1)
        def _():
            tiles([0], False)
        for pp in range(group // 2):
            o_ref[0, :, lanes[pp]] = _merge_pair(acc_ref[2 * pp], acc_ref[2 * pp + 1]).astype(o_ref.dtype)
        return 0

    lax.fori_loop(0, N_HEADS // group, head_group, 0)


def _sb(q, k, v, past, Lp, *, tq, tk, q_off):
    B, T, _ = q.shape
    qspec, kspec = _pair_specs(tq, k.shape[1])
    past_specs, bufs, past = _past_specs_and_bufs(past, T, Lp)
    return pl.pallas_call(
        functools.partial(_sb_kernel, tq=tq, tk=tk, q_off=q_off, has_past=bool(past)),
        grid=(B, T // tq),
        in_specs=[qspec, kspec, kspec] + past_specs,
        out_specs=qspec,
        out_shape=jax.ShapeDtypeStruct(q.shape, BF16),
        scratch_shapes=[pltpu.VMEM((SB_GROUP, tq, LANES), F32), pltpu.VMEM((SB_GROUP, tq, LANES), F32)] + bufs,
        compiler_params=_cparams(("parallel", "parallel")),
        name="sb_attn",
    )(q, k, v, *past)


def _const_key(v):
    bits = int(np.float32(v).view(np.int32))
    return (bits ^ 0x7FFFFFFF) + 1 if bits < 0 else bits


def _float_key(x):
    bits = lax.bitcast_convert_type(x, jnp.int32)
    return jnp.where(bits < 0, jnp.bitwise_xor(bits, 0x7FFFFFFF) + 1, bits)


KEY_MIN = -2 ** 31
KEY_HALF_NEG = _const_key(0.5 * NEG)


KEY_MAX = 2 ** 31 - 1
KEY_BITS = 32
COUNT_CHAINS = 8
SB_GROUP = 8
FOX_GROUP = 2


def _dsa_kernel(aq_ref, iq_ref, iwt_ref, k_ref, vt_ref, ik_ref, o_ref, key_ref, q8_ref, m_ref, acc_ref,
                *, tq, tk, q_off, n_rows, topk):
    q0 = q_off + pl.program_id(1) * tq
    n_tiles = (q0 + n_rows + tk - 1) // tk

    def tile_start(j):
        return pl.multiple_of(j * tk, LANES)

    iq = iq_ref[0]
    iwt = iwt_ref[0]
    qh = [_keep_head(iq[:, (h // 2) * LANES:(h // 2 + 1) * LANES], h % 2) for h in range(N_IDX)]
    q_chunk = jnp.right_shift(q0 + lax.broadcasted_iota(jnp.int32, (tk, tq), 1), CHUNK_SHIFT)

    def score_tile(j, _):
        k0 = tile_start(j)
        ik = ik_ref[0, pl.ds(k0, tk), :]
        sc = jnp.zeros((tk, tq), F32)
        for h in range(N_IDX):
            sc = sc + jnp.maximum(_dot_t(ik, qh[h]), 0.0) * iwt[h:h + 1, :]
        key_chunk = jnp.right_shift(k0 + lax.broadcasted_iota(jnp.int32, (tk, tq), 0), CHUNK_SHIFT)
        key_ref[pl.ds(k0, tk), :] = _float_key(jnp.where(key_chunk <= q_chunk, sc, NEG))
        return 0

    lax.fori_loop(0, n_tiles, score_tile, 0)

    def search():
        full_tiles = (q0 + n_rows) // tk
        tail_blocks = (q0 + n_rows - full_tiles * tk + LANES - 1) // LANES

        def step(_, cr):
            lo, hi, above_hi = cr
            mid = jnp.right_shift(lo, 1) + jnp.right_shift(hi, 1) + jnp.bitwise_and(jnp.bitwise_and(lo, hi), 1)

            def count_rows(r0, rows, cnt):
                above = jnp.where(key_ref[pl.ds(pl.multiple_of(r0, LANES), rows), :] > mid, 1.0, 0.0)
                return cnt + jnp.sum(above.reshape((-1,) + cnt.shape), axis=0)

            cnt = lax.fori_loop(0, full_tiles, lambda j, c: count_rows(j * tk, tk, c),
                                jnp.zeros((COUNT_CHAINS, SUBLANES, tq), F32))
            cnt = lax.fori_loop(0, tail_blocks, lambda b, c: count_rows(full_tiles * tk + b * LANES, LANES, c), cnt)
            cnt = jnp.sum(jnp.sum(cnt, axis=0), axis=0, keepdims=True)
            less = cnt < topk
            return jnp.where(less, lo, mid), jnp.where(less, mid, hi), jnp.where(less, cnt, above_hi)

        init = (jnp.full((1, tq), KEY_MIN, jnp.int32), jnp.full((1, tq), KEY_MAX, jnp.int32), jnp.zeros((1, tq), F32))
        _, hi, above_hi = lax.fori_loop(0, KEY_BITS, step, init)
        return hi, topk - above_hi

    thr, room = lax.cond(q0 + n_rows > topk, search,
                         lambda: (jnp.full((1, tq), KEY_MIN, jnp.int32), jnp.zeros((1, tq), F32)))
    row_ok = lax.broadcasted_iota(jnp.int32, (1, tq), 1) < n_rows
    room = jnp.where(row_ok, room, 0.0)

    @pl.when(jnp.max(room) > 0.0)
    def _():
        r = lax.broadcasted_iota(jnp.int32, (LANES, LANES), 0)
        c = lax.broadcasted_iota(jnp.int32, (LANES, LANES), 1)
        before = jnp.where(c < r, 1.0, 0.0).astype(BF16)

        def tie_tile(j, seen):
            k0 = tile_start(j)
            for b in range(tk // LANES):
                rows = pl.ds(k0 + b * LANES, LANES)
                keys = key_ref[rows, :]
                eq = keys == thr
                eqf = jnp.where(eq, 1.0, 0.0)
                rank = _dot(before, eqf.astype(BF16)) + seen
                key_ref[rows, :] = jnp.where(eq, jnp.where(rank < room, keys + 1, keys), keys)
                seen = seen + jnp.sum(eqf, axis=0, keepdims=True)
            return seen

        lax.fori_loop(0, n_tiles, tie_tile, jnp.zeros((1, tq), F32))

    aq = aq_ref[0]
    for h in range(N_HEADS):
        q8_ref[h * tq:(h + 1) * tq, :] = _keep_head(aq[:, (h // 2) * LANES:(h // 2 + 1) * LANES], h % 2)
    m_ref[...] = jnp.full(m_ref.shape, NEG, F32)
    acc_ref[...] = jnp.zeros(acc_ref.shape, F32)
    thr_sel = jnp.maximum(thr, KEY_HALF_NEG)

    def attend_tile(j, _):
        k0 = tile_start(j)
        kt = k_ref[0, pl.ds(k0, tk), :]
        v1t = vt_ref[0, :, pl.ds(k0, tk)]
        bias = jnp.where(key_ref[pl.ds(k0, tk), :] > thr_sel, 0.0, NEG)
        groups = [slice(g * (N_HEADS // 2) * tq, (g + 1) * (N_HEADS // 2) * tq) for g in (0, 1)]
        st = [_dot_t(kt, q8_ref[cols, :]) for cols in groups]
        st = [jnp.concatenate([s[:, h * tq:(h + 1) * tq] + bias for h in range(N_HEADS // 2)], axis=1) for s in st]
        m_old = [m_ref[:, cols] for cols in groups]
        m_new = [jnp.maximum(mo, jnp.max(s, axis=0, keepdims=True)) for mo, s in zip(m_old, st)]
        p = [jnp.exp2(s - mn).astype(BF16) for s, mn in zip(st, m_new)]
        for g, cols in enumerate(groups):
            acc_ref[:, cols] = jnp.exp2(m_old[g] - m_new[g]) * acc_ref[:, cols] + _dot(v1t, p[g])
            m_ref[:, cols] = m_new[g]
        return 0

    lax.fori_loop(0, n_tiles, attend_tile, 0)
    for hp in range(N_PAIRS):
        pair = []
        for h in (2 * hp, 2 * hp + 1):
            a = acc_ref[:, h * tq:(h + 1) * tq]
            pair.append(a[:HEAD_DIM] / a[HEAD_DIM:])
        o_ref[0, :, hp * LANES:(hp + 1) * LANES] = jnp.concatenate(pair, axis=0).T.astype(o_ref.dtype)


def _dsa(aq, iq, iw, k2, v1t, ik2, *, tq, q_off, topk):
    B, T, _ = aq.shape
    Lp = k2.shape[1]
    tk = Lp if T <= LANES else next(t for t in (512, 384, 256, 128) if Lp % t == 0)
    Tp = -(-T // LANES) * LANES
    tq = max(tq, LANES)
    n_rows = min(T, tq)
    pad_t = lambda a: jnp.pad(a, ((0, 0), (0, Tp - T), (0, 0)))
    qspec = lambda w: pl.BlockSpec((1, tq, w), lambda b, qi: (b, qi, 0))
    kspec = pl.BlockSpec((1, Lp, LANES), lambda b, qi: (b, 0, 0))
    out = pl.pallas_call(
        functools.partial(_dsa_kernel, tq=tq, tk=tk, q_off=q_off, n_rows=n_rows, topk=topk),
        grid=(B, Tp // tq),
        in_specs=[qspec(BRANCH_W), qspec(N_IDX * IDX_DIM), pl.BlockSpec((1, SUBLANES, tq), lambda b, qi: (b, 0, qi)),
                  kspec, pl.BlockSpec((1, LANES, Lp), lambda b, qi: (b, 0, 0)), kspec],
        out_specs=qspec(BRANCH_W),
        out_shape=jax.ShapeDtypeStruct((B, Tp, BRANCH_W), BF16),
        scratch_shapes=[pltpu.VMEM((Lp, tq), jnp.int32), pltpu.VMEM((N_HEADS * tq, LANES), BF16),
                        pltpu.VMEM((1, N_HEADS * tq), F32), pltpu.VMEM((LANES, N_HEADS * tq), F32)],
        compiler_params=_cparams(("parallel", "parallel")),
        name="dsa_attn",
    )(pad_t(aq), pad_t(iq), jnp.swapaxes(pad_t(iw), 1, 2), k2, v1t, ik2)
    return out[:, :T]


def _merge_kernel(x_ref, oa_ref, ob_ref, oc_ref, g_ref, wg_ref, wb_ref, wo_ref, y_ref):
    x = x_ref[0]
    D = x.shape[-1]
    xn = _rms(x, g_ref[...]).astype(BF16)
    merged = jnp.zeros(x.shape, F32)
    for n, o_ref in enumerate((oa_ref, ob_ref, oc_ref)):
        gate = _sigmoid(_dot(xn, wg_ref[:, n * D:(n + 1) * D]))
        merged = merged + gate * _dot(o_ref[0], wb_ref[n])
    y_ref[0] = x + _dot(merged.astype(BF16), wo_ref[...])


def _merge(x, oa, ob, oc, g, wg, wb, wo, tm):
    B, T, D = x.shape
    tspec = lambda w_: pl.BlockSpec((1, tm, w_), lambda b, t: (b, t, 0))
    const = lambda a: pl.BlockSpec(a.shape, lambda b, t: (0,) * a.ndim)
    return pl.pallas_call(
        _merge_kernel,
        grid=(B, T // tm),
        in_specs=[tspec(D), tspec(BRANCH_W), tspec(BRANCH_W), tspec(BRANCH_W), const(g), const(wg), const(wb),
                  const(wo)],
        out_specs=tspec(D),
        out_shape=jax.ShapeDtypeStruct(x.shape, F32),
        compiler_params=_cparams(("parallel", "parallel")),
        name="merge_out",
    )(x, oa, ob, oc, g, wg, wb, wo)


def _ffn_kernel(x_ref, g_ref, past_ref, wu_ref, cw_ref, cb_ref, wd_ref, gf_ref, y_ref, st_ref,
                hbuf_ref, carry_ref, acc_ref, *, tm, final_norm):
    fc = FF_CHUNK
    d_ff = wd_ref.shape[0]
    n_chunks = d_ff // fc

    @pl.when(pl.program_id(1) == 0)
    def _():
        carry_ref[...] = past_ref[0]

    x = x_ref[0]
    xn = _rms(x, g_ref[...]).astype(BF16)
    acc_ref[...] = jnp.zeros(acc_ref.shape, F32)

    def up_proj(c):
        return jnp.concatenate([_dot(xn, wu_ref[:, c * fc:(c + 1) * fc]),
                                _dot(xn, wu_ref[:, d_ff + c * fc:d_ff + (c + 1) * fc])], axis=1)

    h_next = up_proj(0)
    for c in range(n_chunks):
        hbuf = hbuf_ref.at[c % 2]
        both = lambda ref: jnp.concatenate([ref[:, c * fc:(c + 1) * fc], ref[:, d_ff + c * fc:d_ff + (c + 1) * fc]],
                                           axis=1)
        h = h_next
        if c + 1 < n_chunks:
            h_next = up_proj(c + 1)
        hbuf[0:SUBLANES, :] = carry_ref[c]
        hbuf[SUBLANES:SUBLANES + tm, :] = h
        carry_ref[c] = h[tm - SUBLANES:tm, :]
        cw = both(cw_ref)
        hc = (both(cb_ref) + cw[0:1] * hbuf[SUBLANES - 2:SUBLANES - 2 + tm, :]
              + cw[1:2] * hbuf[SUBLANES - 1:SUBLANES - 1 + tm, :] + cw[2:3] * h)
        gate, up = hc[:, :fc], hc[:, fc:]
        act = gate * _sigmoid(gate) * up
        acc_ref[...] += _dot(act.astype(BF16), wd_ref[c * fc:(c + 1) * fc, :])
    st_ref[0] = carry_ref[...]
    y = x + acc_ref[...]
    if final_norm:
        y = _rms(y, gf_ref[...])
    y_ref[0] = y


def _ffn(x, g, past, wu, cw, cb, wd, gf, tm, final_norm):
    B, T, D = x.shape
    nc, fc2 = wd.shape[0] // FF_CHUNK, 2 * FF_CHUNK
    tspec = pl.BlockSpec((1, tm, D), lambda b, t: (b, t, 0))
    const = lambda a: pl.BlockSpec(a.shape, lambda b, t: (0,) * a.ndim, pipeline_mode=pl.Buffered(1))
    sspec = pl.BlockSpec((1, nc, SUBLANES, fc2), lambda b, t: (b, 0, 0, 0))
    return pl.pallas_call(
        functools.partial(_ffn_kernel, tm=tm, final_norm=final_norm),
        grid=(B, T // tm),
        in_specs=[tspec, const(g), sspec, const(wu), const(cw), const(cb), const(wd), const(gf)],
        out_specs=[tspec, sspec],
        out_shape=[jax.ShapeDtypeStruct(x.shape, F32), jax.ShapeDtypeStruct((B, nc, SUBLANES, fc2), F32)],
        scratch_shapes=[pltpu.VMEM((2, tm + SUBLANES, fc2), F32), pltpu.VMEM((nc, SUBLANES, fc2), F32),
                        pltpu.VMEM((tm, D), F32)],
        compiler_params=_cparams(("arbitrary", "arbitrary")),
        name="conv_ffn",
    )(x, g, past, wu, cw, cb, wd, gf)


def _rope_tables(pos):
    half = ROT_DIM // 2
    freq = ROPE_THETA ** (-jnp.arange(half, dtype=F32) / half)
    ang = pos.astype(F32)[:, None] * freq[None, :]
    lane = np.arange(LANES) % HEAD_DIM
    cos = jnp.cos(ang)[:, lane % half]
    sin = jnp.sin(ang)[:, lane % half]
    rc = jnp.where(lane < ROT_DIM, cos, 1.0)
    rs1 = jnp.where(lane < half, -sin, 0.0)
    rs2 = jnp.where((lane >= half) & (lane < ROT_DIM), sin, 0.0)
    return rc, rs1, rs2


IN_SIZES = (BRANCH_W, HEAD_DIM, HEAD_DIM, N_IDX * IDX_DIM, IDX_DIM, N_IDX) + (BRANCH_W,) * 6 + (N_HEADS,)
PACKED_W = ROPE_W + MISC_W + 6 * BRANCH_W


def _pack_w_kernel(w_ref, o_ref, s_ref):
    offs = np.concatenate([[0], np.cumsum(IN_SIZES)]).tolist()
    a_q, a_k, a_v, i_q, i_k, i_w, b_q = offs[:7]
    c_f = offs[12]
    src = lambda c0, width: w_ref[0, :, c0:c0 + width]
    s_ref[:, 0:BRANCH_W] = src(a_q, BRANCH_W)
    s_ref[:, BRANCH_W:BRANCH_W + 256] = src(i_q, 256)
    for dst, c0 in ((768, a_k), (896, i_k), (ROPE_W, a_v)):
        s_ref[:, dst:dst + HEAD_DIM] = src(c0, HEAD_DIM)
        s_ref[:, dst + HEAD_DIM:dst + 2 * HEAD_DIM] = src(c0, HEAD_DIM)
    misc = ROPE_W + LANES
    s_ref[:, misc:misc + LANES] = jnp.zeros((s_ref.shape[0], LANES), F32)
    s_ref[:, misc:misc + N_IDX] = src(i_w, N_IDX)
    s_ref[:, misc + CF_LANE:misc + CF_LANE + N_HEADS] = src(c_f, N_HEADS)
    s_ref[:, ROPE_W + MISC_W:PACKED_W] = src(b_q, 6 * BRANCH_W)
    o_ref[...] = s_ref[...].astype(BF16)


def _pack_w(w_in, l):
    D = w_in.shape[1]
    rows = 256
    return pl.pallas_call(
        _pack_w_kernel,
        grid=(D // rows,),
        in_specs=[pl.BlockSpec((1, rows, w_in.shape[2]), lambda r: (l, r, 0))],
        out_specs=pl.BlockSpec((rows, PACKED_W), lambda r: (r, 0)),
        out_shape=jax.ShapeDtypeStruct((D, PACKED_W), BF16),
        scratch_shapes=[pltpu.VMEM((rows, PACKED_W), F32)],
        compiler_params=_cparams(("parallel",)),
        name="pack_w_in",
    )(w_in)


def _pack_layer(l, norm_mix, w_in, b_f, w_branch, w_gate, w_out, norm_ffn, w_up, conv_w, conv_b, w_down):
    w = _pack_w(w_in, l)
    bf_row = jnp.zeros((1, LANES), F32).at[0, CF_LANE:CF_LANE + N_HEADS].set(b_f[l])

    d_ff = w_down.shape[1]
    fc = FF_CHUNK
    nc = d_ff // fc
    assert nc * fc == d_ff
    halves = lambda a: jnp.concatenate([a[..., :d_ff].reshape(a.shape[:-1] + (nc, fc)),
                                        a[..., d_ff:].reshape(a.shape[:-1] + (nc, fc))], axis=-1)
    wu, cw, cb, wd = w_up[l].astype(BF16), conv_w[l], conv_b[l][None], w_down[l].astype(BF16)
    return dict(g_mix=norm_mix[l][None], w=w, bf_row=bf_row, wg=w_gate[l].astype(BF16),
                wb=w_branch[l].astype(BF16), wo=w_out[l].astype(BF16), g_ffn=norm_ffn[l][None],
                wu=wu, cw=cw, cb=cb, wd=wd, halves=halves, nc=nc, d_ff=d_ff)


def _conv_state_in(state, halves):
    st = jnp.moveaxis(halves(state), 2, 1)
    return jnp.pad(st, ((0, 0), (0, 0), (SUBLANES - (CONV_W - 1), 0), (0, 0)))


def _conv_state_out(st, d_ff):
    st = st[:, :, SUBLANES - (CONV_W - 1):, :]
    B, nc, r, fc2 = st.shape
    fc = fc2 // 2
    gate = jnp.moveaxis(st[..., :fc], 1, 2).reshape(B, r, d_ff)
    up = jnp.moveaxis(st[..., fc:], 1, 2).reshape(B, r, d_ff)
    return jnp.concatenate([gate, up], axis=-1)


def _pad_keys(a, Lp):
    return jnp.pad(a, ((0, 0), (0, Lp - a.shape[1]), (0, 0)))


def _group_layer(x, past, lw, gf, layer, depth, stacked, tiles):
    B, T, D = x.shape
    P = 0 if past is None else past[0].shape[1]
    L = P + T
    Lp = -(-L // KEY_TILE) * KEY_TILE
    topk = max(1, min(TOPK_MAX, L // 4))
    tables = _rope_tables(jnp.arange(P, L))

    stacked, (iw, aq16, iq16, ak16, avt16, ik16, bq16, bk16, bv16, cq16, ck16, cv16) = _project(
        x, lw["g_mix"], lw["w"], lw["bf_row"], tables, tiles["proj"], layer, depth, stacked)
    lf = stacked[-1][layer]

    if past is None:
        keys = (ak16, ik16)
        lf_all = lf
        conv_in = jnp.zeros((B, lw["nc"], SUBLANES, 2 * FF_CHUNK), F32)
        past_b = past_c = None
    else:
        p_ak, p_av, p_ik, p_bk, p_bv, p_ck, p_cv, p_lf, p_conv = past
        dup = lambda a: jnp.concatenate([a, a], axis=-1).astype(BF16)
        keys = tuple(jnp.concatenate([o, n], axis=1) for o, n in zip((dup(p_ak), dup(p_ik)), (ak16, ik16)))
        old_vt = jnp.swapaxes(p_av, 1, 2).astype(BF16)
        avt16 = jnp.concatenate([jnp.concatenate([old_vt, jnp.ones_like(old_vt)], axis=1), avt16], axis=2)
        lf_all = jnp.concatenate([p_lf, lf], axis=1)
        conv_in = _conv_state_in(p_conv, lw["halves"])
        past_b, past_c = (layer, p_bk, p_bv), (layer, p_ck, p_cv)
    ak2, ik2 = (_pad_keys(a, Lp) for a in keys)
    v1t = jnp.pad(avt16, ((0, 0), (0, 0), (0, Lp - L)))

    fk_t = _cumsum_time(jnp.swapaxes(_pad_keys(lf_all, Lp), 1, 2))
    fq = jnp.swapaxes(fk_t[:, :, P:L], 1, 2)

    o_a = _dsa(aq16, iq16, iw, ak2, v1t, ik2, tq=tiles["dsa"], q_off=P, topk=topk)
    o_b = _sb(bq16, bk16, bv16, past_b, Lp, tq=tiles["sb"][0], tk=tiles["sb"][1], q_off=P)
    o_c = _fox(cq16, ck16, cv16, past_c, fq, fk_t, tq=tiles["fox"][0], tk=tiles["fox"][1], q_off=P)

    x = _merge(x, o_a, o_b, o_c, lw["g_mix"], lw["wg"], lw["wb"], lw["wo"], tiles["merge"])
    x, st = _ffn(x, lw["g_ffn"], conv_in, lw["wu"], lw["cw"], lw["cb"], lw["wd"], gf, tiles["ffn"],
                 layer == depth - 1)
    return x, stacked, _conv_state_out(st, lw["d_ff"])


def _tiles(T, P):
    pick = lambda want: min(want, T)
    if T % 512 == 0 and P % 512 == 0:
        sb, fox = (256, 256), (256, 512)
    else:
        assert T <= KEY_TILE
        sb = fox = (T, -(-(P + T) // KEY_TILE) * KEY_TILE)
    return dict(proj=pick(512), dsa=pick(128), sb=sb, fox=fox, merge=pick(512), ffn=pick(512))


def kernel(x_prompt, x_sample, cache_a_k, cache_a_v, cache_a_idx_k, cache_b_k, cache_b_v, cache_c_k, cache_c_v,
           cache_c_logf, state_ffn_conv, norm_mix, w_in, b_f, w_branch, w_gate, w_out, norm_ffn, w_up, conv_w,
           conv_b, w_down, norm_final):
    depth = w_in.shape[0]
    caches = (cache_a_k, cache_a_v, cache_a_idx_k, cache_b_k, cache_b_v, cache_c_k, cache_c_v, cache_c_logf,
              state_ffn_conv)
    layers = [_pack_layer(l, norm_mix, w_in, b_f, w_branch, w_gate, w_out, norm_ffn, w_up, conv_w, conv_b, w_down)
              for l in range(depth)]
    gf = norm_final[None]

    def trunk(x, past):
        B, T, _ = x.shape
        tiles = _tiles(T, 0 if past is None else past[0].shape[2])
        stacked, conv_states = None, []
        if past is not None:
            P = past[0].shape[2]
            wide = lambda c: jnp.transpose(c, (0, 1, 3, 4, 2)).reshape(depth, B, N_PAIRS, LANES, P)
        for l in range(depth):
            lp = None if past is None else tuple(wide(c) if 3 <= i <= 6 else c[l] for i, c in enumerate(past))
            x, stacked, conv_state = _group_layer(x, lp, layers[l], gf, l, depth, stacked, tiles)
            conv_states.append(conv_state)
        ak, av, ik, bk, bv, ck, cv, lf = stacked
        heads = lambda a: a.reshape(depth, B, T, N_HEADS, HEAD_DIM)
        return x, (ak, av, ik, heads(bk), heads(bv), heads(ck), heads(cv), lf, jnp.stack(conv_states, axis=0))

    y_prompt, p_state = trunk(x_prompt, None)
    y_sample, s_state = trunk(x_sample, caches)
    return (y_prompt, y_sample) + p_state + s_state
```

```python
import functools

import jax
import jax.numpy as jnp
import numpy as np
from jax import lax
from jax.experimental import pallas as pl
from jax.experimental.pallas import tpu as pltpu

F32 = jnp.float32
BF16 = jnp.bfloat16

LANES = 128
SUBLANES = 8
HEAD_DIM = 64
N_HEADS = 8
N_PAIRS = N_HEADS // 2
BRANCH_W = N_HEADS * HEAD_DIM
N_BRANCH = 3
N_IDX = 4
IDX_DIM = 64
CHUNK = 64
CHUNK_SHIFT = 6
TOPK_MAX = 256
ROT_DIM = HEAD_DIM // 4
ROPE_THETA = 500000.0
CONV_W = 3
EPS = 1e-6
NEG = -1e30
KEY_TILE = 128
FF_CHUNK = 256
VMEM_LIMIT = 56 * 1024 * 1024

ROPE_W = 1024
MISC_W = 256
CF_LANE = 8
LOG2E = 1.4426950408889634
Q_SCALE = HEAD_DIM ** -0.5 * LOG2E


def _cparams(sem):
    return pltpu.CompilerParams(dimension_semantics=sem, vmem_limit_bytes=VMEM_LIMIT)


def _rms(x, g):
    return x * lax.rsqrt(jnp.mean(x * x, axis=-1, keepdims=True) + EPS) * g


def _sigmoid(z):
    return 1.0 / (1.0 + jnp.exp(-z))


def _log_sigmoid(z):
    return jnp.minimum(z, 0.0) - jnp.log1p(jnp.exp(-jnp.abs(z)))


def _dot_t(a, b):
    return lax.dot_general(a, b, (((1,), (1,)), ((), ())), preferred_element_type=F32)


def _dot(a, b):
    return jnp.dot(a, b, preferred_element_type=F32)


def _keep_head(blk, which):
    lane = lax.broadcasted_iota(jnp.int32, blk.shape, blk.ndim - 1)
    keep = (lane >= HEAD_DIM) if which else (lane < HEAD_DIM)
    return jnp.where(keep, blk, jnp.zeros_like(blk))


def _merge_pair(o0, o1):
    lane = lax.broadcasted_iota(jnp.int32, o0.shape, o0.ndim - 1)
    return jnp.where(lane < HEAD_DIM, o0, o1)


N_CACHE_ROWS = 8


def _proj_kernel(x_ref, g_ref, w_ref, bf_ref, rc_ref, rs1_ref, rs2_ref, *refs):
    (ak_ref, av_ref, ik_ref, bk_ref, bv_ref, ck_ref, cv_ref, lf_ref, iw_ref,
     aq16, iq16, ak16, avt16, ik16, bq16, bk16, bv16, cq16, ck16, cv16) = refs[-(N_CACHE_ROWS + 12):]
    xn = _rms(x_ref[0], g_ref[...]).astype(BF16)
    rc, rs1, rs2 = rc_ref[...], rs1_ref[...], rs2_ref[...]

    def cols(c0, width):
        return _dot(xn, w_ref[:, c0:c0 + width])

    def group(blk, g):
        return blk[:, g * LANES:(g + 1) * LANES]

    def rope(blk):
        half = ROT_DIM // 2
        return blk * rc + pltpu.roll(blk, LANES - half, 1) * rs1 + pltpu.roll(blk, half, 1) * rs2

    wide = cols(0, BRANCH_W)
    for g in range(4):
        aq16[0, :, g * LANES:(g + 1) * LANES] = (rope(group(wide, g)) * Q_SCALE).astype(BF16)
    wide = cols(BRANCH_W, ROPE_W - BRANCH_W)
    for g in range(2):
        iq16[0, :, g * LANES:(g + 1) * LANES] = (rope(group(wide, g)) * IDX_DIM ** -0.5).astype(BF16)
    blk = rope(group(wide, 2))
    ak_ref[0, 0] = blk[:, :HEAD_DIM]
    ak16[0] = blk.astype(BF16)
    blk = rope(group(wide, 3))
    ik_ref[0, 0] = blk[:, :IDX_DIM]
    ik16[0] = blk.astype(BF16)

    wide = cols(ROPE_W, MISC_W)
    blk = group(wide, 0)
    av_ref[0, 0] = blk[:, :HEAD_DIM]
    row = lax.broadcasted_iota(jnp.int32, (LANES, blk.shape[0]), 0)
    avt16[0] = jnp.where(row < HEAD_DIM, blk.T, 1.0).astype(BF16)
    blk = group(wide, 1)
    iw_ref[0] = blk[:, :SUBLANES] * N_IDX ** -0.5
    lf_ref[0, 0] = _log_sigmoid(blk[:, CF_LANE:CF_LANE + N_HEADS] + bf_ref[:, CF_LANE:CF_LANE + N_HEADS])

    base = ROPE_W + MISC_W
    outs = ((None, bq16), (bk_ref, bk16), (bv_ref, bv16), (None, cq16), (ck_ref, ck16), (cv_ref, cv16))
    for n, (o32, o16) in enumerate(outs):
        wide = cols(base + n * BRANCH_W, BRANCH_W)
        if o32 is None:
            o16[0] = (wide * Q_SCALE).astype(BF16)
        else:
            o32[0, 0] = wide
            o16[0] = wide.astype(BF16)


def _project(x, g, w, bf_row, tables, tm, layer, depth, stacked):
    B, T, D = x.shape
    tok = lambda w_, dt: jax.ShapeDtypeStruct((B, T, w_), dt)
    tspec = lambda w_: pl.BlockSpec((1, tm, w_), lambda b, t: (b, t, 0))
    const = lambda a: pl.BlockSpec(a.shape, lambda b, t: (0,) * a.ndim, pipeline_mode=pl.Buffered(1))
    cache_w = (64, 64, 64, 512, 512, 512, 512, 8)
    cache_shapes = [jax.ShapeDtypeStruct((depth, B, T, w_), F32) for w_ in cache_w]
    cache_specs = [pl.BlockSpec((1, 1, tm, w_), lambda b, t: (layer, b, t, 0)) for w_ in cache_w]
    work = [(tok(w_, dt), tspec(w_)) for w_, dt in ((8, F32), (512, BF16), (256, BF16), (128, BF16))]
    work += [(jax.ShapeDtypeStruct((B, LANES, T), BF16), pl.BlockSpec((1, LANES, tm), lambda b, t: (b, 0, t)))]
    work += [(tok(w_, BF16), tspec(w_)) for w_ in (128,) + (512,) * 6]
    work_shapes, work_specs = [s for s, _ in work], [p for _, p in work]
    rspec = pl.BlockSpec((tm, LANES), lambda b, t: (t, 0))
    prev = () if stacked is None else tuple(stacked)
    n_in = 7
    outs = pl.pallas_call(
        _proj_kernel,
        grid=(B, T // tm),
        in_specs=[tspec(D), const(g), const(w), const(bf_row), rspec, rspec, rspec]
        + [pl.BlockSpec(memory_space=pl.ANY)] * len(prev),
        out_specs=cache_specs + work_specs,
        out_shape=cache_shapes + work_shapes,
        input_output_aliases={n_in + i: i for i in range(len(prev))},
        compiler_params=_cparams(("parallel", "parallel")),
        name="in_proj",
    )(x, g, w, bf_row, *tables, *prev)
    return outs[:N_CACHE_ROWS], outs[N_CACHE_ROWS:]


def _split3(x):
    hi = x.astype(BF16)
    r = x - hi.astype(F32)
    mid = r.astype(BF16)
    lo = (r - mid.astype(F32)).astype(BF16)
    return hi, mid, lo


def _cumsum_kernel(lf_ref, o_ref, *, blk):
    L = lf_ref.shape[-1]
    r = lax.broadcasted_iota(jnp.int32, (blk, blk), 0)
    c = lax.broadcasted_iota(jnp.int32, (blk, blk), 1)
    upper = jnp.where(r <= c, 1.0, 0.0).astype(BF16)
    carry = jnp.zeros((N_HEADS, 1), F32)
    for j in range(L // blk):
        hi, mid, lo = _split3(lf_ref[0, :, j * blk:(j + 1) * blk])
        f = _dot(hi, upper) + _dot(mid, upper) + _dot(lo, upper) + carry
        o_ref[0, :, j * blk:(j + 1) * blk] = f * LOG2E
        carry = f[:, blk - 1:blk]


def _cumsum_time(lf_t):
    B, H, L = lf_t.shape
    blk = LANES
    assert L % blk == 0
    spec = pl.BlockSpec((1, H, L), lambda b: (b, 0, 0))
    return pl.pallas_call(
        functools.partial(_cumsum_kernel, blk=blk),
        grid=(B,),
        in_specs=[spec],
        out_specs=spec,
        out_shape=jax.ShapeDtypeStruct((B, H, L), F32),
        compiler_params=_cparams(("parallel",)),
        name="logf_cumsum",
    )(lf_t)


def _head_column(blk, h):
    lane = lax.broadcasted_iota(jnp.int32, blk.shape, 1)
    return jnp.sum(jnp.where(lane == h, blk, 0.0), axis=1, keepdims=True)


def _positions(tq, tk, q0, k0):
    qpos = q0 + lax.broadcasted_iota(jnp.int32, (tq, tk), 0)
    kpos = k0 + lax.broadcasted_iota(jnp.int32, (tq, tk), 1)
    return qpos, kpos


def _pair_specs(tq, Lp):
    qspec = pl.BlockSpec((1, tq, BRANCH_W), lambda b, qi: (b, qi, 0))
    kspec = pl.BlockSpec((1, Lp, BRANCH_W), lambda b, qi: (b, 0, 0))
    return qspec, kspec


def _pair_lanes(hp):
    return pl.ds(pl.multiple_of(hp * LANES, LANES), LANES)


def _split_past(rest, has_past):
    return (rest[:2], rest[2:-2], rest[-2:]) if has_past else ((), rest, ())


def _key_ops(k_ref, v_ref, past, bufs, tk):
    if not past:
        def qk(qmat, hp, k0):
            return _dot_t(qmat, k_ref[0, pl.ds(k0, tk), _pair_lanes(hp)])

        def pv(p, hp, k0, ones):
            v = v_ref[0, pl.ds(k0, tk), _pair_lanes(hp)]
            return _dot(p, v if ones is None else jnp.concatenate([v, ones], axis=1))

        return qk, pv

    P, T = past[0].shape[-1], k_ref.shape[1]
    for new, buf in zip((k_ref, v_ref), bufs):
        buf[0:T, :] = new[0]
        if buf.shape[0] > T:
            buf[T:, :] = jnp.zeros((buf.shape[0] - T, BRANCH_W), BF16)

    def qk(qmat, hp, k0):
        return jnp.concatenate([_dot(qmat, past[0][0, hp].astype(BF16)), _dot_t(qmat, bufs[0][:, _pair_lanes(hp)])],
                               axis=1)

    def pv(p, hp, k0, ones):
        vt, v_tail = past[1][0, hp].astype(BF16), bufs[1][:, _pair_lanes(hp)]
        if ones is not None:
            vt = jnp.concatenate([vt, jnp.ones(vt.shape, BF16)], axis=0)
            v_tail = jnp.concatenate([v_tail, ones[:v_tail.shape[0]]], axis=1)
        return _dot_t(p[:, :P], vt) + _dot(p[:, P:], v_tail)

    return qk, pv


def _past_specs_and_bufs(past, T, Lp):
    if past is None:
        return [], [], ()
    layer, pk, pv = past
    P = pk.shape[-1]
    assert P % LANES == 0 and pk.shape[2:4] == (N_PAIRS, LANES)
    spec = pl.BlockSpec((None, 1, N_PAIRS, LANES, P), lambda b, qi: (layer, b, 0, 0, 0))
    return [spec, spec], [pltpu.VMEM((Lp - P, BRANCH_W), BF16)] * 2, (pk, pv)


def _fox_kernel(q_ref, k_ref, v_ref, *rest, tq, tk, q_off, has_past):
    past, (fq_ref, fk_ref, o_ref, m_ref, acc_ref), bufs = _split_past(rest, has_past)
    qk, pv = _key_ops(k_ref, v_ref, past, bufs, tk)
    q0 = q_off + pl.program_id(1) * tq
    diag = q0 // tk
    fq_blk = fq_ref[0]
    ones = jnp.ones((tk, LANES), BF16)

    group = acc_ref.shape[0]

    def head_group(g, _):
        hps = [g * (group // 2) + pp for pp in range(group // 2)]
        lanes = [_pair_lanes(hp) for hp in hps]
        qm = [_keep_head(q_ref[0, :, lanes[hh // 2]], hh % 2) for hh in range(group)]
        fq = [_head_column(fq_blk, g * group + hh) for hh in range(group)]
        m_ref[...] = jnp.full(m_ref.shape, NEG, F32)
        acc_ref[...] = jnp.zeros(acc_ref.shape, F32)

        def tiles(js, masked):
            k0 = [j * tk if isinstance(j, int) else pl.multiple_of(j * tk, tk) for j in js]
            chains = [(t, hh) for t in range(len(js)) for hh in range(group)]
            if masked:
                qpos, kpos = _positions(tq, tk, q0, k0[0])
                causal = kpos <= qpos
            s = {}
            for t, hh in chains:
                fk = fk_ref[0, g * group + hh, :, pl.ds(k0[t], tk)]
                s[t, hh] = qk(qm[hh], hps[hh // 2], k0[t]) + fq[hh] - fk
                if masked and t == 0:
                    s[t, hh] = jnp.where(causal, s[t, hh], NEG)
            m_old = [m_ref[hh] for hh in range(group)]
            m_new = list(m_old)
            for t, hh in chains:
                m_new[hh] = jnp.maximum(m_new[hh], jnp.max(s[t, hh], axis=1, keepdims=True))
            p = {c: jnp.exp2(s[c] - m_new[c[1]]).astype(BF16) for c in chains}
            for hh in range(group):
                upd = jnp.exp2(m_old[hh] - m_new[hh]) * acc_ref[hh]
                for t in range(len(js)):
                    upd = upd + pv(p[t, hh], hps[hh // 2], k0[t], ones)
                acc_ref[hh] = upd
                m_ref[hh] = m_new[hh]

        tiles([diag], True)

        def body(i, _):
            j = diag - 1 - 2 * i
            tiles([j, j - 1], False)
            return 0

        lax.fori_loop(0, diag // 2, body, 0)

        @pl.when(diag % 2 == 1)
        def _():
            tiles([0], False)
        for pp in range(group // 2):
            o0 = acc_ref[2 * pp, :, :LANES] / acc_ref[2 * pp, :, LANES:]
            o1 = acc_ref[2 * pp + 1, :, :LANES] / acc_ref[2 * pp + 1, :, LANES:]
            o_ref[0, :, lanes[pp]] = _merge_pair(o0, o1).astype(o_ref.dtype)
        return 0

    lax.fori_loop(0, N_HEADS // group, head_group, 0)


def _fox(q, k, v, past, fq, fk_t, *, tq, tk, q_off):
    B, T, _ = q.shape
    Lp = fk_t.shape[-1]
    qspec, kspec = _pair_specs(tq, k.shape[1])
    past_specs, bufs, past = _past_specs_and_bufs(past, T, Lp)
    return pl.pallas_call(
        functools.partial(_fox_kernel, tq=tq, tk=tk, q_off=q_off, has_past=bool(past)),
        grid=(B, T // tq),
        in_specs=[qspec, kspec, kspec] + past_specs
        + [pl.BlockSpec((1, tq, N_HEADS), lambda b, qi: (b, qi, 0)),
           pl.BlockSpec((1, N_HEADS, 1, Lp), lambda b, qi: (b, 0, 0, 0))],
        out_specs=qspec,
        out_shape=jax.ShapeDtypeStruct(q.shape, BF16),
        scratch_shapes=[pltpu.VMEM((FOX_GROUP, tq, 1), F32), pltpu.VMEM((FOX_GROUP, tq, 2 * LANES), F32)] + bufs,
        compiler_params=_cparams(("parallel", "parallel")),
        name="fox_attn",
    )(q, k, v, *past, fq, fk_t[:, :, None, :])


def _sb_kernel(q_ref, k_ref, v_ref, *rest, tq, tk, q_off, has_past):
    past, (o_ref, acc_ref, tail_ref), bufs = _split_past(rest, has_past)
    qk, pv = _key_ops(k_ref, v_ref, past, bufs, tk)
    q0 = q_off + pl.program_id(1) * tq
    diag = q0 // tk
    r = lax.broadcasted_iota(jnp.int32, (LANES, 2 * LANES), 0)
    c = lax.broadcasted_iota(jnp.int32, (LANES, 2 * LANES), 1)
    suffix_w = jnp.where((c >= LANES) | (r > c), 1.0, 0.0).astype(BF16)

    group = acc_ref.shape[0]

    def head_group(g, _):
        hps = [g * (group // 2) + pp for pp in range(group // 2)]
        lanes = [_pair_lanes(hp) for hp in hps]
        qm = [_keep_head(q_ref[0, :, lanes[hh // 2]], hh % 2) for hh in range(group)]
        acc_ref[...] = jnp.zeros(acc_ref.shape, F32)
        tail_ref[...] = jnp.zeros(tail_ref.shape, F32)

        def tiles(js, masked):
            k0 = [j * tk if isinstance(j, int) else pl.multiple_of(j * tk, tk) for j in js]
            chains = [(t, hh) for t in range(len(js)) for hh in range(group)]
            if masked:
                qpos, kpos = _positions(tq, tk, q0, k0[0])
                strict = kpos < qpos
            z = {c: qk(qm[c[1]], hps[c[1] // 2], k0[c[0]]) for c in chains}
            ls, la, after = {}, {}, {}
            for c in chains:
                ls[c] = jnp.minimum(z[c], 0.0) - jnp.log2(1.0 + jnp.exp2(-jnp.abs(z[c])))
                la[c] = ls[c] - z[c]
                if masked and c[0] == 0:
                    la[c] = jnp.where(strict, la[c], 0.0)
                    ls[c] = jnp.where(strict, ls[c], NEG)
                la[c] = la[c].astype(BF16)
            for hh in range(group):
                run = tail_ref[hh]
                for t in range(len(js)):
                    blocks = [None] * (tk // LANES)
                    for b in reversed(range(tk // LANES)):
                        res = _dot(la[t, hh][:, b * LANES:(b + 1) * LANES], suffix_w)
                        blocks[b] = res[:, :LANES] + run
                        run = run + res[:, LANES:]
                    after[t, hh] = jnp.concatenate(blocks, axis=1)
                tail_ref[hh] = run
            a = {c: jnp.exp2(ls[c] + after[c]).astype(BF16) for c in chains}
            for c in chains:
                acc_ref[c[1]] += pv(a[c], hps[c[1] // 2], k0[c[0]], None)

        tiles([diag], True)

        def body(i, _):
            j = diag - 1 - 2 * i
            tiles([j, j - 1], False)
            return 0

        lax.fori_loop(0, diag // 2, body, 0)

        @pl.when(diag % 2 == 1)
        def _():
            tiles([0], False)
        for pp in range(group // 2):
            o_ref[0, :, lanes[pp]] = _merge_pair(acc_ref[2 * pp], acc_ref[2 * pp + 1]).astype(o_ref.dtype)
        return 0

    lax.fori_loop(0, N_HEADS // group, head_group, 0)


def _sb(q, k, v, past, Lp, *, tq, tk, q_off):
    B, T, _ = q.shape
    qspec, kspec = _pair_specs(tq, k.shape[1])
    past_specs, bufs, past = _past_specs_and_bufs(past, T, Lp)
    return pl.pallas_call(
        functools.partial(_sb_kernel, tq=tq, tk=tk, q_off=q_off, has_past=bool(past)),
        grid=(B, T // tq),
        in_specs=[qspec, kspec, kspec] + past_specs,
        out_specs=qspec,
        out_shape=jax.ShapeDtypeStruct(q.shape, BF16),
        scratch_shapes=[pltpu.VMEM((SB_GROUP, tq, LANES), F32), pltpu.VMEM((SB_GROUP, tq, LANES), F32)] + bufs,
        compiler_params=_cparams(("parallel", "parallel")),
        name="sb_attn",
    )(q, k, v, *past)


def _const_key(v):
    bits = int(np.float32(v).view(np.int32))
    return (bits ^ 0x7FFFFFFF) + 1 if bits < 0 else bits


def _float_key(x):
    bits = lax.bitcast_convert_type(x, jnp.int32)
    return jnp.where(bits < 0, jnp.bitwise_xor(bits, 0x7FFFFFFF) + 1, bits)


KEY_MIN = -2 ** 31
KEY_HALF_NEG = _const_key(0.5 * NEG)


KEY_MAX = 2 ** 31 - 1
KEY_BITS = 32
COUNT_CHAINS = 8
SB_GROUP = 8
FOX_GROUP = 2


def _dsa_kernel(aq_ref, iq_ref, iwt_ref, k_ref, vt_ref, ik_ref, o_ref, key_ref, q8_ref, m_ref, acc_ref,
                *, tq, tk, q_off, n_rows, topk):
    q0 = q_off + pl.program_id(1) * tq
    n_tiles = (q0 + n_rows + tk - 1) // tk

    def tile_start(j):
        return pl.multiple_of(j * tk, LANES)

    iq = iq_ref[0]
    iwt = iwt_ref[0]
    qh = [_keep_head(iq[:, (h // 2) * LANES:(h // 2 + 1) * LANES], h % 2) for h in range(N_IDX)]
    q_chunk = jnp.right_shift(q0 + lax.broadcasted_iota(jnp.int32, (tk, tq), 1), CHUNK_SHIFT)

    def score_tile(j, _):
        k0 = tile_start(j)
        ik = ik_ref[0, pl.ds(k0, tk), :]
        sc = jnp.zeros((tk, tq), F32)
        for h in range(N_IDX):
            sc = sc + jnp.maximum(_dot_t(ik, qh[h]), 0.0) * iwt[h:h + 1, :]
        key_chunk = jnp.right_shift(k0 + lax.broadcasted_iota(jnp.int32, (tk, tq), 0), CHUNK_SHIFT)
        key_ref[pl.ds(k0, tk), :] = _float_key(jnp.where(key_chunk <= q_chunk, sc, NEG))
        return 0

    lax.fori_loop(0, n_tiles, score_tile, 0)

    def search():
        full_tiles = (q0 + n_rows) // tk
        tail_blocks = (q0 + n_rows - full_tiles * tk + LANES - 1) // LANES

        def step(_, cr):
            lo, hi, above_hi = cr
            mid = jnp.right_shift(lo, 1) + jnp.right_shift(hi, 1) + jnp.bitwise_and(jnp.bitwise_and(lo, hi), 1)

            def count_rows(r0, rows, cnt):
                above = jnp.where(key_ref[pl.ds(pl.multiple_of(r0, LANES), rows), :] > mid, 1.0, 0.0)
                return cnt + jnp.sum(above.reshape((-1,) + cnt.shape), axis=0)

            cnt = lax.fori_loop(0, full_tiles, lambda j, c: count_rows(j * tk, tk, c),
                                jnp.zeros((COUNT_CHAINS, SUBLANES, tq), F32))
            cnt = lax.fori_loop(0, tail_blocks, lambda b, c: count_rows(full_tiles * tk + b * LANES, LANES, c), cnt)
            cnt = jnp.sum(jnp.sum(cnt, axis=0), axis=0, keepdims=True)
            less = cnt < topk
            return jnp.where(less, lo, mid), jnp.where(less, mid, hi), jnp.where(less, cnt, above_hi)

        init = (jnp.full((1, tq), KEY_MIN, jnp.int32), jnp.full((1, tq), KEY_MAX, jnp.int32), jnp.zeros((1, tq), F32))
        _, hi, above_hi = lax.fori_loop(0, KEY_BITS, step, init)
        return hi, topk - above_hi

    thr, room = lax.cond(q0 + n_rows > topk, search,
                         lambda: (jnp.full((1, tq), KEY_MIN, jnp.int32), jnp.zeros((1, tq), F32)))
    row_ok = lax.broadcasted_iota(jnp.int32, (1, tq), 1) < n_rows
    room = jnp.where(row_ok, room, 0.0)

    @pl.when(jnp.max(room) > 0.0)
    def _():
        r = lax.broadcasted_iota(jnp.int32, (LANES, LANES), 0)
        c = lax.broadcasted_iota(jnp.int32, (LANES, LANES), 1)
        before = jnp.where(c < r, 1.0, 0.0).astype(BF16)

        def tie_tile(j, seen):
            k0 = tile_start(j)
            for b in range(tk // LANES):
                rows = pl.ds(k0 + b * LANES, LANES)
                keys = key_ref[rows, :]
                eq = keys == thr
                eqf = jnp.where(eq, 1.0, 0.0)
                rank = _dot(before, eqf.astype(BF16)) + seen
                key_ref[rows, :] = jnp.where(eq, jnp.where(rank < room, keys + 1, keys), keys)
                seen = seen + jnp.sum(eqf, axis=0, keepdims=True)
            return seen

        lax.fori_loop(0, n_tiles, tie_tile, jnp.zeros((1, tq), F32))

    aq = aq_ref[0]
    for h in range(N_HEADS):
        q8_ref[h * tq:(h + 1) * tq, :] = _keep_head(aq[:, (h // 2) * LANES:(h // 2 + 1) * LANES], h % 2)
    m_ref[...] = jnp.full(m_ref.shape, NEG, F32)
    acc_ref[...] = jnp.zeros(acc_ref.shape, F32)
    thr_sel = jnp.maximum(thr, KEY_HALF_NEG)

    def attend_tiles(js):
        k0 = [j * tk if isinstance(j, int) else tile_start(j) for j in js]
        groups = [slice(g * (N_HEADS // 2) * tq, (g + 1) * (N_HEADS // 2) * tq) for g in (0, 1)]
        chains = [(t, g) for t in range(len(js)) for g in (0, 1)]
        kt = [k_ref[0, pl.ds(k, tk), :] for k in k0]
        bias = [jnp.where(key_ref[pl.ds(k, tk), :] > thr_sel, 0.0, NEG) for k in k0]
        st = {(t, g): _dot_t(kt[t], q8_ref[groups[g], :]) for t, g in chains}
        st = {(t, g): jnp.concatenate([st[t, g][:, h * tq:(h + 1) * tq] + bias[t] for h in range(N_HEADS // 2)], axis=1)
              for t, g in chains}
        m_old = [m_ref[:, cols] for cols in groups]
        m_new = list(m_old)
        for t, g in chains:
            m_new[g] = jnp.maximum(m_new[g], jnp.max(st[t, g], axis=0, keepdims=True))
        p = {(t, g): jnp.exp2(st[t, g] - m_new[g]).astype(BF16) for t, g in chains}
        for g, cols in enumerate(groups):
            upd = jnp.exp2(m_old[g] - m_new[g]) * acc_ref[:, cols]
            for t in range(len(js)):
                upd = upd + _dot(vt_ref[0, :, pl.ds(k0[t], tk)], p[t, g])
            acc_ref[:, cols] = upd
            m_ref[:, cols] = m_new[g]

    def attend_pair(i, _):
        attend_tiles([2 * i, 2 * i + 1])
        return 0

    lax.fori_loop(0, n_tiles // 2, attend_pair, 0)

    @pl.when(n_tiles % 2 == 1)
    def _():
        attend_tiles([n_tiles - 1])
    for hp in range(N_PAIRS):
        pair = []
        for h in (2 * hp, 2 * hp + 1):
            a = acc_ref[:, h * tq:(h + 1) * tq]
            pair.append(a[:HEAD_DIM] / a[HEAD_DIM:])
        o_ref[0, :, hp * LANES:(hp + 1) * LANES] = jnp.concatenate(pair, axis=0).T.astype(o_ref.dtype)


def _dsa(aq, iq, iw, k2, v1t, ik2, *, tq, q_off, topk):
    B, T, _ = aq.shape
    Lp = k2.shape[1]
    tk = Lp if T <= LANES else next(t for t in (512, 384, 256, 128) if Lp % t == 0)
    Tp = -(-T // LANES) * LANES
    tq = max(tq, LANES)
    n_rows = min(T, tq)
    pad_t = lambda a: jnp.pad(a, ((0, 0), (0, Tp - T), (0, 0)))
    qspec = lambda w: pl.BlockSpec((1, tq, w), lambda b, qi: (b, qi, 0))
    kspec = pl.BlockSpec((1, Lp, LANES), lambda b, qi: (b, 0, 0))
    out = pl.pallas_call(
        functools.partial(_dsa_kernel, tq=tq, tk=tk, q_off=q_off, n_rows=n_rows, topk=topk),
        grid=(B, Tp // tq),
        in_specs=[qspec(BRANCH_W), qspec(N_IDX * IDX_DIM), pl.BlockSpec((1, SUBLANES, tq), lambda b, qi: (b, 0, qi)),
                  kspec, pl.BlockSpec((1, LANES, Lp), lambda b, qi: (b, 0, 0)), kspec],
        out_specs=qspec(BRANCH_W),
        out_shape=jax.ShapeDtypeStruct((B, Tp, BRANCH_W), BF16),
        scratch_shapes=[pltpu.VMEM((Lp, tq), jnp.int32), pltpu.VMEM((N_HEADS * tq, LANES), BF16),
                        pltpu.VMEM((1, N_HEADS * tq), F32), pltpu.VMEM((LANES, N_HEADS * tq), F32)],
        compiler_params=_cparams(("parallel", "parallel")),
        name="dsa_attn",
    )(pad_t(aq), pad_t(iq), jnp.swapaxes(pad_t(iw), 1, 2), k2, v1t, ik2)
    return out[:, :T]


def _merge_kernel(x_ref, oa_ref, ob_ref, oc_ref, g_ref, wg_ref, wb_ref, wo_ref, y_ref):
    x = x_ref[0]
    D = x.shape[-1]
    xn = _rms(x, g_ref[...]).astype(BF16)
    merged = jnp.zeros(x.shape, F32)
    for n, o_ref in enumerate((oa_ref, ob_ref, oc_ref)):
        gate = _sigmoid(_dot(xn, wg_ref[:, n * D:(n + 1) * D]))
        merged = merged + gate * _dot(o_ref[0], wb_ref[n])
    y_ref[0] = x + _dot(merged.astype(BF16), wo_ref[...])


def _merge(x, oa, ob, oc, g, wg, wb, wo, tm):
    B, T, D = x.shape
    tspec = lambda w_: pl.BlockSpec((1, tm, w_), lambda b, t: (b, t, 0))
    const = lambda a: pl.BlockSpec(a.shape, lambda b, t: (0,) * a.ndim)
    return pl.pallas_call(
        _merge_kernel,
        grid=(B, T // tm),
        in_specs=[tspec(D), tspec(BRANCH_W), tspec(BRANCH_W), tspec(BRANCH_W), const(g), const(wg), const(wb),
                  const(wo)],
        out_specs=tspec(D),
        out_shape=jax.ShapeDtypeStruct(x.shape, F32),
        compiler_params=_cparams(("parallel", "parallel")),
        name="merge_out",
    )(x, oa, ob, oc, g, wg, wb, wo)


def _ffn_kernel(x_ref, g_ref, past_ref, wu_ref, cw_ref, cb_ref, wd_ref, gf_ref, y_ref, st_ref,
                hbuf_ref, carry_ref, acc_ref, *, tm, final_norm):
    fc = FF_CHUNK
    d_ff = wd_ref.shape[0]
    n_chunks = d_ff // fc

    @pl.when(pl.program_id(1) == 0)
    def _():
        carry_ref[...] = past_ref[0]

    x = x_ref[0]
    xn = _rms(x, g_ref[...]).astype(BF16)
    acc_ref[...] = jnp.zeros(acc_ref.shape, F32)

    def up_proj(c):
        return jnp.concatenate([_dot(xn, wu_ref[:, c * fc:(c + 1) * fc]),
                                _dot(xn, wu_ref[:, d_ff + c * fc:d_ff + (c + 1) * fc])], axis=1)

    h_next = up_proj(0)
    for c in range(n_chunks):
        hbuf = hbuf_ref.at[c % 2]
        both = lambda ref: jnp.concatenate([ref[:, c * fc:(c + 1) * fc], ref[:, d_ff + c * fc:d_ff + (c + 1) * fc]],
                                           axis=1)
        h = h_next
        if c + 1 < n_chunks:
            h_next = up_proj(c + 1)
        hbuf[0:SUBLANES, :] = carry_ref[c]
        hbuf[SUBLANES:SUBLANES + tm, :] = h
        carry_ref[c] = h[tm - SUBLANES:tm, :]
        cw = both(cw_ref)
        hc = (both(cb_ref) + cw[0:1] * hbuf[SUBLANES - 2:SUBLANES - 2 + tm, :]
              + cw[1:2] * hbuf[SUBLANES - 1:SUBLANES - 1 + tm, :] + cw[2:3] * h)
        gate, up = hc[:, :fc], hc[:, fc:]
        act = gate * _sigmoid(gate) * up
        acc_ref[...] += _dot(act.astype(BF16), wd_ref[c * fc:(c + 1) * fc, :])
    st_ref[0] = carry_ref[...]
    y = x + acc_ref[...]
    if final_norm:
        y = _rms(y, gf_ref[...])
    y_ref[0] = y


def _ffn(x, g, past, wu, cw, cb, wd, gf, tm, final_norm):
    B, T, D = x.shape
    nc, fc2 = wd.shape[0] // FF_CHUNK, 2 * FF_CHUNK
    tspec = pl.BlockSpec((1, tm, D), lambda b, t: (b, t, 0))
    const = lambda a: pl.BlockSpec(a.shape, lambda b, t: (0,) * a.ndim, pipeline_mode=pl.Buffered(1))
    sspec = pl.BlockSpec((1, nc, SUBLANES, fc2), lambda b, t: (b, 0, 0, 0))
    return pl.pallas_call(
        functools.partial(_ffn_kernel, tm=tm, final_norm=final_norm),
        grid=(B, T // tm),
        in_specs=[tspec, const(g), sspec, const(wu), const(cw), const(cb), const(wd), const(gf)],
        out_specs=[tspec, sspec],
        out_shape=[jax.ShapeDtypeStruct(x.shape, F32), jax.ShapeDtypeStruct((B, nc, SUBLANES, fc2), F32)],
        scratch_shapes=[pltpu.VMEM((2, tm + SUBLANES, fc2), F32), pltpu.VMEM((nc, SUBLANES, fc2), F32),
                        pltpu.VMEM((tm, D), F32)],
        compiler_params=_cparams(("arbitrary", "arbitrary")),
        name="conv_ffn",
    )(x, g, past, wu, cw, cb, wd, gf)


def _rope_tables(pos):
    half = ROT_DIM // 2
    freq = ROPE_THETA ** (-jnp.arange(half, dtype=F32) / half)
    ang = pos.astype(F32)[:, None] * freq[None, :]
    lane = np.arange(LANES) % HEAD_DIM
    cos = jnp.cos(ang)[:, lane % half]
    sin = jnp.sin(ang)[:, lane % half]
    rc = jnp.where(lane < ROT_DIM, cos, 1.0)
    rs1 = jnp.where(lane < half, -sin, 0.0)
    rs2 = jnp.where((lane >= half) & (lane < ROT_DIM), sin, 0.0)
    return rc, rs1, rs2


IN_SIZES = (BRANCH_W, HEAD_DIM, HEAD_DIM, N_IDX * IDX_DIM, IDX_DIM, N_IDX) + (BRANCH_W,) * 6 + (N_HEADS,)
PACKED_W = ROPE_W + MISC_W + 6 * BRANCH_W


def _pack_w_kernel(w_ref, o_ref, s_ref):
    offs = np.concatenate([[0], np.cumsum(IN_SIZES)]).tolist()
    a_q, a_k, a_v, i_q, i_k, i_w, b_q = offs[:7]
    c_f = offs[12]
    src = lambda c0, width: w_ref[0, :, c0:c0 + width]
    s_ref[:, 0:BRANCH_W] = src(a_q, BRANCH_W)
    s_ref[:, BRANCH_W:BRANCH_W + 256] = src(i_q, 256)
    for dst, c0 in ((768, a_k), (896, i_k), (ROPE_W, a_v)):
        s_ref[:, dst:dst + HEAD_DIM] = src(c0, HEAD_DIM)
        s_ref[:, dst + HEAD_DIM:dst + 2 * HEAD_DIM] = src(c0, HEAD_DIM)
    misc = ROPE_W + LANES
    s_ref[:, misc:misc + LANES] = jnp.zeros((s_ref.shape[0], LANES), F32)
    s_ref[:, misc:misc + N_IDX] = src(i_w, N_IDX)
    s_ref[:, misc + CF_LANE:misc + CF_LANE + N_HEADS] = src(c_f, N_HEADS)
    s_ref[:, ROPE_W + MISC_W:PACKED_W] = src(b_q, 6 * BRANCH_W)
    o_ref[...] = s_ref[...].astype(BF16)


def _pack_w(w_in, l):
    D = w_in.shape[1]
    rows = 256
    return pl.pallas_call(
        _pack_w_kernel,
        grid=(D // rows,),
        in_specs=[pl.BlockSpec((1, rows, w_in.shape[2]), lambda r: (l, r, 0))],
        out_specs=pl.BlockSpec((rows, PACKED_W), lambda r: (r, 0)),
        out_shape=jax.ShapeDtypeStruct((D, PACKED_W), BF16),
        scratch_shapes=[pltpu.VMEM((rows, PACKED_W), F32)],
        compiler_params=_cparams(("parallel",)),
        name="pack_w_in",
    )(w_in)


def _pack_layer(l, norm_mix, w_in, b_f, w_branch, w_gate, w_out, norm_ffn, w_up, conv_w, conv_b, w_down):
    w = _pack_w(w_in, l)
    bf_row = jnp.zeros((1, LANES), F32).at[0, CF_LANE:CF_LANE + N_HEADS].set(b_f[l])

    d_ff = w_down.shape[1]
    fc = FF_CHUNK
    nc = d_ff // fc
    assert nc * fc == d_ff
    halves = lambda a: jnp.concatenate([a[..., :d_ff].reshape(a.shape[:-1] + (nc, fc)),
                                        a[..., d_ff:].reshape(a.shape[:-1] + (nc, fc))], axis=-1)
    wu, cw, cb, wd = w_up[l].astype(BF16), conv_w[l], conv_b[l][None], w_down[l].astype(BF16)
    return dict(g_mix=norm_mix[l][None], w=w, bf_row=bf_row, wg=w_gate[l].astype(BF16),
                wb=w_branch[l].astype(BF16), wo=w_out[l].astype(BF16), g_ffn=norm_ffn[l][None],
                wu=wu, cw=cw, cb=cb, wd=wd, halves=halves, nc=nc, d_ff=d_ff)


def _conv_state_in(state, halves):
    st = jnp.moveaxis(halves(state), 2, 1)
    return jnp.pad(st, ((0, 0), (0, 0), (SUBLANES - (CONV_W - 1), 0), (0, 0)))


def _conv_state_out(st, d_ff):
    st = st[:, :, SUBLANES - (CONV_W - 1):, :]
    B, nc, r, fc2 = st.shape
    fc = fc2 // 2
    gate = jnp.moveaxis(st[..., :fc], 1, 2).reshape(B, r, d_ff)
    up = jnp.moveaxis(st[..., fc:], 1, 2).reshape(B, r, d_ff)
    return jnp.concatenate([gate, up], axis=-1)


def _pad_keys(a, Lp):
    return jnp.pad(a, ((0, 0), (0, Lp - a.shape[1]), (0, 0)))


def _group_layer(x, past, lw, gf, layer, depth, stacked, tiles):
    B, T, D = x.shape
    P = 0 if past is None else past[0].shape[1]
    L = P + T
    Lp = -(-L // KEY_TILE) * KEY_TILE
    topk = max(1, min(TOPK_MAX, L // 4))
    tables = _rope_tables(jnp.arange(P, L))

    stacked, (iw, aq16, iq16, ak16, avt16, ik16, bq16, bk16, bv16, cq16, ck16, cv16) = _project(
        x, lw["g_mix"], lw["w"], lw["bf_row"], tables, tiles["proj"], layer, depth, stacked)
    lf = stacked[-1][layer]

    if past is None:
        keys = (ak16, ik16)
        lf_all = lf
        conv_in = jnp.zeros((B, lw["nc"], SUBLANES, 2 * FF_CHUNK), F32)
        past_b = past_c = None
    else:
        p_ak, p_av, p_ik, p_bk, p_bv, p_ck, p_cv, p_lf, p_conv = past
        dup = lambda a: jnp.concatenate([a, a], axis=-1).astype(BF16)
        keys = tuple(jnp.concatenate([o, n], axis=1) for o, n in zip((dup(p_ak), dup(p_ik)), (ak16, ik16)))
        old_vt = jnp.swapaxes(p_av, 1, 2).astype(BF16)
        avt16 = jnp.concatenate([jnp.concatenate([old_vt, jnp.ones_like(old_vt)], axis=1), avt16], axis=2)
        lf_all = jnp.concatenate([p_lf, lf], axis=1)
        conv_in = _conv_state_in(p_conv, lw["halves"])
        past_b, past_c = (layer, p_bk, p_bv), (layer, p_ck, p_cv)
    ak2, ik2 = (_pad_keys(a, Lp) for a in keys)
    v1t = jnp.pad(avt16, ((0, 0), (0, 0), (0, Lp - L)))

    fk_t = _cumsum_time(jnp.swapaxes(_pad_keys(lf_all, Lp), 1, 2))
    fq = jnp.swapaxes(fk_t[:, :, P:L], 1, 2)

    o_a = _dsa(aq16, iq16, iw, ak2, v1t, ik2, tq=tiles["dsa"], q_off=P, topk=topk)
    o_b = _sb(bq16, bk16, bv16, past_b, Lp, tq=tiles["sb"][0], tk=tiles["sb"][1], q_off=P)
    o_c = _fox(cq16, ck16, cv16, past_c, fq, fk_t, tq=tiles["fox"][0], tk=tiles["fox"][1], q_off=P)

    x = _merge(x, o_a, o_b, o_c, lw["g_mix"], lw["wg"], lw["wb"], lw["wo"], tiles["merge"])
    x, st = _ffn(x, lw["g_ffn"], conv_in, lw["wu"], lw["cw"], lw["cb"], lw["wd"], gf, tiles["ffn"],
                 layer == depth - 1)
    return x, stacked, _conv_state_out(st, lw["d_ff"])


def _tiles(T, P):
    pick = lambda want: min(want, T)
    if T % 512 == 0 and P % 512 == 0:
        sb, fox = (256, 256), (256, 512)
    else:
        assert T <= KEY_TILE
        sb = fox = (T, -(-(P + T) // KEY_TILE) * KEY_TILE)
    return dict(proj=pick(512), dsa=pick(128), sb=sb, fox=fox, merge=pick(512), ffn=pick(512))


def kernel(x_prompt, x_sample, cache_a_k, cache_a_v, cache_a_idx_k, cache_b_k, cache_b_v, cache_c_k, cache_c_v,
           cache_c_logf, state_ffn_conv, norm_mix, w_in, b_f, w_branch, w_gate, w_out, norm_ffn, w_up, conv_w,
           conv_b, w_down, norm_final):
    depth = w_in.shape[0]
    caches = (cache_a_k, cache_a_v, cache_a_idx_k, cache_b_k, cache_b_v, cache_c_k, cache_c_v, cache_c_logf,
              state_ffn_conv)
    layers = [_pack_layer(l, norm_mix, w_in, b_f, w_branch, w_gate, w_out, norm_ffn, w_up, conv_w, conv_b, w_down)
              for l in range(depth)]
    gf = norm_final[None]

    def trunk(x, past):
        B, T, _ = x.shape
        tiles = _tiles(T, 0 if past is None else past[0].shape[2])
        stacked, conv_states = None, []
        if past is not None:
            P = past[0].shape[2]
            wide = lambda c: jnp.transpose(c, (0, 1, 3, 4, 2)).reshape(depth, B, N_PAIRS, LANES, P)
        for l in range(depth):
            lp = None if past is None else tuple(wide(c) if 3 <= i <= 6 else c[l] for i, c in enumerate(past))
            x, stacked, conv_state = _group_layer(x, lp, layers[l], gf, l, depth, stacked, tiles)
            conv_states.append(conv_state)
        ak, av, ik, bk, bv, ck, cv, lf = stacked
        heads = lambda a: a.reshape(depth, B, T, N_HEADS, HEAD_DIM)
        return x, (ak, av, ik, heads(bk), heads(bv), heads(ck), heads(cv), lf, jnp.stack(conv_states, axis=0))

    y_prompt, p_state = trunk(x_prompt, None)
    y_sample, s_state = trunk(x_sample, caches)
    return (y_prompt, y_sample) + p_state + s_state
```

```python
import functools

import jax
import jax.numpy as jnp
import numpy as np
from jax import lax
from jax.experimental import pallas as pl
from jax.experimental.pallas import tpu as pltpu

F32 = jnp.float32
BF16 = jnp.bfloat16

LANES = 128
SUBLANES = 8
HEAD_DIM = 64
N_HEADS = 8
N_PAIRS = N_HEADS // 2
BRANCH_W = N_HEADS * HEAD_DIM
N_BRANCH = 3
N_IDX = 4
IDX_DIM = 64
CHUNK = 64
CHUNK_SHIFT = 6
TOPK_MAX = 256
ROT_DIM = HEAD_DIM // 4
ROPE_THETA = 500000.0
CONV_W = 3
EPS = 1e-6
NEG = -1e30
KEY_TILE = 128
FF_CHUNK = 256
VMEM_LIMIT = 56 * 1024 * 1024

ROPE_W = 1024
MISC_W = 256
CF_LANE = 8
LOG2E = 1.4426950408889634
Q_SCALE = HEAD_DIM ** -0.5 * LOG2E


def _cparams(sem):
    return pltpu.CompilerParams(dimension_semantics=sem, vmem_limit_bytes=VMEM_LIMIT)


def _rms(x, g):
    return x * lax.rsqrt(jnp.mean(x * x, axis=-1, keepdims=True) + EPS) * g


def _sigmoid(z):
    return 1.0 / (1.0 + jnp.exp(-z))


def _log_sigmoid(z):
    return jnp.minimum(z, 0.0) - jnp.log1p(jnp.exp(-jnp.abs(z)))


def _dot_t(a, b):
    return lax.dot_general(a, b, (((1,), (1,)), ((), ())), preferred_element_type=F32)


def _dot(a, b):
    return jnp.dot(a, b, preferred_element_type=F32)


def _keep_head(blk, which):
    lane = lax.broadcasted_iota(jnp.int32, blk.shape, blk.ndim - 1)
    keep = (lane >= HEAD_DIM) if which else (lane < HEAD_DIM)
    return jnp.where(keep, blk, jnp.zeros_like(blk))


def _merge_pair(o0, o1):
    lane = lax.broadcasted_iota(jnp.int32, o0.shape, o0.ndim - 1)
    return jnp.where(lane < HEAD_DIM, o0, o1)


N_CACHE_ROWS = 8


def _proj_kernel(x_ref, g_ref, w_ref, bf_ref, rc_ref, rs1_ref, rs2_ref, *refs):
    (ak_ref, av_ref, ik_ref, bk_ref, bv_ref, ck_ref, cv_ref, lf_ref, iw_ref,
     aq16, iq16, ak16, avt16, ik16, bq16, bk16, bv16, cq16, ck16, cv16) = refs[-(N_CACHE_ROWS + 12):]
    xn = _rms(x_ref[0], g_ref[...]).astype(BF16)
    rc, rs1, rs2 = rc_ref[...], rs1_ref[...], rs2_ref[...]

    def cols(c0, width):
        return _dot(xn, w_ref[:, c0:c0 + width])

    def group(blk, g):
        return blk[:, g * LANES:(g + 1) * LANES]

    def rope(blk):
        half = ROT_DIM // 2
        return blk * rc + pltpu.roll(blk, LANES - half, 1) * rs1 + pltpu.roll(blk, half, 1) * rs2

    wide = cols(0, BRANCH_W)
    for g in range(4):
        aq16[0, :, g * LANES:(g + 1) * LANES] = (rope(group(wide, g)) * Q_SCALE).astype(BF16)
    wide = cols(BRANCH_W, ROPE_W - BRANCH_W)
    for g in range(2):
        iq16[0, :, g * LANES:(g + 1) * LANES] = (rope(group(wide, g)) * IDX_DIM ** -0.5).astype(BF16)
    blk = rope(group(wide, 2))
    ak_ref[0, 0] = blk[:, :HEAD_DIM]
    ak16[0] = blk.astype(BF16)
    blk = rope(group(wide, 3))
    ik_ref[0, 0] = blk[:, :IDX_DIM]
    ik16[0] = blk.astype(BF16)

    wide = cols(ROPE_W, MISC_W)
    blk = group(wide, 0)
    av_ref[0, 0] = blk[:, :HEAD_DIM]
    row = lax.broadcasted_iota(jnp.int32, (LANES, blk.shape[0]), 0)
    avt16[0] = jnp.where(row < HEAD_DIM, blk.T, 1.0).astype(BF16)
    blk = group(wide, 1)
    iw_ref[0] = blk[:, :SUBLANES] * N_IDX ** -0.5
    lf_ref[0, 0] = _log_sigmoid(blk[:, CF_LANE:CF_LANE + N_HEADS] + bf_ref[:, CF_LANE:CF_LANE + N_HEADS])

    base = ROPE_W + MISC_W
    outs = ((None, bq16), (bk_ref, bk16), (bv_ref, bv16), (None, cq16), (ck_ref, ck16), (cv_ref, cv16))
    for n, (o32, o16) in enumerate(outs):
        wide = cols(base + n * BRANCH_W, BRANCH_W)
        if o32 is None:
            o16[0] = (wide * Q_SCALE).astype(BF16)
        else:
            o32[0, 0] = wide
            o16[0] = wide.astype(BF16)


def _project(x, g, w, bf_row, tables, tm, layer, depth, stacked):
    B, T, D = x.shape
    tok = lambda w_, dt: jax.ShapeDtypeStruct((B, T, w_), dt)
    tspec = lambda w_: pl.BlockSpec((1, tm, w_), lambda b, t: (b, t, 0))
    const = lambda a: pl.BlockSpec(a.shape, lambda b, t: (0,) * a.ndim, pipeline_mode=pl.Buffered(1))
    cache_w = (64, 64, 64, 512, 512, 512, 512, 8)
    cache_shapes = [jax.ShapeDtypeStruct((depth, B, T, w_), F32) for w_ in cache_w]
    cache_specs = [pl.BlockSpec((1, 1, tm, w_), lambda b, t: (layer, b, t, 0)) for w_ in cache_w]
    work = [(tok(w_, dt), tspec(w_)) for w_, dt in ((8, F32), (512, BF16), (256, BF16), (128, BF16))]
    work += [(jax.ShapeDtypeStruct((B, LANES, T), BF16), pl.BlockSpec((1, LANES, tm), lambda b, t: (b, 0, t)))]
    work += [(tok(w_, BF16), tspec(w_)) for w_ in (128,) + (512,) * 6]
    work_shapes, work_specs = [s for s, _ in work], [p for _, p in work]
    rspec = pl.BlockSpec((tm, LANES), lambda b, t: (t, 0))
    prev = () if stacked is None else tuple(stacked)
    n_in = 7
    outs = pl.pallas_call(
        _proj_kernel,
        grid=(B, T // tm),
        in_specs=[tspec(D), const(g), const(w), const(bf_row), rspec, rspec, rspec]
        + [pl.BlockSpec(memory_space=pl.ANY)] * len(prev),
        out_specs=cache_specs + work_specs,
        out_shape=cache_shapes + work_shapes,
        input_output_aliases={n_in + i: i for i in range(len(prev))},
        compiler_params=_cparams(("parallel", "parallel")),
        name="in_proj",
    )(x, g, w, bf_row, *tables, *prev)
    return outs[:N_CACHE_ROWS], outs[N_CACHE_ROWS:]


def _split3(x):
    hi = x.astype(BF16)
    r = x - hi.astype(F32)
    mid = r.astype(BF16)
    lo = (r - mid.astype(F32)).astype(BF16)
    return hi, mid, lo


def _cumsum_kernel(lf_ref, o_ref, *, blk):
    L = lf_ref.shape[-1]
    r = lax.broadcasted_iota(jnp.int32, (blk, blk), 0)
    c = lax.broadcasted_iota(jnp.int32, (blk, blk), 1)
    upper = jnp.where(r <= c, 1.0, 0.0).astype(BF16)
    carry = jnp.zeros((N_HEADS, 1), F32)
    for j in range(L // blk):
        hi, mid, lo = _split3(lf_ref[0, :, j * blk:(j + 1) * blk])
        f = _dot(hi, upper) + _dot(mid, upper) + _dot(lo, upper) + carry
        o_ref[0, :, j * blk:(j + 1) * blk] = f * LOG2E
        carry = f[:, blk - 1:blk]


def _cumsum_time(lf_t):
    B, H, L = lf_t.shape
    blk = LANES
    assert L % blk == 0
    spec = pl.BlockSpec((1, H, L), lambda b: (b, 0, 0))
    return pl.pallas_call(
        functools.partial(_cumsum_kernel, blk=blk),
        grid=(B,),
        in_specs=[spec],
        out_specs=spec,
        out_shape=jax.ShapeDtypeStruct((B, H, L), F32),
        compiler_params=_cparams(("parallel",)),
        name="logf_cumsum",
    )(lf_t)


def _head_column(blk, h):
    lane = lax.broadcasted_iota(jnp.int32, blk.shape, 1)
    return jnp.sum(jnp.where(lane == h, blk, 0.0), axis=1, keepdims=True)


def _positions(tq, tk, q0, k0):
    qpos = q0 + lax.broadcasted_iota(jnp.int32, (tq, tk), 0)
    kpos = k0 + lax.broadcasted_iota(jnp.int32, (tq, tk), 1)
    return qpos, kpos


def _pair_specs(tq, Lp):
    qspec = pl.BlockSpec((1, tq, BRANCH_W), lambda b, qi: (b, qi, 0))
    kspec = pl.BlockSpec((1, Lp, BRANCH_W), lambda b, qi: (b, 0, 0))
    return qspec, kspec


def _pair_lanes(hp):
    return pl.ds(pl.multiple_of(hp * LANES, LANES), LANES)


def _split_past(rest, has_past):
    return (rest[:2], rest[2:-2], rest[-2:]) if has_past else ((), rest, ())


def _key_ops(k_ref, v_ref, past, bufs, tk):
    if not past:
        def qk(qmat, hp, k0):
            return _dot_t(qmat, k_ref[0, pl.ds(k0, tk), _pair_lanes(hp)])

        def pv(p, hp, k0, ones):
            v = v_ref[0, pl.ds(k0, tk), _pair_lanes(hp)]
            return _dot(p, v if ones is None else jnp.concatenate([v, ones], axis=1))

        return qk, pv

    P, T = past[0].shape[-1], k_ref.shape[1]
    for new, buf in zip((k_ref, v_ref), bufs):
        buf[0:T, :] = new[0]
        if buf.shape[0] > T:
            buf[T:, :] = jnp.zeros((buf.shape[0] - T, BRANCH_W), BF16)

    def qk(qmat, hp, k0):
        return jnp.concatenate([_dot(qmat, past[0][0, hp].astype(BF16)), _dot_t(qmat, bufs[0][:, _pair_lanes(hp)])],
                               axis=1)

    def pv(p, hp, k0, ones):
        vt, v_tail = past[1][0, hp].astype(BF16), bufs[1][:, _pair_lanes(hp)]
        if ones is not None:
            vt = jnp.concatenate([vt, jnp.ones(vt.shape, BF16)], axis=0)
            v_tail = jnp.concatenate([v_tail, ones[:v_tail.shape[0]]], axis=1)
        return _dot_t(p[:, :P], vt) + _dot(p[:, P:], v_tail)

    return qk, pv


def _past_specs_and_bufs(past, T, Lp):
    if past is None:
        return [], [], ()
    layer, pk, pv = past
    P = pk.shape[-1]
    assert P % LANES == 0 and pk.shape[2:4] == (N_PAIRS, LANES)
    spec = pl.BlockSpec((None, 1, N_PAIRS, LANES, P), lambda b, qi: (layer, b, 0, 0, 0))
    return [spec, spec], [pltpu.VMEM((Lp - P, BRANCH_W), BF16)] * 2, (pk, pv)


def _fox_kernel(q_ref, k_ref, v_ref, *rest, tq, tk, q_off, has_past):
    past, (fq_ref, fk_ref, o_ref, m_ref, acc_ref), bufs = _split_past(rest, has_past)
    qk, pv = _key_ops(k_ref, v_ref, past, bufs, tk)
    q0 = q_off + pl.program_id(1) * tq
    diag = q0 // tk
    fq_blk = fq_ref[0]
    ones = jnp.ones((tk, LANES), BF16)

    group = acc_ref.shape[0]

    def head_group(g, _):
        hps = [g * (group // 2) + pp for pp in range(group // 2)]
        lanes = [_pair_lanes(hp) for hp in hps]
        qm = [_keep_head(q_ref[0, :, lanes[hh // 2]], hh % 2) for hh in range(group)]
        fq = [_head_column(fq_blk, g * group + hh) for hh in range(group)]
        m_ref[...] = jnp.full(m_ref.shape, NEG, F32)
        acc_ref[...] = jnp.zeros(acc_ref.shape, F32)

        def tiles(js, masked):
            k0 = [j * tk if isinstance(j, int) else pl.multiple_of(j * tk, tk) for j in js]
            chains = [(t, hh) for t in range(len(js)) for hh in range(group)]
            if masked:
                qpos, kpos = _positions(tq, tk, q0, k0[0])
                causal = kpos <= qpos
            s = {}
            for t, hh in chains:
                fk = fk_ref[0, g * group + hh, :, pl.ds(k0[t], tk)]
                s[t, hh] = qk(qm[hh], hps[hh // 2], k0[t]) + fq[hh] - fk
                if masked and t == 0:
                    s[t, hh] = jnp.where(causal, s[t, hh], NEG)
            m_old = [m_ref[hh] for hh in range(group)]
            m_new = list(m_old)
            for t, hh in chains:
                m_new[hh] = jnp.maximum(m_new[hh], jnp.max(s[t, hh], axis=1, keepdims=True))
            p = {c: jnp.exp2(s[c] - m_new[c[1]]).astype(BF16) for c in chains}
            for hh in range(group):
                upd = jnp.exp2(m_old[hh] - m_new[hh]) * acc_ref[hh]
                for t in range(len(js)):
                    upd = upd + pv(p[t, hh], hps[hh // 2], k0[t], ones)
                acc_ref[hh] = upd
                m_ref[hh] = m_new[hh]

        tiles([diag], True)

        def body(i, _):
            j = diag - 1 - 2 * i
            tiles([j, j - 1], False)
            return 0

        lax.fori_loop(0, diag // 2, body, 0)

        @pl.when(diag % 2 == 1)
        def _():
            tiles([0], False)
        for pp in range(group // 2):
            o0 = acc_ref[2 * pp, :, :LANES] / acc_ref[2 * pp, :, LANES:]
            o1 = acc_ref[2 * pp + 1, :, :LANES] / acc_ref[2 * pp + 1, :, LANES:]
            o_ref[0, :, lanes[pp]] = _merge_pair(o0, o1).astype(o_ref.dtype)
        return 0

    lax.fori_loop(0, N_HEADS // group, head_group, 0)


def _fox(q, k, v, past, fq, fk_t, *, tq, tk, q_off):
    B, T, _ = q.shape
    Lp = fk_t.shape[-1]
    qspec, kspec = _pair_specs(tq, k.shape[1])
    past_specs, bufs, past = _past_specs_and_bufs(past, T, Lp)
    return pl.pallas_call(
        functools.partial(_fox_kernel, tq=tq, tk=tk, q_off=q_off, has_past=bool(past)),
        grid=(B, T // tq),
        in_specs=[qspec, kspec, kspec] + past_specs
        + [pl.BlockSpec((1, tq, N_HEADS), lambda b, qi: (b, qi, 0)),
           pl.BlockSpec((1, N_HEADS, 1, Lp), lambda b, qi: (b, 0, 0, 0))],
        out_specs=qspec,
        out_shape=jax.ShapeDtypeStruct(q.shape, BF16),
        scratch_shapes=[pltpu.VMEM((FOX_GROUP, tq, 1), F32), pltpu.VMEM((FOX_GROUP, tq, 2 * LANES), F32)] + bufs,
        compiler_params=_cparams(("parallel", "parallel")),
        name="fox_attn",
    )(q, k, v, *past, fq, fk_t[:, :, None, :])


def _sb_kernel(q_ref, k_ref, v_ref, *rest, tq, tk, q_off, has_past):
    past, (o_ref, acc_ref, tail_ref), bufs = _split_past(rest, has_past)
    qk, pv = _key_ops(k_ref, v_ref, past, bufs, tk)
    q0 = q_off + pl.program_id(1) * tq
    diag = q0 // tk
    r = lax.broadcasted_iota(jnp.int32, (LANES, 2 * LANES), 0)
    c = lax.broadcasted_iota(jnp.int32, (LANES, 2 * LANES), 1)
    suffix_w = jnp.where((c >= LANES) | (r > c), 1.0, 0.0).astype(BF16)

    group = acc_ref.shape[0]

    def head_group(g, _):
        hps = [g * (group // 2) + pp for pp in range(group // 2)]
        lanes = [_pair_lanes(hp) for hp in hps]
        qm = [_keep_head(q_ref[0, :, lanes[hh // 2]], hh % 2) for hh in range(group)]
        acc_ref[...] = jnp.zeros(acc_ref.shape, F32)
        tail_ref[...] = jnp.zeros(tail_ref.shape, F32)

        def tiles(js, masked):
            k0 = [j * tk if isinstance(j, int) else pl.multiple_of(j * tk, tk) for j in js]
            chains = [(t, hh) for t in range(len(js)) for hh in range(group)]
            if masked:
                qpos, kpos = _positions(tq, tk, q0, k0[0])
                strict = kpos < qpos
            z = {c: qk(qm[c[1]], hps[c[1] // 2], k0[c[0]]) for c in chains}
            ls, la, after = {}, {}, {}
            for c in chains:
                ls[c] = jnp.minimum(z[c], 0.0) - jnp.log2(1.0 + jnp.exp2(-jnp.abs(z[c])))
                la[c] = ls[c] - z[c]
                if masked and c[0] == 0:
                    la[c] = jnp.where(strict, la[c], 0.0)
                    ls[c] = jnp.where(strict, ls[c], NEG)
                la[c] = la[c].astype(BF16)
            for hh in range(group):
                run = tail_ref[hh]
                for t in range(len(js)):
                    blocks = [None] * (tk // LANES)
                    for b in reversed(range(tk // LANES)):
                        res = _dot(la[t, hh][:, b * LANES:(b + 1) * LANES], suffix_w)
                        blocks[b] = res[:, :LANES] + run
                        run = run + res[:, LANES:]
                    after[t, hh] = jnp.concatenate(blocks, axis=1)
                tail_ref[hh] = run
            a = {c: jnp.exp2(ls[c] + after[c]).astype(BF16) for c in chains}
            for c in chains:
                acc_ref[c[1]] += pv(a[c], hps[c[1] // 2], k0[c[0]], None)

        tiles([diag], True)

        def body(i, _):
            j = diag - 1 - 2 * i
            tiles([j, j - 1], False)
            return 0

        lax.fori_loop(0, diag // 2, body, 0)

        @pl.when(diag % 2 == 1)
        def _():
            tiles([0], False)
        for pp in range(group // 2):
            o_ref[0, :, lanes[pp]] = _merge_pair(acc_ref[2 * pp], acc_ref[2 * pp + 1]).astype(o_ref.dtype)
        return 0

    lax.fori_loop(0, N_HEADS // group, head_group, 0)


def _sb(q, k, v, past, Lp, *, tq, tk, q_off):
    B, T, _ = q.shape
    qspec, kspec = _pair_specs(tq, k.shape[1])
    past_specs, bufs, past = _past_specs_and_bufs(past, T, Lp)
    return pl.pallas_call(
        functools.partial(_sb_kernel, tq=tq, tk=tk, q_off=q_off, has_past=bool(past)),
        grid=(B, T // tq),
        in_specs=[qspec, kspec, kspec] + past_specs,
        out_specs=qspec,
        out_shape=jax.ShapeDtypeStruct(q.shape, BF16),
        scratch_shapes=[pltpu.VMEM((SB_GROUP, tq, LANES), F32), pltpu.VMEM((SB_GROUP, tq, LANES), F32)] + bufs,
        compiler_params=_cparams(("parallel", "parallel")),
        name="sb_attn",
    )(q, k, v, *past)


def _const_key(v):
    bits = int(np.float32(v).view(np.int32))
    return (bits ^ 0x7FFFFFFF) + 1 if bits < 0 else bits


def _float_key(x):
    bits = lax.bitcast_convert_type(x, jnp.int32)
    return jnp.where(bits < 0, jnp.bitwise_xor(bits, 0x7FFFFFFF) + 1, bits)


KEY_MIN = -2 ** 31
KEY_HALF_NEG = _const_key(0.5 * NEG)


KEY_MAX = 2 ** 31 - 1
KEY_BITS = 32
COUNT_CHAINS = 8
SB_GROUP = 8
FOX_GROUP = 2


def _dsa_kernel(aq_ref, iq_ref, iwt_ref, k_ref, vt_ref, ik_ref, o_ref, key_ref, q8_ref, m_ref, acc_ref,
                *, tq, tk, q_off, n_rows, topk):
    q0 = q_off + pl.program_id(1) * tq
    n_tiles = (q0 + n_rows + tk - 1) // tk

    def tile_start(j):
        return pl.multiple_of(j * tk, LANES)

    iq = iq_ref[0]
    iwt = iwt_ref[0]
    qh = jnp.concatenate([_keep_head(iq[:, (h // 2) * LANES:(h // 2 + 1) * LANES], h % 2) for h in range(N_IDX)],
                         axis=0)
    q_chunk = jnp.right_shift(q0 + lax.broadcasted_iota(jnp.int32, (tk, tq), 1), CHUNK_SHIFT)

    def score_tile(j, _):
        k0 = tile_start(j)
        dots = _dot_t(ik_ref[0, pl.ds(k0, tk), :], qh)
        sc = jnp.zeros((tk, tq), F32)
        for h in range(N_IDX):
            sc = sc + jnp.maximum(dots[:, h * tq:(h + 1) * tq], 0.0) * iwt[h:h + 1, :]
        key_chunk = jnp.right_shift(k0 + lax.broadcasted_iota(jnp.int32, (tk, tq), 0), CHUNK_SHIFT)
        key_ref[pl.ds(k0, tk), :] = _float_key(jnp.where(key_chunk <= q_chunk, sc, NEG))
        return 0

    lax.fori_loop(0, n_tiles, score_tile, 0)

    def search():
        full_tiles = (q0 + n_rows) // tk
        tail_blocks = (q0 + n_rows - full_tiles * tk + LANES - 1) // LANES

        def step(_, cr):
            lo, hi, above_hi = cr
            mid = jnp.right_shift(lo, 1) + jnp.right_shift(hi, 1) + jnp.bitwise_and(jnp.bitwise_and(lo, hi), 1)

            def count_rows(r0, rows, cnt):
                above = jnp.where(key_ref[pl.ds(pl.multiple_of(r0, LANES), rows), :] > mid, 1.0, 0.0)
                return cnt + jnp.sum(above.reshape((-1,) + cnt.shape), axis=0)

            cnt = lax.fori_loop(0, full_tiles, lambda j, c: count_rows(j * tk, tk, c),
                                jnp.zeros((COUNT_CHAINS, SUBLANES, tq), F32))
            cnt = lax.fori_loop(0, tail_blocks, lambda b, c: count_rows(full_tiles * tk + b * LANES, LANES, c), cnt)
            cnt = jnp.sum(jnp.sum(cnt, axis=0), axis=0, keepdims=True)
            less = cnt < topk
            return jnp.where(less, lo, mid), jnp.where(less, mid, hi), jnp.where(less, cnt, above_hi)

        init = (jnp.full((1, tq), KEY_MIN, jnp.int32), jnp.full((1, tq), KEY_MAX, jnp.int32), jnp.zeros((1, tq), F32))
        _, hi, above_hi = lax.fori_loop(0, KEY_BITS, step, init)
        return hi, topk - above_hi

    thr, room = lax.cond(q0 + n_rows > topk, search,
                         lambda: (jnp.full((1, tq), KEY_MIN, jnp.int32), jnp.zeros((1, tq), F32)))
    row_ok = lax.broadcasted_iota(jnp.int32, (1, tq), 1) < n_rows
    room = jnp.where(row_ok, room, 0.0)

    @pl.when(jnp.max(room) > 0.0)
    def _():
        r = lax.broadcasted_iota(jnp.int32, (LANES, LANES), 0)
        c = lax.broadcasted_iota(jnp.int32, (LANES, LANES), 1)
        before = jnp.where(c < r, 1.0, 0.0).astype(BF16)

        def tie_tile(j, seen):
            k0 = tile_start(j)
            for b in range(tk // LANES):
                rows = pl.ds(k0 + b * LANES, LANES)
                keys = key_ref[rows, :]
                eq = keys == thr
                eqf = jnp.where(eq, 1.0, 0.0)
                rank = _dot(before, eqf.astype(BF16)) + seen
                key_ref[rows, :] = jnp.where(eq, jnp.where(rank < room, keys + 1, keys), keys)
                seen = seen + jnp.sum(eqf, axis=0, keepdims=True)
            return seen

        lax.fori_loop(0, n_tiles, tie_tile, jnp.zeros((1, tq), F32))

    aq = aq_ref[0]
    for h in range(N_HEADS):
        q8_ref[h * tq:(h + 1) * tq, :] = _keep_head(aq[:, (h // 2) * LANES:(h // 2 + 1) * LANES], h % 2)
    m_ref[...] = jnp.full(m_ref.shape, NEG, F32)
    acc_ref[...] = jnp.zeros(acc_ref.shape, F32)
    thr_sel = jnp.maximum(thr, KEY_HALF_NEG)

    def attend_tiles(js):
        k0 = [j * tk if isinstance(j, int) else tile_start(j) for j in js]
        groups = [slice(g * (N_HEADS // 2) * tq, (g + 1) * (N_HEADS // 2) * tq) for g in (0, 1)]
        chains = [(t, g) for t in range(len(js)) for g in (0, 1)]
        kt = [k_ref[0, pl.ds(k, tk), :] for k in k0]
        bias = [jnp.where(key_ref[pl.ds(k, tk), :] > thr_sel, 0.0, NEG) for k in k0]
        st = {(t, g): _dot_t(kt[t], q8_ref[groups[g], :]) for t, g in chains}
        st = {(t, g): jnp.concatenate([st[t, g][:, h * tq:(h + 1) * tq] + bias[t] for h in range(N_HEADS // 2)], axis=1)
              for t, g in chains}
        m_old = [m_ref[:, cols] for cols in groups]
        m_new = list(m_old)
        for t, g in chains:
            m_new[g] = jnp.maximum(m_new[g], jnp.max(st[t, g], axis=0, keepdims=True))
        p = {(t, g): jnp.exp2(st[t, g] - m_new[g]).astype(BF16) for t, g in chains}
        for g, cols in enumerate(groups):
            upd = jnp.exp2(m_old[g] - m_new[g]) * acc_ref[:, cols]
            for t in range(len(js)):
                upd = upd + _dot(vt_ref[0, :, pl.ds(k0[t], tk)], p[t, g])
            acc_ref[:, cols] = upd
            m_ref[:, cols] = m_new[g]

    def attend_pair(i, _):
        attend_tiles([2 * i, 2 * i + 1])
        return 0

    lax.fori_loop(0, n_tiles // 2, attend_pair, 0)

    @pl.when(n_tiles % 2 == 1)
    def _():
        attend_tiles([n_tiles - 1])
    for hp in range(N_PAIRS):
        pair = []
        for h in (2 * hp, 2 * hp + 1):
            a = acc_ref[:, h * tq:(h + 1) * tq]
            pair.append(a[:HEAD_DIM] / a[HEAD_DIM:])
        o_ref[0, :, hp * LANES:(hp + 1) * LANES] = jnp.concatenate(pair, axis=0).T.astype(o_ref.dtype)


def _dsa(aq, iq, iw, k2, v1t, ik2, *, tq, q_off, topk):
    B, T, _ = aq.shape
    Lp = k2.shape[1]
    tk = Lp if T <= LANES else next(t for t in (512, 384, 256, 128) if Lp % t == 0)
    Tp = -(-T // LANES) * LANES
    tq = max(tq, LANES)
    n_rows = min(T, tq)
    pad_t = lambda a: jnp.pad(a, ((0, 0), (0, Tp - T), (0, 0)))
    qspec = lambda w: pl.BlockSpec((1, tq, w), lambda b, qi: (b, qi, 0))
    kspec = pl.BlockSpec((1, Lp, LANES), lambda b, qi: (b, 0, 0))
    out = pl.pallas_call(
        functools.partial(_dsa_kernel, tq=tq, tk=tk, q_off=q_off, n_rows=n_rows, topk=topk),
        grid=(B, Tp // tq),
        in_specs=[qspec(BRANCH_W), qspec(N_IDX * IDX_DIM), pl.BlockSpec((1, SUBLANES, tq), lambda b, qi: (b, 0, qi)),
                  kspec, pl.BlockSpec((1, LANES, Lp), lambda b, qi: (b, 0, 0)), kspec],
        out_specs=qspec(BRANCH_W),
        out_shape=jax.ShapeDtypeStruct((B, Tp, BRANCH_W), BF16),
        scratch_shapes=[pltpu.VMEM((Lp, tq), jnp.int32), pltpu.VMEM((N_HEADS * tq, LANES), BF16),
                        pltpu.VMEM((1, N_HEADS * tq), F32), pltpu.VMEM((LANES, N_HEADS * tq), F32)],
        compiler_params=_cparams(("parallel", "parallel")),
        name="dsa_attn",
    )(pad_t(aq), pad_t(iq), jnp.swapaxes(pad_t(iw), 1, 2), k2, v1t, ik2)
    return out[:, :T]


def _merge_kernel(x_ref, oa_ref, ob_ref, oc_ref, g_ref, wg_ref, wb_ref, wo_ref, y_ref):
    x = x_ref[0]
    D = x.shape[-1]
    xn = _rms(x, g_ref[...]).astype(BF16)
    merged = jnp.zeros(x.shape, F32)
    for n, o_ref in enumerate((oa_ref, ob_ref, oc_ref)):
        gate = _sigmoid(_dot(xn, wg_ref[:, n * D:(n + 1) * D]))
        merged = merged + gate * _dot(o_ref[0], wb_ref[n])
    y_ref[0] = x + _dot(merged.astype(BF16), wo_ref[...])


def _merge(x, oa, ob, oc, g, wg, wb, wo, tm):
    B, T, D = x.shape
    tspec = lambda w_: pl.BlockSpec((1, tm, w_), lambda b, t: (b, t, 0))
    const = lambda a: pl.BlockSpec(a.shape, lambda b, t: (0,) * a.ndim)
    return pl.pallas_call(
        _merge_kernel,
        grid=(B, T // tm),
        in_specs=[tspec(D), tspec(BRANCH_W), tspec(BRANCH_W), tspec(BRANCH_W), const(g), const(wg), const(wb),
                  const(wo)],
        out_specs=tspec(D),
        out_shape=jax.ShapeDtypeStruct(x.shape, F32),
        compiler_params=_cparams(("parallel", "parallel")),
        name="merge_out",
    )(x, oa, ob, oc, g, wg, wb, wo)


def _ffn_kernel(x_ref, g_ref, past_ref, wu_ref, cw_ref, cb_ref, wd_ref, gf_ref, y_ref, st_ref,
                hbuf_ref, carry_ref, acc_ref, *, tm, final_norm):
    fc = FF_CHUNK
    d_ff = wd_ref.shape[0]
    n_chunks = d_ff // fc

    @pl.when(pl.program_id(1) == 0)
    def _():
        carry_ref[...] = past_ref[0]

    x = x_ref[0]
    xn = _rms(x, g_ref[...]).astype(BF16)
    acc_ref[...] = jnp.zeros(acc_ref.shape, F32)

    def up_proj(c):
        return jnp.concatenate([_dot(xn, wu_ref[:, c * fc:(c + 1) * fc]),
                                _dot(xn, wu_ref[:, d_ff + c * fc:d_ff + (c + 1) * fc])], axis=1)

    h_next = up_proj(0)
    for c in range(n_chunks):
        hbuf = hbuf_ref.at[c % 2]
        both = lambda ref: jnp.concatenate([ref[:, c * fc:(c + 1) * fc], ref[:, d_ff + c * fc:d_ff + (c + 1) * fc]],
                                           axis=1)
        h = h_next
        if c + 1 < n_chunks:
            h_next = up_proj(c + 1)
        hbuf[0:SUBLANES, :] = carry_ref[c]
        hbuf[SUBLANES:SUBLANES + tm, :] = h
        carry_ref[c] = h[tm - SUBLANES:tm, :]
        cw = both(cw_ref)
        hc = (both(cb_ref) + cw[0:1] * hbuf[SUBLANES - 2:SUBLANES - 2 + tm, :]
              + cw[1:2] * hbuf[SUBLANES - 1:SUBLANES - 1 + tm, :] + cw[2:3] * h)
        gate, up = hc[:, :fc], hc[:, fc:]
        act = gate * _sigmoid(gate) * up
        acc_ref[...] += _dot(act.astype(BF16), wd_ref[c * fc:(c + 1) * fc, :])
    st_ref[0] = carry_ref[...]
    y = x + acc_ref[...]
    if final_norm:
        y = _rms(y, gf_ref[...])
    y_ref[0] = y


def _ffn(x, g, past, wu, cw, cb, wd, gf, tm, final_norm):
    B, T, D = x.shape
    nc, fc2 = wd.shape[0] // FF_CHUNK, 2 * FF_CHUNK
    tspec = pl.BlockSpec((1, tm, D), lambda b, t: (b, t, 0))
    const = lambda a: pl.BlockSpec(a.shape, lambda b, t: (0,) * a.ndim, pipeline_mode=pl.Buffered(1))
    sspec = pl.BlockSpec((1, nc, SUBLANES, fc2), lambda b, t: (b, 0, 0, 0))
    return pl.pallas_call(
        functools.partial(_ffn_kernel, tm=tm, final_norm=final_norm),
        grid=(B, T // tm),
        in_specs=[tspec, const(g), sspec, const(wu), const(cw), const(cb), const(wd), const(gf)],
        out_specs=[tspec, sspec],
        out_shape=[jax.ShapeDtypeStruct(x.shape, F32), jax.ShapeDtypeStruct((B, nc, SUBLANES, fc2), F32)],
        scratch_shapes=[pltpu.VMEM((2, tm + SUBLANES, fc2), F32), pltpu.VMEM((nc, SUBLANES, fc2), F32),
                        pltpu.VMEM((tm, D), F32)],
        compiler_params=_cparams(("arbitrary", "arbitrary")),
        name="conv_ffn",
    )(x, g, past, wu, cw, cb, wd, gf)


def _rope_tables(pos):
    half = ROT_DIM // 2
    freq = ROPE_THETA ** (-jnp.arange(half, dtype=F32) / half)
    ang = pos.astype(F32)[:, None] * freq[None, :]
    lane = np.arange(LANES) % HEAD_DIM
    cos = jnp.cos(ang)[:, lane % half]
    sin = jnp.sin(ang)[:, lane % half]
    rc = jnp.where(lane < ROT_DIM, cos, 1.0)
    rs1 = jnp.where(lane < half, -sin, 0.0)
    rs2 = jnp.where((lane >= half) & (lane < ROT_DIM), sin, 0.0)
    return rc, rs1, rs2


IN_SIZES = (BRANCH_W, HEAD_DIM, HEAD_DIM, N_IDX * IDX_DIM, IDX_DIM, N_IDX) + (BRANCH_W,) * 6 + (N_HEADS,)
PACKED_W = ROPE_W + MISC_W + 6 * BRANCH_W


def _pack_w_kernel(w_ref, o_ref, s_ref):
    offs = np.concatenate([[0], np.cumsum(IN_SIZES)]).tolist()
    a_q, a_k, a_v, i_q, i_k, i_w, b_q = offs[:7]
    c_f = offs[12]
    src = lambda c0, width: w_ref[0, :, c0:c0 + width]
    s_ref[:, 0:BRANCH_W] = src(a_q, BRANCH_W)
    s_ref[:, BRANCH_W:BRANCH_W + 256] = src(i_q, 256)
    for dst, c0 in ((768, a_k), (896, i_k), (ROPE_W, a_v)):
        s_ref[:, dst:dst + HEAD_DIM] = src(c0, HEAD_DIM)
        s_ref[:, dst + HEAD_DIM:dst + 2 * HEAD_DIM] = src(c0, HEAD_DIM)
    misc = ROPE_W + LANES
    s_ref[:, misc:misc + LANES] = jnp.zeros((s_ref.shape[0], LANES), F32)
    s_ref[:, misc:misc + N_IDX] = src(i_w, N_IDX)
    s_ref[:, misc + CF_LANE:misc + CF_LANE + N_HEADS] = src(c_f, N_HEADS)
    s_ref[:, ROPE_W + MISC_W:PACKED_W] = src(b_q, 6 * BRANCH_W)
    o_ref[...] = s_ref[...].astype(BF16)


def _pack_w(w_in, l):
    D = w_in.shape[1]
    rows = 256
    return pl.pallas_call(
        _pack_w_kernel,
        grid=(D // rows,),
        in_specs=[pl.BlockSpec((1, rows, w_in.shape[2]), lambda r: (l, r, 0))],
        out_specs=pl.BlockSpec((rows, PACKED_W), lambda r: (r, 0)),
        out_shape=jax.ShapeDtypeStruct((D, PACKED_W), BF16),
        scratch_shapes=[pltpu.VMEM((rows, PACKED_W), F32)],
        compiler_params=_cparams(("parallel",)),
        name="pack_w_in",
    )(w_in)


def _pack_layer(l, norm_mix, w_in, b_f, w_branch, w_gate, w_out, norm_ffn, w_up, conv_w, conv_b, w_down):
    w = _pack_w(w_in, l)
    bf_row = jnp.zeros((1, LANES), F32).at[0, CF_LANE:CF_LANE + N_HEADS].set(b_f[l])

    d_ff = w_down.shape[1]
    fc = FF_CHUNK
    nc = d_ff // fc
    assert nc * fc == d_ff
    halves = lambda a: jnp.concatenate([a[..., :d_ff].reshape(a.shape[:-1] + (nc, fc)),
                                        a[..., d_ff:].reshape(a.shape[:-1] + (nc, fc))], axis=-1)
    wu, cw, cb, wd = w_up[l].astype(BF16), conv_w[l], conv_b[l][None], w_down[l].astype(BF16)
    return dict(g_mix=norm_mix[l][None], w=w, bf_row=bf_row, wg=w_gate[l].astype(BF16),
                wb=w_branch[l].astype(BF16), wo=w_out[l].astype(BF16), g_ffn=norm_ffn[l][None],
                wu=wu, cw=cw, cb=cb, wd=wd, halves=halves, nc=nc, d_ff=d_ff)


def _conv_state_in(state, halves):
    st = jnp.moveaxis(halves(state), 2, 1)
    return jnp.pad(st, ((0, 0), (0, 0), (SUBLANES - (CONV_W - 1), 0), (0, 0)))


def _conv_state_out(st, d_ff):
    st = st[:, :, SUBLANES - (CONV_W - 1):, :]
    B, nc, r, fc2 = st.shape
    fc = fc2 // 2
    gate = jnp.moveaxis(st[..., :fc], 1, 2).reshape(B, r, d_ff)
    up = jnp.moveaxis(st[..., fc:], 1, 2).reshape(B, r, d_ff)
    return jnp.concatenate([gate, up], axis=-1)


def _pad_keys(a, Lp):
    return jnp.pad(a, ((0, 0), (0, Lp - a.shape[1]), (0, 0)))


def _group_layer(x, past, lw, gf, layer, depth, stacked, tiles):
    B, T, D = x.shape
    P = 0 if past is None else past[0].shape[1]
    L = P + T
    Lp = -(-L // KEY_TILE) * KEY_TILE
    topk = max(1, min(TOPK_MAX, L // 4))
    tables = _rope_tables(jnp.arange(P, L))

    stacked, (iw, aq16, iq16, ak16, avt16, ik16, bq16, bk16, bv16, cq16, ck16, cv16) = _project(
        x, lw["g_mix"], lw["w"], lw["bf_row"], tables, tiles["proj"], layer, depth, stacked)
    lf = stacked[-1][layer]

    if past is None:
        keys = (ak16, ik16)
        lf_all = lf
        conv_in = jnp.zeros((B, lw["nc"], SUBLANES, 2 * FF_CHUNK), F32)
        past_b = past_c = None
    else:
        p_ak, p_av, p_ik, p_bk, p_bv, p_ck, p_cv, p_lf, p_conv = past
        dup = lambda a: jnp.concatenate([a, a], axis=-1).astype(BF16)
        keys = tuple(jnp.concatenate([o, n], axis=1) for o, n in zip((dup(p_ak), dup(p_ik)), (ak16, ik16)))
        old_vt = jnp.swapaxes(p_av, 1, 2).astype(BF16)
        avt16 = jnp.concatenate([jnp.concatenate([old_vt, jnp.ones_like(old_vt)], axis=1), avt16], axis=2)
        lf_all = jnp.concatenate([p_lf, lf], axis=1)
        conv_in = _conv_state_in(p_conv, lw["halves"])
        past_b, past_c = (layer, p_bk, p_bv), (layer, p_ck, p_cv)
    ak2, ik2 = (_pad_keys(a, Lp) for a in keys)
    v1t = jnp.pad(avt16, ((0, 0), (0, 0), (0, Lp - L)))

    fk_t = _cumsum_time(jnp.swapaxes(_pad_keys(lf_all, Lp), 1, 2))
    fq = jnp.swapaxes(fk_t[:, :, P:L], 1, 2)

    o_a = _dsa(aq16, iq16, iw, ak2, v1t, ik2, tq=tiles["dsa"], q_off=P, topk=topk)
    o_b = _sb(bq16, bk16, bv16, past_b, Lp, tq=tiles["sb"][0], tk=tiles["sb"][1], q_off=P)
    o_c = _fox(cq16, ck16, cv16, past_c, fq, fk_t, tq=tiles["fox"][0], tk=tiles["fox"][1], q_off=P)

    x = _merge(x, o_a, o_b, o_c, lw["g_mix"], lw["wg"], lw["wb"], lw["wo"], tiles["merge"])
    x, st = _ffn(x, lw["g_ffn"], conv_in, lw["wu"], lw["cw"], lw["cb"], lw["wd"], gf, tiles["ffn"],
                 layer == depth - 1)
    return x, stacked, _conv_state_out(st, lw["d_ff"])


def _tiles(T, P):
    pick = lambda want: min(want, T)
    if T % 512 == 0 and P % 512 == 0:
        sb, fox = (256, 256), (256, 512)
    else:
        assert T <= KEY_TILE
        sb = fox = (T, -(-(P + T) // KEY_TILE) * KEY_TILE)
    return dict(proj=pick(512), dsa=pick(128), sb=sb, fox=fox, merge=pick(512), ffn=pick(512))


def kernel(x_prompt, x_sample, cache_a_k, cache_a_v, cache_a_idx_k, cache_b_k, cache_b_v, cache_c_k, cache_c_v,
           cache_c_logf, state_ffn_conv, norm_mix, w_in, b_f, w_branch, w_gate, w_out, norm_ffn, w_up, conv_w,
           conv_b, w_down, norm_final):
    depth = w_in.shape[0]
    caches = (cache_a_k, cache_a_v, cache_a_idx_k, cache_b_k, cache_b_v, cache_c_k, cache_c_v, cache_c_logf,
              state_ffn_conv)
    layers = [_pack_layer(l, norm_mix, w_in, b_f, w_branch, w_gate, w_out, norm_ffn, w_up, conv_w, conv_b, w_down)
              for l in range(depth)]
    gf = norm_final[None]

    def trunk(x, past):
        B, T, _ = x.shape
        tiles = _tiles(T, 0 if past is None else past[0].shape[2])
        stacked, conv_states = None, []
        if past is not None:
            P = past[0].shape[2]
            wide = lambda c: jnp.transpose(c, (0, 1, 3, 4, 2)).reshape(depth, B, N_PAIRS, LANES, P)
        for l in range(depth):
            lp = None if past is None else tuple(wide(c) if 3 <= i <= 6 else c[l] for i, c in enumerate(past))
            x, stacked, conv_state = _group_layer(x, lp, layers[l], gf, l, depth, stacked, tiles)
            conv_states.append(conv_state)
        ak, av, ik, bk, bv, ck, cv, lf = stacked
        heads = lambda a: a.reshape(depth, B, T, N_HEADS, HEAD_DIM)
        return x, (ak, av, ik, heads(bk), heads(bv), heads(ck), heads(cv), lf, jnp.stack(conv_states, axis=0))

    y_prompt, p_state = trunk(x_prompt, None)
    y_sample, s_state = trunk(x_sample, caches)
    return (y_prompt, y_sample) + p_state + s_state
```

```python
import functools

import jax
import jax.numpy as jnp
import numpy as np
from jax import lax
from jax.experimental import pallas as pl
from jax.experimental.pallas import tpu as pltpu

F32 = jnp.float32
BF16 = jnp.bfloat16

LANES = 128
SUBLANES = 8
HEAD_DIM = 64
N_HEADS = 8
N_PAIRS = N_HEADS // 2
BRANCH_W = N_HEADS * HEAD_DIM
N_IDX = 4
IDX_DIM = 64
CHUNK = 64
CHUNK_SHIFT = CHUNK.bit_length() - 1
TOPK_MAX = 256
ROT_DIM = HEAD_DIM // 4
ROPE_THETA = 500000.0
CONV_W = 3
EPS = 1e-6
NEG = -1e30
KEY_TILE = 128
FF_CHUNK = 256
VMEM_LIMIT = 56 * 1024 * 1024

ROPE_W = 1024
MISC_W = 256
CF_LANE = 8
LOG2E = 1.4426950408889634
Q_SCALE = HEAD_DIM ** -0.5 * LOG2E


def _cparams(sem):
    return pltpu.CompilerParams(dimension_semantics=sem, vmem_limit_bytes=VMEM_LIMIT)


def _rms(x, g):
    return x * lax.rsqrt(jnp.mean(x * x, axis=-1, keepdims=True) + EPS) * g


def _sigmoid(z):
    return 1.0 / (1.0 + jnp.exp(-z))


def _log_sigmoid(z):
    return jnp.minimum(z, 0.0) - jnp.log1p(jnp.exp(-jnp.abs(z)))


def _dot_t(a, b):
    return lax.dot_general(a, b, (((1,), (1,)), ((), ())), preferred_element_type=F32)


def _dot(a, b):
    return jnp.dot(a, b, preferred_element_type=F32)


def _keep_head(blk, which):
    lane = lax.broadcasted_iota(jnp.int32, blk.shape, blk.ndim - 1)
    keep = (lane >= HEAD_DIM) if which else (lane < HEAD_DIM)
    return jnp.where(keep, blk, jnp.zeros_like(blk))


def _merge_pair(o0, o1):
    lane = lax.broadcasted_iota(jnp.int32, o0.shape, o0.ndim - 1)
    return jnp.where(lane < HEAD_DIM, o0, o1)


N_CACHE_ROWS = 8


def _proj_kernel(x_ref, g_ref, w_ref, bf_ref, rc_ref, rs1_ref, rs2_ref, *refs):
    (ak_ref, av_ref, ik_ref, bk_ref, bv_ref, ck_ref, cv_ref, lf_ref, iw_ref,
     aq16, iq16, ak16, avt16, ik16, bq16, bk16, bv16, cq16, ck16, cv16) = refs[-(N_CACHE_ROWS + 12):]
    xn = _rms(x_ref[0], g_ref[...]).astype(BF16)
    rc, rs1, rs2 = rc_ref[...], rs1_ref[...], rs2_ref[...]

    def cols(c0, width):
        return _dot(xn, w_ref[:, c0:c0 + width])

    def group(blk, g):
        return blk[:, g * LANES:(g + 1) * LANES]

    def rope(blk):
        half = ROT_DIM // 2
        return blk * rc + pltpu.roll(blk, LANES - half, 1) * rs1 + pltpu.roll(blk, half, 1) * rs2

    wide = cols(0, BRANCH_W)
    for g in range(4):
        aq16[0, :, g * LANES:(g + 1) * LANES] = (rope(group(wide, g)) * Q_SCALE).astype(BF16)
    wide = cols(BRANCH_W, ROPE_W - BRANCH_W)
    for g in range(2):
        iq16[0, :, g * LANES:(g + 1) * LANES] = (rope(group(wide, g)) * IDX_DIM ** -0.5).astype(BF16)
    blk = rope(group(wide, 2))
    ak_ref[0, 0] = blk[:, :HEAD_DIM]
    ak16[0] = blk.astype(BF16)
    blk = rope(group(wide, 3))
    ik_ref[0, 0] = blk[:, :IDX_DIM]
    ik16[0] = blk.astype(BF16)

    wide = cols(ROPE_W, MISC_W)
    blk = group(wide, 0)
    av_ref[0, 0] = blk[:, :HEAD_DIM]
    row = lax.broadcasted_iota(jnp.int32, (LANES, blk.shape[0]), 0)
    avt16[0] = jnp.where(row < HEAD_DIM, blk.T, 1.0).astype(BF16)
    blk = group(wide, 1)
    iw_ref[0] = blk[:, :SUBLANES] * N_IDX ** -0.5
    lf_ref[0, 0] = _log_sigmoid(blk[:, CF_LANE:CF_LANE + N_HEADS] + bf_ref[:, CF_LANE:CF_LANE + N_HEADS])

    base = ROPE_W + MISC_W
    outs = ((None, bq16), (bk_ref, bk16), (bv_ref, bv16), (None, cq16), (ck_ref, ck16), (cv_ref, cv16))
    for n, (o32, o16) in enumerate(outs):
        wide = cols(base + n * BRANCH_W, BRANCH_W)
        if o32 is None:
            o16[0] = (wide * Q_SCALE).astype(BF16)
        else:
            o32[0, 0] = wide
            o16[0] = wide.astype(BF16)


def _project(x, g, w, bf_row, tables, tm, layer, depth, stacked):
    B, T, D = x.shape
    tok = lambda w_, dt: jax.ShapeDtypeStruct((B, T, w_), dt)
    tspec = lambda w_: pl.BlockSpec((1, tm, w_), lambda b, t: (b, t, 0))
    const = lambda a: pl.BlockSpec(a.shape, lambda b, t: (0,) * a.ndim, pipeline_mode=pl.Buffered(1))
    cache_w = (HEAD_DIM, HEAD_DIM, IDX_DIM) + (BRANCH_W,) * 4 + (N_HEADS,)
    cache_shapes = [jax.ShapeDtypeStruct((depth, B, T, w_), F32) for w_ in cache_w]
    cache_specs = [pl.BlockSpec((1, 1, tm, w_), lambda b, t: (layer, b, t, 0)) for w_ in cache_w]
    work = [(tok(w_, dt), tspec(w_)) for w_, dt in ((SUBLANES, F32), (BRANCH_W, BF16), (N_IDX * IDX_DIM, BF16),
                                                       (LANES, BF16))]
    work += [(jax.ShapeDtypeStruct((B, LANES, T), BF16), pl.BlockSpec((1, LANES, tm), lambda b, t: (b, 0, t)))]
    work += [(tok(w_, BF16), tspec(w_)) for w_ in (LANES,) + (BRANCH_W,) * 6]
    work_shapes, work_specs = [s for s, _ in work], [p for _, p in work]
    rspec = pl.BlockSpec((tm, LANES), lambda b, t: (t, 0))
    prev = () if stacked is None else tuple(stacked)
    n_in = 7
    outs = pl.pallas_call(
        _proj_kernel,
        grid=(B, T // tm),
        in_specs=[tspec(D), const(g), const(w), const(bf_row), rspec, rspec, rspec]
        + [pl.BlockSpec(memory_space=pl.ANY)] * len(prev),
        out_specs=cache_specs + work_specs,
        out_shape=cache_shapes + work_shapes,
        input_output_aliases={n_in + i: i for i in range(len(prev))},
        compiler_params=_cparams(("parallel", "parallel")),
        name="in_proj",
    )(x, g, w, bf_row, *tables, *prev)
    return outs[:N_CACHE_ROWS], outs[N_CACHE_ROWS:]


def _split3(x):
    hi = x.astype(BF16)
    r = x - hi.astype(F32)
    mid = r.astype(BF16)
    lo = (r - mid.astype(F32)).astype(BF16)
    return hi, mid, lo


def _cumsum_kernel(lf_ref, o_ref, *, blk):
    L = lf_ref.shape[-1]
    r = lax.broadcasted_iota(jnp.int32, (blk, blk), 0)
    c = lax.broadcasted_iota(jnp.int32, (blk, blk), 1)
    upper = jnp.where(r <= c, 1.0, 0.0).astype(BF16)
    carry = jnp.zeros((N_HEADS, 1), F32)
    for j in range(L // blk):
        hi, mid, lo = _split3(lf_ref[0, :, j * blk:(j + 1) * blk])
        f = _dot(hi, upper) + _dot(mid, upper) + _dot(lo, upper) + carry
        o_ref[0, :, j * blk:(j + 1) * blk] = f * LOG2E
        carry = f[:, blk - 1:blk]


def _cumsum_time(lf_t):
    B, H, L = lf_t.shape
    blk = LANES
    assert L % blk == 0
    spec = pl.BlockSpec((1, H, L), lambda b: (b, 0, 0))
    return pl.pallas_call(
        functools.partial(_cumsum_kernel, blk=blk),
        grid=(B,),
        in_specs=[spec],
        out_specs=spec,
        out_shape=jax.ShapeDtypeStruct((B, H, L), F32),
        compiler_params=_cparams(("parallel",)),
        name="logf_cumsum",
    )(lf_t)


def _head_column(blk, h):
    lane = lax.broadcasted_iota(jnp.int32, blk.shape, 1)
    return jnp.sum(jnp.where(lane == h, blk, 0.0), axis=1, keepdims=True)


def _positions(tq, tk, q0, k0):
    qpos = q0 + lax.broadcasted_iota(jnp.int32, (tq, tk), 0)
    kpos = k0 + lax.broadcasted_iota(jnp.int32, (tq, tk), 1)
    return qpos, kpos


def _pair_specs(tq, Lp):
    qspec = pl.BlockSpec((1, tq, BRANCH_W), lambda b, qi: (b, qi, 0))
    kspec = pl.BlockSpec((1, Lp, BRANCH_W), lambda b, qi: (b, 0, 0))
    return qspec, kspec


def _pair_lanes(hp):
    return pl.ds(pl.multiple_of(hp * LANES, LANES), LANES)


def _split_past(rest, has_past):
    return (rest[:2], rest[2:-2], rest[-2:]) if has_past else ((), rest, ())


def _key_ops(k_ref, v_ref, past, bufs, tk):
    if not past:
        def qk(qmat, hp, k0):
            return _dot_t(qmat, k_ref[0, pl.ds(k0, tk), _pair_lanes(hp)])

        def pv(p, hp, k0, ones):
            v = v_ref[0, pl.ds(k0, tk), _pair_lanes(hp)]
            return _dot(p, v if ones is None else jnp.concatenate([v, ones], axis=1))

        return qk, pv

    P, T = past[0].shape[-1], k_ref.shape[1]
    for new, buf in zip((k_ref, v_ref), bufs):
        buf[0:T, :] = new[0]
        if buf.shape[0] > T:
            buf[T:, :] = jnp.zeros((buf.shape[0] - T, BRANCH_W), BF16)

    def qk(qmat, hp, k0):
        return jnp.concatenate([_dot(qmat, past[0][0, hp].astype(BF16)), _dot_t(qmat, bufs[0][:, _pair_lanes(hp)])],
                               axis=1)

    def pv(p, hp, k0, ones):
        vt, v_tail = past[1][0, hp].astype(BF16), bufs[1][:, _pair_lanes(hp)]
        if ones is not None:
            vt = jnp.concatenate([vt, jnp.ones(vt.shape, BF16)], axis=0)
            v_tail = jnp.concatenate([v_tail, ones[:v_tail.shape[0]]], axis=1)
        return _dot_t(p[:, :P], vt) + _dot(p[:, P:], v_tail)

    return qk, pv


def _past_specs_and_bufs(past, T, Lp):
    if past is None:
        return [], [], ()
    layer, pk, pv = past
    P = pk.shape[-1]
    assert P % LANES == 0 and pk.shape[2:4] == (N_PAIRS, LANES)
    spec = pl.BlockSpec((None, 1, N_PAIRS, LANES, P), lambda b, qi: (layer, b, 0, 0, 0))
    return [spec, spec], [pltpu.VMEM((Lp - P, BRANCH_W), BF16)] * 2, (pk, pv)


def _fox_kernel(q_ref, k_ref, v_ref, *rest, tq, tk, q_off, has_past):
    past, (fq_ref, fk_ref, o_ref, m_ref, acc_ref), bufs = _split_past(rest, has_past)
    qk, pv = _key_ops(k_ref, v_ref, past, bufs, tk)
    q0 = q_off + pl.program_id(1) * tq
    diag = q0 // tk
    fq_blk = fq_ref[0]
    ones = jnp.ones((tk, LANES), BF16)

    group = acc_ref.shape[0]

    def head_group(g, _):
        hps = [g * (group // 2) + pp for pp in range(group // 2)]
        lanes = [_pair_lanes(hp) for hp in hps]
        qm = [_keep_head(q_ref[0, :, lanes[hh // 2]], hh % 2) for hh in range(group)]
        fq = [_head_column(fq_blk, g * group + hh) for hh in range(group)]
        m_ref[...] = jnp.full(m_ref.shape, NEG, F32)
        acc_ref[...] = jnp.zeros(acc_ref.shape, F32)

        def tiles(js, masked):
            k0 = [j * tk if isinstance(j, int) else pl.multiple_of(j * tk, tk) for j in js]
            chains = [(t, hh) for t in range(len(js)) for hh in range(group)]
            if masked:
                qpos, kpos = _positions(tq, tk, q0, k0[0])
                causal = kpos <= qpos
            s = {}
            for t, hh in chains:
                fk = fk_ref[0, g * group + hh, :, pl.ds(k0[t], tk)]
                s[t, hh] = qk(qm[hh], hps[hh // 2], k0[t]) + fq[hh] - fk
                if masked and t == 0:
                    s[t, hh] = jnp.where(causal, s[t, hh], NEG)
            m_old = [m_ref[hh] for hh in range(group)]
            m_new = list(m_old)
            for t, hh in chains:
                m_new[hh] = jnp.maximum(m_new[hh], jnp.max(s[t, hh], axis=1, keepdims=True))
            p = {c: jnp.exp2(s[c] - m_new[c[1]]).astype(BF16) for c in chains}
            for hh in range(group):
                upd = jnp.exp2(m_old[hh] - m_new[hh]) * acc_ref[hh]
                for t in range(len(js)):
                    upd = upd + pv(p[t, hh], hps[hh // 2], k0[t], ones)
                acc_ref[hh] = upd
                m_ref[hh] = m_new[hh]

        tiles([diag], True)

        def body(i, _):
            j = diag - 1 - 2 * i
            tiles([j, j - 1], False)
            return 0

        lax.fori_loop(0, diag // 2, body, 0)

        @pl.when(diag % 2 == 1)
        def _():
            tiles([0], False)
        for pp in range(group // 2):
            o0 = acc_ref[2 * pp, :, :LANES] / acc_ref[2 * pp, :, LANES:]
            o1 = acc_ref[2 * pp + 1, :, :LANES] / acc_ref[2 * pp + 1, :, LANES:]
            o_ref[0, :, lanes[pp]] = _merge_pair(o0, o1).astype(o_ref.dtype)
        return 0

    lax.fori_loop(0, N_HEADS // group, head_group, 0)


def _fox(q, k, v, past, fq, fk_t, *, tq, tk, q_off):
    B, T, _ = q.shape
    Lp = fk_t.shape[-1]
    qspec, kspec = _pair_specs(tq, k.shape[1])
    past_specs, bufs, past = _past_specs_and_bufs(past, T, Lp)
    return pl.pallas_call(
        functools.partial(_fox_kernel, tq=tq, tk=tk, q_off=q_off, has_past=bool(past)),
        grid=(B, T // tq),
        in_specs=[qspec, kspec, kspec] + past_specs
        + [pl.BlockSpec((1, tq, N_HEADS), lambda b, qi: (b, qi, 0)),
           pl.BlockSpec((1, N_HEADS, 1, Lp), lambda b, qi: (b, 0, 0, 0))],
        out_specs=qspec,
        out_shape=jax.ShapeDtypeStruct(q.shape, BF16),
        scratch_shapes=[pltpu.VMEM((FOX_GROUP, tq, 1), F32), pltpu.VMEM((FOX_GROUP, tq, 2 * LANES), F32)] + bufs,
        compiler_params=_cparams(("parallel", "parallel")),
        name="fox_attn",
    )(q, k, v, *past, fq, fk_t[:, :, None, :])


def _sb_kernel(q_ref, k_ref, v_ref, *rest, tq, tk, q_off, has_past):
    past, (o_ref, acc_ref, tail_ref), bufs = _split_past(rest, has_past)
    qk, pv = _key_ops(k_ref, v_ref, past, bufs, tk)
    q0 = q_off + pl.program_id(1) * tq
    diag = q0 // tk
    r = lax.broadcasted_iota(jnp.int32, (LANES, 2 * LANES), 0)
    c = lax.broadcasted_iota(jnp.int32, (LANES, 2 * LANES), 1)
    suffix_w = jnp.where((c >= LANES) | (r > c), 1.0, 0.0).astype(BF16)

    group = acc_ref.shape[0]

    def head_group(g, _):
        hps = [g * (group // 2) + pp for pp in range(group // 2)]
        lanes = [_pair_lanes(hp) for hp in hps]
        qm = [_keep_head(q_ref[0, :, lanes[hh // 2]], hh % 2) for hh in range(group)]
        acc_ref[...] = jnp.zeros(acc_ref.shape, F32)
        tail_ref[...] = jnp.zeros(tail_ref.shape, F32)

        def tiles(js, masked):
            k0 = [j * tk if isinstance(j, int) else pl.multiple_of(j * tk, tk) for j in js]
            chains = [(t, hh) for t in range(len(js)) for hh in range(group)]
            if masked:
                qpos, kpos = _positions(tq, tk, q0, k0[0])
                strict = kpos < qpos
            z = {c: qk(qm[c[1]], hps[c[1] // 2], k0[c[0]]) for c in chains}
            ls, la, after = {}, {}, {}
            for c in chains:
                ls[c] = jnp.minimum(z[c], 0.0) - jnp.log2(1.0 + jnp.exp2(-jnp.abs(z[c])))
                la[c] = ls[c] - z[c]
                if masked and c[0] == 0:
                    la[c] = jnp.where(strict, la[c], 0.0)
                    ls[c] = jnp.where(strict, ls[c], NEG)
                la[c] = la[c].astype(BF16)
            for hh in range(group):
                run = tail_ref[hh]
                for t in range(len(js)):
                    blocks = [None] * (tk // LANES)
                    for b in reversed(range(tk // LANES)):
                        res = _dot(la[t, hh][:, b * LANES:(b + 1) * LANES], suffix_w)
                        blocks[b] = res[:, :LANES] + run
                        run = run + res[:, LANES:]
                    after[t, hh] = jnp.concatenate(blocks, axis=1)
                tail_ref[hh] = run
            a = {c: jnp.exp2(ls[c] + after[c]).astype(BF16) for c in chains}
            for c in chains:
                acc_ref[c[1]] += pv(a[c], hps[c[1] // 2], k0[c[0]], None)

        tiles([diag], True)

        def body(i, _):
            j = diag - 1 - 2 * i
            tiles([j, j - 1], False)
            return 0

        lax.fori_loop(0, diag // 2, body, 0)

        @pl.when(diag % 2 == 1)
        def _():
            tiles([0], False)
        for pp in range(group // 2):
            o_ref[0, :, lanes[pp]] = _merge_pair(acc_ref[2 * pp], acc_ref[2 * pp + 1]).astype(o_ref.dtype)
        return 0

    lax.fori_loop(0, N_HEADS // group, head_group, 0)


def _sb(q, k, v, past, Lp, *, tq, tk, q_off):
    B, T, _ = q.shape
    qspec, kspec = _pair_specs(tq, k.shape[1])
    past_specs, bufs, past = _past_specs_and_bufs(past, T, Lp)
    return pl.pallas_call(
        functools.partial(_sb_kernel, tq=tq, tk=tk, q_off=q_off, has_past=bool(past)),
        grid=(B, T // tq),
        in_specs=[qspec, kspec, kspec] + past_specs,
        out_specs=qspec,
        out_shape=jax.ShapeDtypeStruct(q.shape, BF16),
        scratch_shapes=[pltpu.VMEM((SB_GROUP, tq, LANES), F32), pltpu.VMEM((SB_GROUP, tq, LANES), F32)] + bufs,
        compiler_params=_cparams(("parallel", "parallel")),
        name="sb_attn",
    )(q, k, v, *past)


def _const_key(v):
    bits = int(np.float32(v).view(np.int32))
    return (bits ^ 0x7FFFFFFF) + 1 if bits < 0 else bits


def _float_key(x):
    bits = lax.bitcast_convert_type(x, jnp.int32)
    return jnp.where(bits < 0, jnp.bitwise_xor(bits, 0x7FFFFFFF) + 1, bits)


KEY_MIN = -2 ** 31
KEY_HALF_NEG = _const_key(0.5 * NEG)


KEY_MAX = 2 ** 31 - 1
KEY_BITS = 32
COUNT_CHAINS = 8
SB_GROUP = 8
FOX_GROUP = 2


def _dsa_kernel(aq_ref, iq_ref, iwt_ref, k_ref, vt_ref, ik_ref, o_ref, key_ref, q8_ref, m_ref, acc_ref,
                *, tq, tk, q_off, n_rows, topk):
    q0 = q_off + pl.program_id(1) * tq
    n_tiles = (q0 + n_rows + tk - 1) // tk

    def tile_start(j):
        return pl.multiple_of(j * tk, LANES)

    iq = iq_ref[0]
    iwt = iwt_ref[0]
    qh = jnp.concatenate([_keep_head(iq[:, (h // 2) * LANES:(h // 2 + 1) * LANES], h % 2) for h in range(N_IDX)],
                         axis=0)
    q_chunk = jnp.right_shift(q0 + lax.broadcasted_iota(jnp.int32, (tk, tq), 1), CHUNK_SHIFT)

    def score_tile(j, _):
        k0 = tile_start(j)
        dots = _dot_t(ik_ref[0, pl.ds(k0, tk), :], qh)
        sc = jnp.zeros((tk, tq), F32)
        for h in range(N_IDX):
            sc = sc + jnp.maximum(dots[:, h * tq:(h + 1) * tq], 0.0) * iwt[h:h + 1, :]
        key_chunk = jnp.right_shift(k0 + lax.broadcasted_iota(jnp.int32, (tk, tq), 0), CHUNK_SHIFT)
        key_ref[pl.ds(k0, tk), :] = _float_key(jnp.where(key_chunk <= q_chunk, sc, NEG))
        return 0

    lax.fori_loop(0, n_tiles, score_tile, 0)

    def search():
        full_tiles = (q0 + n_rows) // tk
        tail_blocks = (q0 + n_rows - full_tiles * tk + LANES - 1) // LANES

        def step(_, cr):
            lo, hi, above_hi = cr
            mid = jnp.right_shift(lo, 1) + jnp.right_shift(hi, 1) + jnp.bitwise_and(jnp.bitwise_and(lo, hi), 1)

            def count_rows(r0, rows, cnt):
                above = jnp.where(key_ref[pl.ds(pl.multiple_of(r0, LANES), rows), :] > mid, 1.0, 0.0)
                return cnt + jnp.sum(above.reshape((-1,) + cnt.shape), axis=0)

            cnt = lax.fori_loop(0, full_tiles, lambda j, c: count_rows(j * tk, tk, c),
                                jnp.zeros((COUNT_CHAINS, SUBLANES, tq), F32))
            cnt = lax.fori_loop(0, tail_blocks, lambda b, c: count_rows(full_tiles * tk + b * LANES, LANES, c), cnt)
            cnt = jnp.sum(jnp.sum(cnt, axis=0), axis=0, keepdims=True)
            less = cnt < topk
            return jnp.where(less, lo, mid), jnp.where(less, mid, hi), jnp.where(less, cnt, above_hi)

        init = (jnp.full((1, tq), KEY_MIN, jnp.int32), jnp.full((1, tq), KEY_MAX, jnp.int32), jnp.zeros((1, tq), F32))
        _, hi, above_hi = lax.fori_loop(0, KEY_BITS, step, init)
        return hi, topk - above_hi

    thr, room = lax.cond(q0 + n_rows > topk, search,
                         lambda: (jnp.full((1, tq), KEY_MIN, jnp.int32), jnp.zeros((1, tq), F32)))
    row_ok = lax.broadcasted_iota(jnp.int32, (1, tq), 1) < n_rows
    room = jnp.where(row_ok, room, 0.0)

    @pl.when(jnp.max(room) > 0.0)
    def _():
        r = lax.broadcasted_iota(jnp.int32, (LANES, LANES), 0)
        c = lax.broadcasted_iota(jnp.int32, (LANES, LANES), 1)
        before = jnp.where(c < r, 1.0, 0.0).astype(BF16)

        def tie_tile(j, seen):
            k0 = tile_start(j)
            for b in range(tk // LANES):
                rows = pl.ds(k0 + b * LANES, LANES)
                keys = key_ref[rows, :]
                eq = keys == thr
                eqf = jnp.where(eq, 1.0, 0.0)
                rank = _dot(before, eqf.astype(BF16)) + seen
                key_ref[rows, :] = jnp.where(eq, jnp.where(rank < room, keys + 1, keys), keys)
                seen = seen + jnp.sum(eqf, axis=0, keepdims=True)
            return seen

        lax.fori_loop(0, n_tiles, tie_tile, jnp.zeros((1, tq), F32))

    aq = aq_ref[0]
    for h in range(N_HEADS):
        q8_ref[h * tq:(h + 1) * tq, :] = _keep_head(aq[:, (h // 2) * LANES:(h // 2 + 1) * LANES], h % 2)
    m_ref[...] = jnp.full(m_ref.shape, NEG, F32)
    acc_ref[...] = jnp.zeros(acc_ref.shape, F32)
    thr_sel = jnp.maximum(thr, KEY_HALF_NEG)

    def attend_tiles(js):
        k0 = [j * tk if isinstance(j, int) else tile_start(j) for j in js]
        groups = [slice(g * (N_HEADS // 2) * tq, (g + 1) * (N_HEADS // 2) * tq) for g in (0, 1)]
        chains = [(t, g) for t in range(len(js)) for g in (0, 1)]
        kt = [k_ref[0, pl.ds(k, tk), :] for k in k0]
        bias = [jnp.where(key_ref[pl.ds(k, tk), :] > thr_sel, 0.0, NEG) for k in k0]
        st = {(t, g): _dot_t(kt[t], q8_ref[groups[g], :]) for t, g in chains}
        st = {(t, g): jnp.concatenate([st[t, g][:, h * tq:(h + 1) * tq] + bias[t] for h in range(N_HEADS // 2)], axis=1)
              for t, g in chains}
        m_old = [m_ref[:, cols] for cols in groups]
        m_new = list(m_old)
        for t, g in chains:
            m_new[g] = jnp.maximum(m_new[g], jnp.max(st[t, g], axis=0, keepdims=True))
        p = {(t, g): jnp.exp2(st[t, g] - m_new[g]).astype(BF16) for t, g in chains}
        for g, cols in enumerate(groups):
            upd = jnp.exp2(m_old[g] - m_new[g]) * acc_ref[:, cols]
            for t in range(len(js)):
                upd = upd + _dot(vt_ref[0, :, pl.ds(k0[t], tk)], p[t, g])
            acc_ref[:, cols] = upd
            m_ref[:, cols] = m_new[g]

    def attend_pair(i, _):
        attend_tiles([2 * i, 2 * i + 1])
        return 0

    lax.fori_loop(0, n_tiles // 2, attend_pair, 0)

    @pl.when(n_tiles % 2 == 1)
    def _():
        attend_tiles([n_tiles - 1])
    for hp in range(N_PAIRS):
        pair = []
        for h in (2 * hp, 2 * hp + 1):
            a = acc_ref[:, h * tq:(h + 1) * tq]
            pair.append(a[:HEAD_DIM] / a[HEAD_DIM:])
        o_ref[0, :, hp * LANES:(hp + 1) * LANES] = jnp.concatenate(pair, axis=0).T.astype(o_ref.dtype)


def _dsa(aq, iq, iw, k2, v1t, ik2, *, tq, q_off, topk):
    B, T, _ = aq.shape
    Lp = k2.shape[1]
    tk = Lp if T <= LANES else next(t for t in (512, 384, 256, 128) if Lp % t == 0)
    Tp = -(-T // LANES) * LANES
    tq = max(tq, LANES)
    n_rows = min(T, tq)
    pad_t = lambda a: jnp.pad(a, ((0, 0), (0, Tp - T), (0, 0)))
    qspec = lambda w: pl.BlockSpec((1, tq, w), lambda b, qi: (b, qi, 0))
    kspec = pl.BlockSpec((1, Lp, LANES), lambda b, qi: (b, 0, 0))
    out = pl.pallas_call(
        functools.partial(_dsa_kernel, tq=tq, tk=tk, q_off=q_off, n_rows=n_rows, topk=topk),
        grid=(B, Tp // tq),
        in_specs=[qspec(BRANCH_W), qspec(N_IDX * IDX_DIM), pl.BlockSpec((1, SUBLANES, tq), lambda b, qi: (b, 0, qi)),
                  kspec, pl.BlockSpec((1, LANES, Lp), lambda b, qi: (b, 0, 0)), kspec],
        out_specs=qspec(BRANCH_W),
        out_shape=jax.ShapeDtypeStruct((B, Tp, BRANCH_W), BF16),
        scratch_shapes=[pltpu.VMEM((Lp, tq), jnp.int32), pltpu.VMEM((N_HEADS * tq, LANES), BF16),
                        pltpu.VMEM((1, N_HEADS * tq), F32), pltpu.VMEM((LANES, N_HEADS * tq), F32)],
        compiler_params=_cparams(("parallel", "parallel")),
        name="dsa_attn",
    )(pad_t(aq), pad_t(iq), jnp.swapaxes(pad_t(iw), 1, 2), k2, v1t, ik2)
    return out[:, :T]


def _merge_kernel(x_ref, oa_ref, ob_ref, oc_ref, g_ref, wg_ref, wb_ref, wo_ref, y_ref):
    x = x_ref[0]
    D = x.shape[-1]
    xn = _rms(x, g_ref[...]).astype(BF16)
    merged = jnp.zeros(x.shape, F32)
    for n, o_ref in enumerate((oa_ref, ob_ref, oc_ref)):
        gate = _sigmoid(_dot(xn, wg_ref[:, n * D:(n + 1) * D]))
        merged = merged + gate * _dot(o_ref[0], wb_ref[n])
    y_ref[0] = x + _dot(merged.astype(BF16), wo_ref[...])


def _merge(x, oa, ob, oc, g, wg, wb, wo, tm):
    B, T, D = x.shape
    tspec = lambda w_: pl.BlockSpec((1, tm, w_), lambda b, t: (b, t, 0))
    const = lambda a: pl.BlockSpec(a.shape, lambda b, t: (0,) * a.ndim)
    return pl.pallas_call(
        _merge_kernel,
        grid=(B, T // tm),
        in_specs=[tspec(D), tspec(BRANCH_W), tspec(BRANCH_W), tspec(BRANCH_W), const(g), const(wg), const(wb),
                  const(wo)],
        out_specs=tspec(D),
        out_shape=jax.ShapeDtypeStruct(x.shape, F32),
        compiler_params=_cparams(("parallel", "parallel")),
        name="merge_out",
    )(x, oa, ob, oc, g, wg, wb, wo)


def _ffn_kernel(x_ref, g_ref, past_ref, wu_ref, cw_ref, cb_ref, wd_ref, gf_ref, y_ref, st_ref,
                hbuf_ref, carry_ref, acc_ref, *, tm, final_norm):
    fc = FF_CHUNK
    d_ff = wd_ref.shape[0]
    n_chunks = d_ff // fc

    @pl.when(pl.program_id(1) == 0)
    def _():
        carry_ref[...] = past_ref[0]

    x = x_ref[0]
    xn = _rms(x, g_ref[...]).astype(BF16)
    acc_ref[...] = jnp.zeros(acc_ref.shape, F32)

    def up_proj(c):
        return jnp.concatenate([_dot(xn, wu_ref[:, c * fc:(c + 1) * fc]),
                                _dot(xn, wu_ref[:, d_ff + c * fc:d_ff + (c + 1) * fc])], axis=1)

    h_next = up_proj(0)
    for c in range(n_chunks):
        hbuf = hbuf_ref.at[c % 2]
        both = lambda ref: jnp.concatenate([ref[:, c * fc:(c + 1) * fc], ref[:, d_ff + c * fc:d_ff + (c + 1) * fc]],
                                           axis=1)
        h = h_next
        if c + 1 < n_chunks:
            h_next = up_proj(c + 1)
        hbuf[0:SUBLANES, :] = carry_ref[c]
        hbuf[SUBLANES:SUBLANES + tm, :] = h
        carry_ref[c] = h[tm - SUBLANES:tm, :]
        cw = both(cw_ref)
        hc = (both(cb_ref) + cw[0:1] * hbuf[SUBLANES - 2:SUBLANES - 2 + tm, :]
              + cw[1:2] * hbuf[SUBLANES - 1:SUBLANES - 1 + tm, :] + cw[2:3] * h)
        gate, up = hc[:, :fc], hc[:, fc:]
        act = gate * _sigmoid(gate) * up
        acc_ref[...] += _dot(act.astype(BF16), wd_ref[c * fc:(c + 1) * fc, :])
    st_ref[0] = carry_ref[...]
    y = x + acc_ref[...]
    if final_norm:
        y = _rms(y, gf_ref[...])
    y_ref[0] = y


def _ffn(x, g, past, wu, cw, cb, wd, gf, tm, final_norm):
    B, T, D = x.shape
    nc, fc2 = wd.shape[0] // FF_CHUNK, 2 * FF_CHUNK
    tspec = pl.BlockSpec((1, tm, D), lambda b, t: (b, t, 0))
    const = lambda a: pl.BlockSpec(a.shape, lambda b, t: (0,) * a.ndim, pipeline_mode=pl.Buffered(1))
    sspec = pl.BlockSpec((1, nc, SUBLANES, fc2), lambda b, t: (b, 0, 0, 0))
    return pl.pallas_call(
        functools.partial(_ffn_kernel, tm=tm, final_norm=final_norm),
        grid=(B, T // tm),
        in_specs=[tspec, const(g), sspec, const(wu), const(cw), const(cb), const(wd), const(gf)],
        out_specs=[tspec, sspec],
        out_shape=[jax.ShapeDtypeStruct(x.shape, F32), jax.ShapeDtypeStruct((B, nc, SUBLANES, fc2), F32)],
        scratch_shapes=[pltpu.VMEM((2, tm + SUBLANES, fc2), F32), pltpu.VMEM((nc, SUBLANES, fc2), F32),
                        pltpu.VMEM((tm, D), F32)],
        compiler_params=_cparams(("arbitrary", "arbitrary")),
        name="conv_ffn",
    )(x, g, past, wu, cw, cb, wd, gf)


def _rope_tables(pos):
    half = ROT_DIM // 2
    freq = ROPE_THETA ** (-jnp.arange(half, dtype=F32) / half)
    ang = pos.astype(F32)[:, None] * freq[None, :]
    lane = np.arange(LANES) % HEAD_DIM
    cos = jnp.cos(ang)[:, lane % half]
    sin = jnp.sin(ang)[:, lane % half]
    rc = jnp.where(lane < ROT_DIM, cos, 1.0)
    rs1 = jnp.where(lane < half, -sin, 0.0)
    rs2 = jnp.where((lane >= half) & (lane < ROT_DIM), sin, 0.0)
    return rc, rs1, rs2


IN_SIZES = (BRANCH_W, HEAD_DIM, HEAD_DIM, N_IDX * IDX_DIM, IDX_DIM, N_IDX) + (BRANCH_W,) * 6 + (N_HEADS,)
PACKED_W = ROPE_W + MISC_W + 6 * BRANCH_W


def _pack_w_kernel(w_ref, o_ref, s_ref):
    offs = np.concatenate([[0], np.cumsum(IN_SIZES)]).tolist()
    a_q, a_k, a_v, i_q, i_k, i_w, b_q = offs[:7]
    c_f = offs[12]
    src = lambda c0, width: w_ref[0, :, c0:c0 + width]
    s_ref[:, 0:BRANCH_W] = src(a_q, BRANCH_W)
    iq_w = N_IDX * IDX_DIM
    s_ref[:, BRANCH_W:BRANCH_W + iq_w] = src(i_q, iq_w)
    ak_col = BRANCH_W + iq_w
    for dst, c0 in ((ak_col, a_k), (ak_col + LANES, i_k), (ROPE_W, a_v)):
        s_ref[:, dst:dst + HEAD_DIM] = src(c0, HEAD_DIM)
        s_ref[:, dst + HEAD_DIM:dst + 2 * HEAD_DIM] = src(c0, HEAD_DIM)
    misc = ROPE_W + LANES
    s_ref[:, misc:misc + LANES] = jnp.zeros((s_ref.shape[0], LANES), F32)
    s_ref[:, misc:misc + N_IDX] = src(i_w, N_IDX)
    s_ref[:, misc + CF_LANE:misc + CF_LANE + N_HEADS] = src(c_f, N_HEADS)
    s_ref[:, ROPE_W + MISC_W:PACKED_W] = src(b_q, 6 * BRANCH_W)
    o_ref[...] = s_ref[...].astype(BF16)


def _pack_w(w_in, l):
    D = w_in.shape[1]
    rows = 256
    return pl.pallas_call(
        _pack_w_kernel,
        grid=(D // rows,),
        in_specs=[pl.BlockSpec((1, rows, w_in.shape[2]), lambda r: (l, r, 0))],
        out_specs=pl.BlockSpec((rows, PACKED_W), lambda r: (r, 0)),
        out_shape=jax.ShapeDtypeStruct((D, PACKED_W), BF16),
        scratch_shapes=[pltpu.VMEM((rows, PACKED_W), F32)],
        compiler_params=_cparams(("parallel",)),
        name="pack_w_in",
    )(w_in)


def _pack_layer(l, norm_mix, w_in, b_f, w_branch, w_gate, w_out, norm_ffn, w_up, conv_w, conv_b, w_down):
    w = _pack_w(w_in, l)
    bf_row = jnp.zeros((1, LANES), F32).at[0, CF_LANE:CF_LANE + N_HEADS].set(b_f[l])

    d_ff = w_down.shape[1]
    fc = FF_CHUNK
    nc = d_ff // fc
    assert nc * fc == d_ff
    halves = lambda a: jnp.concatenate([a[..., :d_ff].reshape(a.shape[:-1] + (nc, fc)),
                                        a[..., d_ff:].reshape(a.shape[:-1] + (nc, fc))], axis=-1)
    wu, cw, cb, wd = w_up[l].astype(BF16), conv_w[l], conv_b[l][None], w_down[l].astype(BF16)
    return dict(g_mix=norm_mix[l][None], w=w, bf_row=bf_row, wg=w_gate[l].astype(BF16),
                wb=w_branch[l].astype(BF16), wo=w_out[l].astype(BF16), g_ffn=norm_ffn[l][None],
                wu=wu, cw=cw, cb=cb, wd=wd, halves=halves, nc=nc, d_ff=d_ff)


def _conv_state_in(state, halves):
    st = jnp.moveaxis(halves(state), 2, 1)
    return jnp.pad(st, ((0, 0), (0, 0), (SUBLANES - (CONV_W - 1), 0), (0, 0)))


def _conv_state_out(st, d_ff):
    st = st[:, :, SUBLANES - (CONV_W - 1):, :]
    B, nc, r, fc2 = st.shape
    fc = fc2 // 2
    gate = jnp.moveaxis(st[..., :fc], 1, 2).reshape(B, r, d_ff)
    up = jnp.moveaxis(st[..., fc:], 1, 2).reshape(B, r, d_ff)
    return jnp.concatenate([gate, up], axis=-1)


def _pad_keys(a, Lp):
    return jnp.pad(a, ((0, 0), (0, Lp - a.shape[1]), (0, 0)))


def _group_layer(x, past, lw, gf, layer, depth, stacked, tiles):
    B, T, D = x.shape
    P = 0 if past is None else past[0].shape[1]
    L = P + T
    Lp = -(-L // KEY_TILE) * KEY_TILE
    topk = max(1, min(TOPK_MAX, L // 4))
    tables = _rope_tables(jnp.arange(P, L))

    stacked, (iw, aq16, iq16, ak16, avt16, ik16, bq16, bk16, bv16, cq16, ck16, cv16) = _project(
        x, lw["g_mix"], lw["w"], lw["bf_row"], tables, tiles["proj"], layer, depth, stacked)
    lf = stacked[-1][layer]

    if past is None:
        keys = (ak16, ik16)
        lf_all = lf
        conv_in = jnp.zeros((B, lw["nc"], SUBLANES, 2 * FF_CHUNK), F32)
        past_b = past_c = None
    else:
        p_ak, p_av, p_ik, p_bk, p_bv, p_ck, p_cv, p_lf, p_conv = past
        dup = lambda a: jnp.concatenate([a, a], axis=-1).astype(BF16)
        keys = tuple(jnp.concatenate([o, n], axis=1) for o, n in zip((dup(p_ak), dup(p_ik)), (ak16, ik16)))
        old_vt = jnp.swapaxes(p_av, 1, 2).astype(BF16)
        avt16 = jnp.concatenate([jnp.concatenate([old_vt, jnp.ones_like(old_vt)], axis=1), avt16], axis=2)
        lf_all = jnp.concatenate([p_lf, lf], axis=1)
        conv_in = _conv_state_in(p_conv, lw["halves"])
        past_b, past_c = (layer, p_bk, p_bv), (layer, p_ck, p_cv)
    ak2, ik2 = (_pad_keys(a, Lp) for a in keys)
    v1t = jnp.pad(avt16, ((0, 0), (0, 0), (0, Lp - L)))

    fk_t = _cumsum_time(jnp.swapaxes(_pad_keys(lf_all, Lp), 1, 2))
    fq = jnp.swapaxes(fk_t[:, :, P:L], 1, 2)

    o_a = _dsa(aq16, iq16, iw, ak2, v1t, ik2, tq=tiles["dsa"], q_off=P, topk=topk)
    o_b = _sb(bq16, bk16, bv16, past_b, Lp, tq=tiles["sb"][0], tk=tiles["sb"][1], q_off=P)
    o_c = _fox(cq16, ck16, cv16, past_c, fq, fk_t, tq=tiles["fox"][0], tk=tiles["fox"][1], q_off=P)

    x = _merge(x, o_a, o_b, o_c, lw["g_mix"], lw["wg"], lw["wb"], lw["wo"], tiles["merge"])
    x, st = _ffn(x, lw["g_ffn"], conv_in, lw["wu"], lw["cw"], lw["cb"], lw["wd"], gf, tiles["ffn"],
                 layer == depth - 1)
    return x, stacked, _conv_state_out(st, lw["d_ff"])


def _tiles(T, P):
    pick = lambda want: min(want, T)
    if T % 512 == 0 and P % 512 == 0:
        sb, fox = (256, 256), (256, 512)
    else:
        assert T <= KEY_TILE
        sb = fox = (T, -(-(P + T) // KEY_TILE) * KEY_TILE)
    return dict(proj=pick(512), dsa=pick(128), sb=sb, fox=fox, merge=pick(512), ffn=pick(512))


def kernel(x_prompt, x_sample, cache_a_k, cache_a_v, cache_a_idx_k, cache_b_k, cache_b_v, cache_c_k, cache_c_v,
           cache_c_logf, state_ffn_conv, norm_mix, w_in, b_f, w_branch, w_gate, w_out, norm_ffn, w_up, conv_w,
           conv_b, w_down, norm_final):
    depth = w_in.shape[0]
    caches = (cache_a_k, cache_a_v, cache_a_idx_k, cache_b_k, cache_b_v, cache_c_k, cache_c_v, cache_c_logf,
              state_ffn_conv)
    layers = [_pack_layer(l, norm_mix, w_in, b_f, w_branch, w_gate, w_out, norm_ffn, w_up, conv_w, conv_b, w_down)
              for l in range(depth)]
    gf = norm_final[None]

    def trunk(x, past):
        B, T, _ = x.shape
        tiles = _tiles(T, 0 if past is None else past[0].shape[2])
        stacked, conv_states = None, []
        if past is not None:
            P = past[0].shape[2]
            wide = lambda c: jnp.transpose(c, (0, 1, 3, 4, 2)).reshape(depth, B, N_PAIRS, LANES, P)
        for l in range(depth):
            lp = None if past is None else tuple(wide(c) if 3 <= i <= 6 else c[l] for i, c in enumerate(past))
            x, stacked, conv_state = _group_layer(x, lp, layers[l], gf, l, depth, stacked, tiles)
            conv_states.append(conv_state)
        ak, av, ik, bk, bv, ck, cv, lf = stacked
        heads = lambda a: a.reshape(depth, B, T, N_HEADS, HEAD_DIM)
        return x, (ak, av, ik, heads(bk), heads(bv), heads(ck), heads(cv), lf, jnp.stack(conv_states, axis=0))

    y_prompt, p_state = trunk(x_prompt, None)
    y_sample, s_state = trunk(x_sample, caches)
    return (y_prompt, y_sample) + p_state + s_state
```

```python
import functools

import jax
import jax.numpy as jnp
import numpy as np
from jax import lax
from jax.experimental import pallas as pl
from jax.experimental.pallas import tpu as pltpu

F32 = jnp.float32
BF16 = jnp.bfloat16

LANES = 128
SUBLANES = 8
HEAD_DIM = 64
N_HEADS = 8
N_PAIRS = N_HEADS // 2
BRANCH_W = N_HEADS * HEAD_DIM
N_IDX = 4
IDX_DIM = 64
CHUNK = 64
CHUNK_SHIFT = CHUNK.bit_length() - 1
TOPK_MAX = 256
ROT_DIM = HEAD_DIM // 4
ROPE_THETA = 500000.0
CONV_W = 3
EPS = 1e-6
NEG = -1e30
KEY_TILE = 128
FF_CHUNK = 256
VMEM_LIMIT = 56 * 1024 * 1024

ROPE_W = 1024
MISC_W = 256
CF_LANE = 8
LOG2E = 1.4426950408889634
Q_SCALE = HEAD_DIM ** -0.5 * LOG2E


def _cparams(sem):
    return pltpu.CompilerParams(dimension_semantics=sem, vmem_limit_bytes=VMEM_LIMIT)


def _rms(x, g):
    return x * lax.rsqrt(jnp.mean(x * x, axis=-1, keepdims=True) + EPS) * g


def _sigmoid(z):
    return 1.0 / (1.0 + jnp.exp(-z))


def _log_sigmoid(z):
    return jnp.minimum(z, 0.0) - jnp.log1p(jnp.exp(-jnp.abs(z)))


def _dot_t(a, b):
    return lax.dot_general(a, b, (((1,), (1,)), ((), ())), preferred_element_type=F32)


def _dot(a, b):
    return jnp.dot(a, b, preferred_element_type=F32)


def _keep_head(blk, which):
    lane = lax.broadcasted_iota(jnp.int32, blk.shape, blk.ndim - 1)
    keep = (lane >= HEAD_DIM) if which else (lane < HEAD_DIM)
    return jnp.where(keep, blk, jnp.zeros_like(blk))


def _merge_pair(o0, o1):
    lane = lax.broadcasted_iota(jnp.int32, o0.shape, o0.ndim - 1)
    return jnp.where(lane < HEAD_DIM, o0, o1)


N_CACHE_ROWS = 8


def _proj_kernel(x_ref, g_ref, w_ref, bf_ref, rc_ref, rs1_ref, rs2_ref, *refs):
    (ak_ref, av_ref, ik_ref, bk_ref, bv_ref, ck_ref, cv_ref, lf_ref, iw_ref,
     aq16, iq16, ak16, avt16, ik16, bq16, bk16, bv16, cq16, ck16, cv16) = refs[-(N_CACHE_ROWS + 12):]
    xn = _rms(x_ref[0], g_ref[...]).astype(BF16)
    rc, rs1, rs2 = rc_ref[...], rs1_ref[...], rs2_ref[...]

    def cols(c0, width):
        return _dot(xn, w_ref[:, c0:c0 + width])

    def group(blk, g):
        return blk[:, g * LANES:(g + 1) * LANES]

    def rope(blk):
        half = ROT_DIM // 2
        return blk * rc + pltpu.roll(blk, LANES - half, 1) * rs1 + pltpu.roll(blk, half, 1) * rs2

    wide = cols(0, BRANCH_W)
    for g in range(4):
        aq16[0, :, g * LANES:(g + 1) * LANES] = (rope(group(wide, g)) * Q_SCALE).astype(BF16)
    wide = cols(BRANCH_W, ROPE_W - BRANCH_W)
    for g in range(2):
        iq16[0, :, g * LANES:(g + 1) * LANES] = (rope(group(wide, g)) * IDX_DIM ** -0.5).astype(BF16)
    blk = rope(group(wide, 2))
    ak_ref[0, 0] = blk[:, :HEAD_DIM]
    ak16[0] = blk.astype(BF16)
    blk = rope(group(wide, 3))
    ik_ref[0, 0] = blk[:, :IDX_DIM]
    ik16[0] = blk.astype(BF16)

    wide = cols(ROPE_W, MISC_W)
    blk = group(wide, 0)
    av_ref[0, 0] = blk[:, :HEAD_DIM]
    row = lax.broadcasted_iota(jnp.int32, (LANES, blk.shape[0]), 0)
    avt16[0] = jnp.where(row < HEAD_DIM, blk.T, 1.0).astype(BF16)
    blk = group(wide, 1)
    iw_ref[0] = blk[:, :SUBLANES] * N_IDX ** -0.5
    lf_ref[0, 0] = _log_sigmoid(blk[:, CF_LANE:CF_LANE + N_HEADS] + bf_ref[:, CF_LANE:CF_LANE + N_HEADS])

    base = ROPE_W + MISC_W
    outs = ((None, bq16), (bk_ref, bk16), (bv_ref, bv16), (None, cq16), (ck_ref, ck16), (cv_ref, cv16))
    for n, (o32, o16) in enumerate(outs):
        wide = cols(base + n * BRANCH_W, BRANCH_W)
        if o32 is None:
            o16[0] = (wide * Q_SCALE).astype(BF16)
        else:
            o32[0, 0] = wide
            o16[0] = wide.astype(BF16)


def _project(x, g, w, bf_row, tables, tm, layer, depth, stacked):
    B, T, D = x.shape
    tok = lambda w_, dt: jax.ShapeDtypeStruct((B, T, w_), dt)
    tspec = lambda w_: pl.BlockSpec((1, tm, w_), lambda b, t: (b, t, 0))
    const = lambda a: pl.BlockSpec(a.shape, lambda b, t: (0,) * a.ndim, pipeline_mode=pl.Buffered(1))
    cache_w = (HEAD_DIM, HEAD_DIM, IDX_DIM) + (BRANCH_W,) * 4 + (N_HEADS,)
    cache_shapes = [jax.ShapeDtypeStruct((depth, B, T, w_), F32) for w_ in cache_w]
    cache_specs = [pl.BlockSpec((1, 1, tm, w_), lambda b, t: (layer, b, t, 0)) for w_ in cache_w]
    work = [(tok(w_, dt), tspec(w_)) for w_, dt in ((SUBLANES, F32), (BRANCH_W, BF16), (N_IDX * IDX_DIM, BF16),
                                                       (LANES, BF16))]
    work += [(jax.ShapeDtypeStruct((B, LANES, T), BF16), pl.BlockSpec((1, LANES, tm), lambda b, t: (b, 0, t)))]
    work += [(tok(w_, BF16), tspec(w_)) for w_ in (LANES,) + (BRANCH_W,) * 6]
    work_shapes, work_specs = [s for s, _ in work], [p for _, p in work]
    rspec = pl.BlockSpec((tm, LANES), lambda b, t: (t, 0))
    prev = () if stacked is None else tuple(stacked)
    n_in = 7
    outs = pl.pallas_call(
        _proj_kernel,
        grid=(B, T // tm),
        in_specs=[tspec(D), const(g), const(w), const(bf_row), rspec, rspec, rspec]
        + [pl.BlockSpec(memory_space=pl.ANY)] * len(prev),
        out_specs=cache_specs + work_specs,
        out_shape=cache_shapes + work_shapes,
        input_output_aliases={n_in + i: i for i in range(len(prev))},
        compiler_params=_cparams(("parallel", "parallel")),
        name="in_proj",
    )(x, g, w, bf_row, *tables, *prev)
    return outs[:N_CACHE_ROWS], outs[N_CACHE_ROWS:]


def _split3(x):
    hi = x.astype(BF16)
    r = x - hi.astype(F32)
    mid = r.astype(BF16)
    lo = (r - mid.astype(F32)).astype(BF16)
    return hi, mid, lo


def _cumsum_kernel(lf_ref, o_ref, *, blk):
    L = lf_ref.shape[-1]
    r = lax.broadcasted_iota(jnp.int32, (blk, blk), 0)
    c = lax.broadcasted_iota(jnp.int32, (blk, blk), 1)
    upper = jnp.where(r <= c, 1.0, 0.0).astype(BF16)
    carry = jnp.zeros((N_HEADS, 1), F32)
    for j in range(L // blk):
        hi, mid, lo = _split3(lf_ref[0, :, j * blk:(j + 1) * blk])
        f = _dot(hi, upper) + _dot(mid, upper) + _dot(lo, upper) + carry
        o_ref[0, :, j * blk:(j + 1) * blk] = f * LOG2E
        carry = f[:, blk - 1:blk]


def _cumsum_time(lf_t):
    B, H, L = lf_t.shape
    blk = LANES
    assert L % blk == 0
    spec = pl.BlockSpec((1, H, L), lambda b: (b, 0, 0))
    return pl.pallas_call(
        functools.partial(_cumsum_kernel, blk=blk),
        grid=(B,),
        in_specs=[spec],
        out_specs=spec,
        out_shape=jax.ShapeDtypeStruct((B, H, L), F32),
        compiler_params=_cparams(("parallel",)),
        name="logf_cumsum",
    )(lf_t)


def _head_column(blk, h):
    lane = lax.broadcasted_iota(jnp.int32, blk.shape, 1)
    return jnp.sum(jnp.where(lane == h, blk, 0.0), axis=1, keepdims=True)


def _positions(tq, tk, q0, k0):
    qpos = q0 + lax.broadcasted_iota(jnp.int32, (tq, tk), 0)
    kpos = k0 + lax.broadcasted_iota(jnp.int32, (tq, tk), 1)
    return qpos, kpos


def _pair_specs(tq, Lp):
    qspec = pl.BlockSpec((1, tq, BRANCH_W), lambda b, qi: (b, qi, 0))
    kspec = pl.BlockSpec((1, Lp, BRANCH_W), lambda b, qi: (b, 0, 0))
    return qspec, kspec


def _pair_lanes(hp):
    return pl.ds(pl.multiple_of(hp * LANES, LANES), LANES)


def _split_past(rest, has_past):
    return (rest[:2], rest[2:-2], rest[-2:]) if has_past else ((), rest, ())


def _key_ops(k_ref, v_ref, past, bufs, tk):
    if not past:
        def qk(qmat, hp, k0):
            return _dot_t(qmat, k_ref[0, pl.ds(k0, tk), _pair_lanes(hp)])

        def pv(p, hp, k0, ones):
            v = v_ref[0, pl.ds(k0, tk), _pair_lanes(hp)]
            return _dot(p, v if ones is None else jnp.concatenate([v, ones], axis=1))

        return qk, pv

    P, T = past[0].shape[-1], k_ref.shape[1]
    for new, buf in zip((k_ref, v_ref), bufs):
        buf[0:T, :] = new[0]
        if buf.shape[0] > T:
            buf[T:, :] = jnp.zeros((buf.shape[0] - T, BRANCH_W), BF16)

    def qk(qmat, hp, k0):
        return jnp.concatenate([_dot(qmat, past[0][0, hp].astype(BF16)), _dot_t(qmat, bufs[0][:, _pair_lanes(hp)])],
                               axis=1)

    def pv(p, hp, k0, ones):
        vt, v_tail = past[1][0, hp].astype(BF16), bufs[1][:, _pair_lanes(hp)]
        if ones is not None:
            vt = jnp.concatenate([vt, jnp.ones(vt.shape, BF16)], axis=0)
            v_tail = jnp.concatenate([v_tail, ones[:v_tail.shape[0]]], axis=1)
        return _dot_t(p[:, :P], vt) + _dot(p[:, P:], v_tail)

    return qk, pv


def _past_specs_and_bufs(past, T, Lp):
    if past is None:
        return [], [], ()
    layer, pk, pv = past
    P = pk.shape[-1]
    assert P % LANES == 0 and pk.shape[2:4] == (N_PAIRS, LANES)
    spec = pl.BlockSpec((None, 1, N_PAIRS, LANES, P), lambda b, qi: (layer, b, 0, 0, 0))
    return [spec, spec], [pltpu.VMEM((Lp - P, BRANCH_W), BF16)] * 2, (pk, pv)


def _fox_kernel(q_ref, k_ref, v_ref, *rest, tq, tk, q_off, has_past):
    past, (fq_ref, fk_ref, o_ref, m_ref, acc_ref), bufs = _split_past(rest, has_past)
    qk, pv = _key_ops(k_ref, v_ref, past, bufs, tk)
    q0 = q_off + pl.program_id(1) * tq
    diag = q0 // tk
    fq_blk = fq_ref[0]
    ones = jnp.ones((tk, LANES), BF16)

    group = acc_ref.shape[0]

    def head_group(g, _):
        hps = [g * (group // 2) + pp for pp in range(group // 2)]
        lanes = [_pair_lanes(hp) for hp in hps]
        qm = [_keep_head(q_ref[0, :, lanes[hh // 2]], hh % 2) for hh in range(group)]
        fq = [_head_column(fq_blk, g * group + hh) for hh in range(group)]
        m_ref[...] = jnp.full(m_ref.shape, NEG, F32)
        acc_ref[...] = jnp.zeros(acc_ref.shape, F32)

        def tiles(js, masked):
            k0 = [j * tk if isinstance(j, int) else pl.multiple_of(j * tk, tk) for j in js]
            chains = [(t, hh) for t in range(len(js)) for hh in range(group)]
            if masked:
                qpos, kpos = _positions(tq, tk, q0, k0[0])
                causal = kpos <= qpos
            s = {}
            for t, hh in chains:
                fk = fk_ref[0, g * group + hh, :, pl.ds(k0[t], tk)]
                s[t, hh] = qk(qm[hh], hps[hh // 2], k0[t]) + fq[hh] - fk
                if masked and t == 0:
                    s[t, hh] = jnp.where(causal, s[t, hh], NEG)
            m_old = [m_ref[hh] for hh in range(group)]
            m_new = list(m_old)
            for t, hh in chains:
                m_new[hh] = jnp.maximum(m_new[hh], jnp.max(s[t, hh], axis=1, keepdims=True))
            p = {c: jnp.exp2(s[c] - m_new[c[1]]).astype(BF16) for c in chains}
            for hh in range(group):
                upd = jnp.exp2(m_old[hh] - m_new[hh]) * acc_ref[hh]
                for t in range(len(js)):
                    upd = upd + pv(p[t, hh], hps[hh // 2], k0[t], ones)
                acc_ref[hh] = upd
                m_ref[hh] = m_new[hh]

        tiles([diag], True)

        def body(i, _):
            j = diag - 1 - 2 * i
            tiles([j, j - 1], False)
            return 0

        lax.fori_loop(0, diag // 2, body, 0)

        @pl.when(diag % 2 == 1)
        def _():
            tiles([0], False)
        for pp in range(group // 2):
            o0 = acc_ref[2 * pp, :, :LANES] / acc_ref[2 * pp, :, LANES:]
            o1 = acc_ref[2 * pp + 1, :, :LANES] / acc_ref[2 * pp + 1, :, LANES:]
            o_ref[0, :, lanes[pp]] = _merge_pair(o0, o1).astype(o_ref.dtype)
        return 0

    lax.fori_loop(0, N_HEADS // group, head_group, 0)


def _fox(q, k, v, past, fq, fk_t, *, tq, tk, q_off):
    B, T, _ = q.shape
    Lp = fk_t.shape[-1]
    qspec, kspec = _pair_specs(tq, k.shape[1])
    past_specs, bufs, past = _past_specs_and_bufs(past, T, Lp)
    return pl.pallas_call(
        functools.partial(_fox_kernel, tq=tq, tk=tk, q_off=q_off, has_past=bool(past)),
        grid=(B, T // tq),
        in_specs=[qspec, kspec, kspec] + past_specs
        + [pl.BlockSpec((1, tq, N_HEADS), lambda b, qi: (b, qi, 0)),
           pl.BlockSpec((1, N_HEADS, 1, Lp), lambda b, qi: (b, 0, 0, 0))],
        out_specs=qspec,
        out_shape=jax.ShapeDtypeStruct(q.shape, BF16),
        scratch_shapes=[pltpu.VMEM((FOX_GROUP, tq, 1), F32), pltpu.VMEM((FOX_GROUP, tq, 2 * LANES), F32)] + bufs,
        compiler_params=_cparams(("parallel", "parallel")),
        name="fox_attn",
    )(q, k, v, *past, fq, fk_t[:, :, None, :])


def _sb_kernel(q_ref, k_ref, v_ref, *rest, tq, tk, q_off, has_past):
    past, (o_ref, acc_ref, tail_ref), bufs = _split_past(rest, has_past)
    qk, pv = _key_ops(k_ref, v_ref, past, bufs, tk)
    q0 = q_off + pl.program_id(1) * tq
    diag = q0 // tk
    r = lax.broadcasted_iota(jnp.int32, (LANES, 2 * LANES), 0)
    c = lax.broadcasted_iota(jnp.int32, (LANES, 2 * LANES), 1)
    suffix_w = jnp.where((c >= LANES) | (r > c), 1.0, 0.0).astype(BF16)

    group = acc_ref.shape[0]

    def head_group(g, _):
        hps = [g * (group // 2) + pp for pp in range(group // 2)]
        lanes = [_pair_lanes(hp) for hp in hps]
        qm = [_keep_head(q_ref[0, :, lanes[hh // 2]], hh % 2) for hh in range(group)]
        acc_ref[...] = jnp.zeros(acc_ref.shape, F32)
        tail_ref[...] = jnp.zeros(tail_ref.shape, F32)

        def tiles(js, masked):
            k0 = [j * tk if isinstance(j, int) else pl.multiple_of(j * tk, tk) for j in js]
            chains = [(t, hh) for t in range(len(js)) for hh in range(group)]
            if masked:
                qpos, kpos = _positions(tq, tk, q0, k0[0])
                strict = kpos < qpos
            z = {c: qk(qm[c[1]], hps[c[1] // 2], k0[c[0]]) for c in chains}
            ls, la, after = {}, {}, {}
            for c in chains:
                ls[c] = jnp.minimum(z[c], 0.0) - jnp.log2(1.0 + jnp.exp2(-jnp.abs(z[c])))
                la[c] = ls[c] - z[c]
                if masked and c[0] == 0:
                    la[c] = jnp.where(strict, la[c], 0.0)
                    ls[c] = jnp.where(strict, ls[c], NEG)
                la[c] = la[c].astype(BF16)
            for hh in range(group):
                run = tail_ref[hh]
                for t in range(len(js)):
                    blocks = [None] * (tk // LANES)
                    for b in reversed(range(tk // LANES)):
                        res = _dot(la[t, hh][:, b * LANES:(b + 1) * LANES], suffix_w)
                        blocks[b] = res[:, :LANES] + run
                        run = run + res[:, LANES:]
                    after[t, hh] = jnp.concatenate(blocks, axis=1)
                tail_ref[hh] = run
            a = {c: jnp.exp2(ls[c] + after[c]).astype(BF16) for c in chains}
            for c in chains:
                acc_ref[c[1]] += pv(a[c], hps[c[1] // 2], k0[c[0]], None)

        tiles([diag], True)

        def body(i, _):
            j = diag - 1 - 2 * i
            tiles([j, j - 1], False)
            return 0

        lax.fori_loop(0, diag // 2, body, 0)

        @pl.when(diag % 2 == 1)
        def _():
            tiles([0], False)
        for pp in range(group // 2):
            o_ref[0, :, lanes[pp]] = _merge_pair(acc_ref[2 * pp], acc_ref[2 * pp + 1]).astype(o_ref.dtype)
        return 0

    lax.fori_loop(0, N_HEADS // group, head_group, 0)


def _sb(q, k, v, past, Lp, *, tq, tk, q_off):
    B, T, _ = q.shape
    qspec, kspec = _pair_specs(tq, k.shape[1])
    past_specs, bufs, past = _past_specs_and_bufs(past, T, Lp)
    return pl.pallas_call(
        functools.partial(_sb_kernel, tq=tq, tk=tk, q_off=q_off, has_past=bool(past)),
        grid=(B, T // tq),
        in_specs=[qspec, kspec, kspec] + past_specs,
        out_specs=qspec,
        out_shape=jax.ShapeDtypeStruct(q.shape, BF16),
        scratch_shapes=[pltpu.VMEM((SB_GROUP, tq, LANES), F32), pltpu.VMEM((SB_GROUP, tq, LANES), F32)] + bufs,
        compiler_params=_cparams(("parallel", "parallel")),
        name="sb_attn",
    )(q, k, v, *past)


def _const_key(v):
    bits = int(np.float32(v).view(np.int32))
    return (bits ^ 0x7FFFFFFF) + 1 if bits < 0 else bits


def _float_key(x):
    bits = lax.bitcast_convert_type(x, jnp.int32)
    return jnp.where(bits < 0, jnp.bitwise_xor(bits, 0x7FFFFFFF) + 1, bits)


KEY_MIN = -2 ** 31
KEY_HALF_NEG = _const_key(0.5 * NEG)


KEY_MAX = 2 ** 31 - 1
KEY_BITS = 32
COUNT_CHAINS = 8
SB_GROUP = 8
FOX_GROUP = 2


def _dsa_kernel(aq_ref, iq_ref, iwt_ref, k_ref, vt_ref, ik_ref, o_ref, key_ref, q8_ref, m_ref, acc_ref,
                *, tq, tk, q_off, n_rows, topk):
    q0 = q_off + pl.program_id(1) * tq
    n_tiles = (q0 + n_rows + tk - 1) // tk

    def tile_start(j):
        return pl.multiple_of(j * tk, LANES)

    iq = iq_ref[0]
    iwt = iwt_ref[0]
    qh = jnp.concatenate([_keep_head(iq[:, (h // 2) * LANES:(h // 2 + 1) * LANES], h % 2) for h in range(N_IDX)],
                         axis=0)
    q_chunk = jnp.right_shift(q0 + lax.broadcasted_iota(jnp.int32, (tk, tq), 1), CHUNK_SHIFT)

    def score_tile(j, _):
        k0 = tile_start(j)
        dots = _dot_t(ik_ref[0, pl.ds(k0, tk), :], qh)
        sc = jnp.zeros((tk, tq), F32)
        for h in range(N_IDX):
            sc = sc + jnp.maximum(dots[:, h * tq:(h + 1) * tq], 0.0) * iwt[h:h + 1, :]
        key_chunk = jnp.right_shift(k0 + lax.broadcasted_iota(jnp.int32, (tk, tq), 0), CHUNK_SHIFT)
        key_ref[pl.ds(k0, tk), :] = _float_key(jnp.where(key_chunk <= q_chunk, sc, NEG))
        return 0

    lax.fori_loop(0, n_tiles, score_tile, 0)

    def search():
        full_tiles = (q0 + n_rows) // tk
        tail_blocks = (q0 + n_rows - full_tiles * tk + LANES - 1) // LANES

        def step(_, cr):
            lo, hi, above_hi = cr
            mid = jnp.right_shift(lo, 1) + jnp.right_shift(hi, 1) + jnp.bitwise_and(jnp.bitwise_and(lo, hi), 1)

            def count_rows(r0, rows, cnt):
                above = jnp.where(key_ref[pl.ds(pl.multiple_of(r0, LANES), rows), :] > mid, 1.0, 0.0)
                return cnt + jnp.sum(above.reshape((-1,) + cnt.shape), axis=0)

            cnt = lax.fori_loop(0, full_tiles, lambda j, c: count_rows(j * tk, tk, c),
                                jnp.zeros((COUNT_CHAINS, SUBLANES, tq), F32))
            cnt = lax.fori_loop(0, tail_blocks, lambda b, c: count_rows(full_tiles * tk + b * LANES, LANES, c), cnt)
            cnt = jnp.sum(jnp.sum(cnt, axis=0), axis=0, keepdims=True)
            less = cnt < topk
            return jnp.where(less, lo, mid), jnp.where(less, mid, hi), jnp.where(less, cnt, above_hi)

        init = (jnp.full((1, tq), KEY_MIN, jnp.int32), jnp.full((1, tq), KEY_MAX, jnp.int32), jnp.zeros((1, tq), F32))
        _, hi, above_hi = lax.fori_loop(0, KEY_BITS, step, init)
        return hi, topk - above_hi

    thr, room = lax.cond(q0 + n_rows > topk, search,
                         lambda: (jnp.full((1, tq), KEY_MIN, jnp.int32), jnp.zeros((1, tq), F32)))
    row_ok = lax.broadcasted_iota(jnp.int32, (1, tq), 1) < n_rows
    room = jnp.where(row_ok, room, 0.0)

    @pl.when(jnp.max(room) > 0.0)
    def _():
        r = lax.broadcasted_iota(jnp.int32, (LANES, LANES), 0)
        c = lax.broadcasted_iota(jnp.int32, (LANES, LANES), 1)
        before = jnp.where(c < r, 1.0, 0.0).astype(BF16)

        def tie_tile(j, seen):
            k0 = tile_start(j)
            for b in range(tk // LANES):
                rows = pl.ds(k0 + b * LANES, LANES)
                keys = key_ref[rows, :]
                eq = keys == thr
                eqf = jnp.where(eq, 1.0, 0.0)
                rank = _dot(before, eqf.astype(BF16)) + seen
                key_ref[rows, :] = jnp.where(eq, jnp.where(rank < room, keys + 1, keys), keys)
                seen = seen + jnp.sum(eqf, axis=0, keepdims=True)
            return seen

        lax.fori_loop(0, n_tiles, tie_tile, jnp.zeros((1, tq), F32))

    aq = aq_ref[0]
    for h in range(N_HEADS):
        q8_ref[h * tq:(h + 1) * tq, :] = _keep_head(aq[:, (h // 2) * LANES:(h // 2 + 1) * LANES], h % 2)
    m_ref[...] = jnp.full(m_ref.shape, NEG, F32)
    acc_ref[...] = jnp.zeros(acc_ref.shape, F32)
    thr_sel = jnp.maximum(thr, KEY_HALF_NEG)

    def attend_tiles(js):
        k0 = [j * tk if isinstance(j, int) else tile_start(j) for j in js]
        groups = [slice(g * (N_HEADS // 2) * tq, (g + 1) * (N_HEADS // 2) * tq) for g in (0, 1)]
        chains = [(t, g) for t in range(len(js)) for g in (0, 1)]
        kt = [k_ref[0, pl.ds(k, tk), :] for k in k0]
        bias = [jnp.where(key_ref[pl.ds(k, tk), :] > thr_sel, 0.0, NEG) for k in k0]
        st = {(t, g): _dot_t(kt[t], q8_ref[groups[g], :]) for t, g in chains}
        st = {(t, g): jnp.concatenate([st[t, g][:, h * tq:(h + 1) * tq] + bias[t] for h in range(N_HEADS // 2)], axis=1)
              for t, g in chains}
        m_old = [m_ref[:, cols] for cols in groups]
        m_new = list(m_old)
        for t, g in chains:
            m_new[g] = jnp.maximum(m_new[g], jnp.max(st[t, g], axis=0, keepdims=True))
        p = {(t, g): jnp.exp2(st[t, g] - m_new[g]).astype(BF16) for t, g in chains}
        for g, cols in enumerate(groups):
            upd = jnp.exp2(m_old[g] - m_new[g]) * acc_ref[:, cols]
            for t in range(len(js)):
                upd = upd + _dot(vt_ref[0, :, pl.ds(k0[t], tk)], p[t, g])
            acc_ref[:, cols] = upd
            m_ref[:, cols] = m_new[g]

    def attend_pair(i, _):
        attend_tiles([2 * i, 2 * i + 1])
        return 0

    lax.fori_loop(0, n_tiles // 2, attend_pair, 0)

    @pl.when(n_tiles % 2 == 1)
    def _():
        attend_tiles([n_tiles - 1])
    for hp in range(N_PAIRS):
        pair = []
        for h in (2 * hp, 2 * hp + 1):
            a = acc_ref[:, h * tq:(h + 1) * tq]
            pair.append(a[:HEAD_DIM] / a[HEAD_DIM:])
        o_ref[0, :, hp * LANES:(hp + 1) * LANES] = jnp.concatenate(pair, axis=0).T.astype(o_ref.dtype)


def _dsa(aq, iq, iw, k2, v1t, ik2, *, tq, q_off, topk):
    B, T, _ = aq.shape
    Lp = k2.shape[1]
    tk = Lp if T <= LANES else next(t for t in (512, 384, 256, 128) if Lp % t == 0)
    Tp = -(-T // LANES) * LANES
    tq = max(tq, LANES)
    n_rows = min(T, tq)
    pad_t = lambda a: jnp.pad(a, ((0, 0), (0, Tp - T), (0, 0)))
    qspec = lambda w: pl.BlockSpec((1, tq, w), lambda b, qi: (b, qi, 0))
    kspec = pl.BlockSpec((1, Lp, LANES), lambda b, qi: (b, 0, 0))
    out = pl.pallas_call(
        functools.partial(_dsa_kernel, tq=tq, tk=tk, q_off=q_off, n_rows=n_rows, topk=topk),
        grid=(B, Tp // tq),
        in_specs=[qspec(BRANCH_W), qspec(N_IDX * IDX_DIM), pl.BlockSpec((1, SUBLANES, tq), lambda b, qi: (b, 0, qi)),
                  kspec, pl.BlockSpec((1, LANES, Lp), lambda b, qi: (b, 0, 0)), kspec],
        out_specs=qspec(BRANCH_W),
        out_shape=jax.ShapeDtypeStruct((B, Tp, BRANCH_W), BF16),
        scratch_shapes=[pltpu.VMEM((Lp, tq), jnp.int32), pltpu.VMEM((N_HEADS * tq, LANES), BF16),
                        pltpu.VMEM((1, N_HEADS * tq), F32), pltpu.VMEM((LANES, N_HEADS * tq), F32)],
        compiler_params=_cparams(("parallel", "parallel")),
        name="dsa_attn",
    )(pad_t(aq), pad_t(iq), jnp.swapaxes(pad_t(iw), 1, 2), k2, v1t, ik2)
    return out[:, :T]


def _merge_kernel(x_ref, oa_ref, ob_ref, oc_ref, g_ref, wg_ref, wb_ref, wo_ref, y_ref):
    x = x_ref[0]
    D = x.shape[-1]
    xn = _rms(x, g_ref[...]).astype(BF16)
    merged = jnp.zeros(x.shape, F32)
    for n, o_ref in enumerate((oa_ref, ob_ref, oc_ref)):
        gate = _sigmoid(_dot(xn, wg_ref[:, n * D:(n + 1) * D]))
        merged = merged + gate * _dot(o_ref[0], wb_ref[n])
    y_ref[0] = x + _dot(merged.astype(BF16), wo_ref[...])


def _merge(x, oa, ob, oc, g, wg, wb, wo, tm):
    B, T, D = x.shape
    tspec = lambda w_: pl.BlockSpec((1, tm, w_), lambda b, t: (b, t, 0))
    const = lambda a: pl.BlockSpec(a.shape, lambda b, t: (0,) * a.ndim)
    return pl.pallas_call(
        _merge_kernel,
        grid=(B, T // tm),
        in_specs=[tspec(D), tspec(BRANCH_W), tspec(BRANCH_W), tspec(BRANCH_W), const(g), const(wg), const(wb),
                  const(wo)],
        out_specs=tspec(D),
        out_shape=jax.ShapeDtypeStruct(x.shape, F32),
        compiler_params=_cparams(("parallel", "parallel")),
        name="merge_out",
    )(x, oa, ob, oc, g, wg, wb, wo)


def _ffn_kernel(x_ref, g_ref, past_ref, wu_ref, cw_ref, cb_ref, wd_ref, gf_ref, y_ref, st_ref,
                hbuf_ref, carry_ref, acc_ref, *, tm, final_norm):
    fc = FF_CHUNK
    d_ff = wd_ref.shape[0]
    n_chunks = d_ff // fc

    @pl.when(pl.program_id(1) == 0)
    def _():
        carry_ref[...] = past_ref[0]

    x = x_ref[0]
    xn = _rms(x, g_ref[...]).astype(BF16)
    acc_ref[...] = jnp.zeros(acc_ref.shape, F32)

    def up_proj(c):
        return jnp.concatenate([_dot(xn, wu_ref[:, c * fc:(c + 1) * fc]),
                                _dot(xn, wu_ref[:, d_ff + c * fc:d_ff + (c + 1) * fc])], axis=1)

    h_next = up_proj(0)
    for c in range(n_chunks):
        hbuf = hbuf_ref.at[c % 2]
        both = lambda ref: jnp.concatenate([ref[:, c * fc:(c + 1) * fc], ref[:, d_ff + c * fc:d_ff + (c + 1) * fc]],
                                           axis=1)
        h = h_next
        if c + 1 < n_chunks:
            h_next = up_proj(c + 1)
        hbuf[0:SUBLANES, :] = carry_ref[c]
        hbuf[SUBLANES:SUBLANES + tm, :] = h
        carry_ref[c] = h[tm - SUBLANES:tm, :]
        cw = both(cw_ref)
        hc = (both(cb_ref) + cw[0:1] * hbuf[SUBLANES - 2:SUBLANES - 2 + tm, :]
              + cw[1:2] * hbuf[SUBLANES - 1:SUBLANES - 1 + tm, :] + cw[2:3] * h)
        gate, up = hc[:, :fc], hc[:, fc:]
        act = gate * _sigmoid(gate) * up
        acc_ref[...] += _dot(act.astype(BF16), wd_ref[c * fc:(c + 1) * fc, :])
    st_ref[0] = carry_ref[...]
    y = x + acc_ref[...]
    if final_norm:
        y = _rms(y, gf_ref[...])
    y_ref[0] = y


def _ffn(x, g, past, wu, cw, cb, wd, gf, tm, final_norm):
    B, T, D = x.shape
    nc, fc2 = wd.shape[0] // FF_CHUNK, 2 * FF_CHUNK
    tspec = pl.BlockSpec((1, tm, D), lambda b, t: (b, t, 0))
    const = lambda a: pl.BlockSpec(a.shape, lambda b, t: (0,) * a.ndim, pipeline_mode=pl.Buffered(1))
    sspec = pl.BlockSpec((1, nc, SUBLANES, fc2), lambda b, t: (b, 0, 0, 0))
    return pl.pallas_call(
        functools.partial(_ffn_kernel, tm=tm, final_norm=final_norm),
        grid=(B, T // tm),
        in_specs=[tspec, const(g), sspec, const(wu), const(cw), const(cb), const(wd), const(gf)],
        out_specs=[tspec, sspec],
        out_shape=[jax.ShapeDtypeStruct(x.shape, F32), jax.ShapeDtypeStruct((B, nc, SUBLANES, fc2), F32)],
        scratch_shapes=[pltpu.VMEM((2, tm + SUBLANES, fc2), F32), pltpu.VMEM((nc, SUBLANES, fc2), F32),
                        pltpu.VMEM((tm, D), F32)],
        compiler_params=_cparams(("arbitrary", "arbitrary")),
        name="conv_ffn",
    )(x, g, past, wu, cw, cb, wd, gf)


def _rope_tables(pos):
    half = ROT_DIM // 2
    freq = ROPE_THETA ** (-jnp.arange(half, dtype=F32) / half)
    ang = pos.astype(F32)[:, None] * freq[None, :]
    lane = np.arange(LANES) % HEAD_DIM
    cos = jnp.cos(ang)[:, lane % half]
    sin = jnp.sin(ang)[:, lane % half]
    rc = jnp.where(lane < ROT_DIM, cos, 1.0)
    rs1 = jnp.where(lane < half, -sin, 0.0)
    rs2 = jnp.where((lane >= half) & (lane < ROT_DIM), sin, 0.0)
    return rc, rs1, rs2


IN_SIZES = (BRANCH_W, HEAD_DIM, HEAD_DIM, N_IDX * IDX_DIM, IDX_DIM, N_IDX) + (BRANCH_W,) * 6 + (N_HEADS,)
PACKED_W = ROPE_W + MISC_W + 6 * BRANCH_W


def _pack_w_kernel(w_ref, o_ref, s_ref):
    offs = np.concatenate([[0], np.cumsum(IN_SIZES)]).tolist()
    a_q, a_k, a_v, i_q, i_k, i_w, b_q = offs[:7]
    c_f = offs[12]
    src = lambda c0, width: w_ref[0, :, c0:c0 + width]
    s_ref[:, 0:BRANCH_W] = src(a_q, BRANCH_W)
    iq_w = N_IDX * IDX_DIM
    s_ref[:, BRANCH_W:BRANCH_W + iq_w] = src(i_q, iq_w)
    ak_col = BRANCH_W + iq_w
    for dst, c0 in ((ak_col, a_k), (ak_col + LANES, i_k), (ROPE_W, a_v)):
        s_ref[:, dst:dst + HEAD_DIM] = src(c0, HEAD_DIM)
        s_ref[:, dst + HEAD_DIM:dst + 2 * HEAD_DIM] = src(c0, HEAD_DIM)
    misc = ROPE_W + LANES
    s_ref[:, misc:misc + LANES] = jnp.zeros((s_ref.shape[0], LANES), F32)
    s_ref[:, misc:misc + N_IDX] = src(i_w, N_IDX)
    s_ref[:, misc + CF_LANE:misc + CF_LANE + N_HEADS] = src(c_f, N_HEADS)
    s_ref[:, ROPE_W + MISC_W:PACKED_W] = src(b_q, 6 * BRANCH_W)
    o_ref[...] = s_ref[...].astype(BF16)


def _pack_w(w_in, l):
    D = w_in.shape[1]
    rows = 256
    return pl.pallas_call(
        _pack_w_kernel,
        grid=(D // rows,),
        in_specs=[pl.BlockSpec((1, rows, w_in.shape[2]), lambda r: (l, r, 0))],
        out_specs=pl.BlockSpec((rows, PACKED_W), lambda r: (r, 0)),
        out_shape=jax.ShapeDtypeStruct((D, PACKED_W), BF16),
        scratch_shapes=[pltpu.VMEM((rows, PACKED_W), F32)],
        compiler_params=_cparams(("parallel",)),
        name="pack_w_in",
    )(w_in)


def _pack_layer(l, norm_mix, w_in, b_f, w_branch, w_gate, w_out, norm_ffn, w_up, conv_w, conv_b, w_down):
    w = _pack_w(w_in, l)
    bf_row = jnp.zeros((1, LANES), F32).at[0, CF_LANE:CF_LANE + N_HEADS].set(b_f[l])

    d_ff = w_down.shape[1]
    fc = FF_CHUNK
    nc = d_ff // fc
    assert nc * fc == d_ff
    halves = lambda a: jnp.concatenate([a[..., :d_ff].reshape(a.shape[:-1] + (nc, fc)),
                                        a[..., d_ff:].reshape(a.shape[:-1] + (nc, fc))], axis=-1)
    wu, cw, cb, wd = w_up[l].astype(BF16), conv_w[l], conv_b[l][None], w_down[l].astype(BF16)
    return dict(g_mix=norm_mix[l][None], w=w, bf_row=bf_row, wg=w_gate[l].astype(BF16),
                wb=w_branch[l].astype(BF16), wo=w_out[l].astype(BF16), g_ffn=norm_ffn[l][None],
                wu=wu, cw=cw, cb=cb, wd=wd, halves=halves, nc=nc, d_ff=d_ff)


def _conv_state_in(state, halves):
    st = jnp.moveaxis(halves(state), 2, 1)
    return jnp.pad(st, ((0, 0), (0, 0), (SUBLANES - (CONV_W - 1), 0), (0, 0)))


def _conv_state_out(st, d_ff):
    st = st[:, :, SUBLANES - (CONV_W - 1):, :]
    B, nc, r, fc2 = st.shape
    fc = fc2 // 2
    gate = jnp.moveaxis(st[..., :fc], 1, 2).reshape(B, r, d_ff)
    up = jnp.moveaxis(st[..., fc:], 1, 2).reshape(B, r, d_ff)
    return jnp.concatenate([gate, up], axis=-1)


def _pad_keys(a, Lp):
    return jnp.pad(a, ((0, 0), (0, Lp - a.shape[1]), (0, 0)))


def _group_layer(x, past, lw, gf, layer, depth, stacked, tiles):
    B, T, D = x.shape
    P = 0 if past is None else past[0].shape[1]
    L = P + T
    Lp = -(-L // KEY_TILE) * KEY_TILE
    topk = max(1, min(TOPK_MAX, L // 4))
    tables = _rope_tables(jnp.arange(P, L))

    stacked, (iw, aq16, iq16, ak16, avt16, ik16, bq16, bk16, bv16, cq16, ck16, cv16) = _project(
        x, lw["g_mix"], lw["w"], lw["bf_row"], tables, tiles["proj"], layer, depth, stacked)
    lf = stacked[-1][layer]

    if past is None:
        keys = (ak16, ik16)
        lf_all = lf
        conv_in = jnp.zeros((B, lw["nc"], SUBLANES, 2 * FF_CHUNK), F32)
        past_b = past_c = None
    else:
        p_ak, p_av, p_ik, p_bk, p_bv, p_ck, p_cv, p_lf, p_conv = past
        dup = lambda a: jnp.concatenate([a, a], axis=-1).astype(BF16)
        keys = tuple(jnp.concatenate([o, n], axis=1) for o, n in zip((dup(p_ak), dup(p_ik)), (ak16, ik16)))
        old_vt = jnp.swapaxes(p_av, 1, 2).astype(BF16)
        avt16 = jnp.concatenate([jnp.concatenate([old_vt, jnp.ones_like(old_vt)], axis=1), avt16], axis=2)
        lf_all = jnp.concatenate([p_lf, lf], axis=1)
        conv_in = _conv_state_in(p_conv, lw["halves"])
        past_b, past_c = (layer, p_bk, p_bv), (layer, p_ck, p_cv)
    ak2, ik2 = (_pad_keys(a, Lp) for a in keys)
    v1t = jnp.pad(avt16, ((0, 0), (0, 0), (0, Lp - L)))

    fk_t = _cumsum_time(jnp.swapaxes(_pad_keys(lf_all, Lp), 1, 2))
    fq = jnp.swapaxes(fk_t[:, :, P:L], 1, 2)

    o_a = _dsa(aq16, iq16, iw, ak2, v1t, ik2, tq=tiles["dsa"], q_off=P, topk=topk)
    o_b = _sb(bq16, bk16, bv16, past_b, Lp, tq=tiles["sb"][0], tk=tiles["sb"][1], q_off=P)
    o_c = _fox(cq16, ck16, cv16, past_c, fq, fk_t, tq=tiles["fox"][0], tk=tiles["fox"][1], q_off=P)

    x = _merge(x, o_a, o_b, o_c, lw["g_mix"], lw["wg"], lw["wb"], lw["wo"], tiles["merge"])
    x, st = _ffn(x, lw["g_ffn"], conv_in, lw["wu"], lw["cw"], lw["cb"], lw["wd"], gf, tiles["ffn"],
                 layer == depth - 1)
    return x, stacked, _conv_state_out(st, lw["d_ff"])


def _tiles(T, P):
    pick = lambda want: min(want, T)
    if T % 512 == 0 and P % 512 == 0:
        sb, fox = (256, 256), (256, 512)
    else:
        assert T <= KEY_TILE
        sb = fox = (T, -(-(P + T) // KEY_TILE) * KEY_TILE)
    return dict(proj=pick(512), dsa=pick(128), sb=sb, fox=fox, merge=pick(512), ffn=pick(256))


def kernel(x_prompt, x_sample, cache_a_k, cache_a_v, cache_a_idx_k, cache_b_k, cache_b_v, cache_c_k, cache_c_v,
           cache_c_logf, state_ffn_conv, norm_mix, w_in, b_f, w_branch, w_gate, w_out, norm_ffn, w_up, conv_w,
           conv_b, w_down, norm_final):
    depth = w_in.shape[0]
    caches = (cache_a_k, cache_a_v, cache_a_idx_k, cache_b_k, cache_b_v, cache_c_k, cache_c_v, cache_c_logf,
              state_ffn_conv)
    layers = [_pack_layer(l, norm_mix, w_in, b_f, w_branch, w_gate, w_out, norm_ffn, w_up, conv_w, conv_b, w_down)
              for l in range(depth)]
    gf = norm_final[None]

    def trunk(x, past):
        B, T, _ = x.shape
        tiles = _tiles(T, 0 if past is None else past[0].shape[2])
        stacked, conv_states = None, []
        if past is not None:
            P = past[0].shape[2]
            wide = lambda c: jnp.transpose(c, (0, 1, 3, 4, 2)).reshape(depth, B, N_PAIRS, LANES, P)
        for l in range(depth):
            lp = None if past is None else tuple(wide(c) if 3 <= i <= 6 else c[l] for i, c in enumerate(past))
            x, stacked, conv_state = _group_layer(x, lp, layers[l], gf, l, depth, stacked, tiles)
            conv_states.append(conv_state)
        ak, av, ik, bk, bv, ck, cv, lf = stacked
        heads = lambda a: a.reshape(depth, B, T, N_HEADS, HEAD_DIM)
        return x, (ak, av, ik, heads(bk), heads(bv), heads(ck), heads(cv), lf, jnp.stack(conv_states, axis=0))

    y_prompt, p_state = trunk(x_prompt, None)
    y_sample, s_state = trunk(x_sample, caches)
    return (y_prompt, y_sample) + p_state + s_state
```
